```python
import jax, jax.numpy as jnp
from jax import lax
import numpy as np

D_MODEL = 1024
BATCH = 8
SEQ = 8192
DEPTH = 2

POOL_WINDOWS = (2, 4, 8, 16)
POOL_GROUPS = 4
POOL_WIDTH = D_MODEL // 2
POOL_GROUP_DIM = POOL_WIDTH // POOL_GROUPS
CONV_WIDTH = D_MODEL // 2
CONV_KERNEL = 31
N_HEADS = 8
HEAD_DIM = 64
ATTN_WIDTH = N_HEADS * HEAD_DIM
Q_BLOCK = 128
N_BRANCHES = 3
IN_SIZES = (POOL_WIDTH, POOL_WIDTH, 2 * CONV_WIDTH, CONV_WIDTH, 3 * ATTN_WIDTH, ATTN_WIDTH, N_BRANCHES * D_MODEL)
IN_WIDTH = sum(IN_SIZES)
RMS_EPS = 1e-6
LN_EPS = 1e-5

kernel_name = "hybrid_pool_conv_stickbreak_gated_block"


def rms_norm(x, g):
    xf = x.astype(jnp.float32)
    y = xf * lax.rsqrt(jnp.mean(xf * xf, axis=-1, keepdims=True) + RMS_EPS)
    return (y * g.astype(jnp.float32)).astype(x.dtype)


def layer_norm(x, g, b):
    xf = x.astype(jnp.float32)
    mu = jnp.mean(xf, axis=-1, keepdims=True)
    var = jnp.mean(jnp.square(xf - mu), axis=-1, keepdims=True)
    y = (xf - mu) * lax.rsqrt(var + LN_EPS)
    return (y * g.astype(jnp.float32) + b.astype(jnp.float32)).astype(x.dtype)


def multiscale_pool(u, pool_w, pool_b, pool_scale):
    B, T, C = u.shape
    uf = u.astype(jnp.float32)
    cs = jnp.cumsum(uf, axis=1)
    pos = jnp.arange(T)
    diffs = []
    for g, w in enumerate(POOL_WINDOWS):
        sl = slice(g * POOL_GROUP_DIM, (g + 1) * POOL_GROUP_DIM)
        c = cs[..., sl]
        lagged = jnp.pad(c, ((0, 0), (w, 0), (0, 0)))[:, :T]
        count = jnp.minimum(pos + 1, w).astype(jnp.float32)[None, :, None]
        diffs.append((c - lagged) / count - uf[..., sl])
    d = jnp.stack(diffs, axis=2).astype(u.dtype)
    y = jnp.einsum('btgc,gcd->btgd', d, pool_w) + pool_b
    return y.reshape(B, T, C) * pool_scale


def conformer_conv(c2, conv_w, conv_b, ln_g, ln_b):
    a, b = jnp.split(c2, 2, axis=-1)
    u = a * jax.nn.sigmoid(b)
    u = lax.conv_general_dilated(
        u, conv_w[:, None, :].astype(u.dtype), window_strides=(1,),
        padding=[(CONV_KERNEL - 1, 0)], dimension_numbers=('NWC', 'WIO', 'NWC'),
        feature_group_count=CONV_WIDTH) + conv_b
    u = layer_norm(u, ln_g, ln_b)
    return jax.nn.silu(u)


def stick_breaking_attention(q, k, v):
    B, H, T, Dh = q.shape
    nb = T // Q_BLOCK
    scale = 1.0 / np.sqrt(Dh).astype(np.float32)
    qb = q.reshape(B, H, nb, Q_BLOCK, Dh).transpose(2, 0, 1, 3, 4)
    kf = k.astype(jnp.float32)
    vf = v.astype(jnp.float32)
    key_pos = jnp.arange(T)

    def block(args):
        q_blk, i = args
        z = jnp.einsum('bhqd,bhkd->bhqk', q_blk.astype(jnp.float32), kf) * scale
        q_pos = i * Q_BLOCK + jnp.arange(Q_BLOCK)
        mask = key_pos[None, :] < q_pos[:, None]
        log_beta = jax.nn.log_sigmoid(z)
        log_1m = jnp.where(mask, jax.nn.log_sigmoid(-z), 0.0)
        between = lax.cumsum(log_1m, axis=3, reverse=True) - log_1m
        w = jnp.where(mask, jnp.exp(log_beta + between), 0.0)
        return jnp.einsum('bhqk,bhkd->bhqd', w, vf)

    out = lax.map(block, (qb, jnp.arange(nb)))
    return out.transpose(1, 2, 0, 3, 4).reshape(B, H, T, Dh).astype(v.dtype)


def _fwd_setup_inputs(seed: int = 0) -> dict:
    key = jax.random.key(seed)
    ks = jax.random.split(key, 16)
    f32 = jnp.float32
    L, D = DEPTH, D_MODEL
    nrm = lambda k, shape, fan_in: jax.random.normal(k, shape, f32) * (fan_in ** -0.5)
    return {
        "x": jax.random.normal(ks[0], (BATCH, SEQ, D), f32),
        "norm_pre": 1.0 + 0.02 * jax.random.normal(ks[1], (L, D), f32),
        "w_in": nrm(ks[2], (L, D, IN_WIDTH), D),
        "pool_w": nrm(ks[3], (L, POOL_GROUPS, POOL_GROUP_DIM, POOL_GROUP_DIM), POOL_GROUP_DIM),
        "pool_b": 0.02 * jax.random.normal(ks[4], (L, POOL_GROUPS, POOL_GROUP_DIM), f32),
        "pool_scale": 1.0 + 0.02 * jax.random.normal(ks[5], (L, POOL_WIDTH), f32),
        "w_pool_out": nrm(ks[6], (L, POOL_WIDTH, D), POOL_WIDTH),
        "conv_w": nrm(ks[7], (L, CONV_KERNEL, CONV_WIDTH), CONV_KERNEL),
        "conv_b": 0.02 * jax.random.normal(ks[8], (L, CONV_WIDTH), f32),
        "conv_ln_g": 1.0 + 0.02 * jax.random.normal(ks[9], (L, CONV_WIDTH), f32),
        "conv_ln_b": 0.02 * jax.random.normal(ks[10], (L, CONV_WIDTH), f32),
        "w_conv_out": nrm(ks[11], (L, CONV_WIDTH, D), CONV_WIDTH),
        "w_attn_out": nrm(ks[12], (L, ATTN_WIDTH, D), ATTN_WIDTH),
        "w_o": nrm(ks[13], (L, D, D), D),
        "norm_post": 1.0 + 0.02 * jax.random.normal(ks[14], (L, D), f32),
    }


def _fwd_reference(x, norm_pre, w_in, pool_w, pool_b, pool_scale, w_pool_out, conv_w, conv_b,
              conv_ln_g, conv_ln_b, w_conv_out, w_attn_out, w_o, norm_post):
    B, T, D = x.shape
    split_at = list(np.cumsum(IN_SIZES)[:-1])
    for l in range(DEPTH):
        h = rms_norm(x, norm_pre[l])
        proj = jnp.einsum('btd,de->bte', h, w_in[l])
        p, gp, c2, gc, qkv, ga, gm = jnp.split(proj, split_at, axis=-1)

        ya = multiscale_pool(p, pool_w[l], pool_b[l], pool_scale[l]) * jax.nn.silu(gp)
        ya = ya @ w_pool_out[l]

        yb = conformer_conv(c2, conv_w[l], conv_b[l], conv_ln_g[l], conv_ln_b[l]) * jax.nn.silu(gc)
        yb = yb @ w_conv_out[l]

        q, k, v = jnp.split(qkv.reshape(B, T, 3, N_HEADS, HEAD_DIM).transpose(2, 0, 3, 1, 4), 3, axis=0)
        o = stick_breaking_attention(q[0], k[0], v[0])
        o = o.transpose(0, 2, 1, 3).reshape(B, T, ATTN_WIDTH) * jax.nn.silu(ga)
        yc = o @ w_attn_out[l]

        g = jax.nn.sigmoid(gm.reshape(B, T, N_BRANCHES, D))
        m = g[:, :, 0] * ya + g[:, :, 1] * yb + g[:, :, 2] * yc
        out = m @ w_o[l]
        x = x + rms_norm(out, norm_post[l])
    return x


import jax as _jax
import jax.numpy as _jnp

TWIN_FORMAT = 'train_step'
FWD_PARAMS = ['x', 'norm_pre', 'w_in', 'pool_w', 'pool_b', 'pool_scale', 'w_pool_out', 'conv_w', 'conv_b', 'conv_ln_g', 'conv_ln_b', 'w_conv_out', 'w_attn_out', 'w_o', 'norm_post']
TWIN_WEIGHTS = ['norm_pre', 'w_in', 'pool_w', 'pool_b', 'pool_scale', 'w_pool_out', 'conv_w', 'conv_b', 'conv_ln_g', 'conv_ln_b', 'w_conv_out', 'w_attn_out', 'w_o', 'norm_post']
TWIN_DIFF_INPUT = 'x'
TWIN_INPUTS = ['x', 'norm_pre', 'w_in', 'pool_w', 'pool_b', 'pool_scale', 'w_pool_out', 'conv_w', 'conv_b', 'conv_ln_g', 'conv_ln_b', 'w_conv_out', 'w_attn_out', 'w_o', 'norm_post', 'loss_target', 'm_norm_pre', 'm_w_in', 'm_pool_w', 'm_pool_b', 'm_pool_scale', 'm_w_pool_out', 'm_conv_w', 'm_conv_b', 'm_conv_ln_g', 'm_conv_ln_b', 'm_w_conv_out', 'm_w_attn_out', 'm_w_o', 'm_norm_post', 'v_norm_pre', 'v_w_in', 'v_pool_w', 'v_pool_b', 'v_pool_scale', 'v_w_pool_out', 'v_conv_w', 'v_conv_b', 'v_conv_ln_g', 'v_conv_ln_b', 'v_w_conv_out', 'v_w_attn_out', 'v_w_o', 'v_norm_post']
TWIN_OUTPUTS = ['loss', 'grad_x', 'grad_norm_pre', 'grad_w_in', 'grad_pool_w', 'grad_pool_b', 'grad_pool_scale', 'grad_w_pool_out', 'grad_conv_w', 'grad_conv_b', 'grad_conv_ln_g', 'grad_conv_ln_b', 'grad_w_conv_out', 'grad_w_attn_out', 'grad_w_o', 'grad_norm_post', 'delta_norm_pre', 'delta_w_in', 'delta_pool_w', 'delta_pool_b', 'delta_pool_scale', 'delta_w_pool_out', 'delta_conv_w', 'delta_conv_b', 'delta_conv_ln_g', 'delta_conv_ln_b', 'delta_w_conv_out', 'delta_w_attn_out', 'delta_w_o', 'delta_norm_post', 'new_m_norm_pre', 'new_m_w_in', 'new_m_pool_w', 'new_m_pool_b', 'new_m_pool_scale', 'new_m_w_pool_out', 'new_m_conv_w', 'new_m_conv_b', 'new_m_conv_ln_g', 'new_m_conv_ln_b', 'new_m_w_conv_out', 'new_m_w_attn_out', 'new_m_w_o', 'new_m_norm_post', 'new_v_norm_pre', 'new_v_w_in', 'new_v_pool_w', 'new_v_pool_b', 'new_v_pool_scale', 'new_v_w_pool_out', 'new_v_conv_w', 'new_v_conv_b', 'new_v_conv_ln_g', 'new_v_conv_ln_b', 'new_v_w_conv_out', 'new_v_w_attn_out', 'new_v_w_o', 'new_v_norm_post']
TWIN_LEAF_KINDS = {'loss': 'loss', 'grad_x': 'grad_x', 'grad_norm_pre': 'grad_w', 'grad_w_in': 'grad_w', 'grad_pool_w': 'grad_w', 'grad_pool_b': 'grad_w', 'grad_pool_scale': 'grad_w', 'grad_w_pool_out': 'grad_w', 'grad_conv_w': 'grad_w', 'grad_conv_b': 'grad_w', 'grad_conv_ln_g': 'grad_w', 'grad_conv_ln_b': 'grad_w', 'grad_w_conv_out': 'grad_w', 'grad_w_attn_out': 'grad_w', 'grad_w_o': 'grad_w', 'grad_norm_post': 'grad_w', 'delta_norm_pre': 'delta_w', 'delta_w_in': 'delta_w', 'delta_pool_w': 'delta_w', 'delta_pool_b': 'delta_w', 'delta_pool_scale': 'delta_w', 'delta_w_pool_out': 'delta_w', 'delta_conv_w': 'delta_w', 'delta_conv_b': 'delta_w', 'delta_conv_ln_g': 'delta_w', 'delta_conv_ln_b': 'delta_w', 'delta_w_conv_out': 'delta_w', 'delta_w_attn_out': 'delta_w', 'delta_w_o': 'delta_w', 'delta_norm_post': 'delta_w', 'new_m_norm_pre': 'new_m', 'new_m_w_in': 'new_m', 'new_m_pool_w': 'new_m', 'new_m_pool_b': 'new_m', 'new_m_pool_scale': 'new_m', 'new_m_w_pool_out': 'new_m', 'new_m_conv_w': 'new_m', 'new_m_conv_b': 'new_m', 'new_m_conv_ln_g': 'new_m', 'new_m_conv_ln_b': 'new_m', 'new_m_w_conv_out': 'new_m', 'new_m_w_attn_out': 'new_m', 'new_m_w_o': 'new_m', 'new_m_norm_post': 'new_m', 'new_v_norm_pre': 'new_v', 'new_v_w_in': 'new_v', 'new_v_pool_w': 'new_v', 'new_v_pool_b': 'new_v', 'new_v_pool_scale': 'new_v', 'new_v_w_pool_out': 'new_v', 'new_v_conv_w': 'new_v', 'new_v_conv_b': 'new_v', 'new_v_conv_ln_g': 'new_v', 'new_v_conv_ln_b': 'new_v', 'new_v_w_conv_out': 'new_v', 'new_v_w_attn_out': 'new_v', 'new_v_w_o': 'new_v', 'new_v_norm_post': 'new_v'}


def _forward(args):
    return _fwd_reference(*[args[k] for k in FWD_PARAMS])


def _output_shape():
    def fwd():
        inp = _fwd_setup_inputs(0)
        return _fwd_reference(*[inp[k] for k in FWD_PARAMS])
    out = _jax.eval_shape(fwd)
    return out.shape, out.dtype

N_MICROBATCH = 1
ADAM_LR = 0.001
ADAM_B1 = 0.9
ADAM_B2 = 0.999
ADAM_EPS = 1e-08
ADAM_WD = 0.01
ADAM_STEP = 10
PER_EXAMPLE_BATCH_AXIS = {'x': 0, 'loss_target': 0}
SHARED_INPUTS = []
_WEIGHT_DTYPES = {'norm_pre': _jnp.float32, 'w_in': _jnp.float32, 'pool_w': _jnp.float32, 'pool_b': _jnp.float32, 'pool_scale': _jnp.float32, 'w_pool_out': _jnp.float32, 'conv_w': _jnp.float32, 'conv_b': _jnp.float32, 'conv_ln_g': _jnp.float32, 'conv_ln_b': _jnp.float32, 'w_conv_out': _jnp.float32, 'w_attn_out': _jnp.float32, 'w_o': _jnp.float32, 'norm_post': _jnp.float32}
MOMENT_SCALE = {'norm_pre': 9.327641e-01, 'w_in': 3.375526e-01, 'pool_w': 6.962177e-01, 'pool_b': 2.151964e+00, 'pool_scale': 7.194253e-01, 'w_pool_out': 5.141188e-01, 'conv_w': 3.997602e-01, 'conv_b': 1.387204e+00, 'conv_ln_g': 8.123390e-01, 'conv_ln_b': 1.031817e+00, 'w_conv_out': 3.591783e-01, 'w_attn_out': 2.947581e-01, 'w_o': 7.196959e-01, 'norm_post': 6.399735e+01}


def _to_microbatches(a, axis):
    t = _jnp.moveaxis(a, axis, 0)
    t = t.reshape((N_MICROBATCH, t.shape[0] // N_MICROBATCH) + t.shape[1:])
    return _jnp.moveaxis(t, 1, axis + 1)


def setup_inputs(seed: int = 0) -> dict:
    inp = _fwd_setup_inputs(seed)
    key = _jax.random.fold_in(_jax.random.key(seed), 7919)
    shape, _ = _output_shape()
    out = dict(inp)
    out["loss_target"] = _jax.random.normal(_jax.random.fold_in(key, 0), shape, _jnp.float32)
    for i, name in enumerate(TWIN_WEIGHTS):
        w = inp[name].astype(_jnp.float32)
        if MOMENT_SCALE is None:
            s = _jnp.sqrt(_jnp.mean(_jnp.square(w)) + 1e-30)
        else:
            s = MOMENT_SCALE[name]
        km, kv = _jax.random.split(_jax.random.fold_in(key, i + 1))
        out[name] = w
        out["m_" + name] = s * _jax.random.normal(km, w.shape, _jnp.float32)
        out["v_" + name] = (s * s) * _jax.random.uniform(kv, w.shape, _jnp.float32, 0.5, 1.5)
    if N_MICROBATCH > 1:
        for name, axis in PER_EXAMPLE_BATCH_AXIS.items():
            out[name] = _to_microbatches(out[name], axis)
    return {'x': out['x'], 'norm_pre': out['norm_pre'], 'w_in': out['w_in'], 'pool_w': out['pool_w'], 'pool_b': out['pool_b'], 'pool_scale': out['pool_scale'], 'w_pool_out': out['w_pool_out'], 'conv_w': out['conv_w'], 'conv_b': out['conv_b'], 'conv_ln_g': out['conv_ln_g'], 'conv_ln_b': out['conv_ln_b'], 'w_conv_out': out['w_conv_out'], 'w_attn_out': out['w_attn_out'], 'w_o': out['w_o'], 'norm_post': out['norm_post'], 'loss_target': out['loss_target'], 'm_norm_pre': out['m_norm_pre'], 'm_w_in': out['m_w_in'], 'm_pool_w': out['m_pool_w'], 'm_pool_b': out['m_pool_b'], 'm_pool_scale': out['m_pool_scale'], 'm_w_pool_out': out['m_w_pool_out'], 'm_conv_w': out['m_conv_w'], 'm_conv_b': out['m_conv_b'], 'm_conv_ln_g': out['m_conv_ln_g'], 'm_conv_ln_b': out['m_conv_ln_b'], 'm_w_conv_out': out['m_w_conv_out'], 'm_w_attn_out': out['m_w_attn_out'], 'm_w_o': out['m_w_o'], 'm_norm_post': out['m_norm_post'], 'v_norm_pre': out['v_norm_pre'], 'v_w_in': out['v_w_in'], 'v_pool_w': out['v_pool_w'], 'v_pool_b': out['v_pool_b'], 'v_pool_scale': out['v_pool_scale'], 'v_w_pool_out': out['v_w_pool_out'], 'v_conv_w': out['v_conv_w'], 'v_conv_b': out['v_conv_b'], 'v_conv_ln_g': out['v_conv_ln_g'], 'v_conv_ln_b': out['v_conv_ln_b'], 'v_w_conv_out': out['v_w_conv_out'], 'v_w_attn_out': out['v_w_attn_out'], 'v_w_o': out['v_w_o'], 'v_norm_post': out['v_norm_post']}


def _loss(weights, diff, rest, loss_target):
    with _jax.named_scope("forward"):
        args = {**rest, TWIN_DIFF_INPUT: diff, **{k: w.astype(_WEIGHT_DTYPES[k]) for k, w in weights.items()}}
        y = _forward(args)
    with _jax.named_scope("loss_head"):
        err = _jnp.square(y.astype(_jnp.float32) - loss_target)
        return 0.5 * _jnp.sum(_jnp.mean(err, axis=-1)) if err.ndim else 0.5 * err


def _adamw(w, g, m, v):
    m = ADAM_B1 * m + (1.0 - ADAM_B1) * g
    v = ADAM_B2 * v + (1.0 - ADAM_B2) * _jnp.square(g)
    m_hat = m / (1.0 - ADAM_B1 ** ADAM_STEP)
    v_hat = v / (1.0 - ADAM_B2 ** ADAM_STEP)
    delta = -ADAM_LR * (m_hat / (_jnp.sqrt(v_hat) + ADAM_EPS) + ADAM_WD * w)
    return delta, m, v


def reference(x, norm_pre, w_in, pool_w, pool_b, pool_scale, w_pool_out, conv_w, conv_b, conv_ln_g, conv_ln_b, w_conv_out, w_attn_out, w_o, norm_post, loss_target, m_norm_pre, m_w_in, m_pool_w, m_pool_b, m_pool_scale, m_w_pool_out, m_conv_w, m_conv_b, m_conv_ln_g, m_conv_ln_b, m_w_conv_out, m_w_attn_out, m_w_o, m_norm_post, v_norm_pre, v_w_in, v_pool_w, v_pool_b, v_pool_scale, v_w_pool_out, v_conv_w, v_conv_b, v_conv_ln_g, v_conv_ln_b, v_w_conv_out, v_w_attn_out, v_w_o, v_norm_post):
    given = dict(x=x, norm_pre=norm_pre, w_in=w_in, pool_w=pool_w, pool_b=pool_b, pool_scale=pool_scale, w_pool_out=w_pool_out, conv_w=conv_w, conv_b=conv_b, conv_ln_g=conv_ln_g, conv_ln_b=conv_ln_b, w_conv_out=w_conv_out, w_attn_out=w_attn_out, w_o=w_o, norm_post=norm_post, loss_target=loss_target, m_norm_pre=m_norm_pre, m_w_in=m_w_in, m_pool_w=m_pool_w, m_pool_b=m_pool_b, m_pool_scale=m_pool_scale, m_w_pool_out=m_w_pool_out, m_conv_w=m_conv_w, m_conv_b=m_conv_b, m_conv_ln_g=m_conv_ln_g, m_conv_ln_b=m_conv_ln_b, m_w_conv_out=m_w_conv_out, m_w_attn_out=m_w_attn_out, m_w_o=m_w_o, m_norm_post=m_norm_post, v_norm_pre=v_norm_pre, v_w_in=v_w_in, v_pool_w=v_pool_w, v_pool_b=v_pool_b, v_pool_scale=v_pool_scale, v_w_pool_out=v_w_pool_out, v_conv_w=v_conv_w, v_conv_b=v_conv_b, v_conv_ln_g=v_conv_ln_g, v_conv_ln_b=v_conv_ln_b, v_w_conv_out=v_w_conv_out, v_w_attn_out=v_w_attn_out, v_w_o=v_w_o, v_norm_post=v_norm_post)
    weights = {n: given[n] for n in TWIN_WEIGHTS}
    shared = {n: given[n] for n in SHARED_INPUTS}
    per_example = {n: given[n] for n in ['x']}
    grad_fn = _jax.value_and_grad(_loss, argnums=(0, 1))

    def one_microbatch(ex, loss_target):
        ex = dict(ex)
        diff = ex.pop(TWIN_DIFF_INPUT)
        return grad_fn(weights, diff, {**shared, **ex}, loss_target)

    if N_MICROBATCH == 1:
        loss, (grad_w, grad_x) = one_microbatch(per_example, given["loss_target"])
    else:
        def body(carry, xs):
            loss_sum, grad_sum = carry
            l_k, (gw_k, gx_k) = one_microbatch(xs[0], xs[1])
            with _jax.named_scope("update"):
                return (loss_sum + l_k, _jax.tree.map(_jnp.add, grad_sum, gw_k)), gx_k

        init = (_jnp.zeros((), _jnp.float32), _jax.tree.map(_jnp.zeros_like, weights))
        (loss, grad_w), grad_x = _jax.lax.scan(body, init, (per_example, given["loss_target"]))
    with _jax.named_scope("update"):
        delta_w, new_m, new_v = {}, {}, {}
        for n in TWIN_WEIGHTS:
            delta_w[n], new_m[n], new_v[n] = _adamw(weights[n], grad_w[n], given["m_" + n], given["v_" + n])
    return (loss, grad_x, *[grad_w[n] for n in TWIN_WEIGHTS], *[delta_w[n] for n in TWIN_WEIGHTS],
            *[new_m[n] for n in TWIN_WEIGHTS], *[new_v[n] for n in TWIN_WEIGHTS])
```

```python
import functools

import jax
import jax.numpy as jnp
from jax import lax
from jax.experimental import pallas as pl
from jax.experimental.pallas import tpu as pltpu

F32 = jnp.float32
BF16 = jnp.bfloat16

D_MODEL = 1024
DEPTH = 2
POOL_WINDOWS = (2, 4, 8, 16)
POOL_GROUP_DIM = 128
BRANCH_WIDTH = 512
CONV_KERNEL = 31
CONV_HALO = 32
POOL_HALO = 16
HEAD_DIM = 64
HEAD_PAIR = 128
N_HEAD_PAIRS = 4
ATTN_SCALE = 0.125
RMS_EPS = 1e-6
LN_EPS = 1e-5
N_DEV = 8
LANES = 128

ADAM_LR = 0.001
ADAM_B1 = 0.9
ADAM_B2 = 0.999
ADAM_EPS = 1e-08
ADAM_WD = 0.01
ADAM_STEP = 10

ROW_TILE = 256
ATTN_BLOCK = 256
MM_TILE = 1024
ADAM_ROW_TILE = 1024
VMEM_LIMIT = 48 * 1024 * 1024

MESH = pl.DeviceIdType.MESH


def _params(n_axes):
    return pltpu.CompilerParams(dimension_semantics=("arbitrary",) * n_axes, vmem_limit_bytes=VMEM_LIMIT)


def _tile(n, pref):
    if n <= pref:
        return n
    t = (pref // LANES) * LANES
    while n % t:
        t -= LANES
    return t


def _dot(a, b):
    return jnp.dot(a, b, preferred_element_type=F32)


def _dot_nt(a, b):
    return lax.dot_general(a, b, (((1,), (1,)), ((), ())), preferred_element_type=F32)


def _dot_tn(a, b):
    return lax.dot_general(a, b, (((0,), (0,)), ((), ())), preferred_element_type=F32)


def _sigmoid(x):
    return 1.0 / (1.0 + jnp.exp(-x))


def _silu_grad(x, s):
    return s * (1.0 + x * (1.0 - s))


def _matmul(a, b, *, mode, name, out_dtype=F32, acc=None):
    if mode == "nn":
        (m, k), n = a.shape, b.shape[1]
    elif mode == "nt":
        (m, k), n = a.shape, b.shape[0]
    else:
        (k, m), n = a.shape, b.shape[1]
    tm, tn, tk = _tile(m, MM_TILE), _tile(n, MM_TILE), _tile(k, MM_TILE)
    nk = k // tk
    dot = {"nn": _dot, "nt": _dot_nt, "tn": _dot_tn}[mode]
    a_spec = pl.BlockSpec((tk, tm), lambda i, j, kk: (kk, i)) if mode == "tn" else pl.BlockSpec((tm, tk), lambda i, j, kk: (i, kk))
    b_spec = pl.BlockSpec((tn, tk), lambda i, j, kk: (j, kk)) if mode == "nt" else pl.BlockSpec((tk, tn), lambda i, j, kk: (kk, j))
    o_spec = pl.BlockSpec((tm, tn), lambda i, j, kk: (i, j))
    has_acc = acc is not None

    def body(*refs):
        a_ref, b_ref = refs[0], refs[1]
        acc_in = refs[2] if has_acc else None
        o_ref = refs[3] if has_acc else refs[2]
        part = dot(a_ref[...], b_ref[...])
        if nk == 1:
            if has_acc:
                part = part + acc_in[...]
            o_ref[...] = part.astype(out_dtype)
            return
        scr = refs[-1]
        kk = pl.program_id(2)

        @pl.when(kk == 0)
        def _():
            scr[...] = part + acc_in[...] if has_acc else part

        @pl.when(kk > 0)
        def _():
            scr[...] += part

        @pl.when(kk == nk - 1)
        def _():
            o_ref[...] = scr[...].astype(out_dtype)

    in_specs = [a_spec, b_spec] + ([o_spec] if has_acc else [])
    args = (a, b) + ((acc,) if has_acc else ())
    return pl.pallas_call(
        body, name=name, grid=(m // tm, n // tn, nk),
        in_specs=in_specs, out_specs=o_spec,
        out_shape=jax.ShapeDtypeStruct((m, n), out_dtype),
        scratch_shapes=[pltpu.VMEM((tm, tn), F32)] if nk > 1 else [],
        compiler_params=_params(3),
    )(*args)


def _rms_fwd(x, g, *, name, resid=None):
    t = x.shape[0]
    tm = _tile(t, ROW_TILE)
    row = pl.BlockSpec((tm, D_MODEL), lambda i: (i, 0))
    vec = pl.BlockSpec((1, D_MODEL), lambda i: (0, 0))
    has_resid = resid is not None

    def body(*refs):
        x_ref, g_ref = refs[0], refs[1]
        o_ref = refs[-1]
        xv = x_ref[...]
        y = xv * lax.rsqrt(jnp.mean(xv * xv, axis=-1, keepdims=True) + RMS_EPS) * g_ref[...]
        if has_resid:
            o_ref[...] = refs[2][...] + y
        else:
            o_ref[...] = y.astype(BF16)

    return pl.pallas_call(
        body, name=name, grid=(t // tm,),
        in_specs=[row, vec] + ([row] if has_resid else []), out_specs=row,
        out_shape=jax.ShapeDtypeStruct((t, D_MODEL), F32 if has_resid else BF16),
        compiler_params=_params(1),
    )(*((x, g) + ((resid,) if has_resid else ())))


def _rms_bwd(xin, g, dy, *, name, resid=None):
    t = xin.shape[0]
    tm = _tile(t, ROW_TILE)
    row = pl.BlockSpec((tm, D_MODEL), lambda i: (i, 0))
    vec = pl.BlockSpec((1, D_MODEL), lambda i: (0, 0))
    has_resid = resid is not None
    out_dtype = F32 if has_resid else BF16

    def body(*refs):
        x_ref, g_ref, dy_ref = refs[0], refs[1], refs[2]
        dx_ref, dg_ref = refs[-2], refs[-1]
        xv, dyv = x_ref[...], dy_ref[...]
        r = lax.rsqrt(jnp.mean(xv * xv, axis=-1, keepdims=True) + RMS_EPS)
        a = dyv * g_ref[...]
        dx = r * a - xv * (r * r * r) * jnp.mean(a * xv, axis=-1, keepdims=True)
        if has_resid:
            dx = dx + refs[3][...]
        dx_ref[...] = dx.astype(out_dtype)
        part = jnp.sum(dyv * xv * r, axis=0, keepdims=True)

        @pl.when(pl.program_id(0) == 0)
        def _():
            dg_ref[...] = part

        @pl.when(pl.program_id(0) > 0)
        def _():
            dg_ref[...] += part

    return pl.pallas_call(
        body, name=name, grid=(t // tm,),
        in_specs=[row, vec, row] + ([row] if has_resid else []), out_specs=[row, vec],
        out_shape=[jax.ShapeDtypeStruct((t, D_MODEL), out_dtype), jax.ShapeDtypeStruct((1, D_MODEL), F32)],
        compiler_params=_params(1),
    )(*((xin, g, dy) + ((resid,) if has_resid else ())))


def _loss_head(x, target, *, name):
    t = x.shape[0]
    tm = _tile(t, ROW_TILE)
    row = pl.BlockSpec((tm, D_MODEL), lambda i: (i, 0))
    acc = pl.BlockSpec((8, LANES), lambda i: (0, 0))

    def body(x_ref, t_ref, l_ref, dx_ref):
        diff = x_ref[...] - t_ref[...]
        dx_ref[...] = diff * (1.0 / D_MODEL)
        part = 0.5 * jnp.sum(jnp.mean(diff * diff, axis=-1, keepdims=True), axis=0, keepdims=True)

        @pl.when(pl.program_id(0) == 0)
        def _():
            l_ref[...] = jnp.zeros((8, LANES), F32) + part

        @pl.when(pl.program_id(0) > 0)
        def _():
            l_ref[...] += part

    return pl.pallas_call(
        body, name=name, grid=(t // tm,),
        in_specs=[row, row], out_specs=[acc, row],
        out_shape=[jax.ShapeDtypeStruct((8, LANES), F32), jax.ShapeDtypeStruct((t, D_MODEL), F32)],
        compiler_params=_params(1),
    )(x, target)


def _window_sum(ext, n_doublings, forward):
    rows = ext.shape[0]
    s, sh = ext, 1
    for _ in range(n_doublings):
        s = s + pltpu.roll(s, sh if forward else rows - sh, 0)
        sh *= 2
    return s


def _pool_fwd(pg, pool_w, pool_b, pool_scale, *, name):
    t = pg.shape[0]
    tm = _tile(t, ROW_TILE)

    def body(pg_ref, w_ref, b_ref, s_ref, o_ref, halo):
        i = pl.program_id(0)

        @pl.when(i == 0)
        def _():
            halo[...] = jnp.zeros_like(halo)

        p = pg_ref[:, :BRANCH_WIDTH]
        gate = pg_ref[:, BRANCH_WIDTH:]
        ext = jnp.concatenate([halo[...], p], axis=0)
        pos = i * tm + lax.broadcasted_iota(jnp.int32, (tm, 1), 0)
        outs = []
        for g, w in enumerate(POOL_WINDOWS):
            cols = slice(g * POOL_GROUP_DIM, (g + 1) * POOL_GROUP_DIM)
            cnt = jnp.minimum(pos + 1, w).astype(F32)
            d = _window_sum(ext[:, cols], g + 1, True)[POOL_HALO:] / cnt - p[:, cols]
            y = (_dot(d.astype(BF16), w_ref[g]) + b_ref[:, cols]) * s_ref[:, cols]
            gg = gate[:, cols]
            outs.append(y * (gg * _sigmoid(gg)))
        o_ref[...] = jnp.concatenate(outs, axis=1).astype(BF16)
        halo[...] = p[tm - POOL_HALO:, :]

    vec = pl.BlockSpec((1, BRANCH_WIDTH), lambda i: (0, 0))
    return pl.pallas_call(
        body, name=name, grid=(t // tm,),
        in_specs=[pl.BlockSpec((tm, 2 * BRANCH_WIDTH), lambda i: (i, 0)),
                  pl.BlockSpec((4, POOL_GROUP_DIM, POOL_GROUP_DIM), lambda i: (0, 0, 0)), vec, vec],
        out_specs=pl.BlockSpec((tm, BRANCH_WIDTH), lambda i: (i, 0)),
        out_shape=jax.ShapeDtypeStruct((t, BRANCH_WIDTH), BF16),
        scratch_shapes=[pltpu.VMEM((POOL_HALO, BRANCH_WIDTH), F32)],
        compiler_params=_params(1),
    )(pg, pool_w, pool_b, pool_scale)


def _pool_bwd(pg, d_out, pool_w, pool_b, pool_scale, *, name):
    t = pg.shape[0]
    tm = _tile(t, ROW_TILE)
    nt = t // tm
    halo_per_tile = tm // POOL_HALO

    def body(pg_ref, halo_ref, do_ref, w_ref, b_ref, s_ref, dpg_ref, dw_ref, dvec_ref, carry):
        i = pl.program_id(0)
        ri = nt - 1 - i

        @pl.when(i == 0)
        def _():
            carry[...] = jnp.zeros_like(carry)
            dw_ref[...] = jnp.zeros_like(dw_ref)
            dvec_ref[...] = jnp.zeros_like(dvec_ref)

        p = pg_ref[:, :BRANCH_WIDTH]
        gate = pg_ref[:, BRANCH_WIDTH:]
        hp = jnp.where(ri > 0, halo_ref[:, :BRANCH_WIDTH], 0.0)
        ext = jnp.concatenate([hp, p], axis=0)
        pos = ri * tm + lax.broadcasted_iota(jnp.int32, (tm, 1), 0)
        dps, dgs, dbs, dss = [], [], [], []
        for g, w in enumerate(POOL_WINDOWS):
            cols = slice(g * POOL_GROUP_DIM, (g + 1) * POOL_GROUP_DIM)
            cnt = jnp.minimum(pos + 1, w).astype(F32)
            d = (_window_sum(ext[:, cols], g + 1, True)[POOL_HALO:] / cnt - p[:, cols]).astype(BF16)
            y1 = _dot(d, w_ref[g]) + b_ref[:, cols]
            scale = s_ref[:, cols]
            y2 = y1 * scale
            gg = gate[:, cols]
            sg = _sigmoid(gg)
            do = do_ref[:, cols]
            dy2 = do * (gg * sg)
            dgs.append(do * y2 * _silu_grad(gg, sg))
            dss.append(jnp.sum(dy2 * y1, axis=0, keepdims=True))
            dy1 = dy2 * scale
            dbs.append(jnp.sum(dy1, axis=0, keepdims=True))
            dy1b = dy1.astype(BF16)
            dw_ref[g] += _dot_tn(d, dy1b)
            dd = _dot_nt(dy1b, w_ref[g])
            dpool = dd / cnt
            dext = jnp.concatenate([dpool, carry[:, cols]], axis=0)
            dps.append(_window_sum(dext, g + 1, False)[:tm] - dd)
            carry[:, cols] = dpool[:POOL_HALO]
        dpg_ref[...] = jnp.concatenate(dps + dgs, axis=1).astype(BF16)
        dvec_ref[0:1, :] += jnp.concatenate(dbs, axis=1)
        dvec_ref[1:2, :] += jnp.concatenate(dss, axis=1)

    vec = pl.BlockSpec((1, BRANCH_WIDTH), lambda i: (0, 0))
    wspec = pl.BlockSpec((4, POOL_GROUP_DIM, POOL_GROUP_DIM), lambda i: (0, 0, 0))
    return pl.pallas_call(
        body, name=name, grid=(nt,),
        in_specs=[pl.BlockSpec((tm, 2 * BRANCH_WIDTH), lambda i: (nt - 1 - i, 0)),
                  pl.BlockSpec((POOL_HALO, 2 * BRANCH_WIDTH), lambda i: (jnp.maximum((nt - 1 - i) * halo_per_tile - 1, 0), 0)),
                  pl.BlockSpec((tm, BRANCH_WIDTH), lambda i: (nt - 1 - i, 0)), wspec, vec, vec],
        out_specs=[pl.BlockSpec((tm, 2 * BRANCH_WIDTH), lambda i: (nt - 1 - i, 0)), wspec,
                   pl.BlockSpec((8, BRANCH_WIDTH), lambda i: (0, 0))],
        out_shape=[jax.ShapeDtypeStruct((t, 2 * BRANCH_WIDTH), BF16),
                   jax.ShapeDtypeStruct((4, POOL_GROUP_DIM, POOL_GROUP_DIM), F32),
                   jax.ShapeDtypeStruct((8, BRANCH_WIDTH), F32)],
        scratch_shapes=[pltpu.VMEM((POOL_HALO, BRANCH_WIDTH), F32)],
        compiler_params=_params(1),
    )(pg, pg, d_out, pool_w, pool_b, pool_scale)


def _causal_conv(ext, w_ref, rows):
    acc = None
    for j in range(CONV_KERNEL):
        tap = w_ref[CONV_KERNEL - 1 - j:CONV_KERNEL - j, :]
        shifted = ext if j == 0 else pltpu.roll(ext, j, 0)
        term = tap * shifted[CONV_HALO:]
        acc = term if acc is None else acc + term
    return acc


def _conv_fwd(c2gc, conv_w, conv_b, ln_g, ln_b, *, name):
    t = c2gc.shape[0]
    tm = _tile(t, ROW_TILE)

    def body(c_ref, w_ref, cb_ref, g_ref, b_ref, o_ref, halo):
        @pl.when(pl.program_id(0) == 0)
        def _():
            halo[...] = jnp.zeros_like(halo)

        u = c_ref[:, :BRANCH_WIDTH] * _sigmoid(c_ref[:, BRANCH_WIDTH:2 * BRANCH_WIDTH])
        gate = c_ref[:, 2 * BRANCH_WIDTH:]
        ext = jnp.concatenate([halo[...], u], axis=0)
        cv = _causal_conv(ext, w_ref, tm) + cb_ref[...]
        mu = jnp.mean(cv, axis=-1, keepdims=True)
        xc = cv - mu
        var = jnp.mean(xc * xc, axis=-1, keepdims=True)
        ln = xc * lax.rsqrt(var + LN_EPS) * g_ref[...] + b_ref[...]
        o_ref[...] = (ln * _sigmoid(ln) * (gate * _sigmoid(gate))).astype(BF16)
        halo[...] = u[tm - CONV_HALO:, :]

    vec = pl.BlockSpec((1, BRANCH_WIDTH), lambda i: (0, 0))
    return pl.pallas_call(
        body, name=name, grid=(t // tm,),
        in_specs=[pl.BlockSpec((tm, 3 * BRANCH_WIDTH), lambda i: (i, 0)),
                  pl.BlockSpec((CONV_HALO, BRANCH_WIDTH), lambda i: (0, 0)), vec, vec, vec],
        out_specs=pl.BlockSpec((tm, BRANCH_WIDTH), lambda i: (i, 0)),
        out_shape=jax.ShapeDtypeStruct((t, BRANCH_WIDTH), BF16),
        scratch_shapes=[pltpu.VMEM((CONV_HALO, BRANCH_WIDTH), F32)],
        compiler_params=_params(1),
    )(c2gc, conv_w, conv_b, ln_g, ln_b)


def _conv_bwd(c2gc, d_out, conv_w, conv_b, ln_g, ln_b, *, name):
    t = c2gc.shape[0]
    tm = _tile(t, ROW_TILE)
    nt = t // tm
    halo_per_tile = tm // CONV_HALO
    ext_rows = tm + CONV_HALO

    def body(c_ref, halo_ref, do_ref, w_ref, cb_ref, g_ref, b_ref, dc_ref, dw_ref, dvec_ref, carry):
        i = pl.program_id(0)
        ri = nt - 1 - i

        @pl.when(i == 0)
        def _():
            carry[...] = jnp.zeros_like(carry)
            dw_ref[...] = jnp.zeros_like(dw_ref)
            dvec_ref[...] = jnp.zeros_like(dvec_ref)

        a = c_ref[:, :BRANCH_WIDTH]
        sb = _sigmoid(c_ref[:, BRANCH_WIDTH:2 * BRANCH_WIDTH])
        gate = c_ref[:, 2 * BRANCH_WIDTH:]
        u = a * sb
        hu = halo_ref[:, :BRANCH_WIDTH] * _sigmoid(halo_ref[:, BRANCH_WIDTH:2 * BRANCH_WIDTH])
        ext = jnp.concatenate([jnp.where(ri > 0, hu, 0.0), u], axis=0)
        cv = _causal_conv(ext, w_ref, tm) + cb_ref[...]
        mu = jnp.mean(cv, axis=-1, keepdims=True)
        xc = cv - mu
        rs = lax.rsqrt(jnp.mean(xc * xc, axis=-1, keepdims=True) + LN_EPS)
        n = xc * rs
        ln = n * g_ref[...] + b_ref[...]
        sl = _sigmoid(ln)
        sgate = _sigmoid(gate)
        do = do_ref[...]
        dgate = do * (ln * sl) * _silu_grad(gate, sgate)
        dln = do * (gate * sgate) * _silu_grad(ln, sl)
        dn = dln * g_ref[...]
        dcv = rs * (dn - jnp.mean(dn, axis=-1, keepdims=True) - n * jnp.mean(dn * n, axis=-1, keepdims=True))
        dvec_ref[0:1, :] += jnp.sum(dcv, axis=0, keepdims=True)
        dvec_ref[1:2, :] += jnp.sum(dln * n, axis=0, keepdims=True)
        dvec_ref[2:3, :] += jnp.sum(dln, axis=0, keepdims=True)
        dext = jnp.concatenate([dcv, carry[...]], axis=0)
        du = None
        for j in range(CONV_KERNEL):
            k = CONV_KERNEL - 1 - j
            past = ext if j == 0 else pltpu.roll(ext, j, 0)
            dw_ref[k:k + 1, :] += jnp.sum(dcv * past[CONV_HALO:], axis=0, keepdims=True)
            future = dext if j == 0 else pltpu.roll(dext, ext_rows - j, 0)
            term = w_ref[k:k + 1, :] * future[:tm]
            du = term if du is None else du + term
        dc_ref[...] = jnp.concatenate([du * sb, du * a * sb * (1.0 - sb), dgate], axis=1).astype(BF16)
        carry[...] = dcv[:CONV_HALO]

    vec = pl.BlockSpec((1, BRANCH_WIDTH), lambda i: (0, 0))
    wspec = pl.BlockSpec((CONV_HALO, BRANCH_WIDTH), lambda i: (0, 0))
    return pl.pallas_call(
        body, name=name, grid=(nt,),
        in_specs=[pl.BlockSpec((tm, 3 * BRANCH_WIDTH), lambda i: (nt - 1 - i, 0)),
                  pl.BlockSpec((CONV_HALO, 3 * BRANCH_WIDTH), lambda i: (jnp.maximum((nt - 1 - i) * halo_per_tile - 1, 0), 0)),
                  pl.BlockSpec((tm, BRANCH_WIDTH), lambda i: (nt - 1 - i, 0)), wspec, vec, vec, vec],
        out_specs=[pl.BlockSpec((tm, 3 * BRANCH_WIDTH), lambda i: (nt - 1 - i, 0)), wspec,
                   pl.BlockSpec((8, BRANCH_WIDTH), lambda i: (0, 0))],
        out_shape=[jax.ShapeDtypeStruct((t, 3 * BRANCH_WIDTH), BF16),
                   jax.ShapeDtypeStruct((CONV_HALO, BRANCH_WIDTH), F32),
                   jax.ShapeDtypeStruct((8, BRANCH_WIDTH), F32)],
        scratch_shapes=[pltpu.VMEM((CONV_HALO, BRANCH_WIDTH), F32)],
        compiler_params=_params(1),
    )(c2gc, c2gc, d_out, conv_w, conv_b, ln_g, ln_b)


def _merge_fwd(gmga, ya, yb, yc, *, name):
    t = ya.shape[0]
    tm = _tile(t, ROW_TILE)
    row = pl.BlockSpec((tm, D_MODEL), lambda i: (i, 0))

    def body(g0, g1, g2, a_ref, b_ref, c_ref, o_ref):
        m = _sigmoid(g0[...]) * a_ref[...] + _sigmoid(g1[...]) * b_ref[...] + _sigmoid(g2[...]) * c_ref[...]
        o_ref[...] = m.astype(BF16)

    gspecs = [pl.BlockSpec((tm, D_MODEL), functools.partial(lambda i, b: (i, b), b=b)) for b in range(3)]
    return pl.pallas_call(
        body, name=name, grid=(t // tm,),
        in_specs=gspecs + [row, row, row], out_specs=row,
        out_shape=jax.ShapeDtypeStruct((t, D_MODEL), BF16),
        compiler_params=_params(1),
    )(gmga, gmga, gmga, ya, yb, yc)


def _merge_bwd(dm, gmga, ya, yb, yc, *, name):
    t = ya.shape[0]
    tm = _tile(t, ROW_TILE)
    row = pl.BlockSpec((tm, D_MODEL), lambda i: (i, 0))
    wide = pl.BlockSpec((tm, 3 * D_MODEL), lambda i: (i, 0))

    def body(dm_ref, g0, g1, g2, a_ref, b_ref, c_ref, da_ref, db_ref, dc_ref, dg_ref):
        dmv = dm_ref[...]
        for k, (g_ref, y_ref, dy_ref) in enumerate(((g0, a_ref, da_ref), (g1, b_ref, db_ref), (g2, c_ref, dc_ref))):
            s = _sigmoid(g_ref[...])
            dy_ref[...] = (dmv * s).astype(BF16)
            dg_ref[:, k * D_MODEL:(k + 1) * D_MODEL] = (dmv * y_ref[...] * s * (1.0 - s)).astype(BF16)

    gspecs = [pl.BlockSpec((tm, D_MODEL), functools.partial(lambda i, b: (i, b), b=b)) for b in range(3)]
    return pl.pallas_call(
        body, name=name, grid=(t // tm,),
        in_specs=[row] + gspecs + [row, row, row], out_specs=[row, row, row, wide],
        out_shape=[jax.ShapeDtypeStruct((t, D_MODEL), BF16)] * 3 + [jax.ShapeDtypeStruct((t, 3 * D_MODEL), BF16)],
        compiler_params=_params(1),
    )(dm, gmga, gmga, gmga, ya, yb, yc)


GA_BLOCK = 3 * D_MODEL // HEAD_PAIR


def _split_heads(x, lane_is_first):
    zero = jnp.zeros_like(x)
    return jnp.concatenate([jnp.where(lane_is_first, x, zero), jnp.where(lane_is_first, zero, x)], axis=0)


def _side_by_side(x, rows):
    return jnp.concatenate([x[:rows], x[rows:]], axis=1)


def _split_bf16(x):
    hi = x.astype(BF16)
    return hi, (x - hi.astype(F32)).astype(BF16)


def _scores(qcat, kblk, mask):
    z = _dot_nt(qcat, kblk)
    e = jnp.exp(-jnp.abs(z))
    sp = jnp.maximum(z, 0.0) + jnp.log(1.0 + e)
    l1m = -sp
    if mask is not None:
        l1m = jnp.where(mask, l1m, 0.0)
    inv = 1.0 / (1.0 + e)
    pos = z >= 0.0
    return z - sp, l1m, jnp.where(pos, 1.0, e) * inv, jnp.where(pos, e, 1.0) * inv


def _attn_consts(blk):
    lane_is_first = lax.broadcasted_iota(jnp.int32, (1, HEAD_PAIR), 1) < HEAD_DIM
    r = lax.broadcasted_iota(jnp.int32, (blk, blk), 0)
    c = lax.broadcasted_iota(jnp.int32, (blk, blk), 1)
    after = (r > c).astype(BF16)
    from_here = (r >= c).astype(BF16)
    qrow = lax.broadcasted_iota(jnp.int32, (2 * blk, blk), 0)
    qrow = jnp.where(qrow >= blk, qrow - blk, qrow)
    causal = lax.broadcasted_iota(jnp.int32, (2 * blk, blk), 1) < qrow
    return lane_is_first, after, from_here, causal


def _attn_fwd(qkv, gmga, *, name):
    t = qkv.shape[0]
    blk = _tile(t, ATTN_BLOCK)
    nq = t // blk

    def body(q_ref, k_ref, v_ref, ga_ref, o_ref, cv_ref):
        qi = pl.program_id(1)
        lane_is_first, after, _, causal = _attn_consts(blk)
        qcat = _split_heads(q_ref[...], lane_is_first)

        def block(kb, carry, mask):
            run, acc = carry
            rows = pl.ds(pl.multiple_of(kb * blk, blk), blk)
            lb, l1m, _, _ = _scores(qcat, k_ref[rows, :], mask)
            hi, lo = _split_bf16(l1m)
            w = jnp.exp(lb + (_dot(hi, after) + _dot(lo, after) + run))
            if mask is not None:
                w = jnp.where(mask, w, 0.0)
            vcat = _split_heads(v_ref[rows, :], lane_is_first)
            acc = acc + _dot(_side_by_side(w.astype(BF16), blk), vcat)
            return run + jnp.sum(l1m, axis=-1, keepdims=True), acc

        carry = block(qi, (jnp.zeros((2 * blk, 1), F32), jnp.zeros((blk, HEAD_PAIR), F32)), causal)
        _, o = lax.fori_loop(0, qi, lambda it, c: block(qi - 1 - it, c, None), carry)
        o_ref[...] = o
        ga = ga_ref[...]
        cv_ref[...] = (o * (ga * _sigmoid(ga))).astype(BF16)

    qspec = pl.BlockSpec((blk, HEAD_PAIR), lambda p, i: (i, p))
    return pl.pallas_call(
        body, name=name, grid=(N_HEAD_PAIRS, nq),
        in_specs=[qspec,
                  pl.BlockSpec((t, HEAD_PAIR), lambda p, i: (0, N_HEAD_PAIRS + p)),
                  pl.BlockSpec((t, HEAD_PAIR), lambda p, i: (0, 2 * N_HEAD_PAIRS + p)),
                  pl.BlockSpec((blk, HEAD_PAIR), lambda p, i: (i, GA_BLOCK + p))],
        out_specs=[qspec, qspec],
        out_shape=[jax.ShapeDtypeStruct((t, BRANCH_WIDTH), F32), jax.ShapeDtypeStruct((t, BRANCH_WIDTH), BF16)],
        compiler_params=_params(2),
    )(qkv, qkv, qkv, gmga)


def _attn_bwd(qkv, o, gmga, dcv, *, name):
    t = qkv.shape[0]
    blk = _tile(t, ATTN_BLOCK)
    nq = t // blk

    def body(q_ref, k_ref, v_ref, o_ref, ga_ref, dcv_ref, dq_ref, dk_ref, dv_ref, dga_ref, dk_acc, dv_acc):
        qi = pl.program_id(1)
        lane_is_first, after, from_here, causal = _attn_consts(blk)

        @pl.when(qi == 0)
        def _():
            dk_acc[...] = jnp.zeros_like(dk_acc)
            dv_acc[...] = jnp.zeros_like(dv_acc)

        ga, ov, dcvv = ga_ref[...], o_ref[...], dcv_ref[...]
        sg = _sigmoid(ga)
        dob = (dcvv * (ga * sg)).astype(BF16)
        dga_ref[...] = (dcvv * ov * _silu_grad(ga, sg)).astype(BF16)
        gt = dob.astype(F32) * ov
        g_total = jnp.concatenate(
            [jnp.sum(jnp.where(lane_is_first, gt, 0.0), axis=-1, keepdims=True),
             jnp.sum(jnp.where(lane_is_first, 0.0, gt), axis=-1, keepdims=True)], axis=0)
        qcat = _split_heads(q_ref[...], lane_is_first)
        docat = _split_heads(dob, lane_is_first)

        def block(kb, carry, mask):
            run, g_run, dq = carry
            rows = pl.ds(pl.multiple_of(kb * blk, blk), blk)
            kblk = k_ref[rows, :]
            lb, l1m, sig, one_m_sig = _scores(qcat, kblk, mask)
            hi, lo = _split_bf16(l1m)
            w = jnp.exp(lb + (_dot(hi, after) + _dot(lo, after) + run))
            if mask is not None:
                w = jnp.where(mask, w, 0.0)
            wb = w.astype(BF16)
            g = _dot_nt(docat, v_ref[rows, :]) * wb.astype(F32)
            ghi, glo = _split_bf16(g)
            g_before = g_total - g_run - (_dot(ghi, from_here) + _dot(glo, from_here))
            dz = g * one_m_sig - g_before * sig
            if mask is not None:
                dz = jnp.where(mask, dz, 0.0)
            dzb = dz.astype(BF16)
            dq = dq + _dot(_side_by_side(dzb, blk), _split_heads(kblk, lane_is_first))
            dk_acc[rows, :] += _dot_tn(dzb, qcat)
            dv_acc[rows, :] += _dot_tn(wb, docat)
            return (run + jnp.sum(l1m, axis=-1, keepdims=True), g_run + jnp.sum(g, axis=-1, keepdims=True), dq)

        zero = jnp.zeros((2 * blk, 1), F32)
        carry = block(qi, (zero, zero, jnp.zeros((blk, HEAD_PAIR), F32)), causal)
        _, _, dq = lax.fori_loop(0, qi, lambda it, c: block(qi - 1 - it, c, None), carry)
        dq_ref[...] = (dq * ATTN_SCALE).astype(BF16)

        @pl.when(qi == nq - 1)
        def _():
            dk_ref[...] = dk_acc[...].astype(BF16)
            dv_ref[...] = dv_acc[...].astype(BF16)

    qspec = pl.BlockSpec((blk, HEAD_PAIR), lambda p, i: (i, p))
    whole = pl.BlockSpec((t, HEAD_PAIR), lambda p, i: (0, p))
    out = jax.ShapeDtypeStruct((t, BRANCH_WIDTH), BF16)
    return pl.pallas_call(
        body, name=name, grid=(N_HEAD_PAIRS, nq),
        in_specs=[qspec,
                  pl.BlockSpec((t, HEAD_PAIR), lambda p, i: (0, N_HEAD_PAIRS + p)),
                  pl.BlockSpec((t, HEAD_PAIR), lambda p, i: (0, 2 * N_HEAD_PAIRS + p)),
                  qspec,
                  pl.BlockSpec((blk, HEAD_PAIR), lambda p, i: (i, GA_BLOCK + p)),
                  qspec],
        out_specs=[qspec, whole, whole, qspec],
        out_shape=[out, out, out, out],
        scratch_shapes=[pltpu.VMEM((t, HEAD_PAIR), F32), pltpu.VMEM((t, HEAD_PAIR), F32)],
        compiler_params=_params(2),
    )(qkv, qkv, qkv, o, gmga, dcv)


def _mesh_position():
    x, y, c = lax.axis_index("x"), lax.axis_index("y"), lax.axis_index("c")
    return x, y, c, 4 * x + 2 * y + c


def _flipped(x, y, c, k):
    return (1 - x if k & 4 else x, 1 - y if k & 2 else y, 1 - c if k & 1 else c)


def _all_to_all(src, *, name, same_block):
    rows = src.shape[-2]

    def body(src_ref, dst_ref, send_sems, recv_sems, local_sem):
        x, y, c, me = _mesh_position()

        def outgoing(j):
            return src_ref if same_block else src_ref.at[j]

        def copy(k):
            peer = jnp.bitwise_xor(me, k)
            return pltpu.make_async_remote_copy(
                src_ref=outgoing(peer), dst_ref=dst_ref.at[me],
                send_sem=send_sems.at[k - 1], recv_sem=recv_sems.at[k - 1],
                device_id=_flipped(x, y, c, k), device_id_type=MESH)

        def arrival(k):
            peer = jnp.bitwise_xor(me, k)
            return pltpu.make_async_remote_copy(
                src_ref=outgoing(peer), dst_ref=dst_ref.at[peer],
                send_sem=send_sems.at[k - 1], recv_sem=recv_sems.at[k - 1],
                device_id=_flipped(x, y, c, k), device_id_type=MESH)

        mine = pltpu.make_async_copy(outgoing(me), dst_ref.at[me], local_sem)
        mine.start()
        sent = [copy(k) for k in range(1, N_DEV)]
        for cp in sent:
            cp.start()
        for k in range(1, N_DEV):
            arrival(k).wait_recv()
        for cp in sent:
            cp.wait_send()
        mine.wait()

    return pl.pallas_call(
        body, name=name,
        in_specs=[pl.BlockSpec(memory_space=pl.ANY)], out_specs=pl.BlockSpec(memory_space=pl.ANY),
        out_shape=jax.ShapeDtypeStruct((N_DEV, rows, LANES), src.dtype),
        scratch_shapes=[pltpu.SemaphoreType.DMA((N_DEV - 1,)), pltpu.SemaphoreType.DMA((N_DEV - 1,)),
                        pltpu.SemaphoreType.DMA(())],
    )(src)


def _adamw(parts, w, m, v, *, name):
    rows = w.shape[0]
    tr = _tile(rows, ADAM_ROW_TILE)
    row = pl.BlockSpec((tr, LANES), lambda i: (i, 0))

    def body(p_ref, w_ref, m_ref, v_ref, g_ref, d_ref, nm_ref, nv_ref):
        g = p_ref[0]
        for k in range(1, N_DEV):
            g = g + p_ref[k]
        m2 = ADAM_B1 * m_ref[...] + (1.0 - ADAM_B1) * g
        v2 = ADAM_B2 * v_ref[...] + (1.0 - ADAM_B2) * (g * g)
        m_hat = m2 / (1.0 - ADAM_B1 ** ADAM_STEP)
        v_hat = v2 / (1.0 - ADAM_B2 ** ADAM_STEP)
        g_ref[...] = g
        d_ref[...] = -ADAM_LR * (m_hat / (jnp.sqrt(v_hat) + ADAM_EPS) + ADAM_WD * w_ref[...])
        nm_ref[...] = m2
        nv_ref[...] = v2

    out = jax.ShapeDtypeStruct((rows, LANES), F32)
    return pl.pallas_call(
        body, name=name, grid=(rows // tr,),
        in_specs=[pl.BlockSpec((N_DEV, tr, LANES), lambda i: (0, i, 0)), row, row, row],
        out_specs=[row, row, row, row], out_shape=[out, out, out, out],
        compiler_params=_params(1),
    )(parts, w, m, v)


SHARDED = ("w_in", "w_pool_out", "w_conv_out", "w_attn_out", "w_o", "conv_w")
REPLICATED = ("norm_pre", "pool_w", "pool_b", "pool_scale", "conv_b", "conv_ln_g", "conv_ln_b", "norm_post")
WEIGHT_ORDER = ("norm_pre", "w_in", "pool_w", "pool_b", "pool_scale", "w_pool_out", "conv_w", "conv_b",
                "conv_ln_g", "conv_ln_b", "w_conv_out", "w_attn_out", "w_o", "norm_post")


def _pack_rows(flat_parts, row_multiple):
    flat = jnp.concatenate(flat_parts, axis=-1)
    n = flat.shape[-1]
    chunk = row_multiple * LANES
    total = -(-n // chunk) * chunk
    pad = [(0, 0)] * (flat.ndim - 1) + [(0, total - n)]
    return jnp.pad(flat, pad).reshape(flat.shape[:-1] + (total // LANES, LANES))


def _unpack(buf, shapes):
    flat = buf.reshape(-1)
    out, at = {}, 0
    for name, shape in shapes:
        n = 1
        for s in shape:
            n *= s
        out[name] = flat[at:at + n].reshape(shape)
        at += n
    return out


def _to_dest_major(name, full):
    l = full.shape[0]
    if name == "w_o":
        return full.reshape(l, N_DEV, D_MODEL // N_DEV, D_MODEL).transpose(1, 0, 2, 3).reshape(N_DEV, -1)
    rows, cols = full.shape[1], full.shape[2]
    return full.reshape(l, rows, N_DEV, cols // N_DEV).transpose(2, 0, 1, 3).reshape(N_DEV, -1)


def _from_source_major(name, gathered, shard_shape):
    l = shard_shape[0]
    g = gathered.reshape((N_DEV,) + tuple(shard_shape))
    if name == "w_o":
        return g.transpose(1, 0, 2, 3).reshape(l, D_MODEL, D_MODEL)
    return g.transpose(1, 2, 0, 3).reshape(l, shard_shape[1], N_DEV * shard_shape[2])


def kernel(x, norm_pre, w_in, pool_w, pool_b, pool_scale, w_pool_out, conv_w, conv_b, conv_ln_g, conv_ln_b, w_conv_out, w_attn_out, w_o, norm_post, loss_target, m_norm_pre, m_w_in, m_pool_w, m_pool_b, m_pool_scale, m_w_pool_out, m_conv_w, m_conv_b, m_conv_ln_g, m_conv_ln_b, m_w_conv_out, m_w_attn_out, m_w_o, m_norm_post, v_norm_pre, v_w_in, v_pool_w, v_pool_b, v_pool_scale, v_w_pool_out, v_conv_w, v_conv_b, v_conv_ln_g, v_conv_ln_b, v_w_conv_out, v_w_attn_out, v_w_o, v_norm_post):
    weights = dict(norm_pre=norm_pre, w_in=w_in, pool_w=pool_w, pool_b=pool_b, pool_scale=pool_scale,
                   w_pool_out=w_pool_out, conv_w=conv_w, conv_b=conv_b, conv_ln_g=conv_ln_g, conv_ln_b=conv_ln_b,
                   w_conv_out=w_conv_out, w_attn_out=w_attn_out, w_o=w_o, norm_post=norm_post)
    mom1 = dict(norm_pre=m_norm_pre, w_in=m_w_in, pool_w=m_pool_w, pool_b=m_pool_b, pool_scale=m_pool_scale,
                w_pool_out=m_w_pool_out, conv_w=m_conv_w, conv_b=m_conv_b, conv_ln_g=m_conv_ln_g, conv_ln_b=m_conv_ln_b,
                w_conv_out=m_w_conv_out, w_attn_out=m_w_attn_out, w_o=m_w_o, norm_post=m_norm_post)
    mom2 = dict(norm_pre=v_norm_pre, w_in=v_w_in, pool_w=v_pool_w, pool_b=v_pool_b, pool_scale=v_pool_scale,
                w_pool_out=v_w_pool_out, conv_w=v_conv_w, conv_b=v_conv_b, conv_ln_g=v_conv_ln_g, conv_ln_b=v_conv_ln_b,
                w_conv_out=v_w_conv_out, w_attn_out=v_w_attn_out, w_o=v_w_o, norm_post=v_norm_post)
    xs = x[0]
    target = loss_target[0]

    mm_names = ("w_in", "w_pool_out", "w_conv_out", "w_attn_out", "w_o")
    send = [weights[n].astype(BF16).reshape(-1) for n in mm_names]
    conv_bits = lax.bitcast_convert_type(conv_w.reshape(-1), BF16).reshape(-1)
    gathered = _all_to_all(_pack_rows(send + [conv_bits], 16), name="gather_weights", same_block=True)
    gathered = gathered.reshape(N_DEV, -1)
    full, at = {}, 0
    for n in mm_names:
        size = weights[n].size
        full[n] = _from_source_major(n, gathered[:, at:at + size], weights[n].shape)
        at += size
    conv_flat = lax.bitcast_convert_type(gathered[:, at:at + 2 * conv_w.size].reshape(N_DEV, conv_w.size, 2), F32)
    conv_full = _from_source_major("conv_w", conv_flat, conv_w.shape)
    conv_full = jnp.pad(conv_full, ((0, 0), (0, CONV_HALO - CONV_KERNEL), (0, 0)))

    def in_sections(w):
        return dict(pg=w[:, 0:1024], c2gc=w[:, 1024:2560], q=w[:, 2560:3072], k=w[:, 3072:3584], v=w[:, 3584:4096],
                    gmga=jnp.concatenate([w[:, 4608:7680], w[:, 4096:4608]], axis=1))

    saved = []
    cur = xs
    for l in range(DEPTH):
        sec = in_sections(full["w_in"][l])
        w_qkv = jnp.concatenate([sec["q"] * ATTN_SCALE, sec["k"], sec["v"]], axis=1)
        pw = pool_w[l].astype(BF16)
        pb, ps = pool_b[l].reshape(1, -1), pool_scale[l].reshape(1, -1)
        cb, lg, lb = conv_b[l].reshape(1, -1), conv_ln_g[l].reshape(1, -1), conv_ln_b[l].reshape(1, -1)
        h = _rms_fwd(cur, norm_pre[l].reshape(1, -1), name=f"rms_pre_fwd_{l}")
        pg = _matmul(h, sec["pg"], mode="nn", name=f"proj_pg_{l}")
        c2gc = _matmul(h, sec["c2gc"], mode="nn", name=f"proj_c2gc_{l}")
        qkv = _matmul(h, w_qkv, mode="nn", name=f"proj_qkv_{l}", out_dtype=BF16)
        gmga = _matmul(h, sec["gmga"], mode="nn", name=f"proj_gmga_{l}")
        a_act = _pool_fwd(pg, pw, pb, ps, name=f"pool_fwd_{l}")
        b_act = _conv_fwd(c2gc, conv_full[l], cb, lg, lb, name=f"conv_fwd_{l}")
        o, c_act = _attn_fwd(qkv, gmga, name=f"attn_fwd_{l}")
        ya = _matmul(a_act, full["w_pool_out"][l], mode="nn", name=f"out_pool_{l}")
        yb = _matmul(b_act, full["w_conv_out"][l], mode="nn", name=f"out_conv_{l}")
        yc = _matmul(c_act, full["w_attn_out"][l], mode="nn", name=f"out_attn_{l}")
        mix = _merge_fwd(gmga, ya, yb, yc, name=f"merge_fwd_{l}")
        out = _matmul(mix, full["w_o"][l], mode="nn", name=f"out_proj_{l}")
        nxt = _rms_fwd(out, norm_post[l].reshape(1, -1), name=f"rms_post_fwd_{l}", resid=cur)
        saved.append(dict(x=cur, h=h, pg=pg, c2gc=c2gc, qkv=qkv, gmga=gmga, a=a_act, b=b_act, c=c_act, o=o,
                          ya=ya, yb=yb, yc=yc, mix=mix, out=out, sec=sec, pw=pw, pb=pb, ps=ps, cb=cb, lg=lg, lb=lb))
        cur = nxt

    loss_tile, dx = _loss_head(cur, target, name="loss_head")
    loss = lax.psum(loss_tile[0, 0], ("x", "y", "c"))

    grads = {n: [None] * DEPTH for n in WEIGHT_ORDER}
    for l in reversed(range(DEPTH)):
        s = saved[l]
        dout, grads["norm_post"][l] = _rms_bwd(s["out"], norm_post[l].reshape(1, -1), dx, name=f"rms_post_bwd_{l}")
        dmix = _matmul(dout, full["w_o"][l], mode="nt", name=f"d_mix_{l}")
        grads["w_o"][l] = _matmul(s["mix"], dout, mode="tn", name=f"d_w_o_{l}")
        dya, dyb, dyc, dgm = _merge_bwd(dmix, s["gmga"], s["ya"], s["yb"], s["yc"], name=f"merge_bwd_{l}")
        da = _matmul(dya, full["w_pool_out"][l], mode="nt", name=f"d_pool_act_{l}")
        grads["w_pool_out"][l] = _matmul(s["a"], dya, mode="tn", name=f"d_w_pool_out_{l}")
        db = _matmul(dyb, full["w_conv_out"][l], mode="nt", name=f"d_conv_act_{l}")
        grads["w_conv_out"][l] = _matmul(s["b"], dyb, mode="tn", name=f"d_w_conv_out_{l}")
        dc = _matmul(dyc, full["w_attn_out"][l], mode="nt", name=f"d_attn_act_{l}")
        grads["w_attn_out"][l] = _matmul(s["c"], dyc, mode="tn", name=f"d_w_attn_out_{l}")
        dq, dk, dv, dga = _attn_bwd(s["qkv"], s["o"], s["gmga"], dc, name=f"attn_bwd_{l}")
        dc2gc, dcw, dcvec = _conv_bwd(s["c2gc"], db, conv_full[l], s["cb"], s["lg"], s["lb"], name=f"conv_bwd_{l}")
        dpg, dpw, dpvec = _pool_bwd(s["pg"], da, s["pw"], s["pb"], s["ps"], name=f"pool_bwd_{l}")
        grads["conv_w"][l] = dcw[:CONV_KERNEL]
        grads["conv_b"][l], grads["conv_ln_g"][l], grads["conv_ln_b"][l] = dcvec[0], dcvec[1], dcvec[2]
        grads["pool_w"][l] = dpw
        grads["pool_b"][l] = dpvec[0].reshape(4, POOL_GROUP_DIM)
        grads["pool_scale"][l] = dpvec[1]
        dsec = dict(pg=dpg, c2gc=dc2gc, q=dq, k=dk, v=dv, gmga=jnp.concatenate([dgm, dga], axis=1))
        dh, dws = None, {}
        for n in ("pg", "c2gc", "q", "k", "v", "gmga"):
            dh = _matmul(dsec[n], s["sec"][n], mode="nt", name=f"d_h_{n}_{l}", acc=dh)
            dws[n] = _matmul(s["h"], dsec[n], mode="tn", name=f"d_w_in_{n}_{l}")
        grads["w_in"][l] = jnp.concatenate(
            [dws["pg"], dws["c2gc"], dws["q"], dws["k"], dws["v"], dws["gmga"][:, 3 * D_MODEL:], dws["gmga"][:, :3 * D_MODEL]],
            axis=1)
        dx, dg_pre = _rms_bwd(s["x"], norm_pre[l].reshape(1, -1), dh, name=f"rms_pre_bwd_{l}", resid=dx)
        grads["norm_pre"][l] = dg_pre.reshape(-1)
        grads["norm_post"][l] = grads["norm_post"][l].reshape(-1)
    grads = {n: jnp.stack(g) for n, g in grads.items()}

    replicated = jnp.concatenate([grads[n].reshape(-1) for n in REPLICATED])
    send = [_to_dest_major(n, grads[n]) for n in SHARDED] + [jnp.broadcast_to(replicated, (N_DEV, replicated.size))]
    parts = _all_to_all(_pack_rows(send, ADAM_ROW_TILE), name="exchange_grads", same_block=False)
    shapes = [(n, weights[n].shape) for n in SHARDED + REPLICATED]

    def packed(tree):
        return _pack_rows([tree[n].reshape(-1) for n in SHARDED + REPLICATED], ADAM_ROW_TILE)

    results = _adamw(parts, packed(weights), packed(mom1), packed(mom2), name="adamw")
    g_out, delta, new_m, new_v = (_unpack(r, shapes) for r in results)
    return (loss, dx[None], *[g_out[n] for n in WEIGHT_ORDER], *[delta[n] for n in WEIGHT_ORDER],
            *[new_m[n] for n in WEIGHT_ORDER], *[new_v[n] for n in WEIGHT_ORDER])
```

```python
import functools

import jax
import jax.numpy as jnp
from jax import lax
from jax.experimental import pallas as pl
from jax.experimental.pallas import tpu as pltpu

F32 = jnp.float32
BF16 = jnp.bfloat16

D_MODEL = 1024
DEPTH = 2
POOL_WINDOWS = (2, 4, 8, 16)
POOL_GROUP_DIM = 128
BRANCH_WIDTH = 512
CONV_KERNEL = 31
CONV_HALO = 32
POOL_HALO = 16
HEAD_DIM = 64
HEAD_PAIR = 128
N_HEAD_PAIRS = 4
ATTN_SCALE = 0.125
LOG_F32_ZERO = -104.0
RMS_EPS = 1e-6
LN_EPS = 1e-5
N_DEV = 8
LANES = 128

ADAM_LR = 0.001
ADAM_B1 = 0.9
ADAM_B2 = 0.999
ADAM_EPS = 1e-08
ADAM_WD = 0.01
ADAM_STEP = 10

ROW_TILE = 256
ATTN_BLOCK = 256
MM_TILE = 1024
ADAM_ROW_TILE = 1024
VMEM_LIMIT = 48 * 1024 * 1024

MESH = pl.DeviceIdType.MESH


def _params(n_axes):
    return pltpu.CompilerParams(dimension_semantics=("arbitrary",) * n_axes, vmem_limit_bytes=VMEM_LIMIT)


def _tile(n, pref):
    if n <= pref:
        return n
    t = (pref // LANES) * LANES
    while n % t:
        t -= LANES
    return t


def _dot(a, b):
    return jnp.dot(a, b, preferred_element_type=F32)


def _dot_nt(a, b):
    return lax.dot_general(a, b, (((1,), (1,)), ((), ())), preferred_element_type=F32)


def _dot_tn(a, b):
    return lax.dot_general(a, b, (((0,), (0,)), ((), ())), preferred_element_type=F32)


def _sigmoid(x):
    return 1.0 / (1.0 + jnp.exp(-x))


def _silu_grad(x, s):
    return s * (1.0 + x * (1.0 - s))


def _matmul(a, b, *, mode, name, out_dtype=F32, acc=None):
    if mode == "nn":
        (m, k), n = a.shape, b.shape[1]
    elif mode == "nt":
        (m, k), n = a.shape, b.shape[0]
    else:
        (k, m), n = a.shape, b.shape[1]
    tm, tn, tk = _tile(m, MM_TILE), _tile(n, MM_TILE), _tile(k, MM_TILE)
    nk = k // tk
    dot = {"nn": _dot, "nt": _dot_nt, "tn": _dot_tn}[mode]
    a_spec = pl.BlockSpec((tk, tm), lambda i, j, kk: (kk, i)) if mode == "tn" else pl.BlockSpec((tm, tk), lambda i, j, kk: (i, kk))
    b_spec = pl.BlockSpec((tn, tk), lambda i, j, kk: (j, kk)) if mode == "nt" else pl.BlockSpec((tk, tn), lambda i, j, kk: (kk, j))
    o_spec = pl.BlockSpec((tm, tn), lambda i, j, kk: (i, j))
    has_acc = acc is not None

    def body(*refs):
        a_ref, b_ref = refs[0], refs[1]
        acc_in = refs[2] if has_acc else None
        o_ref = refs[3] if has_acc else refs[2]
        part = dot(a_ref[...], b_ref[...])
        if nk == 1:
            if has_acc:
                part = part + acc_in[...]
            o_ref[...] = part.astype(out_dtype)
            return
        scr = refs[-1]
        kk = pl.program_id(2)

        @pl.when(kk == 0)
        def _():
            scr[...] = part + acc_in[...] if has_acc else part

        @pl.when(kk > 0)
        def _():
            scr[...] += part

        @pl.when(kk == nk - 1)
        def _():
            o_ref[...] = scr[...].astype(out_dtype)

    in_specs = [a_spec, b_spec] + ([o_spec] if has_acc else [])
    args = (a, b) + ((acc,) if has_acc else ())
    return pl.pallas_call(
        body, name=name, grid=(m // tm, n // tn, nk),
        in_specs=in_specs, out_specs=o_spec,
        out_shape=jax.ShapeDtypeStruct((m, n), out_dtype),
        scratch_shapes=[pltpu.VMEM((tm, tn), F32)] if nk > 1 else [],
        compiler_params=_params(3),
    )(*args)


def _rms_fwd(x, g, *, name, resid=None):
    t = x.shape[0]
    tm = _tile(t, ROW_TILE)
    row = pl.BlockSpec((tm, D_MODEL), lambda i: (i, 0))
    vec = pl.BlockSpec((1, D_MODEL), lambda i: (0, 0))
    has_resid = resid is not None

    def body(*refs):
        x_ref, g_ref = refs[0], refs[1]
        o_ref = refs[-1]
        xv = x_ref[...]
        y = xv * lax.rsqrt(jnp.mean(xv * xv, axis=-1, keepdims=True) + RMS_EPS) * g_ref[...]
        if has_resid:
            o_ref[...] = refs[2][...] + y
        else:
            o_ref[...] = y.astype(BF16)

    return pl.pallas_call(
        body, name=name, grid=(t // tm,),
        in_specs=[row, vec] + ([row] if has_resid else []), out_specs=row,
        out_shape=jax.ShapeDtypeStruct((t, D_MODEL), F32 if has_resid else BF16),
        compiler_params=_params(1),
    )(*((x, g) + ((resid,) if has_resid else ())))


def _rms_bwd(xin, g, dy, *, name, resid=None):
    t = xin.shape[0]
    tm = _tile(t, ROW_TILE)
    row = pl.BlockSpec((tm, D_MODEL), lambda i: (i, 0))
    vec = pl.BlockSpec((1, D_MODEL), lambda i: (0, 0))
    has_resid = resid is not None
    out_dtype = F32 if has_resid else BF16

    def body(*refs):
        x_ref, g_ref, dy_ref = refs[0], refs[1], refs[2]
        dx_ref, dg_ref = refs[-2], refs[-1]
        xv, dyv = x_ref[...], dy_ref[...]
        r = lax.rsqrt(jnp.mean(xv * xv, axis=-1, keepdims=True) + RMS_EPS)
        a = dyv * g_ref[...]
        dx = r * a - xv * (r * r * r) * jnp.mean(a * xv, axis=-1, keepdims=True)
        if has_resid:
            dx = dx + refs[3][...]
        dx_ref[...] = dx.astype(out_dtype)
        part = jnp.sum(dyv * xv * r, axis=0, keepdims=True)

        @pl.when(pl.program_id(0) == 0)
        def _():
            dg_ref[...] = part

        @pl.when(pl.program_id(0) > 0)
        def _():
            dg_ref[...] += part

    return pl.pallas_call(
        body, name=name, grid=(t // tm,),
        in_specs=[row, vec, row] + ([row] if has_resid else []), out_specs=[row, vec],
        out_shape=[jax.ShapeDtypeStruct((t, D_MODEL), out_dtype), jax.ShapeDtypeStruct((1, D_MODEL), F32)],
        compiler_params=_params(1),
    )(*((xin, g, dy) + ((resid,) if has_resid else ())))


def _loss_head(x, target, *, name):
    t = x.shape[0]
    tm = _tile(t, ROW_TILE)
    row = pl.BlockSpec((tm, D_MODEL), lambda i: (i, 0))
    acc = pl.BlockSpec((8, LANES), lambda i: (0, 0))

    def body(x_ref, t_ref, l_ref, dx_ref):
        diff = x_ref[...] - t_ref[...]
        dx_ref[...] = diff * (1.0 / D_MODEL)
        part = 0.5 * jnp.sum(jnp.mean(diff * diff, axis=-1, keepdims=True), axis=0, keepdims=True)

        @pl.when(pl.program_id(0) == 0)
        def _():
            l_ref[...] = jnp.zeros((8, LANES), F32) + part

        @pl.when(pl.program_id(0) > 0)
        def _():
            l_ref[...] += part

    return pl.pallas_call(
        body, name=name, grid=(t // tm,),
        in_specs=[row, row], out_specs=[acc, row],
        out_shape=[jax.ShapeDtypeStruct((8, LANES), F32), jax.ShapeDtypeStruct((t, D_MODEL), F32)],
        compiler_params=_params(1),
    )(x, target)


def _window_sum(ext, n_doublings, forward):
    rows = ext.shape[0]
    s, sh = ext, 1
    for _ in range(n_doublings):
        s = s + pltpu.roll(s, sh if forward else rows - sh, 0)
        sh *= 2
    return s


def _pool_fwd(pg, pool_w, pool_b, pool_scale, *, name):
    t = pg.shape[0]
    tm = _tile(t, ROW_TILE)

    def body(pg_ref, w_ref, b_ref, s_ref, o_ref, halo):
        i = pl.program_id(0)

        @pl.when(i == 0)
        def _():
            halo[...] = jnp.zeros_like(halo)

        p = pg_ref[:, :BRANCH_WIDTH]
        gate = pg_ref[:, BRANCH_WIDTH:]
        ext = jnp.concatenate([halo[...], p], axis=0)
        pos = i * tm + lax.broadcasted_iota(jnp.int32, (tm, 1), 0)
        outs = []
        for g, w in enumerate(POOL_WINDOWS):
            cols = slice(g * POOL_GROUP_DIM, (g + 1) * POOL_GROUP_DIM)
            cnt = jnp.minimum(pos + 1, w).astype(F32)
            d = _window_sum(ext[:, cols], g + 1, True)[POOL_HALO:] / cnt - p[:, cols]
            y = (_dot(d.astype(BF16), w_ref[g]) + b_ref[:, cols]) * s_ref[:, cols]
            gg = gate[:, cols]
            outs.append(y * (gg * _sigmoid(gg)))
        o_ref[...] = jnp.concatenate(outs, axis=1).astype(BF16)
        halo[...] = p[tm - POOL_HALO:, :]

    vec = pl.BlockSpec((1, BRANCH_WIDTH), lambda i: (0, 0))
    return pl.pallas_call(
        body, name=name, grid=(t // tm,),
        in_specs=[pl.BlockSpec((tm, 2 * BRANCH_WIDTH), lambda i: (i, 0)),
                  pl.BlockSpec((4, POOL_GROUP_DIM, POOL_GROUP_DIM), lambda i: (0, 0, 0)), vec, vec],
        out_specs=pl.BlockSpec((tm, BRANCH_WIDTH), lambda i: (i, 0)),
        out_shape=jax.ShapeDtypeStruct((t, BRANCH_WIDTH), BF16),
        scratch_shapes=[pltpu.VMEM((POOL_HALO, BRANCH_WIDTH), F32)],
        compiler_params=_params(1),
    )(pg, pool_w, pool_b, pool_scale)


def _pool_bwd(pg, d_out, pool_w, pool_b, pool_scale, *, name):
    t = pg.shape[0]
    tm = _tile(t, ROW_TILE)
    nt = t // tm
    halo_per_tile = tm // POOL_HALO

    def body(pg_ref, halo_ref, do_ref, w_ref, b_ref, s_ref, dpg_ref, dw_ref, dvec_ref, carry):
        i = pl.program_id(0)
        ri = nt - 1 - i

        @pl.when(i == 0)
        def _():
            carry[...] = jnp.zeros_like(carry)
            dw_ref[...] = jnp.zeros_like(dw_ref)
            dvec_ref[...] = jnp.zeros_like(dvec_ref)

        p = pg_ref[:, :BRANCH_WIDTH]
        gate = pg_ref[:, BRANCH_WIDTH:]
        hp = jnp.where(ri > 0, halo_ref[:, :BRANCH_WIDTH], 0.0)
        ext = jnp.concatenate([hp, p], axis=0)
        pos = ri * tm + lax.broadcasted_iota(jnp.int32, (tm, 1), 0)
        dps, dgs, dbs, dss = [], [], [], []
        for g, w in enumerate(POOL_WINDOWS):
            cols = slice(g * POOL_GROUP_DIM, (g + 1) * POOL_GROUP_DIM)
            cnt = jnp.minimum(pos + 1, w).astype(F32)
            d = (_window_sum(ext[:, cols], g + 1, True)[POOL_HALO:] / cnt - p[:, cols]).astype(BF16)
            y1 = _dot(d, w_ref[g]) + b_ref[:, cols]
            scale = s_ref[:, cols]
            y2 = y1 * scale
            gg = gate[:, cols]
            sg = _sigmoid(gg)
            do = do_ref[:, cols]
            dy2 = do * (gg * sg)
            dgs.append(do * y2 * _silu_grad(gg, sg))
            dss.append(jnp.sum(dy2 * y1, axis=0, keepdims=True))
            dy1 = dy2 * scale
            dbs.append(jnp.sum(dy1, axis=0, keepdims=True))
            dy1b = dy1.astype(BF16)
            dw_ref[g] += _dot_tn(d, dy1b)
            dd = _dot_nt(dy1b, w_ref[g])
            dpool = dd / cnt
            dext = jnp.concatenate([dpool, carry[:, cols]], axis=0)
            dps.append(_window_sum(dext, g + 1, False)[:tm] - dd)
            carry[:, cols] = dpool[:POOL_HALO]
        dpg_ref[...] = jnp.concatenate(dps + dgs, axis=1).astype(BF16)
        dvec_ref[0:1, :] += jnp.concatenate(dbs, axis=1)
        dvec_ref[1:2, :] += jnp.concatenate(dss, axis=1)

    vec = pl.BlockSpec((1, BRANCH_WIDTH), lambda i: (0, 0))
    wspec = pl.BlockSpec((4, POOL_GROUP_DIM, POOL_GROUP_DIM), lambda i: (0, 0, 0))
    return pl.pallas_call(
        body, name=name, grid=(nt,),
        in_specs=[pl.BlockSpec((tm, 2 * BRANCH_WIDTH), lambda i: (nt - 1 - i, 0)),
                  pl.BlockSpec((POOL_HALO, 2 * BRANCH_WIDTH), lambda i: (jnp.maximum((nt - 1 - i) * halo_per_tile - 1, 0), 0)),
                  pl.BlockSpec((tm, BRANCH_WIDTH), lambda i: (nt - 1 - i, 0)), wspec, vec, vec],
        out_specs=[pl.BlockSpec((tm, 2 * BRANCH_WIDTH), lambda i: (nt - 1 - i, 0)), wspec,
                   pl.BlockSpec((8, BRANCH_WIDTH), lambda i: (0, 0))],
        out_shape=[jax.ShapeDtypeStruct((t, 2 * BRANCH_WIDTH), BF16),
                   jax.ShapeDtypeStruct((4, POOL_GROUP_DIM, POOL_GROUP_DIM), F32),
                   jax.ShapeDtypeStruct((8, BRANCH_WIDTH), F32)],
        scratch_shapes=[pltpu.VMEM((POOL_HALO, BRANCH_WIDTH), F32)],
        compiler_params=_params(1),
    )(pg, pg, d_out, pool_w, pool_b, pool_scale)


def _causal_conv(ext, w_ref, rows):
    acc = None
    for j in range(CONV_KERNEL):
        tap = w_ref[CONV_KERNEL - 1 - j:CONV_KERNEL - j, :]
        shifted = ext if j == 0 else pltpu.roll(ext, j, 0)
        term = tap * shifted[CONV_HALO:]
        acc = term if acc is None else acc + term
    return acc


def _conv_fwd(c2gc, conv_w, conv_b, ln_g, ln_b, *, name):
    t = c2gc.shape[0]
    tm = _tile(t, ROW_TILE)

    def body(c_ref, w_ref, cb_ref, g_ref, b_ref, o_ref, halo):
        @pl.when(pl.program_id(0) == 0)
        def _():
            halo[...] = jnp.zeros_like(halo)

        u = c_ref[:, :BRANCH_WIDTH] * _sigmoid(c_ref[:, BRANCH_WIDTH:2 * BRANCH_WIDTH])
        gate = c_ref[:, 2 * BRANCH_WIDTH:]
        ext = jnp.concatenate([halo[...], u], axis=0)
        cv = _causal_conv(ext, w_ref, tm) + cb_ref[...]
        mu = jnp.mean(cv, axis=-1, keepdims=True)
        xc = cv - mu
        var = jnp.mean(xc * xc, axis=-1, keepdims=True)
        ln = xc * lax.rsqrt(var + LN_EPS) * g_ref[...] + b_ref[...]
        o_ref[...] = (ln * _sigmoid(ln) * (gate * _sigmoid(gate))).astype(BF16)
        halo[...] = u[tm - CONV_HALO:, :]

    vec = pl.BlockSpec((1, BRANCH_WIDTH), lambda i: (0, 0))
    return pl.pallas_call(
        body, name=name, grid=(t // tm,),
        in_specs=[pl.BlockSpec((tm, 3 * BRANCH_WIDTH), lambda i: (i, 0)),
                  pl.BlockSpec((CONV_HALO, BRANCH_WIDTH), lambda i: (0, 0)), vec, vec, vec],
        out_specs=pl.BlockSpec((tm, BRANCH_WIDTH), lambda i: (i, 0)),
        out_shape=jax.ShapeDtypeStruct((t, BRANCH_WIDTH), BF16),
        scratch_shapes=[pltpu.VMEM((CONV_HALO, BRANCH_WIDTH), F32)],
        compiler_params=_params(1),
    )(c2gc, conv_w, conv_b, ln_g, ln_b)


def _conv_bwd(c2gc, d_out, conv_w, conv_b, ln_g, ln_b, *, name):
    t = c2gc.shape[0]
    tm = _tile(t, ROW_TILE)
    nt = t // tm
    halo_per_tile = tm // CONV_HALO
    ext_rows = tm + CONV_HALO

    def body(c_ref, halo_ref, do_ref, w_ref, cb_ref, g_ref, b_ref, dc_ref, dw_ref, dvec_ref, carry):
        i = pl.program_id(0)
        ri = nt - 1 - i

        @pl.when(i == 0)
        def _():
            carry[...] = jnp.zeros_like(carry)
            dw_ref[...] = jnp.zeros_like(dw_ref)
            dvec_ref[...] = jnp.zeros_like(dvec_ref)

        a = c_ref[:, :BRANCH_WIDTH]
        sb = _sigmoid(c_ref[:, BRANCH_WIDTH:2 * BRANCH_WIDTH])
        gate = c_ref[:, 2 * BRANCH_WIDTH:]
        u = a * sb
        hu = halo_ref[:, :BRANCH_WIDTH] * _sigmoid(halo_ref[:, BRANCH_WIDTH:2 * BRANCH_WIDTH])
        ext = jnp.concatenate([jnp.where(ri > 0, hu, 0.0), u], axis=0)
        cv = _causal_conv(ext, w_ref, tm) + cb_ref[...]
        mu = jnp.mean(cv, axis=-1, keepdims=True)
        xc = cv - mu
        rs = lax.rsqrt(jnp.mean(xc * xc, axis=-1, keepdims=True) + LN_EPS)
        n = xc * rs
        ln = n * g_ref[...] + b_ref[...]
        sl = _sigmoid(ln)
        sgate = _sigmoid(gate)
        do = do_ref[...]
        dgate = do * (ln * sl) * _silu_grad(gate, sgate)
        dln = do * (gate * sgate) * _silu_grad(ln, sl)
        dn = dln * g_ref[...]
        dcv = rs * (dn - jnp.mean(dn, axis=-1, keepdims=True) - n * jnp.mean(dn * n, axis=-1, keepdims=True))
        dvec_ref[0:1, :] += jnp.sum(dcv, axis=0, keepdims=True)
        dvec_ref[1:2, :] += jnp.sum(dln * n, axis=0, keepdims=True)
        dvec_ref[2:3, :] += jnp.sum(dln, axis=0, keepdims=True)
        dext = jnp.concatenate([dcv, carry[...]], axis=0)
        du = None
        for j in range(CONV_KERNEL):
            k = CONV_KERNEL - 1 - j
            past = ext if j == 0 else pltpu.roll(ext, j, 0)
            dw_ref[k:k + 1, :] += jnp.sum(dcv * past[CONV_HALO:], axis=0, keepdims=True)
            future = dext if j == 0 else pltpu.roll(dext, ext_rows - j, 0)
            term = w_ref[k:k + 1, :] * future[:tm]
            du = term if du is None else du + term
        dc_ref[...] = jnp.concatenate([du * sb, du * a * sb * (1.0 - sb), dgate], axis=1).astype(BF16)
        carry[...] = dcv[:CONV_HALO]

    vec = pl.BlockSpec((1, BRANCH_WIDTH), lambda i: (0, 0))
    wspec = pl.BlockSpec((CONV_HALO, BRANCH_WIDTH), lambda i: (0, 0))
    return pl.pallas_call(
        body, name=name, grid=(nt,),
        in_specs=[pl.BlockSpec((tm, 3 * BRANCH_WIDTH), lambda i: (nt - 1 - i, 0)),
                  pl.BlockSpec((CONV_HALO, 3 * BRANCH_WIDTH), lambda i: (jnp.maximum((nt - 1 - i) * halo_per_tile - 1, 0), 0)),
                  pl.BlockSpec((tm, BRANCH_WIDTH), lambda i: (nt - 1 - i, 0)), wspec, vec, vec, vec],
        out_specs=[pl.BlockSpec((tm, 3 * BRANCH_WIDTH), lambda i: (nt - 1 - i, 0)), wspec,
                   pl.BlockSpec((8, BRANCH_WIDTH), lambda i: (0, 0))],
        out_shape=[jax.ShapeDtypeStruct((t, 3 * BRANCH_WIDTH), BF16),
                   jax.ShapeDtypeStruct((CONV_HALO, BRANCH_WIDTH), F32),
                   jax.ShapeDtypeStruct((8, BRANCH_WIDTH), F32)],
        scratch_shapes=[pltpu.VMEM((CONV_HALO, BRANCH_WIDTH), F32)],
        compiler_params=_params(1),
    )(c2gc, c2gc, d_out, conv_w, conv_b, ln_g, ln_b)


def _merge_fwd(gmga, ya, yb, yc, *, name):
    t = ya.shape[0]
    tm = _tile(t, ROW_TILE)
    row = pl.BlockSpec((tm, D_MODEL), lambda i: (i, 0))

    def body(g0, g1, g2, a_ref, b_ref, c_ref, o_ref):
        m = _sigmoid(g0[...]) * a_ref[...] + _sigmoid(g1[...]) * b_ref[...] + _sigmoid(g2[...]) * c_ref[...]
        o_ref[...] = m.astype(BF16)

    gspecs = [pl.BlockSpec((tm, D_MODEL), functools.partial(lambda i, b: (i, b), b=b)) for b in range(3)]
    return pl.pallas_call(
        body, name=name, grid=(t // tm,),
        in_specs=gspecs + [row, row, row], out_specs=row,
        out_shape=jax.ShapeDtypeStruct((t, D_MODEL), BF16),
        compiler_params=_params(1),
    )(gmga, gmga, gmga, ya, yb, yc)


def _merge_bwd(dm, gmga, ya, yb, yc, *, name):
    t = ya.shape[0]
    tm = _tile(t, ROW_TILE)
    row = pl.BlockSpec((tm, D_MODEL), lambda i: (i, 0))
    wide = pl.BlockSpec((tm, 3 * D_MODEL), lambda i: (i, 0))

    def body(dm_ref, g0, g1, g2, a_ref, b_ref, c_ref, da_ref, db_ref, dc_ref, dg_ref):
        dmv = dm_ref[...]
        for k, (g_ref, y_ref, dy_ref) in enumerate(((g0, a_ref, da_ref), (g1, b_ref, db_ref), (g2, c_ref, dc_ref))):
            s = _sigmoid(g_ref[...])
            dy_ref[...] = (dmv * s).astype(BF16)
            dg_ref[:, k * D_MODEL:(k + 1) * D_MODEL] = (dmv * y_ref[...] * s * (1.0 - s)).astype(BF16)

    gspecs = [pl.BlockSpec((tm, D_MODEL), functools.partial(lambda i, b: (i, b), b=b)) for b in range(3)]
    return pl.pallas_call(
        body, name=name, grid=(t // tm,),
        in_specs=[row] + gspecs + [row, row, row], out_specs=[row, row, row, wide],
        out_shape=[jax.ShapeDtypeStruct((t, D_MODEL), BF16)] * 3 + [jax.ShapeDtypeStruct((t, 3 * D_MODEL), BF16)],
        compiler_params=_params(1),
    )(dm, gmga, gmga, gmga, ya, yb, yc)


GA_BLOCK = 3 * D_MODEL // HEAD_PAIR


def _split_heads(x, lane_is_first):
    zero = jnp.zeros_like(x)
    return jnp.concatenate([jnp.where(lane_is_first, x, zero), jnp.where(lane_is_first, zero, x)], axis=0)


def _side_by_side(x, rows):
    return jnp.concatenate([x[:rows], x[rows:]], axis=1)


def _split_bf16(x):
    hi = x.astype(BF16)
    return hi, (x - hi.astype(F32)).astype(BF16)


def _scores(qcat, kblk, mask):
    z = _dot_nt(qcat, kblk)
    e = jnp.exp(-jnp.abs(z))
    sp = jnp.maximum(z, 0.0) + jnp.log(1.0 + e)
    l1m = -sp
    if mask is not None:
        l1m = jnp.where(mask, l1m, 0.0)
    inv = 1.0 / (1.0 + e)
    pos = z >= 0.0
    return z - sp, l1m, jnp.where(pos, 1.0, e) * inv, jnp.where(pos, e, 1.0) * inv


def _attn_consts(blk):
    lane_is_first = lax.broadcasted_iota(jnp.int32, (1, HEAD_PAIR), 1) < HEAD_DIM
    r = lax.broadcasted_iota(jnp.int32, (blk, blk), 0)
    c = lax.broadcasted_iota(jnp.int32, (blk, blk), 1)
    after = (r > c).astype(BF16)
    from_here = (r >= c).astype(BF16)
    qrow = lax.broadcasted_iota(jnp.int32, (2 * blk, blk), 0)
    qrow = jnp.where(qrow >= blk, qrow - blk, qrow)
    causal = lax.broadcasted_iota(jnp.int32, (2 * blk, blk), 1) < qrow
    return lane_is_first, after, from_here, causal


def _while_mass_left(qi, carry, block):
    def alive(c):
        return jnp.max(c[0]) > LOG_F32_ZERO

    def cond(state):
        return jnp.logical_and(state[0] < qi, state[1])

    def step(state):
        new = block(qi - 1 - state[0], state[2])
        return state[0] + 1, alive(new), new

    return lax.while_loop(cond, step, (jnp.int32(0), alive(carry), carry))[2]


def _attn_fwd(qkv, gmga, *, name):
    t = qkv.shape[0]
    blk = _tile(t, ATTN_BLOCK)
    nq = t // blk

    def body(q_ref, k_ref, v_ref, ga_ref, o_ref, cv_ref):
        qi = pl.program_id(1)
        lane_is_first, after, _, causal = _attn_consts(blk)
        qcat = _split_heads(q_ref[...], lane_is_first)

        def block(kb, carry, mask):
            run, acc = carry
            rows = pl.ds(pl.multiple_of(kb * blk, blk), blk)
            lb, l1m, _, _ = _scores(qcat, k_ref[rows, :], mask)
            hi, lo = _split_bf16(l1m)
            w = jnp.exp(lb + (_dot(hi, after) + _dot(lo, after) + run))
            if mask is not None:
                w = jnp.where(mask, w, 0.0)
            vcat = _split_heads(v_ref[rows, :], lane_is_first)
            acc = acc + _dot(_side_by_side(w.astype(BF16), blk), vcat)
            return run + jnp.sum(l1m, axis=-1, keepdims=True), acc

        carry = block(qi, (jnp.zeros((2 * blk, 1), F32), jnp.zeros((blk, HEAD_PAIR), F32)), causal)
        _, o = _while_mass_left(qi, carry, lambda kb, c: block(kb, c, None))
        o_ref[...] = o
        ga = ga_ref[...]
        cv_ref[...] = (o * (ga * _sigmoid(ga))).astype(BF16)

    qspec = pl.BlockSpec((blk, HEAD_PAIR), lambda p, i: (i, p))
    return pl.pallas_call(
        body, name=name, grid=(N_HEAD_PAIRS, nq),
        in_specs=[qspec,
                  pl.BlockSpec((t, HEAD_PAIR), lambda p, i: (0, N_HEAD_PAIRS + p)),
                  pl.BlockSpec((t, HEAD_PAIR), lambda p, i: (0, 2 * N_HEAD_PAIRS + p)),
                  pl.BlockSpec((blk, HEAD_PAIR), lambda p, i: (i, GA_BLOCK + p))],
        out_specs=[qspec, qspec],
        out_shape=[jax.ShapeDtypeStruct((t, BRANCH_WIDTH), F32), jax.ShapeDtypeStruct((t, BRANCH_WIDTH), BF16)],
        compiler_params=_params(2),
    )(qkv, qkv, qkv, gmga)


def _attn_bwd(qkv, o, gmga, dcv, *, name):
    t = qkv.shape[0]
    blk = _tile(t, ATTN_BLOCK)
    nq = t // blk

    def body(q_ref, k_ref, v_ref, o_ref, ga_ref, dcv_ref, dq_ref, dk_ref, dv_ref, dga_ref, dk_acc, dv_acc):
        qi = pl.program_id(1)
        lane_is_first, after, from_here, causal = _attn_consts(blk)

        @pl.when(qi == 0)
        def _():
            dk_acc[...] = jnp.zeros_like(dk_acc)
            dv_acc[...] = jnp.zeros_like(dv_acc)

        ga, ov, dcvv = ga_ref[...], o_ref[...], dcv_ref[...]
        sg = _sigmoid(ga)
        dob = (dcvv * (ga * sg)).astype(BF16)
        dga_ref[...] = (dcvv * ov * _silu_grad(ga, sg)).astype(BF16)
        gt = dob.astype(F32) * ov
        g_total = jnp.concatenate(
            [jnp.sum(jnp.where(lane_is_first, gt, 0.0), axis=-1, keepdims=True),
             jnp.sum(jnp.where(lane_is_first, 0.0, gt), axis=-1, keepdims=True)], axis=0)
        qcat = _split_heads(q_ref[...], lane_is_first)
        docat = _split_heads(dob, lane_is_first)

        def block(kb, carry, mask):
            run, g_run, dq = carry
            rows = pl.ds(pl.multiple_of(kb * blk, blk), blk)
            kblk = k_ref[rows, :]
            lb, l1m, sig, one_m_sig = _scores(qcat, kblk, mask)
            hi, lo = _split_bf16(l1m)
            w = jnp.exp(lb + (_dot(hi, after) + _dot(lo, after) + run))
            if mask is not None:
                w = jnp.where(mask, w, 0.0)
            wb = w.astype(BF16)
            g = _dot_nt(docat, v_ref[rows, :]) * wb.astype(F32)
            ghi, glo = _split_bf16(g)
            g_before = g_total - g_run - (_dot(ghi, from_here) + _dot(glo, from_here))
            dz = g * one_m_sig - g_before * sig
            if mask is not None:
                dz = jnp.where(mask, dz, 0.0)
            dzb = dz.astype(BF16)
            dq = dq + _dot(_side_by_side(dzb, blk), _split_heads(kblk, lane_is_first))
            dk_acc[rows, :] += _dot_tn(dzb, qcat)
            dv_acc[rows, :] += _dot_tn(wb, docat)
            return (run + jnp.sum(l1m, axis=-1, keepdims=True), g_run + jnp.sum(g, axis=-1, keepdims=True), dq)

        zero = jnp.zeros((2 * blk, 1), F32)
        carry = block(qi, (zero, zero, jnp.zeros((blk, HEAD_PAIR), F32)), causal)
        _, _, dq = _while_mass_left(qi, carry, lambda kb, c: block(kb, c, None))
        dq_ref[...] = (dq * ATTN_SCALE).astype(BF16)

        @pl.when(qi == nq - 1)
        def _():
            dk_ref[...] = dk_acc[...].astype(BF16)
            dv_ref[...] = dv_acc[...].astype(BF16)

    qspec = pl.BlockSpec((blk, HEAD_PAIR), lambda p, i: (i, p))
    whole = pl.BlockSpec((t, HEAD_PAIR), lambda p, i: (0, p))
    out = jax.ShapeDtypeStruct((t, BRANCH_WIDTH), BF16)
    return pl.pallas_call(
        body, name=name, grid=(N_HEAD_PAIRS, nq),
        in_specs=[qspec,
                  pl.BlockSpec((t, HEAD_PAIR), lambda p, i: (0, N_HEAD_PAIRS + p)),
                  pl.BlockSpec((t, HEAD_PAIR), lambda p, i: (0, 2 * N_HEAD_PAIRS + p)),
                  qspec,
                  pl.BlockSpec((blk, HEAD_PAIR), lambda p, i: (i, GA_BLOCK + p)),
                  qspec],
        out_specs=[qspec, whole, whole, qspec],
        out_shape=[out, out, out, out],
        scratch_shapes=[pltpu.VMEM((t, HEAD_PAIR), F32), pltpu.VMEM((t, HEAD_PAIR), F32)],
        compiler_params=_params(2),
    )(qkv, qkv, qkv, o, gmga, dcv)


def _mesh_position():
    x, y, c = lax.axis_index("x"), lax.axis_index("y"), lax.axis_index("c")
    return x, y, c, 4 * x + 2 * y + c


def _flipped(x, y, c, k):
    return (1 - x if k & 4 else x, 1 - y if k & 2 else y, 1 - c if k & 1 else c)


def _all_to_all(src, *, name, same_block):
    rows = src.shape[-2]

    def body(src_ref, dst_ref, send_sems, recv_sems, local_sem):
        x, y, c, me = _mesh_position()

        def outgoing(j):
            return src_ref if same_block else src_ref.at[j]

        def copy(k):
            peer = jnp.bitwise_xor(me, k)
            return pltpu.make_async_remote_copy(
                src_ref=outgoing(peer), dst_ref=dst_ref.at[me],
                send_sem=send_sems.at[k - 1], recv_sem=recv_sems.at[k - 1],
                device_id=_flipped(x, y, c, k), device_id_type=MESH)

        def arrival(k):
            peer = jnp.bitwise_xor(me, k)
            return pltpu.make_async_remote_copy(
                src_ref=outgoing(peer), dst_ref=dst_ref.at[peer],
                send_sem=send_sems.at[k - 1], recv_sem=recv_sems.at[k - 1],
                device_id=_flipped(x, y, c, k), device_id_type=MESH)

        mine = pltpu.make_async_copy(outgoing(me), dst_ref.at[me], local_sem)
        mine.start()
        sent = [copy(k) for k in range(1, N_DEV)]
        for cp in sent:
            cp.start()
        for k in range(1, N_DEV):
            arrival(k).wait_recv()
        for cp in sent:
            cp.wait_send()
        mine.wait()

    return pl.pallas_call(
        body, name=name,
        in_specs=[pl.BlockSpec(memory_space=pl.ANY)], out_specs=pl.BlockSpec(memory_space=pl.ANY),
        out_shape=jax.ShapeDtypeStruct((N_DEV, rows, LANES), src.dtype),
        scratch_shapes=[pltpu.SemaphoreType.DMA((N_DEV - 1,)), pltpu.SemaphoreType.DMA((N_DEV - 1,)),
                        pltpu.SemaphoreType.DMA(())],
    )(src)


def _adamw(parts, w, m, v, *, name):
    rows = w.shape[0]
    tr = _tile(rows, ADAM_ROW_TILE)
    row = pl.BlockSpec((tr, LANES), lambda i: (i, 0))

    def body(p_ref, w_ref, m_ref, v_ref, g_ref, d_ref, nm_ref, nv_ref):
        g = p_ref[0].astype(F32)
        for k in range(1, N_DEV):
            g = g + p_ref[k].astype(F32)
        m2 = ADAM_B1 * m_ref[...] + (1.0 - ADAM_B1) * g
        v2 = ADAM_B2 * v_ref[...] + (1.0 - ADAM_B2) * (g * g)
        m_hat = m2 / (1.0 - ADAM_B1 ** ADAM_STEP)
        v_hat = v2 / (1.0 - ADAM_B2 ** ADAM_STEP)
        g_ref[...] = g
        d_ref[...] = -ADAM_LR * (m_hat / (jnp.sqrt(v_hat) + ADAM_EPS) + ADAM_WD * w_ref[...])
        nm_ref[...] = m2
        nv_ref[...] = v2

    out = jax.ShapeDtypeStruct((rows, LANES), F32)
    return pl.pallas_call(
        body, name=name, grid=(rows // tr,),
        in_specs=[pl.BlockSpec((N_DEV, tr, LANES), lambda i: (0, i, 0)), row, row, row],
        out_specs=[row, row, row, row], out_shape=[out, out, out, out],
        compiler_params=_params(1),
    )(parts, w, m, v)


SHARDED = ("w_in", "w_pool_out", "w_conv_out", "w_attn_out", "w_o", "conv_w")
REPLICATED = ("norm_pre", "pool_w", "pool_b", "pool_scale", "conv_b", "conv_ln_g", "conv_ln_b", "norm_post")
WEIGHT_ORDER = ("norm_pre", "w_in", "pool_w", "pool_b", "pool_scale", "w_pool_out", "conv_w", "conv_b",
                "conv_ln_g", "conv_ln_b", "w_conv_out", "w_attn_out", "w_o", "norm_post")


def _pack_rows(flat_parts, row_multiple):
    flat = jnp.concatenate(flat_parts, axis=-1)
    n = flat.shape[-1]
    chunk = row_multiple * LANES
    total = -(-n // chunk) * chunk
    pad = [(0, 0)] * (flat.ndim - 1) + [(0, total - n)]
    return jnp.pad(flat, pad).reshape(flat.shape[:-1] + (total // LANES, LANES))


def _unpack(buf, shapes):
    flat = buf.reshape(-1)
    out, at = {}, 0
    for name, shape in shapes:
        n = 1
        for s in shape:
            n *= s
        out[name] = flat[at:at + n].reshape(shape)
        at += n
    return out


def _to_dest_major(name, full):
    axis = 1 if name == "w_o" else 2
    n = full.shape[axis] // N_DEV
    return jnp.stack([lax.slice_in_dim(full, d * n, (d + 1) * n, axis=axis) for d in range(N_DEV)]).reshape(N_DEV, -1)


def _from_source_major(name, gathered, shard_shape):
    g = gathered.reshape((N_DEV,) + tuple(shard_shape))
    return jnp.concatenate([g[d] for d in range(N_DEV)], axis=1 if name == "w_o" else 2)


def kernel(x, norm_pre, w_in, pool_w, pool_b, pool_scale, w_pool_out, conv_w, conv_b, conv_ln_g, conv_ln_b, w_conv_out, w_attn_out, w_o, norm_post, loss_target, m_norm_pre, m_w_in, m_pool_w, m_pool_b, m_pool_scale, m_w_pool_out, m_conv_w, m_conv_b, m_conv_ln_g, m_conv_ln_b, m_w_conv_out, m_w_attn_out, m_w_o, m_norm_post, v_norm_pre, v_w_in, v_pool_w, v_pool_b, v_pool_scale, v_w_pool_out, v_conv_w, v_conv_b, v_conv_ln_g, v_conv_ln_b, v_w_conv_out, v_w_attn_out, v_w_o, v_norm_post):
    weights = dict(norm_pre=norm_pre, w_in=w_in, pool_w=pool_w, pool_b=pool_b, pool_scale=pool_scale,
                   w_pool_out=w_pool_out, conv_w=conv_w, conv_b=conv_b, conv_ln_g=conv_ln_g, conv_ln_b=conv_ln_b,
                   w_conv_out=w_conv_out, w_attn_out=w_attn_out, w_o=w_o, norm_post=norm_post)
    mom1 = dict(norm_pre=m_norm_pre, w_in=m_w_in, pool_w=m_pool_w, pool_b=m_pool_b, pool_scale=m_pool_scale,
                w_pool_out=m_w_pool_out, conv_w=m_conv_w, conv_b=m_conv_b, conv_ln_g=m_conv_ln_g, conv_ln_b=m_conv_ln_b,
                w_conv_out=m_w_conv_out, w_attn_out=m_w_attn_out, w_o=m_w_o, norm_post=m_norm_post)
    mom2 = dict(norm_pre=v_norm_pre, w_in=v_w_in, pool_w=v_pool_w, pool_b=v_pool_b, pool_scale=v_pool_scale,
                w_pool_out=v_w_pool_out, conv_w=v_conv_w, conv_b=v_conv_b, conv_ln_g=v_conv_ln_g, conv_ln_b=v_conv_ln_b,
                w_conv_out=v_w_conv_out, w_attn_out=v_w_attn_out, w_o=v_w_o, norm_post=v_norm_post)
    xs = x[0]
    target = loss_target[0]

    mm_names = ("w_in", "w_pool_out", "w_conv_out", "w_attn_out", "w_o")
    send = [weights[n].astype(BF16).reshape(-1) for n in mm_names]
    conv_bits = lax.bitcast_convert_type(conv_w.reshape(-1), BF16).reshape(-1)
    gathered = _all_to_all(_pack_rows(send + [conv_bits], 16), name="gather_weights", same_block=True)
    gathered = gathered.reshape(N_DEV, -1)
    full, at = {}, 0
    for n in mm_names:
        size = weights[n].size
        full[n] = _from_source_major(n, gathered[:, at:at + size], weights[n].shape)
        at += size
    conv_flat = lax.bitcast_convert_type(gathered[:, at:at + 2 * conv_w.size].reshape(N_DEV, conv_w.size, 2), F32)
    conv_full = _from_source_major("conv_w", conv_flat, conv_w.shape)
    conv_full = jnp.pad(conv_full, ((0, 0), (0, CONV_HALO - CONV_KERNEL), (0, 0)))

    def in_sections(w):
        return dict(pg=w[:, 0:1024], c2gc=w[:, 1024:2560], q=w[:, 2560:3072], k=w[:, 3072:3584], v=w[:, 3584:4096],
                    gmga=jnp.concatenate([w[:, 4608:7680], w[:, 4096:4608]], axis=1))

    saved = []
    cur = xs
    for l in range(DEPTH):
        sec = in_sections(full["w_in"][l])
        w_qkv = jnp.concatenate([sec["q"] * ATTN_SCALE, sec["k"], sec["v"]], axis=1)
        pw = pool_w[l].astype(BF16)
        pb, ps = pool_b[l].reshape(1, -1), pool_scale[l].reshape(1, -1)
        cb, lg, lb = conv_b[l].reshape(1, -1), conv_ln_g[l].reshape(1, -1), conv_ln_b[l].reshape(1, -1)
        h = _rms_fwd(cur, norm_pre[l].reshape(1, -1), name=f"rms_pre_fwd_{l}")
        pg = _matmul(h, sec["pg"], mode="nn", name=f"proj_pg_{l}")
        c2gc = _matmul(h, sec["c2gc"], mode="nn", name=f"proj_c2gc_{l}")
        qkv = _matmul(h, w_qkv, mode="nn", name=f"proj_qkv_{l}", out_dtype=BF16)
        gmga = _matmul(h, sec["gmga"], mode="nn", name=f"proj_gmga_{l}")
        a_act = _pool_fwd(pg, pw, pb, ps, name=f"pool_fwd_{l}")
        b_act = _conv_fwd(c2gc, conv_full[l], cb, lg, lb, name=f"conv_fwd_{l}")
        o, c_act = _attn_fwd(qkv, gmga, name=f"attn_fwd_{l}")
        ya = _matmul(a_act, full["w_pool_out"][l], mode="nn", name=f"out_pool_{l}")
        yb = _matmul(b_act, full["w_conv_out"][l], mode="nn", name=f"out_conv_{l}")
        yc = _matmul(c_act, full["w_attn_out"][l], mode="nn", name=f"out_attn_{l}")
        mix = _merge_fwd(gmga, ya, yb, yc, name=f"merge_fwd_{l}")
        out = _matmul(mix, full["w_o"][l], mode="nn", name=f"out_proj_{l}")
        nxt = _rms_fwd(out, norm_post[l].reshape(1, -1), name=f"rms_post_fwd_{l}", resid=cur)
        saved.append(dict(x=cur, h=h, pg=pg, c2gc=c2gc, qkv=qkv, gmga=gmga, a=a_act, b=b_act, c=c_act, o=o,
                          ya=ya, yb=yb, yc=yc, mix=mix, out=out, sec=sec, pw=pw, pb=pb, ps=ps, cb=cb, lg=lg, lb=lb))
        cur = nxt

    loss_tile, dx = _loss_head(cur, target, name="loss_head")
    loss = lax.psum(loss_tile[0, 0], ("x", "y", "c"))

    grads = {n: [None] * DEPTH for n in WEIGHT_ORDER}
    for l in reversed(range(DEPTH)):
        s = saved[l]
        dout, grads["norm_post"][l] = _rms_bwd(s["out"], norm_post[l].reshape(1, -1), dx, name=f"rms_post_bwd_{l}")
        dmix = _matmul(dout, full["w_o"][l], mode="nt", name=f"d_mix_{l}")
        grads["w_o"][l] = _matmul(s["mix"], dout, mode="tn", name=f"d_w_o_{l}")
        dya, dyb, dyc, dgm = _merge_bwd(dmix, s["gmga"], s["ya"], s["yb"], s["yc"], name=f"merge_bwd_{l}")
        da = _matmul(dya, full["w_pool_out"][l], mode="nt", name=f"d_pool_act_{l}")
        grads["w_pool_out"][l] = _matmul(s["a"], dya, mode="tn", name=f"d_w_pool_out_{l}")
        db = _matmul(dyb, full["w_conv_out"][l], mode="nt", name=f"d_conv_act_{l}")
        grads["w_conv_out"][l] = _matmul(s["b"], dyb, mode="tn", name=f"d_w_conv_out_{l}")
        dc = _matmul(dyc, full["w_attn_out"][l], mode="nt", name=f"d_attn_act_{l}")
        grads["w_attn_out"][l] = _matmul(s["c"], dyc, mode="tn", name=f"d_w_attn_out_{l}")
        dq, dk, dv, dga = _attn_bwd(s["qkv"], s["o"], s["gmga"], dc, name=f"attn_bwd_{l}")
        dc2gc, dcw, dcvec = _conv_bwd(s["c2gc"], db, conv_full[l], s["cb"], s["lg"], s["lb"], name=f"conv_bwd_{l}")
        dpg, dpw, dpvec = _pool_bwd(s["pg"], da, s["pw"], s["pb"], s["ps"], name=f"pool_bwd_{l}")
        grads["conv_w"][l] = dcw[:CONV_KERNEL]
        grads["conv_b"][l], grads["conv_ln_g"][l], grads["conv_ln_b"][l] = dcvec[0], dcvec[1], dcvec[2]
        grads["pool_w"][l] = dpw
        grads["pool_b"][l] = dpvec[0].reshape(4, POOL_GROUP_DIM)
        grads["pool_scale"][l] = dpvec[1]
        dsec = dict(pg=dpg, c2gc=dc2gc, q=dq, k=dk, v=dv, gmga=jnp.concatenate([dgm, dga], axis=1))
        dh, dws = None, {}
        for n in ("pg", "c2gc", "q", "k", "v", "gmga"):
            dh = _matmul(dsec[n], s["sec"][n], mode="nt", name=f"d_h_{n}_{l}", acc=dh)
            dws[n] = _matmul(s["h"], dsec[n], mode="tn", name=f"d_w_in_{n}_{l}")
        grads["w_in"][l] = jnp.concatenate(
            [dws["pg"], dws["c2gc"], dws["q"], dws["k"], dws["v"], dws["gmga"][:, 3 * D_MODEL:], dws["gmga"][:, :3 * D_MODEL]],
            axis=1)
        dx, dg_pre = _rms_bwd(s["x"], norm_pre[l].reshape(1, -1), dh, name=f"rms_pre_bwd_{l}", resid=dx)
        grads["norm_pre"][l] = dg_pre.reshape(-1)
        grads["norm_post"][l] = grads["norm_post"][l].reshape(-1)
    grads = {n: jnp.stack(g) for n, g in grads.items()}

    replicated = jnp.concatenate([grads[n].reshape(-1) for n in REPLICATED])
    send = [_to_dest_major(n, grads[n]) for n in SHARDED] + [jnp.broadcast_to(replicated, (N_DEV, replicated.size))]
    parts = _all_to_all(_pack_rows([p.astype(BF16) for p in send], ADAM_ROW_TILE), name="exchange_grads", same_block=False)
    shapes = [(n, weights[n].shape) for n in SHARDED + REPLICATED]

    def packed(tree):
        return _pack_rows([tree[n].reshape(-1) for n in SHARDED + REPLICATED], ADAM_ROW_TILE)

    results = _adamw(parts, packed(weights), packed(mom1), packed(mom2), name="adamw")
    g_out, delta, new_m, new_v = (_unpack(r, shapes) for r in results)
    return (loss, dx[None], *[g_out[n] for n in WEIGHT_ORDER], *[delta[n] for n in WEIGHT_ORDER],
            *[new_m[n] for n in WEIGHT_ORDER], *[new_v[n] for n in WEIGHT_ORDER])
```

```python
import functools

import jax
import jax.numpy as jnp
from jax import lax
from jax.experimental import pallas as pl
from jax.experimental.pallas import tpu as pltpu

F32 = jnp.float32
BF16 = jnp.bfloat16

D_MODEL = 1024
DEPTH = 2
POOL_WINDOWS = (2, 4, 8, 16)
POOL_GROUP_DIM = 128
BRANCH_WIDTH = 512
CONV_KERNEL = 31
CONV_HALO = 32
POOL_HALO = 16
HEAD_DIM = 64
HEAD_PAIR = 128
N_HEAD_PAIRS = 4
ATTN_SCALE = 0.125
LOG_F32_ZERO = -104.0
RMS_EPS = 1e-6
LN_EPS = 1e-5
N_DEV = 8
LANES = 128

ADAM_LR = 0.001
ADAM_B1 = 0.9
ADAM_B2 = 0.999
ADAM_EPS = 1e-08
ADAM_WD = 0.01
ADAM_STEP = 10

ROW_TILE = 256
ATTN_BLOCK = 256
MM_TILE = 1024
ADAM_TILE_ELEMS = 256 * 1024
VMEM_LIMIT = 48 * 1024 * 1024

MESH = pl.DeviceIdType.MESH


def _params(n_axes):
    return pltpu.CompilerParams(dimension_semantics=("arbitrary",) * n_axes, vmem_limit_bytes=VMEM_LIMIT)


def _tile(n, pref):
    if n <= pref:
        return n
    t = (pref // LANES) * LANES
    while n % t:
        t -= LANES
    return t


def _dot(a, b):
    return jnp.dot(a, b, preferred_element_type=F32)


def _dot_nt(a, b):
    return lax.dot_general(a, b, (((1,), (1,)), ((), ())), preferred_element_type=F32)


def _dot_tn(a, b):
    return lax.dot_general(a, b, (((0,), (0,)), ((), ())), preferred_element_type=F32)


def _sigmoid(x):
    return 1.0 / (1.0 + jnp.exp(-x))


def _silu_grad(x, s):
    return s * (1.0 + x * (1.0 - s))


def _matmul(a, b, *, mode, name, out_dtype=F32, acc=None):
    if mode == "nn":
        (m, k), n = a.shape, b.shape[1]
    elif mode == "nt":
        (m, k), n = a.shape, b.shape[0]
    else:
        (k, m), n = a.shape, b.shape[1]
    tm, tn, tk = _tile(m, MM_TILE), _tile(n, MM_TILE), _tile(k, MM_TILE)
    nk = k // tk
    dot = {"nn": _dot, "nt": _dot_nt, "tn": _dot_tn}[mode]
    a_spec = pl.BlockSpec((tk, tm), lambda i, j, kk: (kk, i)) if mode == "tn" else pl.BlockSpec((tm, tk), lambda i, j, kk: (i, kk))
    b_spec = pl.BlockSpec((tn, tk), lambda i, j, kk: (j, kk)) if mode == "nt" else pl.BlockSpec((tk, tn), lambda i, j, kk: (kk, j))
    o_spec = pl.BlockSpec((tm, tn), lambda i, j, kk: (i, j))
    has_acc = acc is not None

    def body(*refs):
        a_ref, b_ref = refs[0], refs[1]
        acc_in = refs[2] if has_acc else None
        o_ref = refs[3] if has_acc else refs[2]
        part = dot(a_ref[...], b_ref[...])
        if nk == 1:
            if has_acc:
                part = part + acc_in[...]
            o_ref[...] = part.astype(out_dtype)
            return
        scr = refs[-1]
        kk = pl.program_id(2)

        @pl.when(kk == 0)
        def _():
            scr[...] = part + acc_in[...] if has_acc else part

        @pl.when(kk > 0)
        def _():
            scr[...] += part

        @pl.when(kk == nk - 1)
        def _():
            o_ref[...] = scr[...].astype(out_dtype)

    in_specs = [a_spec, b_spec] + ([o_spec] if has_acc else [])
    args = (a, b) + ((acc,) if has_acc else ())
    return pl.pallas_call(
        body, name=name, grid=(m // tm, n // tn, nk),
        in_specs=in_specs, out_specs=o_spec,
        out_shape=jax.ShapeDtypeStruct((m, n), out_dtype),
        scratch_shapes=[pltpu.VMEM((tm, tn), F32)] if nk > 1 else [],
        compiler_params=_params(3),
    )(*args)


def _rms_fwd(x, g, *, name, resid=None):
    t = x.shape[0]
    tm = _tile(t, ROW_TILE)
    row = pl.BlockSpec((tm, D_MODEL), lambda i: (i, 0))
    vec = pl.BlockSpec((1, D_MODEL), lambda i: (0, 0))
    has_resid = resid is not None

    def body(*refs):
        x_ref, g_ref = refs[0], refs[1]
        o_ref = refs[-1]
        xv = x_ref[...]
        y = xv * lax.rsqrt(jnp.mean(xv * xv, axis=-1, keepdims=True) + RMS_EPS) * g_ref[...]
        if has_resid:
            o_ref[...] = refs[2][...] + y
        else:
            o_ref[...] = y.astype(BF16)

    return pl.pallas_call(
        body, name=name, grid=(t // tm,),
        in_specs=[row, vec] + ([row] if has_resid else []), out_specs=row,
        out_shape=jax.ShapeDtypeStruct((t, D_MODEL), F32 if has_resid else BF16),
        compiler_params=_params(1),
    )(*((x, g) + ((resid,) if has_resid else ())))


def _rms_bwd(xin, g, dy, *, name, resid=None):
    t = xin.shape[0]
    tm = _tile(t, ROW_TILE)
    row = pl.BlockSpec((tm, D_MODEL), lambda i: (i, 0))
    vec = pl.BlockSpec((1, D_MODEL), lambda i: (0, 0))
    has_resid = resid is not None
    out_dtype = F32 if has_resid else BF16

    def body(*refs):
        x_ref, g_ref, dy_ref = refs[0], refs[1], refs[2]
        dx_ref, dg_ref = refs[-2], refs[-1]
        xv, dyv = x_ref[...], dy_ref[...]
        r = lax.rsqrt(jnp.mean(xv * xv, axis=-1, keepdims=True) + RMS_EPS)
        a = dyv * g_ref[...]
        dx = r * a - xv * (r * r * r) * jnp.mean(a * xv, axis=-1, keepdims=True)
        if has_resid:
            dx = dx + refs[3][...]
        dx_ref[...] = dx.astype(out_dtype)
        part = jnp.sum(dyv * xv * r, axis=0, keepdims=True)

        @pl.when(pl.program_id(0) == 0)
        def _():
            dg_ref[...] = part

        @pl.when(pl.program_id(0) > 0)
        def _():
            dg_ref[...] += part

    return pl.pallas_call(
        body, name=name, grid=(t // tm,),
        in_specs=[row, vec, row] + ([row] if has_resid else []), out_specs=[row, vec],
        out_shape=[jax.ShapeDtypeStruct((t, D_MODEL), out_dtype), jax.ShapeDtypeStruct((1, D_MODEL), F32)],
        compiler_params=_params(1),
    )(*((xin, g, dy) + ((resid,) if has_resid else ())))


def _loss_head(x, target, *, name):
    t = x.shape[0]
    tm = _tile(t, ROW_TILE)
    row = pl.BlockSpec((tm, D_MODEL), lambda i: (i, 0))
    acc = pl.BlockSpec((8, LANES), lambda i: (0, 0))

    def body(x_ref, t_ref, l_ref, dx_ref):
        diff = x_ref[...] - t_ref[...]
        dx_ref[...] = diff * (1.0 / D_MODEL)
        part = 0.5 * jnp.sum(jnp.mean(diff * diff, axis=-1, keepdims=True), axis=0, keepdims=True)

        @pl.when(pl.program_id(0) == 0)
        def _():
            l_ref[...] = jnp.zeros((8, LANES), F32) + part

        @pl.when(pl.program_id(0) > 0)
        def _():
            l_ref[...] += part

    return pl.pallas_call(
        body, name=name, grid=(t // tm,),
        in_specs=[row, row], out_specs=[acc, row],
        out_shape=[jax.ShapeDtypeStruct((8, LANES), F32), jax.ShapeDtypeStruct((t, D_MODEL), F32)],
        compiler_params=_params(1),
    )(x, target)


def _window_sum(ext, n_doublings, forward):
    rows = ext.shape[0]
    s, sh = ext, 1
    for _ in range(n_doublings):
        s = s + pltpu.roll(s, sh if forward else rows - sh, 0)
        sh *= 2
    return s


def _pool_fwd(pg, pool_w, pool_b, pool_scale, *, name):
    t = pg.shape[0]
    tm = _tile(t, ROW_TILE)

    def body(pg_ref, w_ref, b_ref, s_ref, o_ref, halo):
        i = pl.program_id(0)

        @pl.when(i == 0)
        def _():
            halo[...] = jnp.zeros_like(halo)

        p = pg_ref[:, :BRANCH_WIDTH]
        gate = pg_ref[:, BRANCH_WIDTH:]
        ext = jnp.concatenate([halo[...], p], axis=0)
        pos = i * tm + lax.broadcasted_iota(jnp.int32, (tm, 1), 0)
        outs = []
        for g, w in enumerate(POOL_WINDOWS):
            cols = slice(g * POOL_GROUP_DIM, (g + 1) * POOL_GROUP_DIM)
            cnt = jnp.minimum(pos + 1, w).astype(F32)
            d = _window_sum(ext[:, cols], g + 1, True)[POOL_HALO:] / cnt - p[:, cols]
            y = (_dot(d.astype(BF16), w_ref[g]) + b_ref[:, cols]) * s_ref[:, cols]
            gg = gate[:, cols]
            outs.append(y * (gg * _sigmoid(gg)))
        o_ref[...] = jnp.concatenate(outs, axis=1).astype(BF16)
        halo[...] = p[tm - POOL_HALO:, :]

    vec = pl.BlockSpec((1, BRANCH_WIDTH), lambda i: (0, 0))
    return pl.pallas_call(
        body, name=name, grid=(t // tm,),
        in_specs=[pl.BlockSpec((tm, 2 * BRANCH_WIDTH), lambda i: (i, 0)),
                  pl.BlockSpec((4, POOL_GROUP_DIM, POOL_GROUP_DIM), lambda i: (0, 0, 0)), vec, vec],
        out_specs=pl.BlockSpec((tm, BRANCH_WIDTH), lambda i: (i, 0)),
        out_shape=jax.ShapeDtypeStruct((t, BRANCH_WIDTH), BF16),
        scratch_shapes=[pltpu.VMEM((POOL_HALO, BRANCH_WIDTH), F32)],
        compiler_params=_params(1),
    )(pg, pool_w, pool_b, pool_scale)


def _pool_bwd(pg, d_out, pool_w, pool_b, pool_scale, *, name):
    t = pg.shape[0]
    tm = _tile(t, ROW_TILE)
    nt = t // tm
    halo_per_tile = tm // POOL_HALO

    def body(pg_ref, halo_ref, do_ref, w_ref, b_ref, s_ref, dpg_ref, dw_ref, dvec_ref, carry):
        i = pl.program_id(0)
        ri = nt - 1 - i

        @pl.when(i == 0)
        def _():
            carry[...] = jnp.zeros_like(carry)
            dw_ref[...] = jnp.zeros_like(dw_ref)
            dvec_ref[...] = jnp.zeros_like(dvec_ref)

        p = pg_ref[:, :BRANCH_WIDTH]
        gate = pg_ref[:, BRANCH_WIDTH:]
        hp = jnp.where(ri > 0, halo_ref[:, :BRANCH_WIDTH], 0.0)
        ext = jnp.concatenate([hp, p], axis=0)
        pos = ri * tm + lax.broadcasted_iota(jnp.int32, (tm, 1), 0)
        dps, dgs, dbs, dss = [], [], [], []
        for g, w in enumerate(POOL_WINDOWS):
            cols = slice(g * POOL_GROUP_DIM, (g + 1) * POOL_GROUP_DIM)
            cnt = jnp.minimum(pos + 1, w).astype(F32)
            d = (_window_sum(ext[:, cols], g + 1, True)[POOL_HALO:] / cnt - p[:, cols]).astype(BF16)
            y1 = _dot(d, w_ref[g]) + b_ref[:, cols]
            scale = s_ref[:, cols]
            y2 = y1 * scale
            gg = gate[:, cols]
            sg = _sigmoid(gg)
            do = do_ref[:, cols]
            dy2 = do * (gg * sg)
            dgs.append(do * y2 * _silu_grad(gg, sg))
            dss.append(jnp.sum(dy2 * y1, axis=0, keepdims=True))
            dy1 = dy2 * scale
            dbs.append(jnp.sum(dy1, axis=0, keepdims=True))
            dy1b = dy1.astype(BF16)
            dw_ref[g] += _dot_tn(d, dy1b)
            dd = _dot_nt(dy1b, w_ref[g])
            dpool = dd / cnt
            dext = jnp.concatenate([dpool, carry[:, cols]], axis=0)
            dps.append(_window_sum(dext, g + 1, False)[:tm] - dd)
            carry[:, cols] = dpool[:POOL_HALO]
        dpg_ref[...] = jnp.concatenate(dps + dgs, axis=1).astype(BF16)
        dvec_ref[0:1, :] += jnp.concatenate(dbs, axis=1)
        dvec_ref[1:2, :] += jnp.concatenate(dss, axis=1)

    vec = pl.BlockSpec((1, BRANCH_WIDTH), lambda i: (0, 0))
    wspec = pl.BlockSpec((4, POOL_GROUP_DIM, POOL_GROUP_DIM), lambda i: (0, 0, 0))
    return pl.pallas_call(
        body, name=name, grid=(nt,),
        in_specs=[pl.BlockSpec((tm, 2 * BRANCH_WIDTH), lambda i: (nt - 1 - i, 0)),
                  pl.BlockSpec((POOL_HALO, 2 * BRANCH_WIDTH), lambda i: (jnp.maximum((nt - 1 - i) * halo_per_tile - 1, 0), 0)),
                  pl.BlockSpec((tm, BRANCH_WIDTH), lambda i: (nt - 1 - i, 0)), wspec, vec, vec],
        out_specs=[pl.BlockSpec((tm, 2 * BRANCH_WIDTH), lambda i: (nt - 1 - i, 0)), wspec,
                   pl.BlockSpec((8, BRANCH_WIDTH), lambda i: (0, 0))],
        out_shape=[jax.ShapeDtypeStruct((t, 2 * BRANCH_WIDTH), BF16),
                   jax.ShapeDtypeStruct((4, POOL_GROUP_DIM, POOL_GROUP_DIM), F32),
                   jax.ShapeDtypeStruct((8, BRANCH_WIDTH), F32)],
        scratch_shapes=[pltpu.VMEM((POOL_HALO, BRANCH_WIDTH), F32)],
        compiler_params=_params(1),
    )(pg, pg, d_out, pool_w, pool_b, pool_scale)


def _causal_conv(ext, w_ref, rows):
    acc = None
    for j in range(CONV_KERNEL):
        tap = w_ref[CONV_KERNEL - 1 - j:CONV_KERNEL - j, :]
        shifted = ext if j == 0 else pltpu.roll(ext, j, 0)
        term = tap * shifted[CONV_HALO:]
        acc = term if acc is None else acc + term
    return acc


def _conv_fwd(c2gc, conv_w, conv_b, ln_g, ln_b, *, name):
    t = c2gc.shape[0]
    tm = _tile(t, ROW_TILE)

    def body(c_ref, w_ref, cb_ref, g_ref, b_ref, o_ref, halo):
        @pl.when(pl.program_id(0) == 0)
        def _():
            halo[...] = jnp.zeros_like(halo)

        u = c_ref[:, :BRANCH_WIDTH] * _sigmoid(c_ref[:, BRANCH_WIDTH:2 * BRANCH_WIDTH])
        gate = c_ref[:, 2 * BRANCH_WIDTH:]
        ext = jnp.concatenate([halo[...], u], axis=0)
        cv = _causal_conv(ext, w_ref, tm) + cb_ref[...]
        mu = jnp.mean(cv, axis=-1, keepdims=True)
        xc = cv - mu
        var = jnp.mean(xc * xc, axis=-1, keepdims=True)
        ln = xc * lax.rsqrt(var + LN_EPS) * g_ref[...] + b_ref[...]
        o_ref[...] = (ln * _sigmoid(ln) * (gate * _sigmoid(gate))).astype(BF16)
        halo[...] = u[tm - CONV_HALO:, :]

    vec = pl.BlockSpec((1, BRANCH_WIDTH), lambda i: (0, 0))
    return pl.pallas_call(
        body, name=name, grid=(t // tm,),
        in_specs=[pl.BlockSpec((tm, 3 * BRANCH_WIDTH), lambda i: (i, 0)),
                  pl.BlockSpec((CONV_HALO, BRANCH_WIDTH), lambda i: (0, 0)), vec, vec, vec],
        out_specs=pl.BlockSpec((tm, BRANCH_WIDTH), lambda i: (i, 0)),
        out_shape=jax.ShapeDtypeStruct((t, BRANCH_WIDTH), BF16),
        scratch_shapes=[pltpu.VMEM((CONV_HALO, BRANCH_WIDTH), F32)],
        compiler_params=_params(1),
    )(c2gc, conv_w, conv_b, ln_g, ln_b)


def _conv_bwd(c2gc, d_out, conv_w, conv_b, ln_g, ln_b, *, name):
    t = c2gc.shape[0]
    tm = _tile(t, ROW_TILE)
    nt = t // tm
    halo_per_tile = tm // CONV_HALO
    ext_rows = tm + CONV_HALO

    def body(c_ref, halo_ref, do_ref, w_ref, cb_ref, g_ref, b_ref, dc_ref, dw_ref, dvec_ref, carry):
        i = pl.program_id(0)
        ri = nt - 1 - i

        @pl.when(i == 0)
        def _():
            carry[...] = jnp.zeros_like(carry)
            dw_ref[...] = jnp.zeros_like(dw_ref)
            dvec_ref[...] = jnp.zeros_like(dvec_ref)

        a = c_ref[:, :BRANCH_WIDTH]
        sb = _sigmoid(c_ref[:, BRANCH_WIDTH:2 * BRANCH_WIDTH])
        gate = c_ref[:, 2 * BRANCH_WIDTH:]
        u = a * sb
        hu = halo_ref[:, :BRANCH_WIDTH] * _sigmoid(halo_ref[:, BRANCH_WIDTH:2 * BRANCH_WIDTH])
        ext = jnp.concatenate([jnp.where(ri > 0, hu, 0.0), u], axis=0)
        cv = _causal_conv(ext, w_ref, tm) + cb_ref[...]
        mu = jnp.mean(cv, axis=-1, keepdims=True)
        xc = cv - mu
        rs = lax.rsqrt(jnp.mean(xc * xc, axis=-1, keepdims=True) + LN_EPS)
        n = xc * rs
        ln = n * g_ref[...] + b_ref[...]
        sl = _sigmoid(ln)
        sgate = _sigmoid(gate)
        do = do_ref[...]
        dgate = do * (ln * sl) * _silu_grad(gate, sgate)
        dln = do * (gate * sgate) * _silu_grad(ln, sl)
        dn = dln * g_ref[...]
        dcv = rs * (dn - jnp.mean(dn, axis=-1, keepdims=True) - n * jnp.mean(dn * n, axis=-1, keepdims=True))
        dvec_ref[0:1, :] += jnp.sum(dcv, axis=0, keepdims=True)
        dvec_ref[1:2, :] += jnp.sum(dln * n, axis=0, keepdims=True)
        dvec_ref[2:3, :] += jnp.sum(dln, axis=0, keepdims=True)
        dext = jnp.concatenate([dcv, carry[...]], axis=0)
        du = None
        for j in range(CONV_KERNEL):
            k = CONV_KERNEL - 1 - j
            past = ext if j == 0 else pltpu.roll(ext, j, 0)
            dw_ref[k:k + 1, :] += jnp.sum(dcv * past[CONV_HALO:], axis=0, keepdims=True)
            future = dext if j == 0 else pltpu.roll(dext, ext_rows - j, 0)
            term = w_ref[k:k + 1, :] * future[:tm]
            du = term if du is None else du + term
        dc_ref[...] = jnp.concatenate([du * sb, du * a * sb * (1.0 - sb), dgate], axis=1).astype(BF16)
        carry[...] = dcv[:CONV_HALO]

    vec = pl.BlockSpec((1, BRANCH_WIDTH), lambda i: (0, 0))
    wspec = pl.BlockSpec((CONV_HALO, BRANCH_WIDTH), lambda i: (0, 0))
    return pl.pallas_call(
        body, name=name, grid=(nt,),
        in_specs=[pl.BlockSpec((tm, 3 * BRANCH_WIDTH), lambda i: (nt - 1 - i, 0)),
                  pl.BlockSpec((CONV_HALO, 3 * BRANCH_WIDTH), lambda i: (jnp.maximum((nt - 1 - i) * halo_per_tile - 1, 0), 0)),
                  pl.BlockSpec((tm, BRANCH_WIDTH), lambda i: (nt - 1 - i, 0)), wspec, vec, vec, vec],
        out_specs=[pl.BlockSpec((tm, 3 * BRANCH_WIDTH), lambda i: (nt - 1 - i, 0)), wspec,
                   pl.BlockSpec((8, BRANCH_WIDTH), lambda i: (0, 0))],
        out_shape=[jax.ShapeDtypeStruct((t, 3 * BRANCH_WIDTH), BF16),
                   jax.ShapeDtypeStruct((CONV_HALO, BRANCH_WIDTH), F32),
                   jax.ShapeDtypeStruct((8, BRANCH_WIDTH), F32)],
        scratch_shapes=[pltpu.VMEM((CONV_HALO, BRANCH_WIDTH), F32)],
        compiler_params=_params(1),
    )(c2gc, c2gc, d_out, conv_w, conv_b, ln_g, ln_b)


def _merge_fwd(gmga, ya, yb, yc, *, name):
    t = ya.shape[0]
    tm = _tile(t, ROW_TILE)
    row = pl.BlockSpec((tm, D_MODEL), lambda i: (i, 0))

    def body(g0, g1, g2, a_ref, b_ref, c_ref, o_ref):
        m = _sigmoid(g0[...]) * a_ref[...] + _sigmoid(g1[...]) * b_ref[...] + _sigmoid(g2[...]) * c_ref[...]
        o_ref[...] = m.astype(BF16)

    gspecs = [pl.BlockSpec((tm, D_MODEL), functools.partial(lambda i, b: (i, b), b=b)) for b in range(3)]
    return pl.pallas_call(
        body, name=name, grid=(t // tm,),
        in_specs=gspecs + [row, row, row], out_specs=row,
        out_shape=jax.ShapeDtypeStruct((t, D_MODEL), BF16),
        compiler_params=_params(1),
    )(gmga, gmga, gmga, ya, yb, yc)


def _merge_bwd(dm, gmga, ya, yb, yc, *, name):
    t = ya.shape[0]
    tm = _tile(t, ROW_TILE)
    row = pl.BlockSpec((tm, D_MODEL), lambda i: (i, 0))
    wide = pl.BlockSpec((tm, 3 * D_MODEL), lambda i: (i, 0))

    def body(dm_ref, g0, g1, g2, a_ref, b_ref, c_ref, da_ref, db_ref, dc_ref, dg_ref):
        dmv = dm_ref[...]
        for k, (g_ref, y_ref, dy_ref) in enumerate(((g0, a_ref, da_ref), (g1, b_ref, db_ref), (g2, c_ref, dc_ref))):
            s = _sigmoid(g_ref[...])
            dy_ref[...] = (dmv * s).astype(BF16)
            dg_ref[:, k * D_MODEL:(k + 1) * D_MODEL] = (dmv * y_ref[...] * s * (1.0 - s)).astype(BF16)

    gspecs = [pl.BlockSpec((tm, D_MODEL), functools.partial(lambda i, b: (i, b), b=b)) for b in range(3)]
    return pl.pallas_call(
        body, name=name, grid=(t // tm,),
        in_specs=[row] + gspecs + [row, row, row], out_specs=[row, row, row, wide],
        out_shape=[jax.ShapeDtypeStruct((t, D_MODEL), BF16)] * 3 + [jax.ShapeDtypeStruct((t, 3 * D_MODEL), BF16)],
        compiler_params=_params(1),
    )(dm, gmga, gmga, gmga, ya, yb, yc)


GA_BLOCK = 3 * D_MODEL // HEAD_PAIR


def _split_heads(x, lane_is_first):
    zero = jnp.zeros_like(x)
    return jnp.concatenate([jnp.where(lane_is_first, x, zero), jnp.where(lane_is_first, zero, x)], axis=0)


def _side_by_side(x, rows):
    return jnp.concatenate([x[:rows], x[rows:]], axis=1)


def _split_bf16(x):
    hi = x.astype(BF16)
    return hi, (x - hi.astype(F32)).astype(BF16)


def _scores(qcat, kblk, mask):
    z = _dot_nt(qcat, kblk)
    e = jnp.exp(-jnp.abs(z))
    sp = jnp.maximum(z, 0.0) + jnp.log(1.0 + e)
    l1m = -sp
    if mask is not None:
        l1m = jnp.where(mask, l1m, 0.0)
    inv = 1.0 / (1.0 + e)
    pos = z >= 0.0
    return z - sp, l1m, jnp.where(pos, 1.0, e) * inv, jnp.where(pos, e, 1.0) * inv


def _attn_consts(blk):
    lane_is_first = lax.broadcasted_iota(jnp.int32, (1, HEAD_PAIR), 1) < HEAD_DIM
    r = lax.broadcasted_iota(jnp.int32, (blk, blk), 0)
    c = lax.broadcasted_iota(jnp.int32, (blk, blk), 1)
    after = (r > c).astype(BF16)
    from_here = (r >= c).astype(BF16)
    qrow = lax.broadcasted_iota(jnp.int32, (2 * blk, blk), 0)
    qrow = jnp.where(qrow >= blk, qrow - blk, qrow)
    causal = lax.broadcasted_iota(jnp.int32, (2 * blk, blk), 1) < qrow
    return lane_is_first, after, from_here, causal


def _while_mass_left(qi, carry, block):
    def alive(c):
        return jnp.max(c[0]) > LOG_F32_ZERO

    def cond(state):
        return jnp.logical_and(state[0] < qi, state[1])

    def step(state):
        new = block(qi - 1 - state[0], state[2])
        return state[0] + 1, alive(new), new

    return lax.while_loop(cond, step, (jnp.int32(0), alive(carry), carry))[2]


def _attn_fwd(qkv, gmga, *, name):
    t = qkv.shape[0]
    blk = _tile(t, ATTN_BLOCK)
    nq = t // blk

    def body(q_ref, k_ref, v_ref, ga_ref, o_ref, cv_ref):
        qi = pl.program_id(1)
        lane_is_first, after, _, causal = _attn_consts(blk)
        qcat = _split_heads(q_ref[...], lane_is_first)

        def block(kb, carry, mask):
            run, acc = carry
            rows = pl.ds(pl.multiple_of(kb * blk, blk), blk)
            lb, l1m, _, _ = _scores(qcat, k_ref[rows, :], mask)
            hi, lo = _split_bf16(l1m)
            w = jnp.exp(lb + (_dot(hi, after) + _dot(lo, after) + run))
            if mask is not None:
                w = jnp.where(mask, w, 0.0)
            vcat = _split_heads(v_ref[rows, :], lane_is_first)
            acc = acc + _dot(_side_by_side(w.astype(BF16), blk), vcat)
            return run + jnp.sum(l1m, axis=-1, keepdims=True), acc

        carry = block(qi, (jnp.zeros((2 * blk, 1), F32), jnp.zeros((blk, HEAD_PAIR), F32)), causal)
        _, o = _while_mass_left(qi, carry, lambda kb, c: block(kb, c, None))
        o_ref[...] = o
        ga = ga_ref[...]
        cv_ref[...] = (o * (ga * _sigmoid(ga))).astype(BF16)

    qspec = pl.BlockSpec((blk, HEAD_PAIR), lambda p, i: (i, p))
    return pl.pallas_call(
        body, name=name, grid=(N_HEAD_PAIRS, nq),
        in_specs=[qspec,
                  pl.BlockSpec((t, HEAD_PAIR), lambda p, i: (0, N_HEAD_PAIRS + p)),
                  pl.BlockSpec((t, HEAD_PAIR), lambda p, i: (0, 2 * N_HEAD_PAIRS + p)),
                  pl.BlockSpec((blk, HEAD_PAIR), lambda p, i: (i, GA_BLOCK + p))],
        out_specs=[qspec, qspec],
        out_shape=[jax.ShapeDtypeStruct((t, BRANCH_WIDTH), F32), jax.ShapeDtypeStruct((t, BRANCH_WIDTH), BF16)],
        compiler_params=_params(2),
    )(qkv, qkv, qkv, gmga)


def _attn_bwd(qkv, o, gmga, dcv, *, name):
    t = qkv.shape[0]
    blk = _tile(t, ATTN_BLOCK)
    nq = t // blk

    def body(q_ref, k_ref, v_ref, o_ref, ga_ref, dcv_ref, dq_ref, dk_ref, dv_ref, dga_ref, dk_acc, dv_acc):
        qi = pl.program_id(1)
        lane_is_first, after, from_here, causal = _attn_consts(blk)

        @pl.when(qi == 0)
        def _():
            dk_acc[...] = jnp.zeros_like(dk_acc)
            dv_acc[...] = jnp.zeros_like(dv_acc)

        ga, ov, dcvv = ga_ref[...], o_ref[...], dcv_ref[...]
        sg = _sigmoid(ga)
        dob = (dcvv * (ga * sg)).astype(BF16)
        dga_ref[...] = (dcvv * ov * _silu_grad(ga, sg)).astype(BF16)
        gt = dob.astype(F32) * ov
        g_total = jnp.concatenate(
            [jnp.sum(jnp.where(lane_is_first, gt, 0.0), axis=-1, keepdims=True),
             jnp.sum(jnp.where(lane_is_first, 0.0, gt), axis=-1, keepdims=True)], axis=0)
        qcat = _split_heads(q_ref[...], lane_is_first)
        docat = _split_heads(dob, lane_is_first)

        def block(kb, carry, mask):
            run, g_run, dq = carry
            rows = pl.ds(pl.multiple_of(kb * blk, blk), blk)
            kblk = k_ref[rows, :]
            lb, l1m, sig, one_m_sig = _scores(qcat, kblk, mask)
            hi, lo = _split_bf16(l1m)
            w = jnp.exp(lb + (_dot(hi, after) + _dot(lo, after) + run))
            if mask is not None:
                w = jnp.where(mask, w, 0.0)
            wb = w.astype(BF16)
            g = _dot_nt(docat, v_ref[rows, :]) * wb.astype(F32)
            ghi, glo = _split_bf16(g)
            g_before = g_total - g_run - (_dot(ghi, from_here) + _dot(glo, from_here))
            dz = g * one_m_sig - g_before * sig
            if mask is not None:
                dz = jnp.where(mask, dz, 0.0)
            dzb = dz.astype(BF16)
            dq = dq + _dot(_side_by_side(dzb, blk), _split_heads(kblk, lane_is_first))
            dk_acc[rows, :] += _dot_tn(dzb, qcat)
            dv_acc[rows, :] += _dot_tn(wb, docat)
            return (run + jnp.sum(l1m, axis=-1, keepdims=True), g_run + jnp.sum(g, axis=-1, keepdims=True), dq)

        zero = jnp.zeros((2 * blk, 1), F32)
        carry = block(qi, (zero, zero, jnp.zeros((blk, HEAD_PAIR), F32)), causal)
        _, _, dq = _while_mass_left(qi, carry, lambda kb, c: block(kb, c, None))
        dq_ref[...] = (dq * ATTN_SCALE).astype(BF16)

        @pl.when(qi == nq - 1)
        def _():
            dk_ref[...] = dk_acc[...].astype(BF16)
            dv_ref[...] = dv_acc[...].astype(BF16)

    qspec = pl.BlockSpec((blk, HEAD_PAIR), lambda p, i: (i, p))
    whole = pl.BlockSpec((t, HEAD_PAIR), lambda p, i: (0, p))
    out = jax.ShapeDtypeStruct((t, BRANCH_WIDTH), BF16)
    return pl.pallas_call(
        body, name=name, grid=(N_HEAD_PAIRS, nq),
        in_specs=[qspec,
                  pl.BlockSpec((t, HEAD_PAIR), lambda p, i: (0, N_HEAD_PAIRS + p)),
                  pl.BlockSpec((t, HEAD_PAIR), lambda p, i: (0, 2 * N_HEAD_PAIRS + p)),
                  qspec,
                  pl.BlockSpec((blk, HEAD_PAIR), lambda p, i: (i, GA_BLOCK + p)),
                  qspec],
        out_specs=[qspec, whole, whole, qspec],
        out_shape=[out, out, out, out],
        scratch_shapes=[pltpu.VMEM((t, HEAD_PAIR), F32), pltpu.VMEM((t, HEAD_PAIR), F32)],
        compiler_params=_params(2),
    )(qkv, qkv, qkv, o, gmga, dcv)


def _mesh_position():
    x, y, c = lax.axis_index("x"), lax.axis_index("y"), lax.axis_index("c")
    return x, y, c, 4 * x + 2 * y + c


def _flipped(x, y, c, k):
    return (1 - x if k & 4 else x, 1 - y if k & 2 else y, 1 - c if k & 1 else c)


def _all_to_all(srcs, *, name, same_block):
    n = len(srcs)

    def body(*refs):
        src_refs, dst_refs = refs[:n], refs[n:2 * n]
        send_sems, recv_sems, local_sems = refs[2 * n:]
        x, y, c, me = _mesh_position()

        def outgoing(i, j):
            return src_refs[i] if same_block else src_refs[i].at[j]

        def remote(i, k, slot):
            return pltpu.make_async_remote_copy(
                src_ref=outgoing(i, jnp.bitwise_xor(me, k)), dst_ref=dst_refs[i].at[slot],
                send_sem=send_sems.at[k - 1, i], recv_sem=recv_sems.at[k - 1, i],
                device_id=_flipped(x, y, c, k), device_id_type=MESH)

        mine = [pltpu.make_async_copy(outgoing(i, me), dst_refs[i].at[me], local_sems.at[i]) for i in range(n)]
        for cp in mine:
            cp.start()
        sent = [remote(i, k, me) for k in range(1, N_DEV) for i in range(n)]
        for cp in sent:
            cp.start()
        for k in range(1, N_DEV):
            for i in range(n):
                remote(i, k, jnp.bitwise_xor(me, k)).wait_recv()
        for cp in sent:
            cp.wait_send()
        for cp in mine:
            cp.wait()

    shapes = [s.shape if same_block else s.shape[1:] for s in srcs]
    return pl.pallas_call(
        body, name=name,
        in_specs=[pl.BlockSpec(memory_space=pl.ANY)] * n, out_specs=[pl.BlockSpec(memory_space=pl.ANY)] * n,
        out_shape=[jax.ShapeDtypeStruct((N_DEV,) + tuple(sh), s.dtype) for sh, s in zip(shapes, srcs)],
        scratch_shapes=[pltpu.SemaphoreType.DMA((N_DEV - 1, n)), pltpu.SemaphoreType.DMA((N_DEV - 1, n)),
                        pltpu.SemaphoreType.DMA((n,))],
    )(*srcs)


def _adamw(parts, w, m, v, *, name):
    rows, cols = w.shape
    tr = rows
    while tr * cols > ADAM_TILE_ELEMS and tr % 32 == 0:
        tr //= 2
    row = pl.BlockSpec((tr, cols), lambda i: (i, 0))

    def body(p_ref, w_ref, m_ref, v_ref, g_ref, d_ref, nm_ref, nv_ref):
        g = p_ref[0].astype(F32)
        for k in range(1, N_DEV):
            g = g + p_ref[k].astype(F32)
        m2 = ADAM_B1 * m_ref[...] + (1.0 - ADAM_B1) * g
        v2 = ADAM_B2 * v_ref[...] + (1.0 - ADAM_B2) * (g * g)
        m_hat = m2 / (1.0 - ADAM_B1 ** ADAM_STEP)
        v_hat = v2 / (1.0 - ADAM_B2 ** ADAM_STEP)
        g_ref[...] = g
        d_ref[...] = -ADAM_LR * (m_hat / (jnp.sqrt(v_hat) + ADAM_EPS) + ADAM_WD * w_ref[...])
        nm_ref[...] = m2
        nv_ref[...] = v2

    out = jax.ShapeDtypeStruct((rows, cols), F32)
    return pl.pallas_call(
        body, name=name, grid=(rows // tr,),
        in_specs=[pl.BlockSpec((N_DEV, tr, cols), lambda i: (0, i, 0)), row, row, row],
        out_specs=[row, row, row, row], out_shape=[out, out, out, out],
        compiler_params=_params(1),
    )(parts, w, m, v)


MATMUL_WEIGHTS = ("w_in", "w_pool_out", "w_conv_out", "w_attn_out", "w_o")
SMALL = ("conv_w", "norm_pre", "pool_w", "pool_b", "pool_scale", "conv_b", "conv_ln_g", "conv_ln_b", "norm_post")
WEIGHT_ORDER = ("norm_pre", "w_in", "pool_w", "pool_b", "pool_scale", "w_pool_out", "conv_w", "conv_b",
                "conv_ln_g", "conv_ln_b", "w_conv_out", "w_attn_out", "w_o", "norm_post")


def _shard_axis(name):
    return 1 if name == "w_o" else 2


def _pack_rows(flat_parts, row_multiple):
    flat = jnp.concatenate(flat_parts, axis=-1)
    n = flat.shape[-1]
    chunk = row_multiple * LANES
    total = -(-n // chunk) * chunk
    pad = [(0, 0)] * (flat.ndim - 1) + [(0, total - n)]
    return jnp.pad(flat, pad).reshape(flat.shape[:-1] + (total // LANES, LANES))


def _unpack(buf, shapes):
    flat = buf.reshape(-1)
    out, at = {}, 0
    for name, shape in shapes:
        n = 1
        for s in shape:
            n *= s
        out[name] = flat[at:at + n].reshape(shape)
        at += n
    return out


def _to_dest_major(name, full):
    axis = _shard_axis(name)
    n = full.shape[axis] // N_DEV
    return jnp.stack([lax.slice_in_dim(full, d * n, (d + 1) * n, axis=axis) for d in range(N_DEV)])


def _from_source_major(name, gathered):
    return jnp.concatenate([gathered[d] for d in range(N_DEV)], axis=_shard_axis(name))


def _rows_2d(a):
    return a.reshape(-1, a.shape[-1])


def kernel(x, norm_pre, w_in, pool_w, pool_b, pool_scale, w_pool_out, conv_w, conv_b, conv_ln_g, conv_ln_b, w_conv_out, w_attn_out, w_o, norm_post, loss_target, m_norm_pre, m_w_in, m_pool_w, m_pool_b, m_pool_scale, m_w_pool_out, m_conv_w, m_conv_b, m_conv_ln_g, m_conv_ln_b, m_w_conv_out, m_w_attn_out, m_w_o, m_norm_post, v_norm_pre, v_w_in, v_pool_w, v_pool_b, v_pool_scale, v_w_pool_out, v_conv_w, v_conv_b, v_conv_ln_g, v_conv_ln_b, v_w_conv_out, v_w_attn_out, v_w_o, v_norm_post):
    weights = dict(norm_pre=norm_pre, w_in=w_in, pool_w=pool_w, pool_b=pool_b, pool_scale=pool_scale,
                   w_pool_out=w_pool_out, conv_w=conv_w, conv_b=conv_b, conv_ln_g=conv_ln_g, conv_ln_b=conv_ln_b,
                   w_conv_out=w_conv_out, w_attn_out=w_attn_out, w_o=w_o, norm_post=norm_post)
    mom1 = dict(norm_pre=m_norm_pre, w_in=m_w_in, pool_w=m_pool_w, pool_b=m_pool_b, pool_scale=m_pool_scale,
                w_pool_out=m_w_pool_out, conv_w=m_conv_w, conv_b=m_conv_b, conv_ln_g=m_conv_ln_g, conv_ln_b=m_conv_ln_b,
                w_conv_out=m_w_conv_out, w_attn_out=m_w_attn_out, w_o=m_w_o, norm_post=m_norm_post)
    mom2 = dict(norm_pre=v_norm_pre, w_in=v_w_in, pool_w=v_pool_w, pool_b=v_pool_b, pool_scale=v_pool_scale,
                w_pool_out=v_w_pool_out, conv_w=v_conv_w, conv_b=v_conv_b, conv_ln_g=v_conv_ln_g, conv_ln_b=v_conv_ln_b,
                w_conv_out=v_w_conv_out, w_attn_out=v_w_attn_out, w_o=v_w_o, norm_post=v_norm_post)
    xs = x[0]
    target = loss_target[0]

    conv_rows = jnp.pad(conv_w, ((0, 0), (0, CONV_HALO - CONV_KERNEL), (0, 0)))
    gathered = _all_to_all([weights[n].astype(BF16) for n in MATMUL_WEIGHTS] + [conv_rows],
                           name="gather_weights", same_block=True)
    full = {n: _from_source_major(n, g) for n, g in zip(MATMUL_WEIGHTS, gathered)}
    conv_full = _from_source_major("conv_w", gathered[-1])

    def in_sections(w):
        return dict(pg=w[:, 0:1024], c2gc=w[:, 1024:2560], q=w[:, 2560:3072], k=w[:, 3072:3584], v=w[:, 3584:4096],
                    gmga=jnp.concatenate([w[:, 4608:7680], w[:, 4096:4608]], axis=1))

    saved = []
    cur = xs
    for l in range(DEPTH):
        sec = in_sections(full["w_in"][l])
        w_qkv = jnp.concatenate([sec["q"] * ATTN_SCALE, sec["k"], sec["v"]], axis=1)
        pw = pool_w[l].astype(BF16)
        pb, ps = pool_b[l].reshape(1, -1), pool_scale[l].reshape(1, -1)
        cb, lg, lb = conv_b[l].reshape(1, -1), conv_ln_g[l].reshape(1, -1), conv_ln_b[l].reshape(1, -1)
        h = _rms_fwd(cur, norm_pre[l].reshape(1, -1), name=f"rms_pre_fwd_{l}")
        pg = _matmul(h, sec["pg"], mode="nn", name=f"proj_pg_{l}")
        c2gc = _matmul(h, sec["c2gc"], mode="nn", name=f"proj_c2gc_{l}")
        qkv = _matmul(h, w_qkv, mode="nn", name=f"proj_qkv_{l}", out_dtype=BF16)
        gmga = _matmul(h, sec["gmga"], mode="nn", name=f"proj_gmga_{l}")
        a_act = _pool_fwd(pg, pw, pb, ps, name=f"pool_fwd_{l}")
        b_act = _conv_fwd(c2gc, conv_full[l], cb, lg, lb, name=f"conv_fwd_{l}")
        o, c_act = _attn_fwd(qkv, gmga, name=f"attn_fwd_{l}")
        ya = _matmul(a_act, full["w_pool_out"][l], mode="nn", name=f"out_pool_{l}")
        yb = _matmul(b_act, full["w_conv_out"][l], mode="nn", name=f"out_conv_{l}")
        yc = _matmul(c_act, full["w_attn_out"][l], mode="nn", name=f"out_attn_{l}")
        mix = _merge_fwd(gmga, ya, yb, yc, name=f"merge_fwd_{l}")
        out = _matmul(mix, full["w_o"][l], mode="nn", name=f"out_proj_{l}")
        nxt = _rms_fwd(out, norm_post[l].reshape(1, -1), name=f"rms_post_fwd_{l}", resid=cur)
        saved.append(dict(x=cur, h=h, pg=pg, c2gc=c2gc, qkv=qkv, gmga=gmga, a=a_act, b=b_act, c=c_act, o=o,
                          ya=ya, yb=yb, yc=yc, mix=mix, out=out, sec=sec, pw=pw, pb=pb, ps=ps, cb=cb, lg=lg, lb=lb))
        cur = nxt

    loss_tile, dx = _loss_head(cur, target, name="loss_head")
    loss = lax.psum(loss_tile[0, 0], ("x", "y", "c"))

    grads = {n: [None] * DEPTH for n in WEIGHT_ORDER}
    for l in reversed(range(DEPTH)):
        s = saved[l]
        dout, grads["norm_post"][l] = _rms_bwd(s["out"], norm_post[l].reshape(1, -1), dx, name=f"rms_post_bwd_{l}")
        dmix = _matmul(dout, full["w_o"][l], mode="nt", name=f"d_mix_{l}")
        grads["w_o"][l] = _matmul(s["mix"], dout, mode="tn", name=f"d_w_o_{l}", out_dtype=BF16)
        dya, dyb, dyc, dgm = _merge_bwd(dmix, s["gmga"], s["ya"], s["yb"], s["yc"], name=f"merge_bwd_{l}")
        da = _matmul(dya, full["w_pool_out"][l], mode="nt", name=f"d_pool_act_{l}")
        grads["w_pool_out"][l] = _matmul(s["a"], dya, mode="tn", name=f"d_w_pool_out_{l}", out_dtype=BF16)
        db = _matmul(dyb, full["w_conv_out"][l], mode="nt", name=f"d_conv_act_{l}")
        grads["w_conv_out"][l] = _matmul(s["b"], dyb, mode="tn", name=f"d_w_conv_out_{l}", out_dtype=BF16)
        dc = _matmul(dyc, full["w_attn_out"][l], mode="nt", name=f"d_attn_act_{l}")
        grads["w_attn_out"][l] = _matmul(s["c"], dyc, mode="tn", name=f"d_w_attn_out_{l}", out_dtype=BF16)
        dq, dk, dv, dga = _attn_bwd(s["qkv"], s["o"], s["gmga"], dc, name=f"attn_bwd_{l}")
        dc2gc, dcw, dcvec = _conv_bwd(s["c2gc"], db, conv_full[l], s["cb"], s["lg"], s["lb"], name=f"conv_bwd_{l}")
        dpg, dpw, dpvec = _pool_bwd(s["pg"], da, s["pw"], s["pb"], s["ps"], name=f"pool_bwd_{l}")
        grads["conv_w"][l] = dcw[:CONV_KERNEL]
        grads["conv_b"][l], grads["conv_ln_g"][l], grads["conv_ln_b"][l] = dcvec[0], dcvec[1], dcvec[2]
        grads["pool_w"][l] = dpw
        grads["pool_b"][l] = dpvec[0].reshape(4, POOL_GROUP_DIM)
        grads["pool_scale"][l] = dpvec[1]
        dsec = dict(pg=dpg, c2gc=dc2gc, q=dq, k=dk, v=dv, gmga=jnp.concatenate([dgm, dga], axis=1))
        dh, dws = None, {}
        for n in ("pg", "c2gc", "q", "k", "v", "gmga"):
            dh = _matmul(dsec[n], s["sec"][n], mode="nt", name=f"d_h_{n}_{l}", acc=dh)
            dws[n] = _matmul(s["h"], dsec[n], mode="tn", name=f"d_w_in_{n}_{l}", out_dtype=BF16)
        grads["w_in"][l] = jnp.concatenate(
            [dws["pg"], dws["c2gc"], dws["q"], dws["k"], dws["v"], dws["gmga"][:, 3 * D_MODEL:], dws["gmga"][:, :3 * D_MODEL]],
            axis=1)
        dx, dg_pre = _rms_bwd(s["x"], norm_pre[l].reshape(1, -1), dh, name=f"rms_pre_bwd_{l}", resid=dx)
        grads["norm_pre"][l] = dg_pre.reshape(-1)
        grads["norm_post"][l] = grads["norm_post"][l].reshape(-1)
    grads = {n: jnp.stack(g) for n, g in grads.items()}

    replicated = jnp.concatenate([grads[n].reshape(-1) for n in SMALL[1:]])
    small = _pack_rows([_to_dest_major("conv_w", grads["conv_w"]).reshape(N_DEV, -1),
                        jnp.broadcast_to(replicated, (N_DEV, replicated.size))], 16).astype(BF16)
    send = [_to_dest_major(n, grads[n]) for n in MATMUL_WEIGHTS] + [small]
    parts = _all_to_all(send, name="exchange_grads", same_block=False)

    outs = [dict(), dict(), dict(), dict()]
    for n, p in zip(MATMUL_WEIGHTS, parts):
        res = _adamw(p.reshape((N_DEV, -1, p.shape[-1])), _rows_2d(weights[n]), _rows_2d(mom1[n]), _rows_2d(mom2[n]),
                     name=f"adamw_{n}")
        for o, r in zip(outs, res):
            o[n] = r.reshape(weights[n].shape)

    def packed(tree):
        return _pack_rows([tree[n].reshape(-1) for n in SMALL], 16)

    shapes = [(n, weights[n].shape) for n in SMALL]
    res = _adamw(parts[-1], packed(weights), packed(mom1), packed(mom2), name="adamw_small")
    for o, r in zip(outs, res):
        o.update(_unpack(r, shapes))
    return (loss, dx[None], *[o[n] for o in outs for n in WEIGHT_ORDER])
```

```python
import functools

import jax
import jax.numpy as jnp
from jax import lax
from jax.experimental import pallas as pl
from jax.experimental.pallas import tpu as pltpu

F32 = jnp.float32
BF16 = jnp.bfloat16

D_MODEL = 1024
DEPTH = 2
POOL_WINDOWS = (2, 4, 8, 16)
POOL_GROUP_DIM = 128
BRANCH_WIDTH = 512
CONV_KERNEL = 31
CONV_HALO = 32
POOL_HALO = 16
HEAD_DIM = 64
HEAD_PAIR = 128
N_HEAD_PAIRS = 4
ATTN_SCALE = 0.125
LOG_F32_ZERO = -104.0
RMS_EPS = 1e-6
LN_EPS = 1e-5
N_DEV = 8
LANES = 128

ADAM_LR = 0.001
ADAM_B1 = 0.9
ADAM_B2 = 0.999
ADAM_EPS = 1e-08
ADAM_WD = 0.01
ADAM_STEP = 10

ROW_TILE = 256
ATTN_BLOCK = 256
MM_TILE = 1024
ADAM_TILE_ELEMS = 256 * 1024
VMEM_LIMIT = 48 * 1024 * 1024

MESH = pl.DeviceIdType.MESH


def _params(n_axes):
    return pltpu.CompilerParams(dimension_semantics=("arbitrary",) * n_axes, vmem_limit_bytes=VMEM_LIMIT)


def _tile(n, pref):
    if n <= pref:
        return n
    t = (pref // LANES) * LANES
    while n % t:
        t -= LANES
    return t


def _dot(a, b):
    return jnp.dot(a, b, preferred_element_type=F32)


def _dot_nt(a, b):
    return lax.dot_general(a, b, (((1,), (1,)), ((), ())), preferred_element_type=F32)


def _dot_tn(a, b):
    return lax.dot_general(a, b, (((0,), (0,)), ((), ())), preferred_element_type=F32)


def _sigmoid(x):
    return 1.0 / (1.0 + jnp.exp(-x))


def _silu_grad(x, s):
    return s * (1.0 + x * (1.0 - s))


def _matmul(a, b, *, mode, name, out_dtype=F32, acc=None):
    if mode == "nn":
        (m, k), n = a.shape, b.shape[1]
    elif mode == "nt":
        (m, k), n = a.shape, b.shape[0]
    else:
        (k, m), n = a.shape, b.shape[1]
    tm, tn, tk = _tile(m, MM_TILE), _tile(n, MM_TILE), _tile(k, MM_TILE)
    nk = k // tk
    dot = {"nn": _dot, "nt": _dot_nt, "tn": _dot_tn}[mode]
    a_spec = pl.BlockSpec((tk, tm), lambda i, j, kk: (kk, i)) if mode == "tn" else pl.BlockSpec((tm, tk), lambda i, j, kk: (i, kk))
    b_spec = pl.BlockSpec((tn, tk), lambda i, j, kk: (j, kk)) if mode == "nt" else pl.BlockSpec((tk, tn), lambda i, j, kk: (kk, j))
    o_spec = pl.BlockSpec((tm, tn), lambda i, j, kk: (i, j))
    has_acc = acc is not None

    def body(*refs):
        a_ref, b_ref = refs[0], refs[1]
        acc_in = refs[2] if has_acc else None
        o_ref = refs[3] if has_acc else refs[2]
        part = dot(a_ref[...], b_ref[...])
        if nk == 1:
            if has_acc:
                part = part + acc_in[...]
            o_ref[...] = part.astype(out_dtype)
            return
        scr = refs[-1]
        kk = pl.program_id(2)

        @pl.when(kk == 0)
        def _():
            scr[...] = part + acc_in[...] if has_acc else part

        @pl.when(kk > 0)
        def _():
            scr[...] += part

        @pl.when(kk == nk - 1)
        def _():
            o_ref[...] = scr[...].astype(out_dtype)

    in_specs = [a_spec, b_spec] + ([o_spec] if has_acc else [])
    args = (a, b) + ((acc,) if has_acc else ())
    return pl.pallas_call(
        body, name=name, grid=(m // tm, n // tn, nk),
        in_specs=in_specs, out_specs=o_spec,
        out_shape=jax.ShapeDtypeStruct((m, n), out_dtype),
        scratch_shapes=[pltpu.VMEM((tm, tn), F32)] if nk > 1 else [],
        compiler_params=_params(3),
    )(*args)


def _rms_fwd(x, g, *, name, resid=None):
    t = x.shape[0]
    tm = _tile(t, ROW_TILE)
    row = pl.BlockSpec((tm, D_MODEL), lambda i: (i, 0))
    vec = pl.BlockSpec((1, D_MODEL), lambda i: (0, 0))
    has_resid = resid is not None

    def body(*refs):
        x_ref, g_ref = refs[0], refs[1]
        o_ref = refs[-1]
        xv = x_ref[...]
        y = xv * lax.rsqrt(jnp.mean(xv * xv, axis=-1, keepdims=True) + RMS_EPS) * g_ref[...]
        if has_resid:
            o_ref[...] = refs[2][...] + y
        else:
            o_ref[...] = y.astype(BF16)

    return pl.pallas_call(
        body, name=name, grid=(t // tm,),
        in_specs=[row, vec] + ([row] if has_resid else []), out_specs=row,
        out_shape=jax.ShapeDtypeStruct((t, D_MODEL), F32 if has_resid else BF16),
        compiler_params=_params(1),
    )(*((x, g) + ((resid,) if has_resid else ())))


def _rms_bwd(xin, g, dy, *, name, resid=None):
    t = xin.shape[0]
    tm = _tile(t, ROW_TILE)
    row = pl.BlockSpec((tm, D_MODEL), lambda i: (i, 0))
    vec = pl.BlockSpec((1, D_MODEL), lambda i: (0, 0))
    has_resid = resid is not None
    out_dtype = F32 if has_resid else BF16

    def body(*refs):
        x_ref, g_ref, dy_ref = refs[0], refs[1], refs[2]
        dx_ref, dg_ref = refs[-2], refs[-1]
        xv, dyv = x_ref[...], dy_ref[...]
        r = lax.rsqrt(jnp.mean(xv * xv, axis=-1, keepdims=True) + RMS_EPS)
        a = dyv * g_ref[...]
        dx = r * a - xv * (r * r * r) * jnp.mean(a * xv, axis=-1, keepdims=True)
        if has_resid:
            dx = dx + refs[3][...]
        dx_ref[...] = dx.astype(out_dtype)
        part = jnp.sum(dyv * xv * r, axis=0, keepdims=True)

        @pl.when(pl.program_id(0) == 0)
        def _():
            dg_ref[...] = part

        @pl.when(pl.program_id(0) > 0)
        def _():
            dg_ref[...] += part

    return pl.pallas_call(
        body, name=name, grid=(t // tm,),
        in_specs=[row, vec, row] + ([row] if has_resid else []), out_specs=[row, vec],
        out_shape=[jax.ShapeDtypeStruct((t, D_MODEL), out_dtype), jax.ShapeDtypeStruct((1, D_MODEL), F32)],
        compiler_params=_params(1),
    )(*((xin, g, dy) + ((resid,) if has_resid else ())))


def _loss_head(x, target, *, name):
    t = x.shape[0]
    tm = _tile(t, ROW_TILE)
    row = pl.BlockSpec((tm, D_MODEL), lambda i: (i, 0))
    acc = pl.BlockSpec((8, LANES), lambda i: (0, 0))

    def body(x_ref, t_ref, l_ref, dx_ref):
        diff = x_ref[...] - t_ref[...]
        dx_ref[...] = diff * (1.0 / D_MODEL)
        part = 0.5 * jnp.sum(jnp.mean(diff * diff, axis=-1, keepdims=True), axis=0, keepdims=True)

        @pl.when(pl.program_id(0) == 0)
        def _():
            l_ref[...] = jnp.zeros((8, LANES), F32) + part

        @pl.when(pl.program_id(0) > 0)
        def _():
            l_ref[...] += part

    return pl.pallas_call(
        body, name=name, grid=(t // tm,),
        in_specs=[row, row], out_specs=[acc, row],
        out_shape=[jax.ShapeDtypeStruct((8, LANES), F32), jax.ShapeDtypeStruct((t, D_MODEL), F32)],
        compiler_params=_params(1),
    )(x, target)


def _window_sum(ext, n_doublings, forward):
    rows = ext.shape[0]
    s, sh = ext, 1
    for _ in range(n_doublings):
        s = s + pltpu.roll(s, sh if forward else rows - sh, 0)
        sh *= 2
    return s


def _pool_fwd(pg, pool_w, pool_b, pool_scale, *, name):
    t = pg.shape[0]
    tm = _tile(t, ROW_TILE)

    def body(pg_ref, w_ref, b_ref, s_ref, o_ref, halo):
        i = pl.program_id(0)

        @pl.when(i == 0)
        def _():
            halo[...] = jnp.zeros_like(halo)

        p = pg_ref[:, :BRANCH_WIDTH]
        gate = pg_ref[:, BRANCH_WIDTH:]
        ext = jnp.concatenate([halo[...], p], axis=0)
        pos = i * tm + lax.broadcasted_iota(jnp.int32, (tm, 1), 0)
        outs = []
        for g, w in enumerate(POOL_WINDOWS):
            cols = slice(g * POOL_GROUP_DIM, (g + 1) * POOL_GROUP_DIM)
            cnt = jnp.minimum(pos + 1, w).astype(F32)
            d = _window_sum(ext[:, cols], g + 1, True)[POOL_HALO:] / cnt - p[:, cols]
            y = (_dot(d.astype(BF16), w_ref[g]) + b_ref[:, cols]) * s_ref[:, cols]
            gg = gate[:, cols]
            outs.append(y * (gg * _sigmoid(gg)))
        o_ref[...] = jnp.concatenate(outs, axis=1).astype(BF16)
        halo[...] = p[tm - POOL_HALO:, :]

    vec = pl.BlockSpec((1, BRANCH_WIDTH), lambda i: (0, 0))
    return pl.pallas_call(
        body, name=name, grid=(t // tm,),
        in_specs=[pl.BlockSpec((tm, 2 * BRANCH_WIDTH), lambda i: (i, 0)),
                  pl.BlockSpec((4, POOL_GROUP_DIM, POOL_GROUP_DIM), lambda i: (0, 0, 0)), vec, vec],
        out_specs=pl.BlockSpec((tm, BRANCH_WIDTH), lambda i: (i, 0)),
        out_shape=jax.ShapeDtypeStruct((t, BRANCH_WIDTH), BF16),
        scratch_shapes=[pltpu.VMEM((POOL_HALO, BRANCH_WIDTH), F32)],
        compiler_params=_params(1),
    )(pg, pool_w, pool_b, pool_scale)


def _pool_bwd(pg, d_out, pool_w, pool_b, pool_scale, *, name):
    t = pg.shape[0]
    tm = _tile(t, ROW_TILE)
    nt = t // tm
    halo_per_tile = tm // POOL_HALO

    def body(pg_ref, halo_ref, do_ref, w_ref, b_ref, s_ref, dpg_ref, dw_ref, dvec_ref, carry):
        i = pl.program_id(0)
        ri = nt - 1 - i

        @pl.when(i == 0)
        def _():
            carry[...] = jnp.zeros_like(carry)
            dw_ref[...] = jnp.zeros_like(dw_ref)
            dvec_ref[...] = jnp.zeros_like(dvec_ref)

        p = pg_ref[:, :BRANCH_WIDTH]
        gate = pg_ref[:, BRANCH_WIDTH:]
        hp = jnp.where(ri > 0, halo_ref[:, :BRANCH_WIDTH], 0.0)
        ext = jnp.concatenate([hp, p], axis=0)
        pos = ri * tm + lax.broadcasted_iota(jnp.int32, (tm, 1), 0)
        dps, dgs, dbs, dss = [], [], [], []
        for g, w in enumerate(POOL_WINDOWS):
            cols = slice(g * POOL_GROUP_DIM, (g + 1) * POOL_GROUP_DIM)
            cnt = jnp.minimum(pos + 1, w).astype(F32)
            d = (_window_sum(ext[:, cols], g + 1, True)[POOL_HALO:] / cnt - p[:, cols]).astype(BF16)
            y1 = _dot(d, w_ref[g]) + b_ref[:, cols]
            scale = s_ref[:, cols]
            y2 = y1 * scale
            gg = gate[:, cols]
            sg = _sigmoid(gg)
            do = do_ref[:, cols]
            dy2 = do * (gg * sg)
            dgs.append(do * y2 * _silu_grad(gg, sg))
            dss.append(jnp.sum(dy2 * y1, axis=0, keepdims=True))
            dy1 = dy2 * scale
            dbs.append(jnp.sum(dy1, axis=0, keepdims=True))
            dy1b = dy1.astype(BF16)
            dw_ref[g] += _dot_tn(d, dy1b)
            dd = _dot_nt(dy1b, w_ref[g])
            dpool = dd / cnt
            dext = jnp.concatenate([dpool, carry[:, cols]], axis=0)
            dps.append(_window_sum(dext, g + 1, False)[:tm] - dd)
            carry[:, cols] = dpool[:POOL_HALO]
        dpg_ref[...] = jnp.concatenate(dps + dgs, axis=1).astype(BF16)
        dvec_ref[0:1, :] += jnp.concatenate(dbs, axis=1)
        dvec_ref[1:2, :] += jnp.concatenate(dss, axis=1)

    vec = pl.BlockSpec((1, BRANCH_WIDTH), lambda i: (0, 0))
    wspec = pl.BlockSpec((4, POOL_GROUP_DIM, POOL_GROUP_DIM), lambda i: (0, 0, 0))
    return pl.pallas_call(
        body, name=name, grid=(nt,),
        in_specs=[pl.BlockSpec((tm, 2 * BRANCH_WIDTH), lambda i: (nt - 1 - i, 0)),
                  pl.BlockSpec((POOL_HALO, 2 * BRANCH_WIDTH), lambda i: (jnp.maximum((nt - 1 - i) * halo_per_tile - 1, 0), 0)),
                  pl.BlockSpec((tm, BRANCH_WIDTH), lambda i: (nt - 1 - i, 0)), wspec, vec, vec],
        out_specs=[pl.BlockSpec((tm, 2 * BRANCH_WIDTH), lambda i: (nt - 1 - i, 0)), wspec,
                   pl.BlockSpec((8, BRANCH_WIDTH), lambda i: (0, 0))],
        out_shape=[jax.ShapeDtypeStruct((t, 2 * BRANCH_WIDTH), BF16),
                   jax.ShapeDtypeStruct((4, POOL_GROUP_DIM, POOL_GROUP_DIM), F32),
                   jax.ShapeDtypeStruct((8, BRANCH_WIDTH), F32)],
        scratch_shapes=[pltpu.VMEM((POOL_HALO, BRANCH_WIDTH), F32)],
        compiler_params=_params(1),
    )(pg, pg, d_out, pool_w, pool_b, pool_scale)


def _causal_conv(ext, w_ref, rows):
    acc = None
    for j in range(CONV_KERNEL):
        tap = w_ref[CONV_KERNEL - 1 - j:CONV_KERNEL - j, :]
        shifted = ext if j == 0 else pltpu.roll(ext, j, 0)
        term = tap * shifted[CONV_HALO:]
        acc = term if acc is None else acc + term
    return acc


def _conv_fwd(c2gc, conv_w, conv_b, ln_g, ln_b, *, name):
    t = c2gc.shape[0]
    tm = _tile(t, ROW_TILE)

    def body(c_ref, w_ref, cb_ref, g_ref, b_ref, o_ref, halo):
        @pl.when(pl.program_id(0) == 0)
        def _():
            halo[...] = jnp.zeros_like(halo)

        u = c_ref[:, :BRANCH_WIDTH] * _sigmoid(c_ref[:, BRANCH_WIDTH:2 * BRANCH_WIDTH])
        gate = c_ref[:, 2 * BRANCH_WIDTH:]
        ext = jnp.concatenate([halo[...], u], axis=0)
        cv = _causal_conv(ext, w_ref, tm) + cb_ref[...]
        mu = jnp.mean(cv, axis=-1, keepdims=True)
        xc = cv - mu
        var = jnp.mean(xc * xc, axis=-1, keepdims=True)
        ln = xc * lax.rsqrt(var + LN_EPS) * g_ref[...] + b_ref[...]
        o_ref[...] = (ln * _sigmoid(ln) * (gate * _sigmoid(gate))).astype(BF16)
        halo[...] = u[tm - CONV_HALO:, :]

    vec = pl.BlockSpec((1, BRANCH_WIDTH), lambda i: (0, 0))
    return pl.pallas_call(
        body, name=name, grid=(t // tm,),
        in_specs=[pl.BlockSpec((tm, 3 * BRANCH_WIDTH), lambda i: (i, 0)),
                  pl.BlockSpec((CONV_HALO, BRANCH_WIDTH), lambda i: (0, 0)), vec, vec, vec],
        out_specs=pl.BlockSpec((tm, BRANCH_WIDTH), lambda i: (i, 0)),
        out_shape=jax.ShapeDtypeStruct((t, BRANCH_WIDTH), BF16),
        scratch_shapes=[pltpu.VMEM((CONV_HALO, BRANCH_WIDTH), F32)],
        compiler_params=_params(1),
    )(c2gc, conv_w, conv_b, ln_g, ln_b)


def _conv_bwd(c2gc, d_out, conv_w, conv_b, ln_g, ln_b, *, name):
    t = c2gc.shape[0]
    tm = _tile(t, ROW_TILE)
    nt = t // tm
    halo_per_tile = tm // CONV_HALO
    ext_rows = tm + CONV_HALO

    def body(c_ref, halo_ref, do_ref, w_ref, cb_ref, g_ref, b_ref, dc_ref, dw_ref, dvec_ref, carry):
        i = pl.program_id(0)
        ri = nt - 1 - i

        @pl.when(i == 0)
        def _():
            carry[...] = jnp.zeros_like(carry)
            dw_ref[...] = jnp.zeros_like(dw_ref)
            dvec_ref[...] = jnp.zeros_like(dvec_ref)

        a = c_ref[:, :BRANCH_WIDTH]
        sb = _sigmoid(c_ref[:, BRANCH_WIDTH:2 * BRANCH_WIDTH])
        gate = c_ref[:, 2 * BRANCH_WIDTH:]
        u = a * sb
        hu = halo_ref[:, :BRANCH_WIDTH] * _sigmoid(halo_ref[:, BRANCH_WIDTH:2 * BRANCH_WIDTH])
        ext = jnp.concatenate([jnp.where(ri > 0, hu, 0.0), u], axis=0)
        cv = _causal_conv(ext, w_ref, tm) + cb_ref[...]
        mu = jnp.mean(cv, axis=-1, keepdims=True)
        xc = cv - mu
        rs = lax.rsqrt(jnp.mean(xc * xc, axis=-1, keepdims=True) + LN_EPS)
        n = xc * rs
        ln = n * g_ref[...] + b_ref[...]
        sl = _sigmoid(ln)
        sgate = _sigmoid(gate)
        do = do_ref[...]
        dgate = do * (ln * sl) * _silu_grad(gate, sgate)
        dln = do * (gate * sgate) * _silu_grad(ln, sl)
        dn = dln * g_ref[...]
        dcv = rs * (dn - jnp.mean(dn, axis=-1, keepdims=True) - n * jnp.mean(dn * n, axis=-1, keepdims=True))
        dvec_ref[0:1, :] += jnp.sum(dcv, axis=0, keepdims=True)
        dvec_ref[1:2, :] += jnp.sum(dln * n, axis=0, keepdims=True)
        dvec_ref[2:3, :] += jnp.sum(dln, axis=0, keepdims=True)
        dext = jnp.concatenate([dcv, carry[...]], axis=0)
        du = None
        for j in range(CONV_KERNEL):
            k = CONV_KERNEL - 1 - j
            past = ext if j == 0 else pltpu.roll(ext, j, 0)
            dw_ref[k:k + 1, :] += jnp.sum(dcv * past[CONV_HALO:], axis=0, keepdims=True)
            future = dext if j == 0 else pltpu.roll(dext, ext_rows - j, 0)
            term = w_ref[k:k + 1, :] * future[:tm]
            du = term if du is None else du + term
        dc_ref[...] = jnp.concatenate([du * sb, du * a * sb * (1.0 - sb), dgate], axis=1).astype(BF16)
        carry[...] = dcv[:CONV_HALO]

    vec = pl.BlockSpec((1, BRANCH_WIDTH), lambda i: (0, 0))
    wspec = pl.BlockSpec((CONV_HALO, BRANCH_WIDTH), lambda i: (0, 0))
    return pl.pallas_call(
        body, name=name, grid=(nt,),
        in_specs=[pl.BlockSpec((tm, 3 * BRANCH_WIDTH), lambda i: (nt - 1 - i, 0)),
                  pl.BlockSpec((CONV_HALO, 3 * BRANCH_WIDTH), lambda i: (jnp.maximum((nt - 1 - i) * halo_per_tile - 1, 0), 0)),
                  pl.BlockSpec((tm, BRANCH_WIDTH), lambda i: (nt - 1 - i, 0)), wspec, vec, vec, vec],
        out_specs=[pl.BlockSpec((tm, 3 * BRANCH_WIDTH), lambda i: (nt - 1 - i, 0)), wspec,
                   pl.BlockSpec((8, BRANCH_WIDTH), lambda i: (0, 0))],
        out_shape=[jax.ShapeDtypeStruct((t, 3 * BRANCH_WIDTH), BF16),
                   jax.ShapeDtypeStruct((CONV_HALO, BRANCH_WIDTH), F32),
                   jax.ShapeDtypeStruct((8, BRANCH_WIDTH), F32)],
        scratch_shapes=[pltpu.VMEM((CONV_HALO, BRANCH_WIDTH), F32)],
        compiler_params=_params(1),
    )(c2gc, c2gc, d_out, conv_w, conv_b, ln_g, ln_b)


def _merge_fwd(gmga, ya, yb, yc, *, name):
    t = ya.shape[0]
    tm = _tile(t, ROW_TILE)
    row = pl.BlockSpec((tm, D_MODEL), lambda i: (i, 0))

    def body(g0, g1, g2, a_ref, b_ref, c_ref, o_ref):
        m = _sigmoid(g0[...]) * a_ref[...] + _sigmoid(g1[...]) * b_ref[...] + _sigmoid(g2[...]) * c_ref[...]
        o_ref[...] = m.astype(BF16)

    gspecs = [pl.BlockSpec((tm, D_MODEL), functools.partial(lambda i, b: (i, b), b=b)) for b in range(3)]
    return pl.pallas_call(
        body, name=name, grid=(t // tm,),
        in_specs=gspecs + [row, row, row], out_specs=row,
        out_shape=jax.ShapeDtypeStruct((t, D_MODEL), BF16),
        compiler_params=_params(1),
    )(gmga, gmga, gmga, ya, yb, yc)


def _merge_bwd(dm, gmga, ya, yb, yc, *, name):
    t = ya.shape[0]
    tm = _tile(t, ROW_TILE)
    row = pl.BlockSpec((tm, D_MODEL), lambda i: (i, 0))
    wide = pl.BlockSpec((tm, 3 * D_MODEL), lambda i: (i, 0))

    def body(dm_ref, g0, g1, g2, a_ref, b_ref, c_ref, da_ref, db_ref, dc_ref, dg_ref):
        dmv = dm_ref[...]
        for k, (g_ref, y_ref, dy_ref) in enumerate(((g0, a_ref, da_ref), (g1, b_ref, db_ref), (g2, c_ref, dc_ref))):
            s = _sigmoid(g_ref[...])
            dy_ref[...] = (dmv * s).astype(BF16)
            dg_ref[:, k * D_MODEL:(k + 1) * D_MODEL] = (dmv * y_ref[...] * s * (1.0 - s)).astype(BF16)

    gspecs = [pl.BlockSpec((tm, D_MODEL), functools.partial(lambda i, b: (i, b), b=b)) for b in range(3)]
    return pl.pallas_call(
        body, name=name, grid=(t // tm,),
        in_specs=[row] + gspecs + [row, row, row], out_specs=[row, row, row, wide],
        out_shape=[jax.ShapeDtypeStruct((t, D_MODEL), BF16)] * 3 + [jax.ShapeDtypeStruct((t, 3 * D_MODEL), BF16)],
        compiler_params=_params(1),
    )(dm, gmga, gmga, gmga, ya, yb, yc)


GA_BLOCK = 3 * D_MODEL // HEAD_PAIR


def _split_heads(x, lane_is_first):
    zero = jnp.zeros_like(x)
    return jnp.concatenate([jnp.where(lane_is_first, x, zero), jnp.where(lane_is_first, zero, x)], axis=0)


def _side_by_side(x, rows):
    return jnp.concatenate([x[:rows], x[rows:]], axis=1)


def _split_bf16(x):
    hi = x.astype(BF16)
    return hi, (x - hi.astype(F32)).astype(BF16)


def _scores(qcat, kblk, mask):
    z = _dot_nt(qcat, kblk)
    e = jnp.exp(-jnp.abs(z))
    sp = jnp.maximum(z, 0.0) + jnp.log(1.0 + e)
    l1m = -sp
    if mask is not None:
        l1m = jnp.where(mask, l1m, 0.0)
    inv = 1.0 / (1.0 + e)
    pos = z >= 0.0
    return z - sp, l1m, jnp.where(pos, 1.0, e) * inv, jnp.where(pos, e, 1.0) * inv


def _attn_consts(blk):
    lane_is_first = lax.broadcasted_iota(jnp.int32, (1, HEAD_PAIR), 1) < HEAD_DIM
    r = lax.broadcasted_iota(jnp.int32, (blk, blk), 0)
    c = lax.broadcasted_iota(jnp.int32, (blk, blk), 1)
    after = (r > c).astype(BF16)
    from_here = (r >= c).astype(BF16)
    qrow = lax.broadcasted_iota(jnp.int32, (2 * blk, blk), 0)
    qrow = jnp.where(qrow >= blk, qrow - blk, qrow)
    causal = lax.broadcasted_iota(jnp.int32, (2 * blk, blk), 1) < qrow
    return lane_is_first, after, from_here, causal


def _while_mass_left(qi, carry, block):
    def alive(c):
        return jnp.max(c[0]) > LOG_F32_ZERO

    def cond(state):
        return jnp.logical_and(state[0] < qi, state[1])

    def step(state):
        new = block(qi - 1 - state[0], state[2])
        return state[0] + 1, alive(new), new

    return lax.while_loop(cond, step, (jnp.int32(0), alive(carry), carry))[2]


def _ride(rider, n_in, n_out, refs, grid):
    if rider is None:
        return refs[:n_in], refs[n_in:n_in + n_out], refs[n_in + n_out:], lambda: None
    n = rider.n
    ins, srcs = refs[:n_in], refs[n_in:n_in + n]
    outs, dsts = refs[n_in + n:n_in + n + n_out], refs[n_in + n + n_out:n_in + 2 * n + n_out]
    rest = refs[n_in + 2 * n + n_out:]
    scratch, sems = rest[:len(rest) - 3], rest[len(rest) - 3:]
    step = pl.program_id(0) * grid[1] + pl.program_id(1)

    @pl.when(step == 0)
    def _():
        rider.start(srcs, dsts, sems)

    def finish():
        @pl.when(step == grid[0] * grid[1] - 1)
        def _():
            rider.finish(srcs, dsts, sems)

    return ins, outs, scratch, finish


def _attn_fwd(qkv, gmga, *, name, rider=None):
    t = qkv.shape[0]
    blk = _tile(t, ATTN_BLOCK)
    nq = t // blk

    def body(*refs):
        (q_ref, k_ref, v_ref, ga_ref), (o_ref, cv_ref), _, finish_rider = _ride(rider, 4, 2, refs, (N_HEAD_PAIRS, nq))
        qi = pl.program_id(1)
        lane_is_first, after, _, causal = _attn_consts(blk)
        qcat = _split_heads(q_ref[...], lane_is_first)

        def block(kb, carry, mask):
            run, acc = carry
            rows = pl.ds(pl.multiple_of(kb * blk, blk), blk)
            lb, l1m, _, _ = _scores(qcat, k_ref[rows, :], mask)
            hi, lo = _split_bf16(l1m)
            w = jnp.exp(lb + (_dot(hi, after) + _dot(lo, after) + run))
            if mask is not None:
                w = jnp.where(mask, w, 0.0)
            vcat = _split_heads(v_ref[rows, :], lane_is_first)
            acc = acc + _dot(_side_by_side(w.astype(BF16), blk), vcat)
            return run + jnp.sum(l1m, axis=-1, keepdims=True), acc

        carry = block(qi, (jnp.zeros((2 * blk, 1), F32), jnp.zeros((blk, HEAD_PAIR), F32)), causal)
        _, o = _while_mass_left(qi, carry, lambda kb, c: block(kb, c, None))
        o_ref[...] = o
        ga = ga_ref[...]
        cv_ref[...] = (o * (ga * _sigmoid(ga))).astype(BF16)
        finish_rider()

    qspec = pl.BlockSpec((blk, HEAD_PAIR), lambda p, i: (i, p))
    extra = rider or _NO_RIDER
    return pl.pallas_call(
        body, name=name, grid=(N_HEAD_PAIRS, nq),
        in_specs=[qspec,
                  pl.BlockSpec((t, HEAD_PAIR), lambda p, i: (0, N_HEAD_PAIRS + p)),
                  pl.BlockSpec((t, HEAD_PAIR), lambda p, i: (0, 2 * N_HEAD_PAIRS + p)),
                  pl.BlockSpec((blk, HEAD_PAIR), lambda p, i: (i, GA_BLOCK + p))] + extra.specs,
        out_specs=[qspec, qspec] + extra.specs,
        out_shape=[jax.ShapeDtypeStruct((t, BRANCH_WIDTH), F32), jax.ShapeDtypeStruct((t, BRANCH_WIDTH), BF16)] + extra.out_shape,
        scratch_shapes=extra.scratch_shapes,
        compiler_params=_params(2),
    )(qkv, qkv, qkv, gmga, *extra.srcs)


def _attn_bwd(qkv, o, gmga, dcv, *, name, rider=None):
    t = qkv.shape[0]
    blk = _tile(t, ATTN_BLOCK)
    nq = t // blk

    def body(*refs):
        ins, outs, (dk_acc, dv_acc), finish_rider = _ride(rider, 6, 4, refs, (N_HEAD_PAIRS, nq))
        q_ref, k_ref, v_ref, o_ref, ga_ref, dcv_ref = ins
        dq_ref, dk_ref, dv_ref, dga_ref = outs
        qi = pl.program_id(1)
        lane_is_first, after, from_here, causal = _attn_consts(blk)

        @pl.when(qi == 0)
        def _():
            dk_acc[...] = jnp.zeros_like(dk_acc)
            dv_acc[...] = jnp.zeros_like(dv_acc)

        ga, ov, dcvv = ga_ref[...], o_ref[...], dcv_ref[...]
        sg = _sigmoid(ga)
        dob = (dcvv * (ga * sg)).astype(BF16)
        dga_ref[...] = (dcvv * ov * _silu_grad(ga, sg)).astype(BF16)
        gt = dob.astype(F32) * ov
        g_total = jnp.concatenate(
            [jnp.sum(jnp.where(lane_is_first, gt, 0.0), axis=-1, keepdims=True),
             jnp.sum(jnp.where(lane_is_first, 0.0, gt), axis=-1, keepdims=True)], axis=0)
        qcat = _split_heads(q_ref[...], lane_is_first)
        docat = _split_heads(dob, lane_is_first)

        def block(kb, carry, mask):
            run, g_run, dq = carry
            rows = pl.ds(pl.multiple_of(kb * blk, blk), blk)
            kblk = k_ref[rows, :]
            lb, l1m, sig, one_m_sig = _scores(qcat, kblk, mask)
            hi, lo = _split_bf16(l1m)
            w = jnp.exp(lb + (_dot(hi, after) + _dot(lo, after) + run))
            if mask is not None:
                w = jnp.where(mask, w, 0.0)
            wb = w.astype(BF16)
            g = _dot_nt(docat, v_ref[rows, :]) * wb.astype(F32)
            ghi, glo = _split_bf16(g)
            g_before = g_total - g_run - (_dot(ghi, from_here) + _dot(glo, from_here))
            dz = g * one_m_sig - g_before * sig
            if mask is not None:
                dz = jnp.where(mask, dz, 0.0)
            dzb = dz.astype(BF16)
            dq = dq + _dot(_side_by_side(dzb, blk), _split_heads(kblk, lane_is_first))
            dk_acc[rows, :] += _dot_tn(dzb, qcat)
            dv_acc[rows, :] += _dot_tn(wb, docat)
            return (run + jnp.sum(l1m, axis=-1, keepdims=True), g_run + jnp.sum(g, axis=-1, keepdims=True), dq)

        zero = jnp.zeros((2 * blk, 1), F32)
        carry = block(qi, (zero, zero, jnp.zeros((blk, HEAD_PAIR), F32)), causal)
        _, _, dq = _while_mass_left(qi, carry, lambda kb, c: block(kb, c, None))
        dq_ref[...] = (dq * ATTN_SCALE).astype(BF16)

        @pl.when(qi == nq - 1)
        def _():
            dk_ref[...] = dk_acc[...].astype(BF16)
            dv_ref[...] = dv_acc[...].astype(BF16)

        finish_rider()

    qspec = pl.BlockSpec((blk, HEAD_PAIR), lambda p, i: (i, p))
    whole = pl.BlockSpec((t, HEAD_PAIR), lambda p, i: (0, p))
    out = jax.ShapeDtypeStruct((t, BRANCH_WIDTH), BF16)
    extra = rider or _NO_RIDER
    return pl.pallas_call(
        body, name=name, grid=(N_HEAD_PAIRS, nq),
        in_specs=[qspec,
                  pl.BlockSpec((t, HEAD_PAIR), lambda p, i: (0, N_HEAD_PAIRS + p)),
                  pl.BlockSpec((t, HEAD_PAIR), lambda p, i: (0, 2 * N_HEAD_PAIRS + p)),
                  qspec,
                  pl.BlockSpec((blk, HEAD_PAIR), lambda p, i: (i, GA_BLOCK + p)),
                  qspec] + extra.specs,
        out_specs=[qspec, whole, whole, qspec] + extra.specs,
        out_shape=[out, out, out, out] + extra.out_shape,
        scratch_shapes=[pltpu.VMEM((t, HEAD_PAIR), F32), pltpu.VMEM((t, HEAD_PAIR), F32)] + extra.scratch_shapes,
        compiler_params=_params(2),
    )(qkv, qkv, qkv, o, gmga, dcv, *extra.srcs)


def _mesh_position():
    x, y, c = lax.axis_index("x"), lax.axis_index("y"), lax.axis_index("c")
    return x, y, c, 4 * x + 2 * y + c


def _flipped(x, y, c, k):
    return (1 - x if k & 4 else x, 1 - y if k & 2 else y, 1 - c if k & 1 else c)


def _all_to_all(srcs, *, name, same_block):
    ex = _Exchange(srcs, same_block)

    def body(*refs):
        ex.start(refs[:ex.n], refs[ex.n:2 * ex.n], refs[2 * ex.n:])
        ex.finish(refs[:ex.n], refs[ex.n:2 * ex.n], refs[2 * ex.n:])

    return pl.pallas_call(
        body, name=name, in_specs=ex.specs, out_specs=ex.specs, out_shape=ex.out_shape,
        scratch_shapes=ex.scratch_shapes,
    )(*srcs)


class _Exchange:
    def __init__(self, srcs, same_block):
        self.srcs, self.same_block, self.n = list(srcs), same_block, len(srcs)
        self.specs = [pl.BlockSpec(memory_space=pl.ANY)] * self.n
        self.out_shape = [jax.ShapeDtypeStruct((N_DEV,) + tuple(s.shape if same_block else s.shape[1:]), s.dtype)
                          for s in self.srcs]
        self.scratch_shapes = [pltpu.SemaphoreType.DMA((N_DEV - 1, self.n)), pltpu.SemaphoreType.DMA((N_DEV - 1, self.n)),
                               pltpu.SemaphoreType.DMA((self.n,))] if self.n else []

    def _copies(self, src_refs, dst_refs, sems):
        send_sems, recv_sems, local_sems = sems
        x, y, c, me = _mesh_position()

        def outgoing(i, j):
            return src_refs[i] if self.same_block else src_refs[i].at[j]

        def remote(i, k, slot):
            return pltpu.make_async_remote_copy(
                src_ref=outgoing(i, jnp.bitwise_xor(me, k)), dst_ref=dst_refs[i].at[slot],
                send_sem=send_sems.at[k - 1, i], recv_sem=recv_sems.at[k - 1, i],
                device_id=_flipped(x, y, c, k), device_id_type=MESH)

        pairs = [(i, k) for k in range(1, N_DEV) for i in range(self.n)]
        mine = [pltpu.make_async_copy(outgoing(i, me), dst_refs[i].at[me], local_sems.at[i]) for i in range(self.n)]
        sent = [remote(i, k, me) for i, k in pairs]
        arrivals = [remote(i, k, jnp.bitwise_xor(me, k)) for i, k in pairs]
        return mine, sent, arrivals

    def start(self, src_refs, dst_refs, sems):
        mine, sent, _ = self._copies(src_refs, dst_refs, sems)
        for cp in mine + sent:
            cp.start()

    def finish(self, src_refs, dst_refs, sems):
        mine, sent, arrivals = self._copies(src_refs, dst_refs, sems)
        for cp in arrivals:
            cp.wait_recv()
        for cp in sent:
            cp.wait_send()
        for cp in mine:
            cp.wait()


_NO_RIDER = _Exchange([], True)


def _adamw(parts, w, m, v, *, name):
    layers, rows, cols = w.shape
    assert len(parts) == layers
    tr = rows
    while tr * cols > ADAM_TILE_ELEMS and tr % 32 == 0:
        tr //= 2
    row = pl.BlockSpec((1, tr, cols), lambda l, i: (l, i, 0))

    def body(*refs):
        p_refs = refs[:layers]
        w_ref, m_ref, v_ref, g_ref, d_ref, nm_ref, nv_ref = refs[layers:]
        layer = pl.program_id(0)
        g = None
        for k in range(N_DEV):
            part = p_refs[0][k]
            for l in range(1, layers):
                part = jnp.where(layer == l, p_refs[l][k], part)
            g = part.astype(F32) if g is None else g + part.astype(F32)
        m2 = ADAM_B1 * m_ref[0] + (1.0 - ADAM_B1) * g
        v2 = ADAM_B2 * v_ref[0] + (1.0 - ADAM_B2) * (g * g)
        m_hat = m2 / (1.0 - ADAM_B1 ** ADAM_STEP)
        v_hat = v2 / (1.0 - ADAM_B2 ** ADAM_STEP)
        g_ref[0] = g
        d_ref[0] = -ADAM_LR * (m_hat / (jnp.sqrt(v_hat) + ADAM_EPS) + ADAM_WD * w_ref[0])
        nm_ref[0] = m2
        nv_ref[0] = v2

    out = jax.ShapeDtypeStruct((layers, rows, cols), F32)
    return pl.pallas_call(
        body, name=name, grid=(layers, rows // tr),
        in_specs=[pl.BlockSpec((N_DEV, tr, cols), lambda l, i: (0, i, 0))] * layers + [row, row, row],
        out_specs=[row, row, row, row], out_shape=[out, out, out, out],
        compiler_params=_params(2),
    )(*parts, w, m, v)


MATMUL_WEIGHTS = ("w_in", "w_pool_out", "w_conv_out", "w_attn_out", "w_o")
SMALL = ("conv_w", "norm_pre", "pool_w", "pool_b", "pool_scale", "conv_b", "conv_ln_g", "conv_ln_b", "norm_post")
WEIGHT_ORDER = ("norm_pre", "w_in", "pool_w", "pool_b", "pool_scale", "w_pool_out", "conv_w", "conv_b",
                "conv_ln_g", "conv_ln_b", "w_conv_out", "w_attn_out", "w_o", "norm_post")


def _shard_axis(name):
    return -2 if name == "w_o" else -1


def _pack_rows(flat_parts, row_multiple):
    flat = jnp.concatenate(flat_parts, axis=-1)
    n = flat.shape[-1]
    chunk = row_multiple * LANES
    total = -(-n // chunk) * chunk
    pad = [(0, 0)] * (flat.ndim - 1) + [(0, total - n)]
    return jnp.pad(flat, pad).reshape(flat.shape[:-1] + (total // LANES, LANES))


def _unpack(buf, shapes):
    flat = buf.reshape(-1)
    out, at = {}, 0
    for name, shape in shapes:
        n = 1
        for s in shape:
            n *= s
        out[name] = flat[at:at + n].reshape(shape)
        at += n
    return out


def _to_dest_major(name, full):
    axis = full.ndim + _shard_axis(name)
    n = full.shape[axis] // N_DEV
    return jnp.stack([lax.slice_in_dim(full, d * n, (d + 1) * n, axis=axis) for d in range(N_DEV)])


def _from_source_major(name, gathered):
    return jnp.concatenate([gathered[d] for d in range(N_DEV)], axis=_shard_axis(name))


def kernel(x, norm_pre, w_in, pool_w, pool_b, pool_scale, w_pool_out, conv_w, conv_b, conv_ln_g, conv_ln_b, w_conv_out, w_attn_out, w_o, norm_post, loss_target, m_norm_pre, m_w_in, m_pool_w, m_pool_b, m_pool_scale, m_w_pool_out, m_conv_w, m_conv_b, m_conv_ln_g, m_conv_ln_b, m_w_conv_out, m_w_attn_out, m_w_o, m_norm_post, v_norm_pre, v_w_in, v_pool_w, v_pool_b, v_pool_scale, v_w_pool_out, v_conv_w, v_conv_b, v_conv_ln_g, v_conv_ln_b, v_w_conv_out, v_w_attn_out, v_w_o, v_norm_post):
    weights = dict(norm_pre=norm_pre, w_in=w_in, pool_w=pool_w, pool_b=pool_b, pool_scale=pool_scale,
                   w_pool_out=w_pool_out, conv_w=conv_w, conv_b=conv_b, conv_ln_g=conv_ln_g, conv_ln_b=conv_ln_b,
                   w_conv_out=w_conv_out, w_attn_out=w_attn_out, w_o=w_o, norm_post=norm_post)
    mom1 = dict(norm_pre=m_norm_pre, w_in=m_w_in, pool_w=m_pool_w, pool_b=m_pool_b, pool_scale=m_pool_scale,
                w_pool_out=m_w_pool_out, conv_w=m_conv_w, conv_b=m_conv_b, conv_ln_g=m_conv_ln_g, conv_ln_b=m_conv_ln_b,
                w_conv_out=m_w_conv_out, w_attn_out=m_w_attn_out, w_o=m_w_o, norm_post=m_norm_post)
    mom2 = dict(norm_pre=v_norm_pre, w_in=v_w_in, pool_w=v_pool_w, pool_b=v_pool_b, pool_scale=v_pool_scale,
                w_pool_out=v_w_pool_out, conv_w=v_conv_w, conv_b=v_conv_b, conv_ln_g=v_conv_ln_g, conv_ln_b=v_conv_ln_b,
                w_conv_out=v_w_conv_out, w_attn_out=v_w_attn_out, w_o=v_w_o, norm_post=v_norm_post)
    xs = x[0]
    target = loss_target[0]

    conv_rows = jnp.pad(conv_w, ((0, 0), (0, CONV_HALO - CONV_KERNEL), (0, 0)))
    shards = [[weights[n][l].astype(BF16) for n in MATMUL_WEIGHTS] for l in range(DEPTH)]
    gathered = _all_to_all(shards[0] + [conv_rows], name="gather_weights", same_block=True)
    full = [{n: _from_source_major(n, g) for n, g in zip(MATMUL_WEIGHTS, gathered)}, None]
    conv_full = _from_source_major("conv_w", gathered[-1])

    def in_sections(w):
        return dict(pg=w[:, 0:1024], c2gc=w[:, 1024:2560], q=w[:, 2560:3072], k=w[:, 3072:3584], v=w[:, 3584:4096],
                    gmga=jnp.concatenate([w[:, 4608:7680], w[:, 4096:4608]], axis=1))

    saved = []
    cur = xs
    for l in range(DEPTH):
        sec = in_sections(full[l]["w_in"])
        w_qkv = jnp.concatenate([sec["q"] * ATTN_SCALE, sec["k"], sec["v"]], axis=1)
        pw = pool_w[l].astype(BF16)
        pb, ps = pool_b[l].reshape(1, -1), pool_scale[l].reshape(1, -1)
        cb, lg, lb = conv_b[l].reshape(1, -1), conv_ln_g[l].reshape(1, -1), conv_ln_b[l].reshape(1, -1)
        h = _rms_fwd(cur, norm_pre[l].reshape(1, -1), name=f"rms_pre_fwd_{l}")
        pg = _matmul(h, sec["pg"], mode="nn", name=f"proj_pg_{l}")
        c2gc = _matmul(h, sec["c2gc"], mode="nn", name=f"proj_c2gc_{l}")
        qkv = _matmul(h, w_qkv, mode="nn", name=f"proj_qkv_{l}", out_dtype=BF16)
        gmga = _matmul(h, sec["gmga"], mode="nn", name=f"proj_gmga_{l}")
        a_act = _pool_fwd(pg, pw, pb, ps, name=f"pool_fwd_{l}")
        b_act = _conv_fwd(c2gc, conv_full[l], cb, lg, lb, name=f"conv_fwd_{l}")
        rider = _Exchange(shards[l + 1], True) if l + 1 < DEPTH else None
        o, c_act, *arrived = _attn_fwd(qkv, gmga, name=f"attn_fwd_{l}", rider=rider)
        if rider is not None:
            full[l + 1] = {n: _from_source_major(n, g) for n, g in zip(MATMUL_WEIGHTS, arrived)}
        ya = _matmul(a_act, full[l]["w_pool_out"], mode="nn", name=f"out_pool_{l}")
        yb = _matmul(b_act, full[l]["w_conv_out"], mode="nn", name=f"out_conv_{l}")
        yc = _matmul(c_act, full[l]["w_attn_out"], mode="nn", name=f"out_attn_{l}")
        mix = _merge_fwd(gmga, ya, yb, yc, name=f"merge_fwd_{l}")
        out = _matmul(mix, full[l]["w_o"], mode="nn", name=f"out_proj_{l}")
        nxt = _rms_fwd(out, norm_post[l].reshape(1, -1), name=f"rms_post_fwd_{l}", resid=cur)
        saved.append(dict(x=cur, h=h, pg=pg, c2gc=c2gc, qkv=qkv, gmga=gmga, a=a_act, b=b_act, c=c_act, o=o,
                          ya=ya, yb=yb, yc=yc, mix=mix, out=out, sec=sec, pw=pw, pb=pb, ps=ps, cb=cb, lg=lg, lb=lb))
        cur = nxt

    loss_tile, dx = _loss_head(cur, target, name="loss_head")
    loss = lax.psum(loss_tile[0, 0], ("x", "y", "c"))

    grads = {n: [None] * DEPTH for n in WEIGHT_ORDER}
    parts = [None] * DEPTH
    for l in reversed(range(DEPTH)):
        s = saved[l]
        dout, grads["norm_post"][l] = _rms_bwd(s["out"], norm_post[l].reshape(1, -1), dx, name=f"rms_post_bwd_{l}")
        dmix = _matmul(dout, full[l]["w_o"], mode="nt", name=f"d_mix_{l}")
        grads["w_o"][l] = _matmul(s["mix"], dout, mode="tn", name=f"d_w_o_{l}", out_dtype=BF16)
        dya, dyb, dyc, dgm = _merge_bwd(dmix, s["gmga"], s["ya"], s["yb"], s["yc"], name=f"merge_bwd_{l}")
        da = _matmul(dya, full[l]["w_pool_out"], mode="nt", name=f"d_pool_act_{l}")
        grads["w_pool_out"][l] = _matmul(s["a"], dya, mode="tn", name=f"d_w_pool_out_{l}", out_dtype=BF16)
        db = _matmul(dyb, full[l]["w_conv_out"], mode="nt", name=f"d_conv_act_{l}")
        grads["w_conv_out"][l] = _matmul(s["b"], dyb, mode="tn", name=f"d_w_conv_out_{l}", out_dtype=BF16)
        dc = _matmul(dyc, full[l]["w_attn_out"], mode="nt", name=f"d_attn_act_{l}")
        grads["w_attn_out"][l] = _matmul(s["c"], dyc, mode="tn", name=f"d_w_attn_out_{l}", out_dtype=BF16)
        rider = None
        if l + 1 < DEPTH:
            rider = _Exchange([_to_dest_major(n, grads[n][l + 1]) for n in MATMUL_WEIGHTS], False)
        dq, dk, dv, dga, *arrived = _attn_bwd(s["qkv"], s["o"], s["gmga"], dc, name=f"attn_bwd_{l}", rider=rider)
        if rider is not None:
            parts[l + 1] = arrived
        dc2gc, dcw, dcvec = _conv_bwd(s["c2gc"], db, conv_full[l], s["cb"], s["lg"], s["lb"], name=f"conv_bwd_{l}")
        dpg, dpw, dpvec = _pool_bwd(s["pg"], da, s["pw"], s["pb"], s["ps"], name=f"pool_bwd_{l}")
        grads["conv_w"][l] = dcw[:CONV_KERNEL]
        grads["conv_b"][l], grads["conv_ln_g"][l], grads["conv_ln_b"][l] = dcvec[0], dcvec[1], dcvec[2]
        grads["pool_w"][l] = dpw
        grads["pool_b"][l] = dpvec[0].reshape(4, POOL_GROUP_DIM)
        grads["pool_scale"][l] = dpvec[1]
        dsec = dict(pg=dpg, c2gc=dc2gc, q=dq, k=dk, v=dv, gmga=jnp.concatenate([dgm, dga], axis=1))
        dh, dws = None, {}
        for n in ("pg", "c2gc", "q", "k", "v", "gmga"):
            dh = _matmul(dsec[n], s["sec"][n], mode="nt", name=f"d_h_{n}_{l}", acc=dh)
            dws[n] = _matmul(s["h"], dsec[n], mode="tn", name=f"d_w_in_{n}_{l}", out_dtype=BF16)
        grads["w_in"][l] = jnp.concatenate(
            [dws["pg"], dws["c2gc"], dws["q"], dws["k"], dws["v"], dws["gmga"][:, 3 * D_MODEL:], dws["gmga"][:, :3 * D_MODEL]],
            axis=1)
        dx, dg_pre = _rms_bwd(s["x"], norm_pre[l].reshape(1, -1), dh, name=f"rms_pre_bwd_{l}", resid=dx)
        grads["norm_pre"][l] = dg_pre.reshape(-1)
        grads["norm_post"][l] = grads["norm_post"][l].reshape(-1)
    small_grads = {n: jnp.stack(grads[n]) for n in SMALL}
    replicated = jnp.concatenate([small_grads[n].reshape(-1) for n in SMALL[1:]])
    small = _pack_rows([_to_dest_major("conv_w", small_grads["conv_w"]).reshape(N_DEV, -1),
                        jnp.broadcast_to(replicated, (N_DEV, replicated.size))], 16).astype(BF16)
    *parts[0], small_parts = _all_to_all([_to_dest_major(n, grads[n][0]) for n in MATMUL_WEIGHTS] + [small],
                                         name="exchange_grads", same_block=False)

    outs = [dict(), dict(), dict(), dict()]
    for i, n in enumerate(MATMUL_WEIGHTS):
        res = _adamw([parts[l][i] for l in range(DEPTH)], weights[n], mom1[n], mom2[n], name=f"adamw_{n}")
        for o, r in zip(outs, res):
            o[n] = r

    def packed(tree):
        return _pack_rows([tree[n].reshape(-1) for n in SMALL], 16)[None]

    shapes = [(n, weights[n].shape) for n in SMALL]
    res = _adamw([small_parts], packed(weights), packed(mom1), packed(mom2), name="adamw_small")
    for o, r in zip(outs, res):
        o.update(_unpack(r, shapes))
    return (loss, dx[None], *[o[n] for o in outs for n in WEIGHT_ORDER])
```

```python
import functools

import jax
import jax.numpy as jnp
from jax import lax
from jax.experimental import pallas as pl
from jax.experimental.pallas import tpu as pltpu

F32 = jnp.float32
BF16 = jnp.bfloat16

D_MODEL = 1024
DEPTH = 2
POOL_WINDOWS = (2, 4, 8, 16)
POOL_GROUP_DIM = 128
BRANCH_WIDTH = 512
CONV_KERNEL = 31
CONV_HALO = 32
POOL_HALO = 16
HEAD_DIM = 64
HEAD_PAIR = 128
N_HEAD_PAIRS = 4
ATTN_SCALE = 0.125
LOG_F32_ZERO = -104.0
RMS_EPS = 1e-6
LN_EPS = 1e-5
N_DEV = 8
LANES = 128

ADAM_LR = 0.001
ADAM_B1 = 0.9
ADAM_B2 = 0.999
ADAM_EPS = 1e-08
ADAM_WD = 0.01
ADAM_STEP = 10

ROW_TILE = 256
ATTN_BLOCK = 256
MM_TILE = 1024
ADAM_TILE_ELEMS = 256 * 1024
VMEM_LIMIT = 48 * 1024 * 1024

MESH = pl.DeviceIdType.MESH


def _params(n_axes):
    return pltpu.CompilerParams(dimension_semantics=("arbitrary",) * n_axes, vmem_limit_bytes=VMEM_LIMIT)


def _tile(n, pref):
    if n <= pref:
        return n
    t = (pref // LANES) * LANES
    while n % t:
        t -= LANES
    return t


def _dot(a, b):
    return jnp.dot(a, b, preferred_element_type=F32)


def _dot_nt(a, b):
    return lax.dot_general(a, b, (((1,), (1,)), ((), ())), preferred_element_type=F32)


def _dot_tn(a, b):
    return lax.dot_general(a, b, (((0,), (0,)), ((), ())), preferred_element_type=F32)


def _sigmoid(x):
    return 1.0 / (1.0 + jnp.exp(-x))


def _silu_grad(x, s):
    return s * (1.0 + x * (1.0 - s))


def _matmul(a, b, *, mode, name, out_dtype=F32, acc=None):
    if mode == "nn":
        (m, k), n = a.shape, b.shape[1]
    elif mode == "nt":
        (m, k), n = a.shape, b.shape[0]
    else:
        (k, m), n = a.shape, b.shape[1]
    tm, tn, tk = _tile(m, MM_TILE), _tile(n, MM_TILE), _tile(k, MM_TILE)
    nk = k // tk
    dot = {"nn": _dot, "nt": _dot_nt, "tn": _dot_tn}[mode]
    a_spec = pl.BlockSpec((tk, tm), lambda i, j, kk: (kk, i)) if mode == "tn" else pl.BlockSpec((tm, tk), lambda i, j, kk: (i, kk))
    b_spec = pl.BlockSpec((tn, tk), lambda i, j, kk: (j, kk)) if mode == "nt" else pl.BlockSpec((tk, tn), lambda i, j, kk: (kk, j))
    o_spec = pl.BlockSpec((tm, tn), lambda i, j, kk: (i, j))
    has_acc = acc is not None

    def body(*refs):
        a_ref, b_ref = refs[0], refs[1]
        acc_in = refs[2] if has_acc else None
        o_ref = refs[3] if has_acc else refs[2]
        part = dot(a_ref[...], b_ref[...])
        if nk == 1:
            if has_acc:
                part = part + acc_in[...]
            o_ref[...] = part.astype(out_dtype)
            return
        scr = refs[-1]
        kk = pl.program_id(2)

        @pl.when(kk == 0)
        def _():
            scr[...] = part + acc_in[...] if has_acc else part

        @pl.when(kk > 0)
        def _():
            scr[...] += part

        @pl.when(kk == nk - 1)
        def _():
            o_ref[...] = scr[...].astype(out_dtype)

    in_specs = [a_spec, b_spec] + ([o_spec] if has_acc else [])
    args = (a, b) + ((acc,) if has_acc else ())
    return pl.pallas_call(
        body, name=name, grid=(m // tm, n // tn, nk),
        in_specs=in_specs, out_specs=o_spec,
        out_shape=jax.ShapeDtypeStruct((m, n), out_dtype),
        scratch_shapes=[pltpu.VMEM((tm, tn), F32)] if nk > 1 else [],
        compiler_params=_params(3),
    )(*args)


def _rms_fwd(x, g, *, name, resid=None):
    t = x.shape[0]
    tm = _tile(t, ROW_TILE)
    row = pl.BlockSpec((tm, D_MODEL), lambda i: (i, 0))
    vec = pl.BlockSpec((1, D_MODEL), lambda i: (0, 0))
    has_resid = resid is not None

    def body(*refs):
        x_ref, g_ref = refs[0], refs[1]
        o_ref = refs[-1]
        xv = x_ref[...]
        y = xv * lax.rsqrt(jnp.mean(xv * xv, axis=-1, keepdims=True) + RMS_EPS) * g_ref[...]
        if has_resid:
            o_ref[...] = refs[2][...] + y
        else:
            o_ref[...] = y.astype(BF16)

    return pl.pallas_call(
        body, name=name, grid=(t // tm,),
        in_specs=[row, vec] + ([row] if has_resid else []), out_specs=row,
        out_shape=jax.ShapeDtypeStruct((t, D_MODEL), F32 if has_resid else BF16),
        compiler_params=_params(1),
    )(*((x, g) + ((resid,) if has_resid else ())))


def _rms_bwd(xin, g, dy, *, name, resid=None):
    t = xin.shape[0]
    tm = _tile(t, ROW_TILE)
    row = pl.BlockSpec((tm, D_MODEL), lambda i: (i, 0))
    vec = pl.BlockSpec((1, D_MODEL), lambda i: (0, 0))
    has_resid = resid is not None
    out_dtype = F32 if has_resid else BF16

    def body(*refs):
        x_ref, g_ref, dy_ref = refs[0], refs[1], refs[2]
        dx_ref, dg_ref = refs[-2], refs[-1]
        xv, dyv = x_ref[...], dy_ref[...]
        r = lax.rsqrt(jnp.mean(xv * xv, axis=-1, keepdims=True) + RMS_EPS)
        a = dyv * g_ref[...]
        dx = r * a - xv * (r * r * r) * jnp.mean(a * xv, axis=-1, keepdims=True)
        if has_resid:
            dx = dx + refs[3][...]
        dx_ref[...] = dx.astype(out_dtype)
        part = jnp.sum(dyv * xv * r, axis=0, keepdims=True)

        @pl.when(pl.program_id(0) == 0)
        def _():
            dg_ref[...] = part

        @pl.when(pl.program_id(0) > 0)
        def _():
            dg_ref[...] += part

    return pl.pallas_call(
        body, name=name, grid=(t // tm,),
        in_specs=[row, vec, row] + ([row] if has_resid else []), out_specs=[row, vec],
        out_shape=[jax.ShapeDtypeStruct((t, D_MODEL), out_dtype), jax.ShapeDtypeStruct((1, D_MODEL), F32)],
        compiler_params=_params(1),
    )(*((xin, g, dy) + ((resid,) if has_resid else ())))


def _loss_head(x, target, *, name):
    t = x.shape[0]
    tm = _tile(t, ROW_TILE)
    row = pl.BlockSpec((tm, D_MODEL), lambda i: (i, 0))
    acc = pl.BlockSpec((8, LANES), lambda i: (0, 0))

    def body(x_ref, t_ref, l_ref, dx_ref):
        diff = x_ref[...] - t_ref[...]
        dx_ref[...] = diff * (1.0 / D_MODEL)
        part = 0.5 * jnp.sum(jnp.mean(diff * diff, axis=-1, keepdims=True), axis=0, keepdims=True)

        @pl.when(pl.program_id(0) == 0)
        def _():
            l_ref[...] = jnp.zeros((8, LANES), F32) + part

        @pl.when(pl.program_id(0) > 0)
        def _():
            l_ref[...] += part

    return pl.pallas_call(
        body, name=name, grid=(t // tm,),
        in_specs=[row, row], out_specs=[acc, row],
        out_shape=[jax.ShapeDtypeStruct((8, LANES), F32), jax.ShapeDtypeStruct((t, D_MODEL), F32)],
        compiler_params=_params(1),
    )(x, target)


def _window_sum(ext, n_doublings, forward):
    rows = ext.shape[0]
    s, sh = ext, 1
    for _ in range(n_doublings):
        s = s + pltpu.roll(s, sh if forward else rows - sh, 0)
        sh *= 2
    return s


def _pool_fwd(pg, pool_w, pool_b, pool_scale, *, name):
    t = pg.shape[0]
    tm = _tile(t, ROW_TILE)

    def body(pg_ref, w_ref, b_ref, s_ref, o_ref, halo):
        i = pl.program_id(0)

        @pl.when(i == 0)
        def _():
            halo[...] = jnp.zeros_like(halo)

        p = pg_ref[:, :BRANCH_WIDTH]
        gate = pg_ref[:, BRANCH_WIDTH:]
        ext = jnp.concatenate([halo[...], p], axis=0)
        pos = i * tm + lax.broadcasted_iota(jnp.int32, (tm, 1), 0)
        outs = []
        for g, w in enumerate(POOL_WINDOWS):
            cols = slice(g * POOL_GROUP_DIM, (g + 1) * POOL_GROUP_DIM)
            cnt = jnp.minimum(pos + 1, w).astype(F32)
            d = _window_sum(ext[:, cols], g + 1, True)[POOL_HALO:] / cnt - p[:, cols]
            y = (_dot(d.astype(BF16), w_ref[g]) + b_ref[:, cols]) * s_ref[:, cols]
            gg = gate[:, cols]
            outs.append(y * (gg * _sigmoid(gg)))
        o_ref[...] = jnp.concatenate(outs, axis=1).astype(BF16)
        halo[...] = p[tm - POOL_HALO:, :]

    vec = pl.BlockSpec((1, BRANCH_WIDTH), lambda i: (0, 0))
    return pl.pallas_call(
        body, name=name, grid=(t // tm,),
        in_specs=[pl.BlockSpec((tm, 2 * BRANCH_WIDTH), lambda i: (i, 0)),
                  pl.BlockSpec((4, POOL_GROUP_DIM, POOL_GROUP_DIM), lambda i: (0, 0, 0)), vec, vec],
        out_specs=pl.BlockSpec((tm, BRANCH_WIDTH), lambda i: (i, 0)),
        out_shape=jax.ShapeDtypeStruct((t, BRANCH_WIDTH), BF16),
        scratch_shapes=[pltpu.VMEM((POOL_HALO, BRANCH_WIDTH), F32)],
        compiler_params=_params(1),
    )(pg, pool_w, pool_b, pool_scale)


def _pool_bwd(pg, d_out, pool_w, pool_b, pool_scale, *, name):
    t = pg.shape[0]
    tm = _tile(t, ROW_TILE)
    nt = t // tm
    halo_per_tile = tm // POOL_HALO

    def body(pg_ref, halo_ref, do_ref, w_ref, b_ref, s_ref, dpg_ref, dw_ref, dvec_ref, carry):
        i = pl.program_id(0)
        ri = nt - 1 - i

        @pl.when(i == 0)
        def _():
            carry[...] = jnp.zeros_like(carry)
            dw_ref[...] = jnp.zeros_like(dw_ref)
            dvec_ref[...] = jnp.zeros_like(dvec_ref)

        p = pg_ref[:, :BRANCH_WIDTH]
        gate = pg_ref[:, BRANCH_WIDTH:]
        hp = jnp.where(ri > 0, halo_ref[:, :BRANCH_WIDTH], 0.0)
        ext = jnp.concatenate([hp, p], axis=0)
        pos = ri * tm + lax.broadcasted_iota(jnp.int32, (tm, 1), 0)
        dps, dgs, dbs, dss = [], [], [], []
        for g, w in enumerate(POOL_WINDOWS):
            cols = slice(g * POOL_GROUP_DIM, (g + 1) * POOL_GROUP_DIM)
            cnt = jnp.minimum(pos + 1, w).astype(F32)
            d = (_window_sum(ext[:, cols], g + 1, True)[POOL_HALO:] / cnt - p[:, cols]).astype(BF16)
            y1 = _dot(d, w_ref[g]) + b_ref[:, cols]
            scale = s_ref[:, cols]
            y2 = y1 * scale
            gg = gate[:, cols]
            sg = _sigmoid(gg)
            do = do_ref[:, cols]
            dy2 = do * (gg * sg)
            dgs.append(do * y2 * _silu_grad(gg, sg))
            dss.append(jnp.sum(dy2 * y1, axis=0, keepdims=True))
            dy1 = dy2 * scale
            dbs.append(jnp.sum(dy1, axis=0, keepdims=True))
            dy1b = dy1.astype(BF16)
            dw_ref[g] += _dot_tn(d, dy1b)
            dd = _dot_nt(dy1b, w_ref[g])
            dpool = dd / cnt
            dext = jnp.concatenate([dpool, carry[:, cols]], axis=0)
            dps.append(_window_sum(dext, g + 1, False)[:tm] - dd)
            carry[:, cols] = dpool[:POOL_HALO]
        dpg_ref[...] = jnp.concatenate(dps + dgs, axis=1).astype(BF16)
        dvec_ref[0:1, :] += jnp.concatenate(dbs, axis=1)
        dvec_ref[1:2, :] += jnp.concatenate(dss, axis=1)

    vec = pl.BlockSpec((1, BRANCH_WIDTH), lambda i: (0, 0))
    wspec = pl.BlockSpec((4, POOL_GROUP_DIM, POOL_GROUP_DIM), lambda i: (0, 0, 0))
    return pl.pallas_call(
        body, name=name, grid=(nt,),
        in_specs=[pl.BlockSpec((tm, 2 * BRANCH_WIDTH), lambda i: (nt - 1 - i, 0)),
                  pl.BlockSpec((POOL_HALO, 2 * BRANCH_WIDTH), lambda i: (jnp.maximum((nt - 1 - i) * halo_per_tile - 1, 0), 0)),
                  pl.BlockSpec((tm, BRANCH_WIDTH), lambda i: (nt - 1 - i, 0)), wspec, vec, vec],
        out_specs=[pl.BlockSpec((tm, 2 * BRANCH_WIDTH), lambda i: (nt - 1 - i, 0)), wspec,
                   pl.BlockSpec((8, BRANCH_WIDTH), lambda i: (0, 0))],
        out_shape=[jax.ShapeDtypeStruct((t, 2 * BRANCH_WIDTH), BF16),
                   jax.ShapeDtypeStruct((4, POOL_GROUP_DIM, POOL_GROUP_DIM), F32),
                   jax.ShapeDtypeStruct((8, BRANCH_WIDTH), F32)],
        scratch_shapes=[pltpu.VMEM((POOL_HALO, BRANCH_WIDTH), F32)],
        compiler_params=_params(1),
    )(pg, pg, d_out, pool_w, pool_b, pool_scale)


CONV_TILE_ROWS = 64


def _conv_tiles(tm):
    return [(r, slice(c, c + LANES)) for c in range(0, BRANCH_WIDTH, LANES) for r in range(0, tm, CONV_TILE_ROWS)]


def _tap_sum(src_ref, w_ref, r, cols, first_row_of_tap):
    acc = None
    for j in range(CONV_KERNEL):
        k = CONV_KERNEL - 1 - j
        first = first_row_of_tap(j) + r
        term = w_ref[k:k + 1, cols] * src_ref[first:first + CONV_TILE_ROWS, cols]
        acc = term if acc is None else acc + term
    return acc


def _causal_conv(ext_ref, w_ref, cv_ref, tm):
    for r, cols in _conv_tiles(tm):
        cv_ref[r:r + CONV_TILE_ROWS, cols] = _tap_sum(ext_ref, w_ref, r, cols, lambda j: CONV_HALO - j)


def _conv_fwd(c2gc, conv_w, conv_b, ln_g, ln_b, *, name):
    t = c2gc.shape[0]
    tm = _tile(t, ROW_TILE)

    def body(c_ref, w_ref, cb_ref, g_ref, b_ref, o_ref, ext_ref, cv_ref):
        @pl.when(pl.program_id(0) == 0)
        def _():
            ext_ref[0:CONV_HALO, :] = jnp.zeros((CONV_HALO, BRANCH_WIDTH), F32)

        @pl.when(pl.program_id(0) > 0)
        def _():
            ext_ref[0:CONV_HALO, :] = ext_ref[tm:tm + CONV_HALO, :]

        ext_ref[CONV_HALO:, :] = c_ref[:, :BRANCH_WIDTH] * _sigmoid(c_ref[:, BRANCH_WIDTH:2 * BRANCH_WIDTH])
        gate = c_ref[:, 2 * BRANCH_WIDTH:]
        _causal_conv(ext_ref, w_ref, cv_ref, tm)
        cv = cv_ref[...] + cb_ref[...]
        mu = jnp.mean(cv, axis=-1, keepdims=True)
        xc = cv - mu
        var = jnp.mean(xc * xc, axis=-1, keepdims=True)
        ln = xc * lax.rsqrt(var + LN_EPS) * g_ref[...] + b_ref[...]
        o_ref[...] = (ln * _sigmoid(ln) * (gate * _sigmoid(gate))).astype(BF16)

    vec = pl.BlockSpec((1, BRANCH_WIDTH), lambda i: (0, 0))
    return pl.pallas_call(
        body, name=name, grid=(t // tm,),
        in_specs=[pl.BlockSpec((tm, 3 * BRANCH_WIDTH), lambda i: (i, 0)),
                  pl.BlockSpec((CONV_HALO, BRANCH_WIDTH), lambda i: (0, 0)), vec, vec, vec],
        out_specs=pl.BlockSpec((tm, BRANCH_WIDTH), lambda i: (i, 0)),
        out_shape=jax.ShapeDtypeStruct((t, BRANCH_WIDTH), BF16),
        scratch_shapes=[pltpu.VMEM((tm + CONV_HALO, BRANCH_WIDTH), F32), pltpu.VMEM((tm, BRANCH_WIDTH), F32)],
        compiler_params=_params(1),
    )(c2gc, conv_w, conv_b, ln_g, ln_b)


def _conv_bwd(c2gc, d_out, conv_w, conv_b, ln_g, ln_b, *, name):
    t = c2gc.shape[0]
    tm = _tile(t, ROW_TILE)
    nt = t // tm
    halo_per_tile = tm // CONV_HALO

    def body(c_ref, halo_ref, do_ref, w_ref, cb_ref, g_ref, b_ref, dc_ref, dw_ref, dvec_ref,
             ext_ref, cv_ref, dext_ref, du_ref, dw_acc):
        i = pl.program_id(0)
        ri = nt - 1 - i

        @pl.when(i == 0)
        def _():
            dext_ref[tm:tm + CONV_HALO, :] = jnp.zeros((CONV_HALO, BRANCH_WIDTH), F32)
            dw_acc[...] = jnp.zeros_like(dw_acc)
            dvec_ref[...] = jnp.zeros_like(dvec_ref)

        @pl.when(i > 0)
        def _():
            dext_ref[tm:tm + CONV_HALO, :] = dext_ref[0:CONV_HALO, :]

        a = c_ref[:, :BRANCH_WIDTH]
        sb = _sigmoid(c_ref[:, BRANCH_WIDTH:2 * BRANCH_WIDTH])
        gate = c_ref[:, 2 * BRANCH_WIDTH:]
        hu = halo_ref[:, :BRANCH_WIDTH] * _sigmoid(halo_ref[:, BRANCH_WIDTH:2 * BRANCH_WIDTH])
        ext_ref[0:CONV_HALO, :] = jnp.where(ri > 0, hu, 0.0)
        ext_ref[CONV_HALO:, :] = a * sb
        _causal_conv(ext_ref, w_ref, cv_ref, tm)
        cv = cv_ref[...] + cb_ref[...]
        mu = jnp.mean(cv, axis=-1, keepdims=True)
        xc = cv - mu
        rs = lax.rsqrt(jnp.mean(xc * xc, axis=-1, keepdims=True) + LN_EPS)
        n = xc * rs
        ln = n * g_ref[...] + b_ref[...]
        sl = _sigmoid(ln)
        sgate = _sigmoid(gate)
        do = do_ref[...]
        dgate = do * (ln * sl) * _silu_grad(gate, sgate)
        dln = do * (gate * sgate) * _silu_grad(ln, sl)
        dn = dln * g_ref[...]
        dcv = rs * (dn - jnp.mean(dn, axis=-1, keepdims=True) - n * jnp.mean(dn * n, axis=-1, keepdims=True))
        dvec_ref[0:1, :] += jnp.sum(dcv, axis=0, keepdims=True)
        dvec_ref[1:2, :] += jnp.sum(dln * n, axis=0, keepdims=True)
        dvec_ref[2:3, :] += jnp.sum(dln, axis=0, keepdims=True)
        dext_ref[0:tm, :] = dcv
        for r, cols in _conv_tiles(tm):
            du_ref[r:r + CONV_TILE_ROWS, cols] = _tap_sum(dext_ref, w_ref, r, cols, lambda j: j)
            d_tile = dext_ref[r:r + CONV_TILE_ROWS, cols]
            for j in range(CONV_KERNEL):
                first = CONV_HALO - j + r
                prod = d_tile * ext_ref[first:first + CONV_TILE_ROWS, cols]
                part = prod[0:8]
                for q in range(8, CONV_TILE_ROWS, 8):
                    part = part + prod[q:q + 8]
                dw_acc[CONV_KERNEL - 1 - j, :, cols] += part
        du = du_ref[...]
        dc_ref[...] = jnp.concatenate([du * sb, du * a * sb * (1.0 - sb), dgate], axis=1).astype(BF16)

        @pl.when(i == nt - 1)
        def _():
            dw_ref[...] = jnp.sum(dw_acc[...], axis=1)

    vec = pl.BlockSpec((1, BRANCH_WIDTH), lambda i: (0, 0))
    wspec = pl.BlockSpec((CONV_HALO, BRANCH_WIDTH), lambda i: (0, 0))
    return pl.pallas_call(
        body, name=name, grid=(nt,),
        in_specs=[pl.BlockSpec((tm, 3 * BRANCH_WIDTH), lambda i: (nt - 1 - i, 0)),
                  pl.BlockSpec((CONV_HALO, 3 * BRANCH_WIDTH), lambda i: (jnp.maximum((nt - 1 - i) * halo_per_tile - 1, 0), 0)),
                  pl.BlockSpec((tm, BRANCH_WIDTH), lambda i: (nt - 1 - i, 0)), wspec, vec, vec, vec],
        out_specs=[pl.BlockSpec((tm, 3 * BRANCH_WIDTH), lambda i: (nt - 1 - i, 0)), wspec,
                   pl.BlockSpec((8, BRANCH_WIDTH), lambda i: (0, 0))],
        out_shape=[jax.ShapeDtypeStruct((t, 3 * BRANCH_WIDTH), BF16),
                   jax.ShapeDtypeStruct((CONV_HALO, BRANCH_WIDTH), F32),
                   jax.ShapeDtypeStruct((8, BRANCH_WIDTH), F32)],
        scratch_shapes=[pltpu.VMEM((tm + CONV_HALO, BRANCH_WIDTH), F32), pltpu.VMEM((tm, BRANCH_WIDTH), F32),
                        pltpu.VMEM((tm + CONV_HALO, BRANCH_WIDTH), F32), pltpu.VMEM((tm, BRANCH_WIDTH), F32),
                        pltpu.VMEM((CONV_HALO, 8, BRANCH_WIDTH), F32)],
        compiler_params=_params(1),
    )(c2gc, c2gc, d_out, conv_w, conv_b, ln_g, ln_b)


def _merge_fwd(gmga, ya, yb, yc, *, name):
    t = ya.shape[0]
    tm = _tile(t, ROW_TILE)
    row = pl.BlockSpec((tm, D_MODEL), lambda i: (i, 0))

    def body(g0, g1, g2, a_ref, b_ref, c_ref, o_ref):
        m = _sigmoid(g0[...]) * a_ref[...] + _sigmoid(g1[...]) * b_ref[...] + _sigmoid(g2[...]) * c_ref[...]
        o_ref[...] = m.astype(BF16)

    gspecs = [pl.BlockSpec((tm, D_MODEL), functools.partial(lambda i, b: (i, b), b=b)) for b in range(3)]
    return pl.pallas_call(
        body, name=name, grid=(t // tm,),
        in_specs=gspecs + [row, row, row], out_specs=row,
        out_shape=jax.ShapeDtypeStruct((t, D_MODEL), BF16),
        compiler_params=_params(1),
    )(gmga, gmga, gmga, ya, yb, yc)


def _merge_bwd(dm, gmga, ya, yb, yc, *, name):
    t = ya.shape[0]
    tm = _tile(t, ROW_TILE)
    row = pl.BlockSpec((tm, D_MODEL), lambda i: (i, 0))
    wide = pl.BlockSpec((tm, 3 * D_MODEL), lambda i: (i, 0))

    def body(dm_ref, g0, g1, g2, a_ref, b_ref, c_ref, da_ref, db_ref, dc_ref, dg_ref):
        dmv = dm_ref[...]
        for k, (g_ref, y_ref, dy_ref) in enumerate(((g0, a_ref, da_ref), (g1, b_ref, db_ref), (g2, c_ref, dc_ref))):
            s = _sigmoid(g_ref[...])
            dy_ref[...] = (dmv * s).astype(BF16)
            dg_ref[:, k * D_MODEL:(k + 1) * D_MODEL] = (dmv * y_ref[...] * s * (1.0 - s)).astype(BF16)

    gspecs = [pl.BlockSpec((tm, D_MODEL), functools.partial(lambda i, b: (i, b), b=b)) for b in range(3)]
    return pl.pallas_call(
        body, name=name, grid=(t // tm,),
        in_specs=[row] + gspecs + [row, row, row], out_specs=[row, row, row, wide],
        out_shape=[jax.ShapeDtypeStruct((t, D_MODEL), BF16)] * 3 + [jax.ShapeDtypeStruct((t, 3 * D_MODEL), BF16)],
        compiler_params=_params(1),
    )(dm, gmga, gmga, gmga, ya, yb, yc)


GA_BLOCK = 3 * D_MODEL // HEAD_PAIR


def _split_heads(x, lane_is_first):
    zero = jnp.zeros_like(x)
    return jnp.concatenate([jnp.where(lane_is_first, x, zero), jnp.where(lane_is_first, zero, x)], axis=0)


def _side_by_side(x, rows):
    return jnp.concatenate([x[:rows], x[rows:]], axis=1)


def _split_bf16(x):
    hi = x.astype(BF16)
    return hi, (x - hi.astype(F32)).astype(BF16)


def _scores(qcat, kblk, mask):
    z = _dot_nt(qcat, kblk)
    e = jnp.exp(-jnp.abs(z))
    sp = jnp.maximum(z, 0.0) + jnp.log(1.0 + e)
    l1m = -sp
    if mask is not None:
        l1m = jnp.where(mask, l1m, 0.0)
    inv = 1.0 / (1.0 + e)
    pos = z >= 0.0
    return z - sp, l1m, jnp.where(pos, 1.0, e) * inv, jnp.where(pos, e, 1.0) * inv


def _attn_consts(blk):
    lane_is_first = lax.broadcasted_iota(jnp.int32, (1, HEAD_PAIR), 1) < HEAD_DIM
    r = lax.broadcasted_iota(jnp.int32, (blk, blk), 0)
    c = lax.broadcasted_iota(jnp.int32, (blk, blk), 1)
    after = (r > c).astype(BF16)
    from_here = (r >= c).astype(BF16)
    qrow = lax.broadcasted_iota(jnp.int32, (2 * blk, blk), 0)
    qrow = jnp.where(qrow >= blk, qrow - blk, qrow)
    causal = lax.broadcasted_iota(jnp.int32, (2 * blk, blk), 1) < qrow
    return lane_is_first, after, from_here, causal


def _while_mass_left(qi, carry, block):
    def alive(c):
        return jnp.max(c[0]) > LOG_F32_ZERO

    def cond(state):
        return jnp.logical_and(state[0] < qi, state[1])

    def step(state):
        new = block(qi - 1 - state[0], state[2])
        return state[0] + 1, alive(new), new

    return lax.while_loop(cond, step, (jnp.int32(0), alive(carry), carry))[2]


def _ride(rider, n_in, n_out, refs, grid):
    if rider is None:
        return refs[:n_in], refs[n_in:n_in + n_out], refs[n_in + n_out:], lambda: None
    n = rider.n
    ins, srcs = refs[:n_in], refs[n_in:n_in + n]
    outs, dsts = refs[n_in + n:n_in + n + n_out], refs[n_in + n + n_out:n_in + 2 * n + n_out]
    rest = refs[n_in + 2 * n + n_out:]
    scratch, sems = rest[:len(rest) - 3], rest[len(rest) - 3:]
    step = pl.program_id(0) * grid[1] + pl.program_id(1)

    @pl.when(step == 0)
    def _():
        rider.start(srcs, dsts, sems)

    def finish():
        @pl.when(step == grid[0] * grid[1] - 1)
        def _():
            rider.finish(srcs, dsts, sems)

    return ins, outs, scratch, finish


def _attn_fwd(qkv, gmga, *, name, rider=None):
    t = qkv.shape[0]
    blk = _tile(t, ATTN_BLOCK)
    nq = t // blk

    def body(*refs):
        (q_ref, k_ref, v_ref, ga_ref), (o_ref, cv_ref), _, finish_rider = _ride(rider, 4, 2, refs, (N_HEAD_PAIRS, nq))
        qi = pl.program_id(1)
        lane_is_first, after, _, causal = _attn_consts(blk)
        qcat = _split_heads(q_ref[...], lane_is_first)

        def block(kb, carry, mask):
            run, acc = carry
            rows = pl.ds(pl.multiple_of(kb * blk, blk), blk)
            lb, l1m, _, _ = _scores(qcat, k_ref[rows, :], mask)
            hi, lo = _split_bf16(l1m)
            w = jnp.exp(lb + (_dot(hi, after) + _dot(lo, after) + run))
            if mask is not None:
                w = jnp.where(mask, w, 0.0)
            vcat = _split_heads(v_ref[rows, :], lane_is_first)
            acc = acc + _dot(_side_by_side(w.astype(BF16), blk), vcat)
            return run + jnp.sum(l1m, axis=-1, keepdims=True), acc

        carry = block(qi, (jnp.zeros((2 * blk, 1), F32), jnp.zeros((blk, HEAD_PAIR), F32)), causal)
        _, o = _while_mass_left(qi, carry, lambda kb, c: block(kb, c, None))
        o_ref[...] = o
        ga = ga_ref[...]
        cv_ref[...] = (o * (ga * _sigmoid(ga))).astype(BF16)
        finish_rider()

    qspec = pl.BlockSpec((blk, HEAD_PAIR), lambda p, i: (i, p))
    extra = rider or _NO_RIDER
    return pl.pallas_call(
        body, name=name, grid=(N_HEAD_PAIRS, nq),
        in_specs=[qspec,
                  pl.BlockSpec((t, HEAD_PAIR), lambda p, i: (0, N_HEAD_PAIRS + p)),
                  pl.BlockSpec((t, HEAD_PAIR), lambda p, i: (0, 2 * N_HEAD_PAIRS + p)),
                  pl.BlockSpec((blk, HEAD_PAIR), lambda p, i: (i, GA_BLOCK + p))] + extra.specs,
        out_specs=[qspec, qspec] + extra.specs,
        out_shape=[jax.ShapeDtypeStruct((t, BRANCH_WIDTH), F32), jax.ShapeDtypeStruct((t, BRANCH_WIDTH), BF16)] + extra.out_shape,
        scratch_shapes=extra.scratch_shapes,
        compiler_params=_params(2),
    )(qkv, qkv, qkv, gmga, *extra.srcs)


def _attn_bwd(qkv, o, gmga, dcv, *, name, rider=None):
    t = qkv.shape[0]
    blk = _tile(t, ATTN_BLOCK)
    nq = t // blk

    def body(*refs):
        ins, outs, (dk_acc, dv_acc), finish_rider = _ride(rider, 6, 4, refs, (N_HEAD_PAIRS, nq))
        q_ref, k_ref, v_ref, o_ref, ga_ref, dcv_ref = ins
        dq_ref, dk_ref, dv_ref, dga_ref = outs
        qi = pl.program_id(1)
        lane_is_first, after, from_here, causal = _attn_consts(blk)

        @pl.when(qi == 0)
        def _():
            dk_acc[...] = jnp.zeros_like(dk_acc)
            dv_acc[...] = jnp.zeros_like(dv_acc)

        ga, ov, dcvv = ga_ref[...], o_ref[...], dcv_ref[...]
        sg = _sigmoid(ga)
        dob = (dcvv * (ga * sg)).astype(BF16)
        dga_ref[...] = (dcvv * ov * _silu_grad(ga, sg)).astype(BF16)
        gt = dob.astype(F32) * ov
        g_total = jnp.concatenate(
            [jnp.sum(jnp.where(lane_is_first, gt, 0.0), axis=-1, keepdims=True),
             jnp.sum(jnp.where(lane_is_first, 0.0, gt), axis=-1, keepdims=True)], axis=0)
        qcat = _split_heads(q_ref[...], lane_is_first)
        docat = _split_heads(dob, lane_is_first)

        def block(kb, carry, mask):
            run, g_run, dq = carry
            rows = pl.ds(pl.multiple_of(kb * blk, blk), blk)
            kblk = k_ref[rows, :]
            lb, l1m, sig, one_m_sig = _scores(qcat, kblk, mask)
            hi, lo = _split_bf16(l1m)
            w = jnp.exp(lb + (_dot(hi, after) + _dot(lo, after) + run))
            if mask is not None:
                w = jnp.where(mask, w, 0.0)
            wb = w.astype(BF16)
            g = _dot_nt(docat, v_ref[rows, :]) * wb.astype(F32)
            ghi, glo = _split_bf16(g)
            g_before = g_total - g_run - (_dot(ghi, from_here) + _dot(glo, from_here))
            dz = g * one_m_sig - g_before * sig
            if mask is not None:
                dz = jnp.where(mask, dz, 0.0)
            dzb = dz.astype(BF16)
            dq = dq + _dot(_side_by_side(dzb, blk), _split_heads(kblk, lane_is_first))
            dk_acc[rows, :] += _dot_tn(dzb, qcat)
            dv_acc[rows, :] += _dot_tn(wb, docat)
            return (run + jnp.sum(l1m, axis=-1, keepdims=True), g_run + jnp.sum(g, axis=-1, keepdims=True), dq)

        zero = jnp.zeros((2 * blk, 1), F32)
        carry = block(qi, (zero, zero, jnp.zeros((blk, HEAD_PAIR), F32)), causal)
        _, _, dq = _while_mass_left(qi, carry, lambda kb, c: block(kb, c, None))
        dq_ref[...] = (dq * ATTN_SCALE).astype(BF16)

        @pl.when(qi == nq - 1)
        def _():
            dk_ref[...] = dk_acc[...].astype(BF16)
            dv_ref[...] = dv_acc[...].astype(BF16)

        finish_rider()

    qspec = pl.BlockSpec((blk, HEAD_PAIR), lambda p, i: (i, p))
    whole = pl.BlockSpec((t, HEAD_PAIR), lambda p, i: (0, p))
    out = jax.ShapeDtypeStruct((t, BRANCH_WIDTH), BF16)
    extra = rider or _NO_RIDER
    return pl.pallas_call(
        body, name=name, grid=(N_HEAD_PAIRS, nq),
        in_specs=[qspec,
                  pl.BlockSpec((t, HEAD_PAIR), lambda p, i: (0, N_HEAD_PAIRS + p)),
                  pl.BlockSpec((t, HEAD_PAIR), lambda p, i: (0, 2 * N_HEAD_PAIRS + p)),
                  qspec,
                  pl.BlockSpec((blk, HEAD_PAIR), lambda p, i: (i, GA_BLOCK + p)),
                  qspec] + extra.specs,
        out_specs=[qspec, whole, whole, qspec] + extra.specs,
        out_shape=[out, out, out, out] + extra.out_shape,
        scratch_shapes=[pltpu.VMEM((t, HEAD_PAIR), F32), pltpu.VMEM((t, HEAD_PAIR), F32)] + extra.scratch_shapes,
        compiler_params=_params(2),
    )(qkv, qkv, qkv, o, gmga, dcv, *extra.srcs)


def _mesh_position():
    x, y, c = lax.axis_index("x"), lax.axis_index("y"), lax.axis_index("c")
    return x, y, c, 4 * x + 2 * y + c


def _flipped(x, y, c, k):
    return (1 - x if k & 4 else x, 1 - y if k & 2 else y, 1 - c if k & 1 else c)


def _all_to_all(srcs, *, name, same_block):
    ex = _Exchange(srcs, same_block)

    def body(*refs):
        ex.start(refs[:ex.n], refs[ex.n:2 * ex.n], refs[2 * ex.n:])
        ex.finish(refs[:ex.n], refs[ex.n:2 * ex.n], refs[2 * ex.n:])

    return pl.pallas_call(
        body, name=name, in_specs=ex.specs, out_specs=ex.specs, out_shape=ex.out_shape,
        scratch_shapes=ex.scratch_shapes,
    )(*srcs)


class _Exchange:
    def __init__(self, srcs, same_block):
        self.srcs, self.same_block, self.n = list(srcs), same_block, len(srcs)
        self.specs = [pl.BlockSpec(memory_space=pl.ANY)] * self.n
        self.out_shape = [jax.ShapeDtypeStruct((N_DEV,) + tuple(s.shape if same_block else s.shape[1:]), s.dtype)
                          for s in self.srcs]
        self.scratch_shapes = [pltpu.SemaphoreType.DMA((N_DEV - 1, self.n)), pltpu.SemaphoreType.DMA((N_DEV - 1, self.n)),
                               pltpu.SemaphoreType.DMA((self.n,))] if self.n else []

    def _copies(self, src_refs, dst_refs, sems):
        send_sems, recv_sems, local_sems = sems
        x, y, c, me = _mesh_position()

        def outgoing(i, j):
            return src_refs[i] if self.same_block else src_refs[i].at[j]

        def remote(i, k, slot):
            return pltpu.make_async_remote_copy(
                src_ref=outgoing(i, jnp.bitwise_xor(me, k)), dst_ref=dst_refs[i].at[slot],
                send_sem=send_sems.at[k - 1, i], recv_sem=recv_sems.at[k - 1, i],
                device_id=_flipped(x, y, c, k), device_id_type=MESH)

        pairs = [(i, k) for k in range(1, N_DEV) for i in range(self.n)]
        mine = [pltpu.make_async_copy(outgoing(i, me), dst_refs[i].at[me], local_sems.at[i]) for i in range(self.n)]
        sent = [remote(i, k, me) for i, k in pairs]
        arrivals = [remote(i, k, jnp.bitwise_xor(me, k)) for i, k in pairs]
        return mine, sent, arrivals

    def start(self, src_refs, dst_refs, sems):
        mine, sent, _ = self._copies(src_refs, dst_refs, sems)
        for cp in mine + sent:
            cp.start()

    def finish(self, src_refs, dst_refs, sems):
        mine, sent, arrivals = self._copies(src_refs, dst_refs, sems)
        for cp in arrivals:
            cp.wait_recv()
        for cp in sent:
            cp.wait_send()
        for cp in mine:
            cp.wait()


_NO_RIDER = _Exchange([], True)


def _adamw(parts, w, m, v, *, name):
    layers, rows, cols = w.shape
    assert len(parts) == layers
    tr = rows
    while tr * cols > ADAM_TILE_ELEMS and tr % 32 == 0:
        tr //= 2
    row = pl.BlockSpec((1, tr, cols), lambda l, i: (l, i, 0))

    def body(*refs):
        p_refs = refs[:layers]
        w_ref, m_ref, v_ref, g_ref, d_ref, nm_ref, nv_ref = refs[layers:]
        layer = pl.program_id(0)
        g = None
        for k in range(N_DEV):
            part = p_refs[0][k]
            for l in range(1, layers):
                part = jnp.where(layer == l, p_refs[l][k], part)
            g = part.astype(F32) if g is None else g + part.astype(F32)
        m2 = ADAM_B1 * m_ref[0] + (1.0 - ADAM_B1) * g
        v2 = ADAM_B2 * v_ref[0] + (1.0 - ADAM_B2) * (g * g)
        m_hat = m2 / (1.0 - ADAM_B1 ** ADAM_STEP)
        v_hat = v2 / (1.0 - ADAM_B2 ** ADAM_STEP)
        g_ref[0] = g
        d_ref[0] = -ADAM_LR * (m_hat / (jnp.sqrt(v_hat) + ADAM_EPS) + ADAM_WD * w_ref[0])
        nm_ref[0] = m2
        nv_ref[0] = v2

    out = jax.ShapeDtypeStruct((layers, rows, cols), F32)
    return pl.pallas_call(
        body, name=name, grid=(layers, rows // tr),
        in_specs=[pl.BlockSpec((N_DEV, tr, cols), lambda l, i: (0, i, 0))] * layers + [row, row, row],
        out_specs=[row, row, row, row], out_shape=[out, out, out, out],
        compiler_params=_params(2),
    )(*parts, w, m, v)


MATMUL_WEIGHTS = ("w_in", "w_pool_out", "w_conv_out", "w_attn_out", "w_o")
SMALL = ("conv_w", "norm_pre", "pool_w", "pool_b", "pool_scale", "conv_b", "conv_ln_g", "conv_ln_b", "norm_post")
WEIGHT_ORDER = ("norm_pre", "w_in", "pool_w", "pool_b", "pool_scale", "w_pool_out", "conv_w", "conv_b",
                "conv_ln_g", "conv_ln_b", "w_conv_out", "w_attn_out", "w_o", "norm_post")


def _shard_axis(name):
    return -2 if name == "w_o" else -1


def _pack_rows(flat_parts, row_multiple):
    flat = jnp.concatenate(flat_parts, axis=-1)
    n = flat.shape[-1]
    chunk = row_multiple * LANES
    total = -(-n // chunk) * chunk
    pad = [(0, 0)] * (flat.ndim - 1) + [(0, total - n)]
    return jnp.pad(flat, pad).reshape(flat.shape[:-1] + (total // LANES, LANES))


def _unpack(buf, shapes):
    flat = buf.reshape(-1)
    out, at = {}, 0
    for name, shape in shapes:
        n = 1
        for s in shape:
            n *= s
        out[name] = flat[at:at + n].reshape(shape)
        at += n
    return out


def _to_dest_major(name, full):
    axis = full.ndim + _shard_axis(name)
    n = full.shape[axis] // N_DEV
    return jnp.stack([lax.slice_in_dim(full, d * n, (d + 1) * n, axis=axis) for d in range(N_DEV)])


def _from_source_major(name, gathered):
    return jnp.concatenate([gathered[d] for d in range(N_DEV)], axis=_shard_axis(name))


def kernel(x, norm_pre, w_in, pool_w, pool_b, pool_scale, w_pool_out, conv_w, conv_b, conv_ln_g, conv_ln_b, w_conv_out, w_attn_out, w_o, norm_post, loss_target, m_norm_pre, m_w_in, m_pool_w, m_pool_b, m_pool_scale, m_w_pool_out, m_conv_w, m_conv_b, m_conv_ln_g, m_conv_ln_b, m_w_conv_out, m_w_attn_out, m_w_o, m_norm_post, v_norm_pre, v_w_in, v_pool_w, v_pool_b, v_pool_scale, v_w_pool_out, v_conv_w, v_conv_b, v_conv_ln_g, v_conv_ln_b, v_w_conv_out, v_w_attn_out, v_w_o, v_norm_post):
    weights = dict(norm_pre=norm_pre, w_in=w_in, pool_w=pool_w, pool_b=pool_b, pool_scale=pool_scale,
                   w_pool_out=w_pool_out, conv_w=conv_w, conv_b=conv_b, conv_ln_g=conv_ln_g, conv_ln_b=conv_ln_b,
                   w_conv_out=w_conv_out, w_attn_out=w_attn_out, w_o=w_o, norm_post=norm_post)
    mom1 = dict(norm_pre=m_norm_pre, w_in=m_w_in, pool_w=m_pool_w, pool_b=m_pool_b, pool_scale=m_pool_scale,
                w_pool_out=m_w_pool_out, conv_w=m_conv_w, conv_b=m_conv_b, conv_ln_g=m_conv_ln_g, conv_ln_b=m_conv_ln_b,
                w_conv_out=m_w_conv_out, w_attn_out=m_w_attn_out, w_o=m_w_o, norm_post=m_norm_post)
    mom2 = dict(norm_pre=v_norm_pre, w_in=v_w_in, pool_w=v_pool_w, pool_b=v_pool_b, pool_scale=v_pool_scale,
                w_pool_out=v_w_pool_out, conv_w=v_conv_w, conv_b=v_conv_b, conv_ln_g=v_conv_ln_g, conv_ln_b=v_conv_ln_b,
                w_conv_out=v_w_conv_out, w_attn_out=v_w_attn_out, w_o=v_w_o, norm_post=v_norm_post)
    xs = x[0]
    target = loss_target[0]

    conv_rows = jnp.pad(conv_w, ((0, 0), (0, CONV_HALO - CONV_KERNEL), (0, 0)))
    shards = [[weights[n][l].astype(BF16) for n in MATMUL_WEIGHTS] for l in range(DEPTH)]
    gathered = _all_to_all(shards[0] + [conv_rows], name="gather_weights", same_block=True)
    full = [{n: _from_source_major(n, g) for n, g in zip(MATMUL_WEIGHTS, gathered)}, None]
    conv_full = _from_source_major("conv_w", gathered[-1])

    def in_sections(w):
        return dict(pg=w[:, 0:1024], c2gc=w[:, 1024:2560], q=w[:, 2560:3072], k=w[:, 3072:3584], v=w[:, 3584:4096],
                    gmga=jnp.concatenate([w[:, 4608:7680], w[:, 4096:4608]], axis=1))

    saved = []
    cur = xs
    for l in range(DEPTH):
        sec = in_sections(full[l]["w_in"])
        w_qkv = jnp.concatenate([sec["q"] * ATTN_SCALE, sec["k"], sec["v"]], axis=1)
        pw = pool_w[l].astype(BF16)
        pb, ps = pool_b[l].reshape(1, -1), pool_scale[l].reshape(1, -1)
        cb, lg, lb = conv_b[l].reshape(1, -1), conv_ln_g[l].reshape(1, -1), conv_ln_b[l].reshape(1, -1)
        h = _rms_fwd(cur, norm_pre[l].reshape(1, -1), name=f"rms_pre_fwd_{l}")
        pg = _matmul(h, sec["pg"], mode="nn", name=f"proj_pg_{l}")
        c2gc = _matmul(h, sec["c2gc"], mode="nn", name=f"proj_c2gc_{l}")
        qkv = _matmul(h, w_qkv, mode="nn", name=f"proj_qkv_{l}", out_dtype=BF16)
        gmga = _matmul(h, sec["gmga"], mode="nn", name=f"proj_gmga_{l}")
        a_act = _pool_fwd(pg, pw, pb, ps, name=f"pool_fwd_{l}")
        b_act = _conv_fwd(c2gc, conv_full[l], cb, lg, lb, name=f"conv_fwd_{l}")
        rider = _Exchange(shards[l + 1], True) if l + 1 < DEPTH else None
        o, c_act, *arrived = _attn_fwd(qkv, gmga, name=f"attn_fwd_{l}", rider=rider)
        if rider is not None:
            full[l + 1] = {n: _from_source_major(n, g) for n, g in zip(MATMUL_WEIGHTS, arrived)}
        ya = _matmul(a_act, full[l]["w_pool_out"], mode="nn", name=f"out_pool_{l}")
        yb = _matmul(b_act, full[l]["w_conv_out"], mode="nn", name=f"out_conv_{l}")
        yc = _matmul(c_act, full[l]["w_attn_out"], mode="nn", name=f"out_attn_{l}")
        mix = _merge_fwd(gmga, ya, yb, yc, name=f"merge_fwd_{l}")
        out = _matmul(mix, full[l]["w_o"], mode="nn", name=f"out_proj_{l}")
        nxt = _rms_fwd(out, norm_post[l].reshape(1, -1), name=f"rms_post_fwd_{l}", resid=cur)
        saved.append(dict(x=cur, h=h, pg=pg, c2gc=c2gc, qkv=qkv, gmga=gmga, a=a_act, b=b_act, c=c_act, o=o,
                          ya=ya, yb=yb, yc=yc, mix=mix, out=out, sec=sec, pw=pw, pb=pb, ps=ps, cb=cb, lg=lg, lb=lb))
        cur = nxt

    loss_tile, dx = _loss_head(cur, target, name="loss_head")
    loss = lax.psum(loss_tile[0, 0], ("x", "y", "c"))

    grads = {n: [None] * DEPTH for n in WEIGHT_ORDER}
    parts = {n: [None] * DEPTH for n in MATMUL_WEIGHTS}
    for l in reversed(range(DEPTH)):
        s = saved[l]
        dout, grads["norm_post"][l] = _rms_bwd(s["out"], norm_post[l].reshape(1, -1), dx, name=f"rms_post_bwd_{l}")
        dmix = _matmul(dout, full[l]["w_o"], mode="nt", name=f"d_mix_{l}")
        grads["w_o"][l] = _matmul(s["mix"], dout, mode="tn", name=f"d_w_o_{l}", out_dtype=BF16)
        dya, dyb, dyc, dgm = _merge_bwd(dmix, s["gmga"], s["ya"], s["yb"], s["yc"], name=f"merge_bwd_{l}")
        da = _matmul(dya, full[l]["w_pool_out"], mode="nt", name=f"d_pool_act_{l}")
        grads["w_pool_out"][l] = _matmul(s["a"], dya, mode="tn", name=f"d_w_pool_out_{l}", out_dtype=BF16)
        db = _matmul(dyb, full[l]["w_conv_out"], mode="nt", name=f"d_conv_act_{l}")
        grads["w_conv_out"][l] = _matmul(s["b"], dyb, mode="tn", name=f"d_w_conv_out_{l}", out_dtype=BF16)
        dc = _matmul(dyc, full[l]["w_attn_out"], mode="nt", name=f"d_attn_act_{l}")
        grads["w_attn_out"][l] = _matmul(s["c"], dyc, mode="tn", name=f"d_w_attn_out_{l}", out_dtype=BF16)
        riding = [(n, l) for n in MATMUL_WEIGHTS[1:]] + ([("w_in", l + 1)] if l + 1 < DEPTH else [])
        rider = _Exchange([_to_dest_major(n, grads[n][k]) for n, k in riding], False)
        dq, dk, dv, dga, *arrived = _attn_bwd(s["qkv"], s["o"], s["gmga"], dc, name=f"attn_bwd_{l}", rider=rider)
        for (n, k), p in zip(riding, arrived):
            parts[n][k] = p
        dc2gc, dcw, dcvec = _conv_bwd(s["c2gc"], db, conv_full[l], s["cb"], s["lg"], s["lb"], name=f"conv_bwd_{l}")
        dpg, dpw, dpvec = _pool_bwd(s["pg"], da, s["pw"], s["pb"], s["ps"], name=f"pool_bwd_{l}")
        grads["conv_w"][l] = dcw[:CONV_KERNEL]
        grads["conv_b"][l], grads["conv_ln_g"][l], grads["conv_ln_b"][l] = dcvec[0], dcvec[1], dcvec[2]
        grads["pool_w"][l] = dpw
        grads["pool_b"][l] = dpvec[0].reshape(4, POOL_GROUP_DIM)
        grads["pool_scale"][l] = dpvec[1]
        dsec = dict(pg=dpg, c2gc=dc2gc, q=dq, k=dk, v=dv, gmga=jnp.concatenate([dgm, dga], axis=1))
        dh, dws = None, {}
        for n in ("pg", "c2gc", "q", "k", "v", "gmga"):
            dh = _matmul(dsec[n], s["sec"][n], mode="nt", name=f"d_h_{n}_{l}", acc=dh)
            dws[n] = _matmul(s["h"], dsec[n], mode="tn", name=f"d_w_in_{n}_{l}", out_dtype=BF16)
        grads["w_in"][l] = jnp.concatenate(
            [dws["pg"], dws["c2gc"], dws["q"], dws["k"], dws["v"], dws["gmga"][:, 3 * D_MODEL:], dws["gmga"][:, :3 * D_MODEL]],
            axis=1)
        dx, dg_pre = _rms_bwd(s["x"], norm_pre[l].reshape(1, -1), dh, name=f"rms_pre_bwd_{l}", resid=dx)
        grads["norm_pre"][l] = dg_pre.reshape(-1)
        grads["norm_post"][l] = grads["norm_post"][l].reshape(-1)
    small_grads = {n: jnp.stack(grads[n]) for n in SMALL}
    replicated = jnp.concatenate([small_grads[n].reshape(-1) for n in SMALL[1:]])
    small = _pack_rows([_to_dest_major("conv_w", small_grads["conv_w"]).reshape(N_DEV, -1),
                        jnp.broadcast_to(replicated, (N_DEV, replicated.size))], 16).astype(BF16)
    parts["w_in"][0], small_parts = _all_to_all([_to_dest_major("w_in", grads["w_in"][0]), small],
                                                name="exchange_grads", same_block=False)

    outs = [dict(), dict(), dict(), dict()]
    for n in MATMUL_WEIGHTS:
        res = _adamw(parts[n], weights[n], mom1[n], mom2[n], name=f"adamw_{n}")
        for o, r in zip(outs, res):
            o[n] = r

    def packed(tree):
        return _pack_rows([tree[n].reshape(-1) for n in SMALL], 16)[None]

    shapes = [(n, weights[n].shape) for n in SMALL]
    res = _adamw([small_parts], packed(weights), packed(mom1), packed(mom2), name="adamw_small")
    for o, r in zip(outs, res):
        o.update(_unpack(r, shapes))
    return (loss, dx[None], *[o[n] for o in outs for n in WEIGHT_ORDER])
```

```python
import functools

import jax
import jax.numpy as jnp
from jax import lax
from jax.experimental import pallas as pl
from jax.experimental.pallas import tpu as pltpu

F32 = jnp.float32
BF16 = jnp.bfloat16

D_MODEL = 1024
DEPTH = 2
POOL_WINDOWS = (2, 4, 8, 16)
POOL_GROUP_DIM = 128
BRANCH_WIDTH = 512
CONV_KERNEL = 31
CONV_HALO = 32
POOL_HALO = 16
HEAD_DIM = 64
HEAD_PAIR = 128
N_HEAD_PAIRS = 4
ATTN_SCALE = 0.125
LOG_F32_ZERO = -104.0
RMS_EPS = 1e-6
LN_EPS = 1e-5
N_DEV = 8
LANES = 128

ADAM_LR = 0.001
ADAM_B1 = 0.9
ADAM_B2 = 0.999
ADAM_EPS = 1e-08
ADAM_WD = 0.01
ADAM_STEP = 10

ROW_TILE = 256
ATTN_BLOCK = 256
MM_TILE = 1024
ADAM_TILE_ELEMS = 256 * 1024
VMEM_LIMIT = 48 * 1024 * 1024

MESH = pl.DeviceIdType.MESH


def _params(n_axes):
    return pltpu.CompilerParams(dimension_semantics=("arbitrary",) * n_axes, vmem_limit_bytes=VMEM_LIMIT)


def _tile(n, pref):
    if n <= pref:
        return n
    t = (pref // LANES) * LANES
    while n % t:
        t -= LANES
    return t


def _dot(a, b):
    return jnp.dot(a, b, preferred_element_type=F32)


def _dot_nt(a, b):
    return lax.dot_general(a, b, (((1,), (1,)), ((), ())), preferred_element_type=F32)


def _dot_tn(a, b):
    return lax.dot_general(a, b, (((0,), (0,)), ((), ())), preferred_element_type=F32)


def _sigmoid(x):
    return 1.0 / (1.0 + jnp.exp(-x))


def _silu_grad(x, s):
    return s * (1.0 + x * (1.0 - s))


def _matmul(a, b, *, mode, name, out_dtype=F32, acc=None):
    if mode == "nn":
        (m, k), n = a.shape, b.shape[1]
    elif mode == "nt":
        (m, k), n = a.shape, b.shape[0]
    else:
        (k, m), n = a.shape, b.shape[1]
    tm, tn, tk = _tile(m, MM_TILE), _tile(n, MM_TILE), _tile(k, MM_TILE)
    nk = k // tk
    dot = {"nn": _dot, "nt": _dot_nt, "tn": _dot_tn}[mode]
    a_spec = pl.BlockSpec((tk, tm), lambda i, j, kk: (kk, i)) if mode == "tn" else pl.BlockSpec((tm, tk), lambda i, j, kk: (i, kk))
    b_spec = pl.BlockSpec((tn, tk), lambda i, j, kk: (j, kk)) if mode == "nt" else pl.BlockSpec((tk, tn), lambda i, j, kk: (kk, j))
    o_spec = pl.BlockSpec((tm, tn), lambda i, j, kk: (i, j))
    has_acc = acc is not None

    def body(*refs):
        a_ref, b_ref = refs[0], refs[1]
        acc_in = refs[2] if has_acc else None
        o_ref = refs[3] if has_acc else refs[2]
        part = dot(a_ref[...], b_ref[...])
        if nk == 1:
            if has_acc:
                part = part + acc_in[...]
            o_ref[...] = part.astype(out_dtype)
            return
        scr = refs[-1]
        kk = pl.program_id(2)

        @pl.when(kk == 0)
        def _():
            scr[...] = part + acc_in[...] if has_acc else part

        @pl.when(kk > 0)
        def _():
            scr[...] += part

        @pl.when(kk == nk - 1)
        def _():
            o_ref[...] = scr[...].astype(out_dtype)

    in_specs = [a_spec, b_spec] + ([o_spec] if has_acc else [])
    args = (a, b) + ((acc,) if has_acc else ())
    return pl.pallas_call(
        body, name=name, grid=(m // tm, n // tn, nk),
        in_specs=in_specs, out_specs=o_spec,
        out_shape=jax.ShapeDtypeStruct((m, n), out_dtype),
        scratch_shapes=[pltpu.VMEM((tm, tn), F32)] if nk > 1 else [],
        compiler_params=_params(3),
    )(*args)


def _rms_fwd(x, g, *, name, resid=None):
    t = x.shape[0]
    tm = _tile(t, ROW_TILE)
    row = pl.BlockSpec((tm, D_MODEL), lambda i: (i, 0))
    vec = pl.BlockSpec((1, D_MODEL), lambda i: (0, 0))
    has_resid = resid is not None

    def body(*refs):
        x_ref, g_ref = refs[0], refs[1]
        o_ref = refs[-1]
        xv = x_ref[...]
        y = xv * lax.rsqrt(jnp.mean(xv * xv, axis=-1, keepdims=True) + RMS_EPS) * g_ref[...]
        if has_resid:
            o_ref[...] = refs[2][...] + y
        else:
            o_ref[...] = y.astype(BF16)

    return pl.pallas_call(
        body, name=name, grid=(t // tm,),
        in_specs=[row, vec] + ([row] if has_resid else []), out_specs=row,
        out_shape=jax.ShapeDtypeStruct((t, D_MODEL), F32 if has_resid else BF16),
        compiler_params=_params(1),
    )(*((x, g) + ((resid,) if has_resid else ())))


def _rms_bwd(xin, g, dy, *, name, resid=None):
    t = xin.shape[0]
    tm = _tile(t, ROW_TILE)
    row = pl.BlockSpec((tm, D_MODEL), lambda i: (i, 0))
    vec = pl.BlockSpec((1, D_MODEL), lambda i: (0, 0))
    has_resid = resid is not None
    out_dtype = F32 if has_resid else BF16

    def body(*refs):
        x_ref, g_ref, dy_ref = refs[0], refs[1], refs[2]
        dx_ref, dg_ref = refs[-2], refs[-1]
        xv, dyv = x_ref[...], dy_ref[...]
        r = lax.rsqrt(jnp.mean(xv * xv, axis=-1, keepdims=True) + RMS_EPS)
        a = dyv * g_ref[...]
        dx = r * a - xv * (r * r * r) * jnp.mean(a * xv, axis=-1, keepdims=True)
        if has_resid:
            dx = dx + refs[3][...]
        dx_ref[...] = dx.astype(out_dtype)
        part = jnp.sum(dyv * xv * r, axis=0, keepdims=True)

        @pl.when(pl.program_id(0) == 0)
        def _():
            dg_ref[...] = part

        @pl.when(pl.program_id(0) > 0)
        def _():
            dg_ref[...] += part

    return pl.pallas_call(
        body, name=name, grid=(t // tm,),
        in_specs=[row, vec, row] + ([row] if has_resid else []), out_specs=[row, vec],
        out_shape=[jax.ShapeDtypeStruct((t, D_MODEL), out_dtype), jax.ShapeDtypeStruct((1, D_MODEL), F32)],
        compiler_params=_params(1),
    )(*((xin, g, dy) + ((resid,) if has_resid else ())))


def _loss_head(x, target, *, name):
    t = x.shape[0]
    tm = _tile(t, ROW_TILE)
    row = pl.BlockSpec((tm, D_MODEL), lambda i: (i, 0))
    acc = pl.BlockSpec((8, LANES), lambda i: (0, 0))

    def body(x_ref, t_ref, l_ref, dx_ref):
        diff = x_ref[...] - t_ref[...]
        dx_ref[...] = diff * (1.0 / D_MODEL)
        part = 0.5 * jnp.sum(jnp.mean(diff * diff, axis=-1, keepdims=True), axis=0, keepdims=True)

        @pl.when(pl.program_id(0) == 0)
        def _():
            l_ref[...] = jnp.zeros((8, LANES), F32) + part

        @pl.when(pl.program_id(0) > 0)
        def _():
            l_ref[...] += part

    return pl.pallas_call(
        body, name=name, grid=(t // tm,),
        in_specs=[row, row], out_specs=[acc, row],
        out_shape=[jax.ShapeDtypeStruct((8, LANES), F32), jax.ShapeDtypeStruct((t, D_MODEL), F32)],
        compiler_params=_params(1),
    )(x, target)


def _window_sum(ext, n_doublings, forward):
    rows = ext.shape[0]
    s, sh = ext, 1
    for _ in range(n_doublings):
        s = s + pltpu.roll(s, sh if forward else rows - sh, 0)
        sh *= 2
    return s


def _pool_fwd(pg, pool_w, pool_b, pool_scale, *, name):
    t = pg.shape[0]
    tm = _tile(t, ROW_TILE)

    def body(pg_ref, w_ref, b_ref, s_ref, o_ref, halo):
        i = pl.program_id(0)

        @pl.when(i == 0)
        def _():
            halo[...] = jnp.zeros_like(halo)

        p = pg_ref[:, :BRANCH_WIDTH]
        gate = pg_ref[:, BRANCH_WIDTH:]
        ext = jnp.concatenate([halo[...], p], axis=0)
        pos = i * tm + lax.broadcasted_iota(jnp.int32, (tm, 1), 0)
        outs = []
        for g, w in enumerate(POOL_WINDOWS):
            cols = slice(g * POOL_GROUP_DIM, (g + 1) * POOL_GROUP_DIM)
            cnt = jnp.minimum(pos + 1, w).astype(F32)
            d = _window_sum(ext[:, cols], g + 1, True)[POOL_HALO:] / cnt - p[:, cols]
            y = (_dot(d.astype(BF16), w_ref[g]) + b_ref[:, cols]) * s_ref[:, cols]
            gg = gate[:, cols]
            outs.append(y * (gg * _sigmoid(gg)))
        o_ref[...] = jnp.concatenate(outs, axis=1).astype(BF16)
        halo[...] = p[tm - POOL_HALO:, :]

    vec = pl.BlockSpec((1, BRANCH_WIDTH), lambda i: (0, 0))
    return pl.pallas_call(
        body, name=name, grid=(t // tm,),
        in_specs=[pl.BlockSpec((tm, 2 * BRANCH_WIDTH), lambda i: (i, 0)),
                  pl.BlockSpec((4, POOL_GROUP_DIM, POOL_GROUP_DIM), lambda i: (0, 0, 0)), vec, vec],
        out_specs=pl.BlockSpec((tm, BRANCH_WIDTH), lambda i: (i, 0)),
        out_shape=jax.ShapeDtypeStruct((t, BRANCH_WIDTH), BF16),
        scratch_shapes=[pltpu.VMEM((POOL_HALO, BRANCH_WIDTH), F32)],
        compiler_params=_params(1),
    )(pg, pool_w, pool_b, pool_scale)


def _pool_bwd(pg, d_out, pool_w, pool_b, pool_scale, *, name):
    t = pg.shape[0]
    tm = _tile(t, ROW_TILE)
    nt = t // tm
    halo_per_tile = tm // POOL_HALO

    def body(pg_ref, halo_ref, do_ref, w_ref, b_ref, s_ref, dpg_ref, dw_ref, dvec_ref, carry):
        i = pl.program_id(0)
        ri = nt - 1 - i

        @pl.when(i == 0)
        def _():
            carry[...] = jnp.zeros_like(carry)
            dw_ref[...] = jnp.zeros_like(dw_ref)
            dvec_ref[...] = jnp.zeros_like(dvec_ref)

        p = pg_ref[:, :BRANCH_WIDTH]
        gate = pg_ref[:, BRANCH_WIDTH:]
        hp = jnp.where(ri > 0, halo_ref[:, :BRANCH_WIDTH], 0.0)
        ext = jnp.concatenate([hp, p], axis=0)
        pos = ri * tm + lax.broadcasted_iota(jnp.int32, (tm, 1), 0)
        dps, dgs, dbs, dss = [], [], [], []
        for g, w in enumerate(POOL_WINDOWS):
            cols = slice(g * POOL_GROUP_DIM, (g + 1) * POOL_GROUP_DIM)
            cnt = jnp.minimum(pos + 1, w).astype(F32)
            d = (_window_sum(ext[:, cols], g + 1, True)[POOL_HALO:] / cnt - p[:, cols]).astype(BF16)
            y1 = _dot(d, w_ref[g]) + b_ref[:, cols]
            scale = s_ref[:, cols]
            y2 = y1 * scale
            gg = gate[:, cols]
            sg = _sigmoid(gg)
            do = do_ref[:, cols]
            dy2 = do * (gg * sg)
            dgs.append(do * y2 * _silu_grad(gg, sg))
            dss.append(jnp.sum(dy2 * y1, axis=0, keepdims=True))
            dy1 = dy2 * scale
            dbs.append(jnp.sum(dy1, axis=0, keepdims=True))
            dy1b = dy1.astype(BF16)
            dw_ref[g] += _dot_tn(d, dy1b)
            dd = _dot_nt(dy1b, w_ref[g])
            dpool = dd / cnt
            dext = jnp.concatenate([dpool, carry[:, cols]], axis=0)
            dps.append(_window_sum(dext, g + 1, False)[:tm] - dd)
            carry[:, cols] = dpool[:POOL_HALO]
        dpg_ref[...] = jnp.concatenate(dps + dgs, axis=1).astype(BF16)
        dvec_ref[0:1, :] += jnp.concatenate(dbs, axis=1)
        dvec_ref[1:2, :] += jnp.concatenate(dss, axis=1)

    vec = pl.BlockSpec((1, BRANCH_WIDTH), lambda i: (0, 0))
    wspec = pl.BlockSpec((4, POOL_GROUP_DIM, POOL_GROUP_DIM), lambda i: (0, 0, 0))
    return pl.pallas_call(
        body, name=name, grid=(nt,),
        in_specs=[pl.BlockSpec((tm, 2 * BRANCH_WIDTH), lambda i: (nt - 1 - i, 0)),
                  pl.BlockSpec((POOL_HALO, 2 * BRANCH_WIDTH), lambda i: (jnp.maximum((nt - 1 - i) * halo_per_tile - 1, 0), 0)),
                  pl.BlockSpec((tm, BRANCH_WIDTH), lambda i: (nt - 1 - i, 0)), wspec, vec, vec],
        out_specs=[pl.BlockSpec((tm, 2 * BRANCH_WIDTH), lambda i: (nt - 1 - i, 0)), wspec,
                   pl.BlockSpec((8, BRANCH_WIDTH), lambda i: (0, 0))],
        out_shape=[jax.ShapeDtypeStruct((t, 2 * BRANCH_WIDTH), BF16),
                   jax.ShapeDtypeStruct((4, POOL_GROUP_DIM, POOL_GROUP_DIM), F32),
                   jax.ShapeDtypeStruct((8, BRANCH_WIDTH), F32)],
        scratch_shapes=[pltpu.VMEM((POOL_HALO, BRANCH_WIDTH), F32)],
        compiler_params=_params(1),
    )(pg, pg, d_out, pool_w, pool_b, pool_scale)


CONV_TILE_ROWS = 64


def _for_conv_tiles(tm, fn):
    def step(it, carry):
        rows = pl.ds(pl.multiple_of(it * CONV_TILE_ROWS, CONV_TILE_ROWS), CONV_TILE_ROWS)
        for c in range(0, BRANCH_WIDTH, LANES):
            fn(rows, slice(c, c + LANES))
        return carry

    lax.fori_loop(0, tm // CONV_TILE_ROWS, step, 0)


def _sublane_shifts(shifted_ref, x, direction):
    rows = x.shape[0]
    shifted_ref[0] = x
    for b in range(1, 8):
        shifted_ref[b] = pltpu.roll(x, b if direction > 0 else rows - b, 0)


CONV_REACH = 8 * ((CONV_KERNEL - 1) // 8)


def _tap_tiles(shifted_ref, base, rows, cols, direction):
    for b in range(8):
        lo = pl.multiple_of(base + rows.start - (CONV_REACH if direction > 0 else 0), 8)
        window = shifted_ref[b, pl.ds(lo, CONV_TILE_ROWS + CONV_REACH), cols]
        for a in range((CONV_KERNEL - 1 - b) // 8 + 1):
            off = CONV_REACH - 8 * a if direction > 0 else 8 * a
            yield 8 * a + b, window[off:off + CONV_TILE_ROWS]


def _tap_sum(shifted_ref, w_ref, base, rows, cols, direction):
    acc = None
    for j, tile in _tap_tiles(shifted_ref, base, rows, cols, direction):
        k = CONV_KERNEL - 1 - j
        term = w_ref[k:k + 1, cols] * tile
        acc = term if acc is None else acc + term
    return acc


def _causal_conv(ext8_ref, w_ref, cv_ref, tm):
    def tile(rows, cols):
        cv_ref[rows, cols] = _tap_sum(ext8_ref, w_ref, CONV_HALO, rows, cols, 1)

    _for_conv_tiles(tm, tile)


def _conv_fwd(c2gc, conv_w, conv_b, ln_g, ln_b, *, name):
    t = c2gc.shape[0]
    tm = _tile(t, ROW_TILE)

    def body(c_ref, w_ref, cb_ref, g_ref, b_ref, o_ref, halo, ext8_ref, cv_ref):
        @pl.when(pl.program_id(0) == 0)
        def _():
            halo[...] = jnp.zeros_like(halo)

        u = c_ref[:, :BRANCH_WIDTH] * _sigmoid(c_ref[:, BRANCH_WIDTH:2 * BRANCH_WIDTH])
        gate = c_ref[:, 2 * BRANCH_WIDTH:]
        _sublane_shifts(ext8_ref, jnp.concatenate([halo[...], u], axis=0), 1)
        halo[...] = u[tm - CONV_HALO:, :]
        _causal_conv(ext8_ref, w_ref, cv_ref, tm)
        cv = cv_ref[...] + cb_ref[...]
        mu = jnp.mean(cv, axis=-1, keepdims=True)
        xc = cv - mu
        var = jnp.mean(xc * xc, axis=-1, keepdims=True)
        ln = xc * lax.rsqrt(var + LN_EPS) * g_ref[...] + b_ref[...]
        o_ref[...] = (ln * _sigmoid(ln) * (gate * _sigmoid(gate))).astype(BF16)

    vec = pl.BlockSpec((1, BRANCH_WIDTH), lambda i: (0, 0))
    return pl.pallas_call(
        body, name=name, grid=(t // tm,),
        in_specs=[pl.BlockSpec((tm, 3 * BRANCH_WIDTH), lambda i: (i, 0)),
                  pl.BlockSpec((CONV_HALO, BRANCH_WIDTH), lambda i: (0, 0)), vec, vec, vec],
        out_specs=pl.BlockSpec((tm, BRANCH_WIDTH), lambda i: (i, 0)),
        out_shape=jax.ShapeDtypeStruct((t, BRANCH_WIDTH), BF16),
        scratch_shapes=[pltpu.VMEM((CONV_HALO, BRANCH_WIDTH), F32), pltpu.VMEM((8, tm + CONV_HALO, BRANCH_WIDTH), F32),
                        pltpu.VMEM((tm, BRANCH_WIDTH), F32)],
        compiler_params=_params(1),
    )(c2gc, conv_w, conv_b, ln_g, ln_b)


def _conv_bwd(c2gc, d_out, conv_w, conv_b, ln_g, ln_b, *, name):
    t = c2gc.shape[0]
    tm = _tile(t, ROW_TILE)
    nt = t // tm
    halo_per_tile = tm // CONV_HALO

    def body(c_ref, halo_ref, do_ref, w_ref, cb_ref, g_ref, b_ref, dc_ref, dw_ref, dvec_ref,
             carry, ext8_ref, cv_ref, dext8_ref, du_ref, dw_acc):
        i = pl.program_id(0)
        ri = nt - 1 - i

        @pl.when(i == 0)
        def _():
            carry[...] = jnp.zeros_like(carry)
            dw_acc[...] = jnp.zeros_like(dw_acc)
            dvec_ref[...] = jnp.zeros_like(dvec_ref)

        a = c_ref[:, :BRANCH_WIDTH]
        sb = _sigmoid(c_ref[:, BRANCH_WIDTH:2 * BRANCH_WIDTH])
        gate = c_ref[:, 2 * BRANCH_WIDTH:]
        hu = halo_ref[:, :BRANCH_WIDTH] * _sigmoid(halo_ref[:, BRANCH_WIDTH:2 * BRANCH_WIDTH])
        _sublane_shifts(ext8_ref, jnp.concatenate([jnp.where(ri > 0, hu, 0.0), a * sb], axis=0), 1)
        _causal_conv(ext8_ref, w_ref, cv_ref, tm)
        cv = cv_ref[...] + cb_ref[...]
        mu = jnp.mean(cv, axis=-1, keepdims=True)
        xc = cv - mu
        rs = lax.rsqrt(jnp.mean(xc * xc, axis=-1, keepdims=True) + LN_EPS)
        n = xc * rs
        ln = n * g_ref[...] + b_ref[...]
        sl = _sigmoid(ln)
        sgate = _sigmoid(gate)
        do = do_ref[...]
        dgate = do * (ln * sl) * _silu_grad(gate, sgate)
        dln = do * (gate * sgate) * _silu_grad(ln, sl)
        dn = dln * g_ref[...]
        dcv = rs * (dn - jnp.mean(dn, axis=-1, keepdims=True) - n * jnp.mean(dn * n, axis=-1, keepdims=True))
        dvec_ref[0:1, :] += jnp.sum(dcv, axis=0, keepdims=True)
        dvec_ref[1:2, :] += jnp.sum(dln * n, axis=0, keepdims=True)
        dvec_ref[2:3, :] += jnp.sum(dln, axis=0, keepdims=True)
        _sublane_shifts(dext8_ref, jnp.concatenate([dcv, carry[...]], axis=0), -1)
        carry[...] = dcv[:CONV_HALO]
        def tile(rows, cols):
            du_ref[rows, cols] = _tap_sum(dext8_ref, w_ref, 0, rows, cols, -1)
            d_tile = dext8_ref[0, rows, cols]
            for j, u_tile in _tap_tiles(ext8_ref, CONV_HALO, rows, cols, 1):
                prod = d_tile * u_tile
                part = prod[0:8]
                for q in range(8, CONV_TILE_ROWS, 8):
                    part = part + prod[q:q + 8]
                dw_acc[CONV_KERNEL - 1 - j, :, cols] += part

        _for_conv_tiles(tm, tile)
        du = du_ref[...]
        dc_ref[...] = jnp.concatenate([du * sb, du * a * sb * (1.0 - sb), dgate], axis=1).astype(BF16)

        @pl.when(i == nt - 1)
        def _():
            dw_ref[...] = jnp.sum(dw_acc[...], axis=1)

    vec = pl.BlockSpec((1, BRANCH_WIDTH), lambda i: (0, 0))
    wspec = pl.BlockSpec((CONV_HALO, BRANCH_WIDTH), lambda i: (0, 0))
    return pl.pallas_call(
        body, name=name, grid=(nt,),
        in_specs=[pl.BlockSpec((tm, 3 * BRANCH_WIDTH), lambda i: (nt - 1 - i, 0)),
                  pl.BlockSpec((CONV_HALO, 3 * BRANCH_WIDTH), lambda i: (jnp.maximum((nt - 1 - i) * halo_per_tile - 1, 0), 0)),
                  pl.BlockSpec((tm, BRANCH_WIDTH), lambda i: (nt - 1 - i, 0)), wspec, vec, vec, vec],
        out_specs=[pl.BlockSpec((tm, 3 * BRANCH_WIDTH), lambda i: (nt - 1 - i, 0)), wspec,
                   pl.BlockSpec((8, BRANCH_WIDTH), lambda i: (0, 0))],
        out_shape=[jax.ShapeDtypeStruct((t, 3 * BRANCH_WIDTH), BF16),
                   jax.ShapeDtypeStruct((CONV_HALO, BRANCH_WIDTH), F32),
                   jax.ShapeDtypeStruct((8, BRANCH_WIDTH), F32)],
        scratch_shapes=[pltpu.VMEM((CONV_HALO, BRANCH_WIDTH), F32),
                        pltpu.VMEM((8, tm + CONV_HALO, BRANCH_WIDTH), F32), pltpu.VMEM((tm, BRANCH_WIDTH), F32),
                        pltpu.VMEM((8, tm + CONV_HALO, BRANCH_WIDTH), F32), pltpu.VMEM((tm, BRANCH_WIDTH), F32),
                        pltpu.VMEM((CONV_HALO, 8, BRANCH_WIDTH), F32)],
        compiler_params=_params(1),
    )(c2gc, c2gc, d_out, conv_w, conv_b, ln_g, ln_b)


def _merge_fwd(gmga, ya, yb, yc, *, name):
    t = ya.shape[0]
    tm = _tile(t, ROW_TILE)
    row = pl.BlockSpec((tm, D_MODEL), lambda i: (i, 0))

    def body(g0, g1, g2, a_ref, b_ref, c_ref, o_ref):
        m = _sigmoid(g0[...]) * a_ref[...] + _sigmoid(g1[...]) * b_ref[...] + _sigmoid(g2[...]) * c_ref[...]
        o_ref[...] = m.astype(BF16)

    gspecs = [pl.BlockSpec((tm, D_MODEL), functools.partial(lambda i, b: (i, b), b=b)) for b in range(3)]
    return pl.pallas_call(
        body, name=name, grid=(t // tm,),
        in_specs=gspecs + [row, row, row], out_specs=row,
        out_shape=jax.ShapeDtypeStruct((t, D_MODEL), BF16),
        compiler_params=_params(1),
    )(gmga, gmga, gmga, ya, yb, yc)


def _merge_bwd(dm, gmga, ya, yb, yc, *, name):
    t = ya.shape[0]
    tm = _tile(t, ROW_TILE)
    row = pl.BlockSpec((tm, D_MODEL), lambda i: (i, 0))
    wide = pl.BlockSpec((tm, 3 * D_MODEL), lambda i: (i, 0))

    def body(dm_ref, g0, g1, g2, a_ref, b_ref, c_ref, da_ref, db_ref, dc_ref, dg_ref):
        dmv = dm_ref[...]
        for k, (g_ref, y_ref, dy_ref) in enumerate(((g0, a_ref, da_ref), (g1, b_ref, db_ref), (g2, c_ref, dc_ref))):
            s = _sigmoid(g_ref[...])
            dy_ref[...] = (dmv * s).astype(BF16)
            dg_ref[:, k * D_MODEL:(k + 1) * D_MODEL] = (dmv * y_ref[...] * s * (1.0 - s)).astype(BF16)

    gspecs = [pl.BlockSpec((tm, D_MODEL), functools.partial(lambda i, b: (i, b), b=b)) for b in range(3)]
    return pl.pallas_call(
        body, name=name, grid=(t // tm,),
        in_specs=[row] + gspecs + [row, row, row], out_specs=[row, row, row, wide],
        out_shape=[jax.ShapeDtypeStruct((t, D_MODEL), BF16)] * 3 + [jax.ShapeDtypeStruct((t, 3 * D_MODEL), BF16)],
        compiler_params=_params(1),
    )(dm, gmga, gmga, gmga, ya, yb, yc)


GA_BLOCK = 3 * D_MODEL // HEAD_PAIR


def _split_heads(x, lane_is_first):
    zero = jnp.zeros_like(x)
    return jnp.concatenate([jnp.where(lane_is_first, x, zero), jnp.where(lane_is_first, zero, x)], axis=0)


def _side_by_side(x, rows):
    return jnp.concatenate([x[:rows], x[rows:]], axis=1)


def _split_bf16(x):
    hi = x.astype(BF16)
    return hi, (x - hi.astype(F32)).astype(BF16)


def _scores(qcat, kblk, mask):
    z = _dot_nt(qcat, kblk)
    e = jnp.exp(-jnp.abs(z))
    sp = jnp.maximum(z, 0.0) + jnp.log(1.0 + e)
    l1m = -sp
    if mask is not None:
        l1m = jnp.where(mask, l1m, 0.0)
    inv = 1.0 / (1.0 + e)
    pos = z >= 0.0
    return z - sp, l1m, jnp.where(pos, 1.0, e) * inv, jnp.where(pos, e, 1.0) * inv


def _attn_consts(blk):
    lane_is_first = lax.broadcasted_iota(jnp.int32, (1, HEAD_PAIR), 1) < HEAD_DIM
    r = lax.broadcasted_iota(jnp.int32, (blk, blk), 0)
    c = lax.broadcasted_iota(jnp.int32, (blk, blk), 1)
    after = (r > c).astype(BF16)
    from_here = (r >= c).astype(BF16)
    qrow = lax.broadcasted_iota(jnp.int32, (2 * blk, blk), 0)
    qrow = jnp.where(qrow >= blk, qrow - blk, qrow)
    causal = lax.broadcasted_iota(jnp.int32, (2 * blk, blk), 1) < qrow
    return lane_is_first, after, from_here, causal


def _while_mass_left(qi, carry, block):
    def alive(c):
        return jnp.max(c[0]) > LOG_F32_ZERO

    def cond(state):
        return jnp.logical_and(state[0] < qi, state[1])

    def step(state):
        new = block(qi - 1 - state[0], state[2])
        return state[0] + 1, alive(new), new

    return lax.while_loop(cond, step, (jnp.int32(0), alive(carry), carry))[2]


def _ride(rider, n_in, n_out, refs, grid):
    if rider is None:
        return refs[:n_in], refs[n_in:n_in + n_out], refs[n_in + n_out:], lambda: None
    n = rider.n
    ins, srcs = refs[:n_in], refs[n_in:n_in + n]
    outs, dsts = refs[n_in + n:n_in + n + n_out], refs[n_in + n + n_out:n_in + 2 * n + n_out]
    rest = refs[n_in + 2 * n + n_out:]
    scratch, sems = rest[:len(rest) - 3], rest[len(rest) - 3:]
    step = pl.program_id(0) * grid[1] + pl.program_id(1)

    @pl.when(step == 0)
    def _():
        rider.start(srcs, dsts, sems)

    def finish():
        @pl.when(step == grid[0] * grid[1] - 1)
        def _():
            rider.finish(srcs, dsts, sems)

    return ins, outs, scratch, finish


def _attn_fwd(qkv, gmga, *, name, rider=None):
    t = qkv.shape[0]
    blk = _tile(t, ATTN_BLOCK)
    nq = t // blk

    def body(*refs):
        (q_ref, k_ref, v_ref, ga_ref), (o_ref, cv_ref), _, finish_rider = _ride(rider, 4, 2, refs, (N_HEAD_PAIRS, nq))
        qi = pl.program_id(1)
        lane_is_first, after, _, causal = _attn_consts(blk)
        qcat = _split_heads(q_ref[...], lane_is_first)

        def block(kb, carry, mask):
            run, acc = carry
            rows = pl.ds(pl.multiple_of(kb * blk, blk), blk)
            lb, l1m, _, _ = _scores(qcat, k_ref[rows, :], mask)
            hi, lo = _split_bf16(l1m)
            w = jnp.exp(lb + (_dot(hi, after) + _dot(lo, after) + run))
            if mask is not None:
                w = jnp.where(mask, w, 0.0)
            vcat = _split_heads(v_ref[rows, :], lane_is_first)
            acc = acc + _dot(_side_by_side(w.astype(BF16), blk), vcat)
            return run + jnp.sum(l1m, axis=-1, keepdims=True), acc

        carry = block(qi, (jnp.zeros((2 * blk, 1), F32), jnp.zeros((blk, HEAD_PAIR), F32)), causal)
        _, o = _while_mass_left(qi, carry, lambda kb, c: block(kb, c, None))
        o_ref[...] = o
        ga = ga_ref[...]
        cv_ref[...] = (o * (ga * _sigmoid(ga))).astype(BF16)
        finish_rider()

    qspec = pl.BlockSpec((blk, HEAD_PAIR), lambda p, i: (i, p))
    extra = rider or _NO_RIDER
    return pl.pallas_call(
        body, name=name, grid=(N_HEAD_PAIRS, nq),
        in_specs=[qspec,
                  pl.BlockSpec((t, HEAD_PAIR), lambda p, i: (0, N_HEAD_PAIRS + p)),
                  pl.BlockSpec((t, HEAD_PAIR), lambda p, i: (0, 2 * N_HEAD_PAIRS + p)),
                  pl.BlockSpec((blk, HEAD_PAIR), lambda p, i: (i, GA_BLOCK + p))] + extra.specs,
        out_specs=[qspec, qspec] + extra.specs,
        out_shape=[jax.ShapeDtypeStruct((t, BRANCH_WIDTH), F32), jax.ShapeDtypeStruct((t, BRANCH_WIDTH), BF16)] + extra.out_shape,
        scratch_shapes=extra.scratch_shapes,
        compiler_params=_params(2),
    )(qkv, qkv, qkv, gmga, *extra.srcs)


def _attn_bwd(qkv, o, gmga, dcv, *, name, rider=None):
    t = qkv.shape[0]
    blk = _tile(t, ATTN_BLOCK)
    nq = t // blk

    def body(*refs):
        ins, outs, (dk_acc, dv_acc), finish_rider = _ride(rider, 6, 4, refs, (N_HEAD_PAIRS, nq))
        q_ref, k_ref, v_ref, o_ref, ga_ref, dcv_ref = ins
        dq_ref, dk_ref, dv_ref, dga_ref = outs
        qi = pl.program_id(1)
        lane_is_first, after, from_here, causal = _attn_consts(blk)

        @pl.when(qi == 0)
        def _():
            dk_acc[...] = jnp.zeros_like(dk_acc)
            dv_acc[...] = jnp.zeros_like(dv_acc)

        ga, ov, dcvv = ga_ref[...], o_ref[...], dcv_ref[...]
        sg = _sigmoid(ga)
        dob = (dcvv * (ga * sg)).astype(BF16)
        dga_ref[...] = (dcvv * ov * _silu_grad(ga, sg)).astype(BF16)
        gt = dob.astype(F32) * ov
        g_total = jnp.concatenate(
            [jnp.sum(jnp.where(lane_is_first, gt, 0.0), axis=-1, keepdims=True),
             jnp.sum(jnp.where(lane_is_first, 0.0, gt), axis=-1, keepdims=True)], axis=0)
        qcat = _split_heads(q_ref[...], lane_is_first)
        docat = _split_heads(dob, lane_is_first)

        def block(kb, carry, mask):
            run, g_run, dq = carry
            rows = pl.ds(pl.multiple_of(kb * blk, blk), blk)
            kblk = k_ref[rows, :]
            lb, l1m, sig, one_m_sig = _scores(qcat, kblk, mask)
            hi, lo = _split_bf16(l1m)
            w = jnp.exp(lb + (_dot(hi, after) + _dot(lo, after) + run))
            if mask is not None:
                w = jnp.where(mask, w, 0.0)
            wb = w.astype(BF16)
            g = _dot_nt(docat, v_ref[rows, :]) * wb.astype(F32)
            ghi, glo = _split_bf16(g)
            g_before = g_total - g_run - (_dot(ghi, from_here) + _dot(glo, from_here))
            dz = g * one_m_sig - g_before * sig
            if mask is not None:
                dz = jnp.where(mask, dz, 0.0)
            dzb = dz.astype(BF16)
            dq = dq + _dot(_side_by_side(dzb, blk), _split_heads(kblk, lane_is_first))
            dk_acc[rows, :] += _dot_tn(dzb, qcat)
            dv_acc[rows, :] += _dot_tn(wb, docat)
            return (run + jnp.sum(l1m, axis=-1, keepdims=True), g_run + jnp.sum(g, axis=-1, keepdims=True), dq)

        zero = jnp.zeros((2 * blk, 1), F32)
        carry = block(qi, (zero, zero, jnp.zeros((blk, HEAD_PAIR), F32)), causal)
        _, _, dq = _while_mass_left(qi, carry, lambda kb, c: block(kb, c, None))
        dq_ref[...] = (dq * ATTN_SCALE).astype(BF16)

        @pl.when(qi == nq - 1)
        def _():
            dk_ref[...] = dk_acc[...].astype(BF16)
            dv_ref[...] = dv_acc[...].astype(BF16)

        finish_rider()

    qspec = pl.BlockSpec((blk, HEAD_PAIR), lambda p, i: (i, p))
    whole = pl.BlockSpec((t, HEAD_PAIR), lambda p, i: (0, p))
    out = jax.ShapeDtypeStruct((t, BRANCH_WIDTH), BF16)
    extra = rider or _NO_RIDER
    return pl.pallas_call(
        body, name=name, grid=(N_HEAD_PAIRS, nq),
        in_specs=[qspec,
                  pl.BlockSpec((t, HEAD_PAIR), lambda p, i: (0, N_HEAD_PAIRS + p)),
                  pl.BlockSpec((t, HEAD_PAIR), lambda p, i: (0, 2 * N_HEAD_PAIRS + p)),
                  qspec,
                  pl.BlockSpec((blk, HEAD_PAIR), lambda p, i: (i, GA_BLOCK + p)),
                  qspec] + extra.specs,
        out_specs=[qspec, whole, whole, qspec] + extra.specs,
        out_shape=[out, out, out, out] + extra.out_shape,
        scratch_shapes=[pltpu.VMEM((t, HEAD_PAIR), F32), pltpu.VMEM((t, HEAD_PAIR), F32)] + extra.scratch_shapes,
        compiler_params=_params(2),
    )(qkv, qkv, qkv, o, gmga, dcv, *extra.srcs)


def _mesh_position():
    x, y, c = lax.axis_index("x"), lax.axis_index("y"), lax.axis_index("c")
    return x, y, c, 4 * x + 2 * y + c


def _flipped(x, y, c, k):
    return (1 - x if k & 4 else x, 1 - y if k & 2 else y, 1 - c if k & 1 else c)


def _all_to_all(srcs, *, name, same_block):
    ex = _Exchange(srcs, same_block)

    def body(*refs):
        ex.start(refs[:ex.n], refs[ex.n:2 * ex.n], refs[2 * ex.n:])
        ex.finish(refs[:ex.n], refs[ex.n:2 * ex.n], refs[2 * ex.n:])

    return pl.pallas_call(
        body, name=name, in_specs=ex.specs, out_specs=ex.specs, out_shape=ex.out_shape,
        scratch_shapes=ex.scratch_shapes,
    )(*srcs)


class _Exchange:
    def __init__(self, srcs, same_block):
        self.srcs, self.same_block, self.n = list(srcs), same_block, len(srcs)
        self.specs = [pl.BlockSpec(memory_space=pl.ANY)] * self.n
        self.out_shape = [jax.ShapeDtypeStruct((N_DEV,) + tuple(s.shape if same_block else s.shape[1:]), s.dtype)
                          for s in self.srcs]
        self.scratch_shapes = [pltpu.SemaphoreType.DMA((N_DEV - 1, self.n)), pltpu.SemaphoreType.DMA((N_DEV - 1, self.n)),
                               pltpu.SemaphoreType.DMA((self.n,))] if self.n else []

    def _copies(self, src_refs, dst_refs, sems, with_arrivals):
        send_sems, recv_sems, local_sems = sems
        x, y, c, me = _mesh_position()

        def outgoing(i, j):
            return src_refs[i] if self.same_block else src_refs[i].at[j]

        def remote(i, k, slot):
            return pltpu.make_async_remote_copy(
                src_ref=outgoing(i, jnp.bitwise_xor(me, k)), dst_ref=dst_refs[i].at[slot],
                send_sem=send_sems.at[k - 1, i], recv_sem=recv_sems.at[k - 1, i],
                device_id=_flipped(x, y, c, k), device_id_type=MESH)

        pairs = [(i, k) for k in range(1, N_DEV) for i in range(self.n)]
        mine = [pltpu.make_async_copy(outgoing(i, me), dst_refs[i].at[me], local_sems.at[i]) for i in range(self.n)]
        sent = [remote(i, k, me) for i, k in pairs]
        arrivals = [remote(i, k, jnp.bitwise_xor(me, k)) for i, k in pairs] if with_arrivals else []
        return mine, sent, arrivals

    def start(self, src_refs, dst_refs, sems):
        mine, sent, _ = self._copies(src_refs, dst_refs, sems, False)
        for cp in mine + sent:
            cp.start()

    def finish(self, src_refs, dst_refs, sems):
        mine, sent, arrivals = self._copies(src_refs, dst_refs, sems, True)
        for cp in arrivals:
            cp.wait_recv()
        for cp in sent:
            cp.wait_send()
        for cp in mine:
            cp.wait()


_NO_RIDER = _Exchange([], True)


def _adamw(parts, w, m, v, *, name):
    layers, rows, cols = w.shape
    assert len(parts) == layers
    tr = rows
    while tr * cols > ADAM_TILE_ELEMS and tr % 32 == 0:
        tr //= 2
    row = pl.BlockSpec((1, tr, cols), lambda l, i: (l, i, 0))

    def body(*refs):
        p_refs = refs[:layers]
        w_ref, m_ref, v_ref, g_ref, d_ref, nm_ref, nv_ref = refs[layers:]
        layer = pl.program_id(0)
        g = None
        for k in range(N_DEV):
            part = p_refs[0][k]
            for l in range(1, layers):
                part = jnp.where(layer == l, p_refs[l][k], part)
            g = part.astype(F32) if g is None else g + part.astype(F32)
        m2 = ADAM_B1 * m_ref[0] + (1.0 - ADAM_B1) * g
        v2 = ADAM_B2 * v_ref[0] + (1.0 - ADAM_B2) * (g * g)
        m_hat = m2 / (1.0 - ADAM_B1 ** ADAM_STEP)
        v_hat = v2 / (1.0 - ADAM_B2 ** ADAM_STEP)
        g_ref[0] = g
        d_ref[0] = -ADAM_LR * (m_hat / (jnp.sqrt(v_hat) + ADAM_EPS) + ADAM_WD * w_ref[0])
        nm_ref[0] = m2
        nv_ref[0] = v2

    out = jax.ShapeDtypeStruct((layers, rows, cols), F32)
    return pl.pallas_call(
        body, name=name, grid=(layers, rows // tr),
        in_specs=[pl.BlockSpec((N_DEV, tr, cols), lambda l, i: (0, i, 0))] * layers + [row, row, row],
        out_specs=[row, row, row, row], out_shape=[out, out, out, out],
        compiler_params=_params(2),
    )(*parts, w, m, v)


MATMUL_WEIGHTS = ("w_in", "w_pool_out", "w_conv_out", "w_attn_out", "w_o")
SMALL = ("conv_w", "norm_pre", "pool_w", "pool_b", "pool_scale", "conv_b", "conv_ln_g", "conv_ln_b", "norm_post")
WEIGHT_ORDER = ("norm_pre", "w_in", "pool_w", "pool_b", "pool_scale", "w_pool_out", "conv_w", "conv_b",
                "conv_ln_g", "conv_ln_b", "w_conv_out", "w_attn_out", "w_o", "norm_post")


def _shard_axis(name):
    return -2 if name == "w_o" else -1


def _pack_rows(flat_parts, row_multiple):
    flat = jnp.concatenate(flat_parts, axis=-1)
    n = flat.shape[-1]
    chunk = row_multiple * LANES
    total = -(-n // chunk) * chunk
    pad = [(0, 0)] * (flat.ndim - 1) + [(0, total - n)]
    return jnp.pad(flat, pad).reshape(flat.shape[:-1] + (total // LANES, LANES))


def _unpack(buf, shapes):
    flat = buf.reshape(-1)
    out, at = {}, 0
    for name, shape in shapes:
        n = 1
        for s in shape:
            n *= s
        out[name] = flat[at:at + n].reshape(shape)
        at += n
    return out


def _to_dest_major(name, full):
    axis = full.ndim + _shard_axis(name)
    n = full.shape[axis] // N_DEV
    return jnp.stack([lax.slice_in_dim(full, d * n, (d + 1) * n, axis=axis) for d in range(N_DEV)])


def _from_source_major(name, gathered):
    return jnp.concatenate([gathered[d] for d in range(N_DEV)], axis=_shard_axis(name))


def kernel(x, norm_pre, w_in, pool_w, pool_b, pool_scale, w_pool_out, conv_w, conv_b, conv_ln_g, conv_ln_b, w_conv_out, w_attn_out, w_o, norm_post, loss_target, m_norm_pre, m_w_in, m_pool_w, m_pool_b, m_pool_scale, m_w_pool_out, m_conv_w, m_conv_b, m_conv_ln_g, m_conv_ln_b, m_w_conv_out, m_w_attn_out, m_w_o, m_norm_post, v_norm_pre, v_w_in, v_pool_w, v_pool_b, v_pool_scale, v_w_pool_out, v_conv_w, v_conv_b, v_conv_ln_g, v_conv_ln_b, v_w_conv_out, v_w_attn_out, v_w_o, v_norm_post):
    weights = dict(norm_pre=norm_pre, w_in=w_in, pool_w=pool_w, pool_b=pool_b, pool_scale=pool_scale,
                   w_pool_out=w_pool_out, conv_w=conv_w, conv_b=conv_b, conv_ln_g=conv_ln_g, conv_ln_b=conv_ln_b,
                   w_conv_out=w_conv_out, w_attn_out=w_attn_out, w_o=w_o, norm_post=norm_post)
    mom1 = dict(norm_pre=m_norm_pre, w_in=m_w_in, pool_w=m_pool_w, pool_b=m_pool_b, pool_scale=m_pool_scale,
                w_pool_out=m_w_pool_out, conv_w=m_conv_w, conv_b=m_conv_b, conv_ln_g=m_conv_ln_g, conv_ln_b=m_conv_ln_b,
                w_conv_out=m_w_conv_out, w_attn_out=m_w_attn_out, w_o=m_w_o, norm_post=m_norm_post)
    mom2 = dict(norm_pre=v_norm_pre, w_in=v_w_in, pool_w=v_pool_w, pool_b=v_pool_b, pool_scale=v_pool_scale,
                w_pool_out=v_w_pool_out, conv_w=v_conv_w, conv_b=v_conv_b, conv_ln_g=v_conv_ln_g, conv_ln_b=v_conv_ln_b,
                w_conv_out=v_w_conv_out, w_attn_out=v_w_attn_out, w_o=v_w_o, norm_post=v_norm_post)
    xs = x[0]
    target = loss_target[0]

    conv_rows = jnp.pad(conv_w, ((0, 0), (0, CONV_HALO - CONV_KERNEL), (0, 0)))
    shards = [[weights[n][l].astype(BF16) for n in MATMUL_WEIGHTS] for l in range(DEPTH)]
    gathered = _all_to_all(shards[0] + [conv_rows], name="gather_weights", same_block=True)
    full = [{n: _from_source_major(n, g) for n, g in zip(MATMUL_WEIGHTS, gathered)}, None]
    conv_full = _from_source_major("conv_w", gathered[-1])

    def in_sections(w):
        return dict(pg=w[:, 0:1024], c2gc=w[:, 1024:2560], q=w[:, 2560:3072], k=w[:, 3072:3584], v=w[:, 3584:4096],
                    gmga=jnp.concatenate([w[:, 4608:7680], w[:, 4096:4608]], axis=1))

    saved = []
    cur = xs
    for l in range(DEPTH):
        sec = in_sections(full[l]["w_in"])
        w_qkv = jnp.concatenate([sec["q"] * ATTN_SCALE, sec["k"], sec["v"]], axis=1)
        pw = pool_w[l].astype(BF16)
        pb, ps = pool_b[l].reshape(1, -1), pool_scale[l].reshape(1, -1)
        cb, lg, lb = conv_b[l].reshape(1, -1), conv_ln_g[l].reshape(1, -1), conv_ln_b[l].reshape(1, -1)
        h = _rms_fwd(cur, norm_pre[l].reshape(1, -1), name=f"rms_pre_fwd_{l}")
        pg = _matmul(h, sec["pg"], mode="nn", name=f"proj_pg_{l}")
        c2gc = _matmul(h, sec["c2gc"], mode="nn", name=f"proj_c2gc_{l}")
        qkv = _matmul(h, w_qkv, mode="nn", name=f"proj_qkv_{l}", out_dtype=BF16)
        gmga = _matmul(h, sec["gmga"], mode="nn", name=f"proj_gmga_{l}")
        a_act = _pool_fwd(pg, pw, pb, ps, name=f"pool_fwd_{l}")
        b_act = _conv_fwd(c2gc, conv_full[l], cb, lg, lb, name=f"conv_fwd_{l}")
        rider = _Exchange(shards[l + 1], True) if l + 1 < DEPTH else None
        o, c_act, *arrived = _attn_fwd(qkv, gmga, name=f"attn_fwd_{l}", rider=rider)
        if rider is not None:
            full[l + 1] = {n: _from_source_major(n, g) for n, g in zip(MATMUL_WEIGHTS, arrived)}
        ya = _matmul(a_act, full[l]["w_pool_out"], mode="nn", name=f"out_pool_{l}")
        yb = _matmul(b_act, full[l]["w_conv_out"], mode="nn", name=f"out_conv_{l}")
        yc = _matmul(c_act, full[l]["w_attn_out"], mode="nn", name=f"out_attn_{l}")
        mix = _merge_fwd(gmga, ya, yb, yc, name=f"merge_fwd_{l}")
        out = _matmul(mix, full[l]["w_o"], mode="nn", name=f"out_proj_{l}")
        nxt = _rms_fwd(out, norm_post[l].reshape(1, -1), name=f"rms_post_fwd_{l}", resid=cur)
        saved.append(dict(x=cur, h=h, pg=pg, c2gc=c2gc, qkv=qkv, gmga=gmga, a=a_act, b=b_act, c=c_act, o=o,
                          ya=ya, yb=yb, yc=yc, mix=mix, out=out, sec=sec, pw=pw, pb=pb, ps=ps, cb=cb, lg=lg, lb=lb))
        cur = nxt

    loss_tile, dx = _loss_head(cur, target, name="loss_head")
    loss = lax.psum(loss_tile[0, 0], ("x", "y", "c"))

    grads = {n: [None] * DEPTH for n in WEIGHT_ORDER}
    parts = {n: [None] * DEPTH for n in MATMUL_WEIGHTS}
    for l in reversed(range(DEPTH)):
        s = saved[l]
        dout, grads["norm_post"][l] = _rms_bwd(s["out"], norm_post[l].reshape(1, -1), dx, name=f"rms_post_bwd_{l}")
        dmix = _matmul(dout, full[l]["w_o"], mode="nt", name=f"d_mix_{l}")
        grads["w_o"][l] = _matmul(s["mix"], dout, mode="tn", name=f"d_w_o_{l}", out_dtype=BF16)
        dya, dyb, dyc, dgm = _merge_bwd(dmix, s["gmga"], s["ya"], s["yb"], s["yc"], name=f"merge_bwd_{l}")
        da = _matmul(dya, full[l]["w_pool_out"], mode="nt", name=f"d_pool_act_{l}")
        grads["w_pool_out"][l] = _matmul(s["a"], dya, mode="tn", name=f"d_w_pool_out_{l}", out_dtype=BF16)
        db = _matmul(dyb, full[l]["w_conv_out"], mode="nt", name=f"d_conv_act_{l}")
        grads["w_conv_out"][l] = _matmul(s["b"], dyb, mode="tn", name=f"d_w_conv_out_{l}", out_dtype=BF16)
        dc = _matmul(dyc, full[l]["w_attn_out"], mode="nt", name=f"d_attn_act_{l}")
        grads["w_attn_out"][l] = _matmul(s["c"], dyc, mode="tn", name=f"d_w_attn_out_{l}", out_dtype=BF16)
        riding = [(n, l) for n in MATMUL_WEIGHTS[1:]] + ([("w_in", l + 1)] if l + 1 < DEPTH else [])
        rider = _Exchange([_to_dest_major(n, grads[n][k]) for n, k in riding], False)
        dq, dk, dv, dga, *arrived = _attn_bwd(s["qkv"], s["o"], s["gmga"], dc, name=f"attn_bwd_{l}", rider=rider)
        for (n, k), p in zip(riding, arrived):
            parts[n][k] = p
        dc2gc, dcw, dcvec = _conv_bwd(s["c2gc"], db, conv_full[l], s["cb"], s["lg"], s["lb"], name=f"conv_bwd_{l}")
        dpg, dpw, dpvec = _pool_bwd(s["pg"], da, s["pw"], s["pb"], s["ps"], name=f"pool_bwd_{l}")
        grads["conv_w"][l] = dcw[:CONV_KERNEL]
        grads["conv_b"][l], grads["conv_ln_g"][l], grads["conv_ln_b"][l] = dcvec[0], dcvec[1], dcvec[2]
        grads["pool_w"][l] = dpw
        grads["pool_b"][l] = dpvec[0].reshape(4, POOL_GROUP_DIM)
        grads["pool_scale"][l] = dpvec[1]
        dsec = dict(pg=dpg, c2gc=dc2gc, q=dq, k=dk, v=dv, gmga=jnp.concatenate([dgm, dga], axis=1))
        dh, dws = None, {}
        for n in ("pg", "c2gc", "q", "k", "v", "gmga"):
            dh = _matmul(dsec[n], s["sec"][n], mode="nt", name=f"d_h_{n}_{l}", acc=dh)
            dws[n] = _matmul(s["h"], dsec[n], mode="tn", name=f"d_w_in_{n}_{l}", out_dtype=BF16)
        grads["w_in"][l] = jnp.concatenate(
            [dws["pg"], dws["c2gc"], dws["q"], dws["k"], dws["v"], dws["gmga"][:, 3 * D_MODEL:], dws["gmga"][:, :3 * D_MODEL]],
            axis=1)
        dx, dg_pre = _rms_bwd(s["x"], norm_pre[l].reshape(1, -1), dh, name=f"rms_pre_bwd_{l}", resid=dx)
        grads["norm_pre"][l] = dg_pre.reshape(-1)
        grads["norm_post"][l] = grads["norm_post"][l].reshape(-1)
    small_grads = {n: jnp.stack(grads[n]) for n in SMALL}
    replicated = jnp.concatenate([small_grads[n].reshape(-1) for n in SMALL[1:]])
    small = _pack_rows([_to_dest_major("conv_w", small_grads["conv_w"]).reshape(N_DEV, -1),
                        jnp.broadcast_to(replicated, (N_DEV, replicated.size))], 16).astype(BF16)
    parts["w_in"][0], small_parts = _all_to_all([_to_dest_major("w_in", grads["w_in"][0]), small],
                                                name="exchange_grads", same_block=False)

    outs = [dict(), dict(), dict(), dict()]
    for n in MATMUL_WEIGHTS:
        res = _adamw(parts[n], weights[n], mom1[n], mom2[n], name=f"adamw_{n}")
        for o, r in zip(outs, res):
            o[n] = r

    def packed(tree):
        return _pack_rows([tree[n].reshape(-1) for n in SMALL], 16)[None]

    shapes = [(n, weights[n].shape) for n in SMALL]
    res = _adamw([small_parts], packed(weights), packed(mom1), packed(mom2), name="adamw_small")
    for o, r in zip(outs, res):
        o.update(_unpack(r, shapes))
    return (loss, dx[None], *[o[n] for o in outs for n in WEIGHT_ORDER])
```

```python
import functools

import jax
import jax.numpy as jnp
from jax import lax
from jax.experimental import pallas as pl
from jax.experimental.pallas import tpu as pltpu

F32 = jnp.float32
BF16 = jnp.bfloat16

D_MODEL = 1024
DEPTH = 2
POOL_WINDOWS = (2, 4, 8, 16)
POOL_GROUP_DIM = 128
BRANCH_WIDTH = 512
CONV_KERNEL = 31
CONV_HALO = 32
POOL_HALO = 16
HEAD_DIM = 64
HEAD_PAIR = 128
N_HEAD_PAIRS = 4
ATTN_SCALE = 0.125
LOG_F32_ZERO = -104.0
RMS_EPS = 1e-6
LN_EPS = 1e-5
N_DEV = 8
LANES = 128

ADAM_LR = 0.001
ADAM_B1 = 0.9
ADAM_B2 = 0.999
ADAM_EPS = 1e-08
ADAM_WD = 0.01
ADAM_STEP = 10

ROW_TILE = 256
ATTN_BLOCK = 256
MM_TILE = 1024
MM_K_TILE = 1280
ADAM_TILE_ELEMS = 256 * 1024
VMEM_LIMIT = 48 * 1024 * 1024

MESH = pl.DeviceIdType.MESH


def _params(n_axes):
    return pltpu.CompilerParams(dimension_semantics=("arbitrary",) * n_axes, vmem_limit_bytes=VMEM_LIMIT)


def _tile(n, pref):
    if n <= pref:
        return n
    t = (pref // LANES) * LANES
    while n % t:
        t -= LANES
    return t


def _dot(a, b):
    return jnp.dot(a, b, preferred_element_type=F32)


def _dot_nt(a, b):
    return lax.dot_general(a, b, (((1,), (1,)), ((), ())), preferred_element_type=F32)


def _dot_tn(a, b):
    return lax.dot_general(a, b, (((0,), (0,)), ((), ())), preferred_element_type=F32)


def _sigmoid(x):
    return 1.0 / (1.0 + jnp.exp(-x))


def _silu_grad(x, s):
    return s * (1.0 + x * (1.0 - s))


def _matmul(a, b, *, mode, name, out_dtype=F32, rider=None):
    if mode == "nn":
        (m, k), n = a.shape, b.shape[1]
    elif mode == "nt":
        (m, k), n = a.shape, b.shape[0]
    else:
        (k, m), n = a.shape, b.shape[1]
    tm, tn, tk = _tile(m, MM_TILE), _tile(n, MM_TILE), _tile(k, MM_K_TILE)
    nk = k // tk
    grid = (m // tm, n // tn, nk)
    dot = {"nn": _dot, "nt": _dot_nt, "tn": _dot_tn}[mode]
    a_spec = pl.BlockSpec((tk, tm), lambda i, j, kk: (kk, i)) if mode == "tn" else pl.BlockSpec((tm, tk), lambda i, j, kk: (i, kk))
    b_spec = pl.BlockSpec((tn, tk), lambda i, j, kk: (j, kk)) if mode == "nt" else pl.BlockSpec((tk, tn), lambda i, j, kk: (kk, j))
    o_spec = pl.BlockSpec((tm, tn), lambda i, j, kk: (i, j))

    def body(*refs):
        (a_ref, b_ref), (o_ref,), scratch, finish_rider = _ride(rider, 2, 1, refs, grid)
        part = dot(a_ref[...], b_ref[...])
        if nk == 1:
            o_ref[...] = part.astype(out_dtype)
        else:
            scr = scratch[0]
            kk = pl.program_id(2)

            @pl.when(kk == 0)
            def _():
                scr[...] = part

            @pl.when(kk > 0)
            def _():
                scr[...] += part

            @pl.when(kk == nk - 1)
            def _():
                o_ref[...] = scr[...].astype(out_dtype)

        finish_rider()

    extra = rider or _NO_RIDER
    res = pl.pallas_call(
        body, name=name, grid=grid,
        in_specs=[a_spec, b_spec] + extra.specs, out_specs=[o_spec] + extra.specs,
        out_shape=[jax.ShapeDtypeStruct((m, n), out_dtype)] + extra.out_shape,
        scratch_shapes=([pltpu.VMEM((tm, tn), F32)] if nk > 1 else []) + extra.scratch_shapes,
        compiler_params=_params(3),
    )(a, b, *extra.srcs)
    return res if rider is not None else res[0]


def _rms_fwd(x, g, *, name, resid=None):
    t = x.shape[0]
    tm = _tile(t, ROW_TILE)
    row = pl.BlockSpec((tm, D_MODEL), lambda i: (i, 0))
    vec = pl.BlockSpec((1, D_MODEL), lambda i: (0, 0))
    has_resid = resid is not None

    def body(*refs):
        x_ref, g_ref = refs[0], refs[1]
        o_ref = refs[-1]
        xv = x_ref[...]
        y = xv * lax.rsqrt(jnp.mean(xv * xv, axis=-1, keepdims=True) + RMS_EPS) * g_ref[...]
        if has_resid:
            o_ref[...] = refs[2][...] + y
        else:
            o_ref[...] = y.astype(BF16)

    return pl.pallas_call(
        body, name=name, grid=(t // tm,),
        in_specs=[row, vec] + ([row] if has_resid else []), out_specs=row,
        out_shape=jax.ShapeDtypeStruct((t, D_MODEL), F32 if has_resid else BF16),
        compiler_params=_params(1),
    )(*((x, g) + ((resid,) if has_resid else ())))


def _rms_bwd(xin, g, dy, *, name, resid=None):
    t = xin.shape[0]
    tm = _tile(t, ROW_TILE)
    row = pl.BlockSpec((tm, D_MODEL), lambda i: (i, 0))
    vec = pl.BlockSpec((1, D_MODEL), lambda i: (0, 0))
    has_resid = resid is not None
    out_dtype = F32 if has_resid else BF16

    def body(*refs):
        x_ref, g_ref, dy_ref = refs[0], refs[1], refs[2]
        dx_ref, dg_ref = refs[-2], refs[-1]
        xv, dyv = x_ref[...], dy_ref[...]
        r = lax.rsqrt(jnp.mean(xv * xv, axis=-1, keepdims=True) + RMS_EPS)
        a = dyv * g_ref[...]
        dx = r * a - xv * (r * r * r) * jnp.mean(a * xv, axis=-1, keepdims=True)
        if has_resid:
            dx = dx + refs[3][...]
        dx_ref[...] = dx.astype(out_dtype)
        part = jnp.sum(dyv * xv * r, axis=0, keepdims=True)

        @pl.when(pl.program_id(0) == 0)
        def _():
            dg_ref[...] = part

        @pl.when(pl.program_id(0) > 0)
        def _():
            dg_ref[...] += part

    return pl.pallas_call(
        body, name=name, grid=(t // tm,),
        in_specs=[row, vec, row] + ([row] if has_resid else []), out_specs=[row, vec],
        out_shape=[jax.ShapeDtypeStruct((t, D_MODEL), out_dtype), jax.ShapeDtypeStruct((1, D_MODEL), F32)],
        compiler_params=_params(1),
    )(*((xin, g, dy) + ((resid,) if has_resid else ())))


def _loss_head(x, target, *, name):
    t = x.shape[0]
    tm = _tile(t, ROW_TILE)
    row = pl.BlockSpec((tm, D_MODEL), lambda i: (i, 0))
    acc = pl.BlockSpec((8, LANES), lambda i: (0, 0))

    def body(x_ref, t_ref, l_ref, dx_ref):
        diff = x_ref[...] - t_ref[...]
        dx_ref[...] = diff * (1.0 / D_MODEL)
        part = 0.5 * jnp.sum(jnp.mean(diff * diff, axis=-1, keepdims=True), axis=0, keepdims=True)

        @pl.when(pl.program_id(0) == 0)
        def _():
            l_ref[...] = jnp.zeros((8, LANES), F32) + part

        @pl.when(pl.program_id(0) > 0)
        def _():
            l_ref[...] += part

    return pl.pallas_call(
        body, name=name, grid=(t // tm,),
        in_specs=[row, row], out_specs=[acc, row],
        out_shape=[jax.ShapeDtypeStruct((8, LANES), F32), jax.ShapeDtypeStruct((t, D_MODEL), F32)],
        compiler_params=_params(1),
    )(x, target)


def _window_sum(ext, n_doublings, forward):
    rows = ext.shape[0]
    s, sh = ext, 1
    for _ in range(n_doublings):
        s = s + pltpu.roll(s, sh if forward else rows - sh, 0)
        sh *= 2
    return s


def _pool_fwd(pg, pool_w, pool_b, pool_scale, *, name):
    t = pg.shape[0]
    tm = _tile(t, ROW_TILE)

    def body(pg_ref, w_ref, b_ref, s_ref, o_ref, halo):
        i = pl.program_id(0)

        @pl.when(i == 0)
        def _():
            halo[...] = jnp.zeros_like(halo)

        p = pg_ref[:, :BRANCH_WIDTH]
        gate = pg_ref[:, BRANCH_WIDTH:]
        ext = jnp.concatenate([halo[...], p], axis=0)
        pos = i * tm + lax.broadcasted_iota(jnp.int32, (tm, 1), 0)
        outs = []
        for g, w in enumerate(POOL_WINDOWS):
            cols = slice(g * POOL_GROUP_DIM, (g + 1) * POOL_GROUP_DIM)
            cnt = jnp.minimum(pos + 1, w).astype(F32)
            d = _window_sum(ext[:, cols], g + 1, True)[POOL_HALO:] / cnt - p[:, cols]
            y = (_dot(d.astype(BF16), w_ref[g]) + b_ref[:, cols]) * s_ref[:, cols]
            gg = gate[:, cols]
            outs.append(y * (gg * _sigmoid(gg)))
        o_ref[...] = jnp.concatenate(outs, axis=1).astype(BF16)
        halo[...] = p[tm - POOL_HALO:, :]

    vec = pl.BlockSpec((1, BRANCH_WIDTH), lambda i: (0, 0))
    return pl.pallas_call(
        body, name=name, grid=(t // tm,),
        in_specs=[pl.BlockSpec((tm, 2 * BRANCH_WIDTH), lambda i: (i, 0)),
                  pl.BlockSpec((4, POOL_GROUP_DIM, POOL_GROUP_DIM), lambda i: (0, 0, 0)), vec, vec],
        out_specs=pl.BlockSpec((tm, BRANCH_WIDTH), lambda i: (i, 0)),
        out_shape=jax.ShapeDtypeStruct((t, BRANCH_WIDTH), BF16),
        scratch_shapes=[pltpu.VMEM((POOL_HALO, BRANCH_WIDTH), F32)],
        compiler_params=_params(1),
    )(pg, pool_w, pool_b, pool_scale)


def _pool_bwd(pg, d_out, pool_w, pool_b, pool_scale, *, name):
    t = pg.shape[0]
    tm = _tile(t, ROW_TILE)
    nt = t // tm
    halo_per_tile = tm // POOL_HALO

    def body(pg_ref, halo_ref, do_ref, w_ref, b_ref, s_ref, dpg_ref, dw_ref, dvec_ref, carry):
        i = pl.program_id(0)
        ri = nt - 1 - i

        @pl.when(i == 0)
        def _():
            carry[...] = jnp.zeros_like(carry)
            dw_ref[...] = jnp.zeros_like(dw_ref)
            dvec_ref[...] = jnp.zeros_like(dvec_ref)

        p = pg_ref[:, :BRANCH_WIDTH]
        gate = pg_ref[:, BRANCH_WIDTH:]
        hp = jnp.where(ri > 0, halo_ref[:, :BRANCH_WIDTH], 0.0)
        ext = jnp.concatenate([hp, p], axis=0)
        pos = ri * tm + lax.broadcasted_iota(jnp.int32, (tm, 1), 0)
        dps, dgs, dbs, dss = [], [], [], []
        for g, w in enumerate(POOL_WINDOWS):
            cols = slice(g * POOL_GROUP_DIM, (g + 1) * POOL_GROUP_DIM)
            cnt = jnp.minimum(pos + 1, w).astype(F32)
            d = (_window_sum(ext[:, cols], g + 1, True)[POOL_HALO:] / cnt - p[:, cols]).astype(BF16)
            y1 = _dot(d, w_ref[g]) + b_ref[:, cols]
            scale = s_ref[:, cols]
            y2 = y1 * scale
            gg = gate[:, cols]
            sg = _sigmoid(gg)
            do = do_ref[:, cols]
            dy2 = do * (gg * sg)
            dgs.append(do * y2 * _silu_grad(gg, sg))
            dss.append(jnp.sum(dy2 * y1, axis=0, keepdims=True))
            dy1 = dy2 * scale
            dbs.append(jnp.sum(dy1, axis=0, keepdims=True))
            dy1b = dy1.astype(BF16)
            dw_ref[g] += _dot_tn(d, dy1b)
            dd = _dot_nt(dy1b, w_ref[g])
            dpool = dd / cnt
            dext = jnp.concatenate([dpool, carry[:, cols]], axis=0)
            dps.append(_window_sum(dext, g + 1, False)[:tm] - dd)
            carry[:, cols] = dpool[:POOL_HALO]
        dpg_ref[...] = jnp.concatenate(dps + dgs, axis=1).astype(BF16)
        dvec_ref[0:1, :] += jnp.concatenate(dbs, axis=1)
        dvec_ref[1:2, :] += jnp.concatenate(dss, axis=1)

    vec = pl.BlockSpec((1, BRANCH_WIDTH), lambda i: (0, 0))
    wspec = pl.BlockSpec((4, POOL_GROUP_DIM, POOL_GROUP_DIM), lambda i: (0, 0, 0))
    return pl.pallas_call(
        body, name=name, grid=(nt,),
        in_specs=[pl.BlockSpec((tm, 2 * BRANCH_WIDTH), lambda i: (nt - 1 - i, 0)),
                  pl.BlockSpec((POOL_HALO, 2 * BRANCH_WIDTH), lambda i: (jnp.maximum((nt - 1 - i) * halo_per_tile - 1, 0), 0)),
                  pl.BlockSpec((tm, BRANCH_WIDTH), lambda i: (nt - 1 - i, 0)), wspec, vec, vec],
        out_specs=[pl.BlockSpec((tm, 2 * BRANCH_WIDTH), lambda i: (nt - 1 - i, 0)), wspec,
                   pl.BlockSpec((8, BRANCH_WIDTH), lambda i: (0, 0))],
        out_shape=[jax.ShapeDtypeStruct((t, 2 * BRANCH_WIDTH), BF16),
                   jax.ShapeDtypeStruct((4, POOL_GROUP_DIM, POOL_GROUP_DIM), F32),
                   jax.ShapeDtypeStruct((8, BRANCH_WIDTH), F32)],
        scratch_shapes=[pltpu.VMEM((POOL_HALO, BRANCH_WIDTH), F32)],
        compiler_params=_params(1),
    )(pg, pg, d_out, pool_w, pool_b, pool_scale)


CONV_TILE_ROWS = 64


def _for_conv_tiles(tm, fn):
    def step(it, carry):
        rows = pl.ds(pl.multiple_of(it * CONV_TILE_ROWS, CONV_TILE_ROWS), CONV_TILE_ROWS)
        for c in range(0, BRANCH_WIDTH, LANES):
            fn(rows, slice(c, c + LANES))
        return carry

    lax.fori_loop(0, tm // CONV_TILE_ROWS, step, 0)


def _sublane_shifts(shifted_ref, x, direction):
    rows = x.shape[0]
    shifted_ref[0] = x
    for b in range(1, 8):
        shifted_ref[b] = pltpu.roll(x, b if direction > 0 else rows - b, 0)


CONV_REACH = 8 * ((CONV_KERNEL - 1) // 8)


def _tap_tiles(shifted_ref, base, rows, cols, direction):
    for b in range(8):
        lo = pl.multiple_of(base + rows.start - (CONV_REACH if direction > 0 else 0), 8)
        window = shifted_ref[b, pl.ds(lo, CONV_TILE_ROWS + CONV_REACH), cols]
        for a in range((CONV_KERNEL - 1 - b) // 8 + 1):
            off = CONV_REACH - 8 * a if direction > 0 else 8 * a
            yield 8 * a + b, window[off:off + CONV_TILE_ROWS]


def _tap_sum(shifted_ref, w_ref, base, rows, cols, direction):
    acc = None
    for j, tile in _tap_tiles(shifted_ref, base, rows, cols, direction):
        k = CONV_KERNEL - 1 - j
        term = w_ref[k:k + 1, cols] * tile
        acc = term if acc is None else acc + term
    return acc


def _causal_conv(ext8_ref, w_ref, cv_ref, tm):
    def tile(rows, cols):
        cv_ref[rows, cols] = _tap_sum(ext8_ref, w_ref, CONV_HALO, rows, cols, 1)

    _for_conv_tiles(tm, tile)


def _conv_fwd(c2gc, conv_w, conv_b, ln_g, ln_b, *, name):
    t = c2gc.shape[0]
    tm = _tile(t, ROW_TILE)

    def body(c_ref, w_ref, cb_ref, g_ref, b_ref, o_ref, halo, ext8_ref, cv_ref):
        @pl.when(pl.program_id(0) == 0)
        def _():
            halo[...] = jnp.zeros_like(halo)

        u = c_ref[:, :BRANCH_WIDTH] * _sigmoid(c_ref[:, BRANCH_WIDTH:2 * BRANCH_WIDTH])
        gate = c_ref[:, 2 * BRANCH_WIDTH:]
        _sublane_shifts(ext8_ref, jnp.concatenate([halo[...], u], axis=0), 1)
        halo[...] = u[tm - CONV_HALO:, :]
        _causal_conv(ext8_ref, w_ref, cv_ref, tm)
        cv = cv_ref[...] + cb_ref[...]
        mu = jnp.mean(cv, axis=-1, keepdims=True)
        xc = cv - mu
        var = jnp.mean(xc * xc, axis=-1, keepdims=True)
        ln = xc * lax.rsqrt(var + LN_EPS) * g_ref[...] + b_ref[...]
        o_ref[...] = (ln * _sigmoid(ln) * (gate * _sigmoid(gate))).astype(BF16)

    vec = pl.BlockSpec((1, BRANCH_WIDTH), lambda i: (0, 0))
    return pl.pallas_call(
        body, name=name, grid=(t // tm,),
        in_specs=[pl.BlockSpec((tm, 3 * BRANCH_WIDTH), lambda i: (i, 0)),
                  pl.BlockSpec((CONV_HALO, BRANCH_WIDTH), lambda i: (0, 0)), vec, vec, vec],
        out_specs=pl.BlockSpec((tm, BRANCH_WIDTH), lambda i: (i, 0)),
        out_shape=jax.ShapeDtypeStruct((t, BRANCH_WIDTH), BF16),
        scratch_shapes=[pltpu.VMEM((CONV_HALO, BRANCH_WIDTH), F32), pltpu.VMEM((8, tm + CONV_HALO, BRANCH_WIDTH), F32),
                        pltpu.VMEM((tm, BRANCH_WIDTH), F32)],
        compiler_params=_params(1),
    )(c2gc, conv_w, conv_b, ln_g, ln_b)


def _conv_bwd(c2gc, d_out, conv_w, conv_b, ln_g, ln_b, *, name):
    t = c2gc.shape[0]
    tm = _tile(t, ROW_TILE)
    nt = t // tm
    halo_per_tile = tm // CONV_HALO

    def body(c_ref, halo_ref, do_ref, w_ref, cb_ref, g_ref, b_ref, dc_ref, dw_ref, dvec_ref,
             carry, ext8_ref, cv_ref, dext8_ref, du_ref, dw_acc):
        i = pl.program_id(0)
        ri = nt - 1 - i

        @pl.when(i == 0)
        def _():
            carry[...] = jnp.zeros_like(carry)
            dw_acc[...] = jnp.zeros_like(dw_acc)
            dvec_ref[...] = jnp.zeros_like(dvec_ref)

        a = c_ref[:, :BRANCH_WIDTH]
        sb = _sigmoid(c_ref[:, BRANCH_WIDTH:2 * BRANCH_WIDTH])
        gate = c_ref[:, 2 * BRANCH_WIDTH:]
        hu = halo_ref[:, :BRANCH_WIDTH] * _sigmoid(halo_ref[:, BRANCH_WIDTH:2 * BRANCH_WIDTH])
        _sublane_shifts(ext8_ref, jnp.concatenate([jnp.where(ri > 0, hu, 0.0), a * sb], axis=0), 1)
        _causal_conv(ext8_ref, w_ref, cv_ref, tm)
        cv = cv_ref[...] + cb_ref[...]
        mu = jnp.mean(cv, axis=-1, keepdims=True)
        xc = cv - mu
        rs = lax.rsqrt(jnp.mean(xc * xc, axis=-1, keepdims=True) + LN_EPS)
        n = xc * rs
        ln = n * g_ref[...] + b_ref[...]
        sl = _sigmoid(ln)
        sgate = _sigmoid(gate)
        do = do_ref[...]
        dgate = do * (ln * sl) * _silu_grad(gate, sgate)
        dln = do * (gate * sgate) * _silu_grad(ln, sl)
        dn = dln * g_ref[...]
        dcv = rs * (dn - jnp.mean(dn, axis=-1, keepdims=True) - n * jnp.mean(dn * n, axis=-1, keepdims=True))
        dvec_ref[0:1, :] += jnp.sum(dcv, axis=0, keepdims=True)
        dvec_ref[1:2, :] += jnp.sum(dln * n, axis=0, keepdims=True)
        dvec_ref[2:3, :] += jnp.sum(dln, axis=0, keepdims=True)
        _sublane_shifts(dext8_ref, jnp.concatenate([dcv, carry[...]], axis=0), -1)
        carry[...] = dcv[:CONV_HALO]
        def tile(rows, cols):
            du_ref[rows, cols] = _tap_sum(dext8_ref, w_ref, 0, rows, cols, -1)
            d_tile = dext8_ref[0, rows, cols]
            for j, u_tile in _tap_tiles(ext8_ref, CONV_HALO, rows, cols, 1):
                prod = d_tile * u_tile
                part = prod[0:8]
                for q in range(8, CONV_TILE_ROWS, 8):
                    part = part + prod[q:q + 8]
                dw_acc[CONV_KERNEL - 1 - j, :, cols] += part

        _for_conv_tiles(tm, tile)
        du = du_ref[...]
        dc_ref[...] = jnp.concatenate([du * sb, du * a * sb * (1.0 - sb), dgate], axis=1).astype(BF16)

        @pl.when(i == nt - 1)
        def _():
            dw_ref[...] = jnp.sum(dw_acc[...], axis=1)

    vec = pl.BlockSpec((1, BRANCH_WIDTH), lambda i: (0, 0))
    wspec = pl.BlockSpec((CONV_HALO, BRANCH_WIDTH), lambda i: (0, 0))
    return pl.pallas_call(
        body, name=name, grid=(nt,),
        in_specs=[pl.BlockSpec((tm, 3 * BRANCH_WIDTH), lambda i: (nt - 1 - i, 0)),
                  pl.BlockSpec((CONV_HALO, 3 * BRANCH_WIDTH), lambda i: (jnp.maximum((nt - 1 - i) * halo_per_tile - 1, 0), 0)),
                  pl.BlockSpec((tm, BRANCH_WIDTH), lambda i: (nt - 1 - i, 0)), wspec, vec, vec, vec],
        out_specs=[pl.BlockSpec((tm, 3 * BRANCH_WIDTH), lambda i: (nt - 1 - i, 0)), wspec,
                   pl.BlockSpec((8, BRANCH_WIDTH), lambda i: (0, 0))],
        out_shape=[jax.ShapeDtypeStruct((t, 3 * BRANCH_WIDTH), BF16),
                   jax.ShapeDtypeStruct((CONV_HALO, BRANCH_WIDTH), F32),
                   jax.ShapeDtypeStruct((8, BRANCH_WIDTH), F32)],
        scratch_shapes=[pltpu.VMEM((CONV_HALO, BRANCH_WIDTH), F32),
                        pltpu.VMEM((8, tm + CONV_HALO, BRANCH_WIDTH), F32), pltpu.VMEM((tm, BRANCH_WIDTH), F32),
                        pltpu.VMEM((8, tm + CONV_HALO, BRANCH_WIDTH), F32), pltpu.VMEM((tm, BRANCH_WIDTH), F32),
                        pltpu.VMEM((CONV_HALO, 8, BRANCH_WIDTH), F32)],
        compiler_params=_params(1),
    )(c2gc, c2gc, d_out, conv_w, conv_b, ln_g, ln_b)


def _merge_fwd(gmga, ya, yb, yc, *, name):
    t = ya.shape[0]
    tm = _tile(t, ROW_TILE)
    row = pl.BlockSpec((tm, D_MODEL), lambda i: (i, 0))

    def body(g0, g1, g2, a_ref, b_ref, c_ref, o_ref):
        m = _sigmoid(g0[...]) * a_ref[...] + _sigmoid(g1[...]) * b_ref[...] + _sigmoid(g2[...]) * c_ref[...]
        o_ref[...] = m.astype(BF16)

    gspecs = [pl.BlockSpec((tm, D_MODEL), functools.partial(lambda i, b: (i, b), b=b)) for b in range(3)]
    return pl.pallas_call(
        body, name=name, grid=(t // tm,),
        in_specs=gspecs + [row, row, row], out_specs=row,
        out_shape=jax.ShapeDtypeStruct((t, D_MODEL), BF16),
        compiler_params=_params(1),
    )(gmga, gmga, gmga, ya, yb, yc)


def _merge_bwd(dm, gmga, ya, yb, yc, *, name):
    t = ya.shape[0]
    tm = _tile(t, ROW_TILE)
    row = pl.BlockSpec((tm, D_MODEL), lambda i: (i, 0))
    wide = pl.BlockSpec((tm, 3 * D_MODEL), lambda i: (i, 0))

    def body(dm_ref, g0, g1, g2, a_ref, b_ref, c_ref, da_ref, db_ref, dc_ref, dg_ref):
        dmv = dm_ref[...]
        for k, (g_ref, y_ref, dy_ref) in enumerate(((g0, a_ref, da_ref), (g1, b_ref, db_ref), (g2, c_ref, dc_ref))):
            s = _sigmoid(g_ref[...])
            dy_ref[...] = (dmv * s).astype(BF16)
            dg_ref[:, k * D_MODEL:(k + 1) * D_MODEL] = (dmv * y_ref[...] * s * (1.0 - s)).astype(BF16)

    gspecs = [pl.BlockSpec((tm, D_MODEL), functools.partial(lambda i, b: (i, b), b=b)) for b in range(3)]
    return pl.pallas_call(
        body, name=name, grid=(t // tm,),
        in_specs=[row] + gspecs + [row, row, row], out_specs=[row, row, row, wide],
        out_shape=[jax.ShapeDtypeStruct((t, D_MODEL), BF16)] * 3 + [jax.ShapeDtypeStruct((t, 3 * D_MODEL), BF16)],
        compiler_params=_params(1),
    )(dm, gmga, gmga, gmga, ya, yb, yc)


GA_BLOCK = 3 * D_MODEL // HEAD_PAIR


def _split_heads(x, lane_is_first):
    zero = jnp.zeros_like(x)
    return jnp.concatenate([jnp.where(lane_is_first, x, zero), jnp.where(lane_is_first, zero, x)], axis=0)


def _side_by_side(x, rows):
    return jnp.concatenate([x[:rows], x[rows:]], axis=1)


def _split_bf16(x):
    hi = x.astype(BF16)
    return hi, (x - hi.astype(F32)).astype(BF16)


def _scores(qcat, kblk, mask):
    z = _dot_nt(qcat, kblk)
    e = jnp.exp(-jnp.abs(z))
    sp = jnp.maximum(z, 0.0) + jnp.log(1.0 + e)
    l1m = -sp
    if mask is not None:
        l1m = jnp.where(mask, l1m, 0.0)
    inv = 1.0 / (1.0 + e)
    pos = z >= 0.0
    return z - sp, l1m, jnp.where(pos, 1.0, e) * inv, jnp.where(pos, e, 1.0) * inv


def _attn_consts(blk):
    lane_is_first = lax.broadcasted_iota(jnp.int32, (1, HEAD_PAIR), 1) < HEAD_DIM
    r = lax.broadcasted_iota(jnp.int32, (blk, blk), 0)
    c = lax.broadcasted_iota(jnp.int32, (blk, blk), 1)
    after = (r > c).astype(BF16)
    from_here = (r >= c).astype(BF16)
    qrow = lax.broadcasted_iota(jnp.int32, (2 * blk, blk), 0)
    qrow = jnp.where(qrow >= blk, qrow - blk, qrow)
    causal = lax.broadcasted_iota(jnp.int32, (2 * blk, blk), 1) < qrow
    return lane_is_first, after, from_here, causal


def _while_mass_left(qi, carry, block):
    def alive(c):
        return jnp.max(c[0]) > LOG_F32_ZERO

    def cond(state):
        return jnp.logical_and(state[0] < qi, state[1])

    def step(state):
        new = block(qi - 1 - state[0], state[2])
        return state[0] + 1, alive(new), new

    return lax.while_loop(cond, step, (jnp.int32(0), alive(carry), carry))[2]


def _ride(rider, n_in, n_out, refs, grid):
    if rider is None:
        return refs[:n_in], refs[n_in:n_in + n_out], refs[n_in + n_out:], lambda: None
    n = rider.n
    ins, srcs = refs[:n_in], refs[n_in:n_in + n]
    outs, dsts = refs[n_in + n:n_in + n + n_out], refs[n_in + n + n_out:n_in + 2 * n + n_out]
    rest = refs[n_in + 2 * n + n_out:]
    scratch, sems = rest[:len(rest) - 3], rest[len(rest) - 3:]
    step, n_steps = 0, 1
    for axis, size in enumerate(grid):
        step, n_steps = step * size + pl.program_id(axis), n_steps * size

    @pl.when(step == 0)
    def _():
        rider.start(srcs, dsts, sems)

    def finish():
        @pl.when(step == n_steps - 1)
        def _():
            rider.finish(srcs, dsts, sems)

    return ins, outs, scratch, finish


def _attn_fwd(qkv, gmga, *, name, rider=None):
    t = qkv.shape[0]
    blk = _tile(t, ATTN_BLOCK)
    nq = t // blk

    def body(*refs):
        (q_ref, k_ref, v_ref, ga_ref), (o_ref, cv_ref), _, finish_rider = _ride(rider, 4, 2, refs, (N_HEAD_PAIRS, nq))
        qi = pl.program_id(1)
        lane_is_first, after, _, causal = _attn_consts(blk)
        qcat = _split_heads(q_ref[...], lane_is_first)

        def block(kb, carry, mask):
            run, acc = carry
            rows = pl.ds(pl.multiple_of(kb * blk, blk), blk)
            lb, l1m, _, _ = _scores(qcat, k_ref[rows, :], mask)
            hi, lo = _split_bf16(l1m)
            w = jnp.exp(lb + (_dot(hi, after) + _dot(lo, after) + run))
            if mask is not None:
                w = jnp.where(mask, w, 0.0)
            vcat = _split_heads(v_ref[rows, :], lane_is_first)
            acc = acc + _dot(_side_by_side(w.astype(BF16), blk), vcat)
            return run + jnp.sum(l1m, axis=-1, keepdims=True), acc

        carry = block(qi, (jnp.zeros((2 * blk, 1), F32), jnp.zeros((blk, HEAD_PAIR), F32)), causal)
        _, o = _while_mass_left(qi, carry, lambda kb, c: block(kb, c, None))
        o_ref[...] = o
        ga = ga_ref[...]
        cv_ref[...] = (o * (ga * _sigmoid(ga))).astype(BF16)
        finish_rider()

    qspec = pl.BlockSpec((blk, HEAD_PAIR), lambda p, i: (i, p))
    extra = rider or _NO_RIDER
    return pl.pallas_call(
        body, name=name, grid=(N_HEAD_PAIRS, nq),
        in_specs=[qspec,
                  pl.BlockSpec((t, HEAD_PAIR), lambda p, i: (0, N_HEAD_PAIRS + p)),
                  pl.BlockSpec((t, HEAD_PAIR), lambda p, i: (0, 2 * N_HEAD_PAIRS + p)),
                  pl.BlockSpec((blk, HEAD_PAIR), lambda p, i: (i, GA_BLOCK + p))] + extra.specs,
        out_specs=[qspec, qspec] + extra.specs,
        out_shape=[jax.ShapeDtypeStruct((t, BRANCH_WIDTH), F32), jax.ShapeDtypeStruct((t, BRANCH_WIDTH), BF16)] + extra.out_shape,
        scratch_shapes=extra.scratch_shapes,
        compiler_params=_params(2),
    )(qkv, qkv, qkv, gmga, *extra.srcs)


def _attn_bwd(qkv, o, gmga, dcv, *, name, rider=None):
    t = qkv.shape[0]
    blk = _tile(t, ATTN_BLOCK)
    nq = t // blk

    def body(*refs):
        ins, outs, (dk_acc, dv_acc), finish_rider = _ride(rider, 6, 4, refs, (N_HEAD_PAIRS, nq))
        q_ref, k_ref, v_ref, o_ref, ga_ref, dcv_ref = ins
        dq_ref, dk_ref, dv_ref, dga_ref = outs
        qi = pl.program_id(1)
        lane_is_first, after, from_here, causal = _attn_consts(blk)

        @pl.when(qi == 0)
        def _():
            dk_acc[...] = jnp.zeros_like(dk_acc)
            dv_acc[...] = jnp.zeros_like(dv_acc)

        ga, ov, dcvv = ga_ref[...], o_ref[...], dcv_ref[...]
        sg = _sigmoid(ga)
        dob = (dcvv * (ga * sg)).astype(BF16)
        dga_ref[...] = (dcvv * ov * _silu_grad(ga, sg)).astype(BF16)
        gt = dob.astype(F32) * ov
        g_total = jnp.concatenate(
            [jnp.sum(jnp.where(lane_is_first, gt, 0.0), axis=-1, keepdims=True),
             jnp.sum(jnp.where(lane_is_first, 0.0, gt), axis=-1, keepdims=True)], axis=0)
        qcat = _split_heads(q_ref[...], lane_is_first)
        docat = _split_heads(dob, lane_is_first)

        def block(kb, carry, mask):
            run, g_run, dq = carry
            rows = pl.ds(pl.multiple_of(kb * blk, blk), blk)
            kblk = k_ref[rows, :]
            lb, l1m, sig, one_m_sig = _scores(qcat, kblk, mask)
            hi, lo = _split_bf16(l1m)
            w = jnp.exp(lb + (_dot(hi, after) + _dot(lo, after) + run))
            if mask is not None:
                w = jnp.where(mask, w, 0.0)
            wb = w.astype(BF16)
            g = _dot_nt(docat, v_ref[rows, :]) * wb.astype(F32)
            ghi, glo = _split_bf16(g)
            g_before = g_total - g_run - (_dot(ghi, from_here) + _dot(glo, from_here))
            dz = g * one_m_sig - g_before * sig
            if mask is not None:
                dz = jnp.where(mask, dz, 0.0)
            dzb = dz.astype(BF16)
            dq = dq + _dot(_side_by_side(dzb, blk), _split_heads(kblk, lane_is_first))
            dk_acc[rows, :] += _dot_tn(dzb, qcat)
            dv_acc[rows, :] += _dot_tn(wb, docat)
            return (run + jnp.sum(l1m, axis=-1, keepdims=True), g_run + jnp.sum(g, axis=-1, keepdims=True), dq)

        zero = jnp.zeros((2 * blk, 1), F32)
        carry = block(qi, (zero, zero, jnp.zeros((blk, HEAD_PAIR), F32)), causal)
        _, _, dq = _while_mass_left(qi, carry, lambda kb, c: block(kb, c, None))
        dq_ref[...] = (dq * ATTN_SCALE).astype(BF16)

        @pl.when(qi == nq - 1)
        def _():
            dk_ref[...] = dk_acc[...].astype(BF16)
            dv_ref[...] = dv_acc[...].astype(BF16)

        finish_rider()

    qspec = pl.BlockSpec((blk, HEAD_PAIR), lambda p, i: (i, p))
    whole = pl.BlockSpec((t, HEAD_PAIR), lambda p, i: (0, p))
    out = jax.ShapeDtypeStruct((t, BRANCH_WIDTH), BF16)
    extra = rider or _NO_RIDER
    return pl.pallas_call(
        body, name=name, grid=(N_HEAD_PAIRS, nq),
        in_specs=[qspec,
                  pl.BlockSpec((t, HEAD_PAIR), lambda p, i: (0, N_HEAD_PAIRS + p)),
                  pl.BlockSpec((t, HEAD_PAIR), lambda p, i: (0, 2 * N_HEAD_PAIRS + p)),
                  qspec,
                  pl.BlockSpec((blk, HEAD_PAIR), lambda p, i: (i, GA_BLOCK + p)),
                  qspec] + extra.specs,
        out_specs=[qspec, whole, whole, qspec] + extra.specs,
        out_shape=[out, out, out, out] + extra.out_shape,
        scratch_shapes=[pltpu.VMEM((t, HEAD_PAIR), F32), pltpu.VMEM((t, HEAD_PAIR), F32)] + extra.scratch_shapes,
        compiler_params=_params(2),
    )(qkv, qkv, qkv, o, gmga, dcv, *extra.srcs)


def _mesh_position():
    x, y, c = lax.axis_index("x"), lax.axis_index("y"), lax.axis_index("c")
    return x, y, c, 4 * x + 2 * y + c


def _flipped(x, y, c, k):
    return (1 - x if k & 4 else x, 1 - y if k & 2 else y, 1 - c if k & 1 else c)


def _all_to_all(srcs, *, name, same_block):
    ex = _Exchange(srcs, same_block)

    def body(*refs):
        ex.start(refs[:ex.n], refs[ex.n:2 * ex.n], refs[2 * ex.n:])
        ex.finish(refs[:ex.n], refs[ex.n:2 * ex.n], refs[2 * ex.n:])

    return pl.pallas_call(
        body, name=name, in_specs=ex.specs, out_specs=ex.specs, out_shape=ex.out_shape,
        scratch_shapes=ex.scratch_shapes,
    )(*srcs)


class _Exchange:
    def __init__(self, srcs, same_block):
        self.srcs, self.same_block, self.n = list(srcs), same_block, len(srcs)
        self.specs = [pl.BlockSpec(memory_space=pl.ANY)] * self.n
        self.out_shape = [jax.ShapeDtypeStruct((N_DEV,) + tuple(s.shape if same_block else s.shape[1:]), s.dtype)
                          for s in self.srcs]
        self.scratch_shapes = [pltpu.SemaphoreType.DMA((N_DEV - 1, self.n)), pltpu.SemaphoreType.DMA((N_DEV - 1, self.n)),
                               pltpu.SemaphoreType.DMA((self.n,))] if self.n else []

    def _copies(self, src_refs, dst_refs, sems, with_arrivals):
        send_sems, recv_sems, local_sems = sems
        x, y, c, me = _mesh_position()

        def outgoing(i, j):
            return src_refs[i] if self.same_block else src_refs[i].at[j]

        def remote(i, k, slot):
            return pltpu.make_async_remote_copy(
                src_ref=outgoing(i, jnp.bitwise_xor(me, k)), dst_ref=dst_refs[i].at[slot],
                send_sem=send_sems.at[k - 1, i], recv_sem=recv_sems.at[k - 1, i],
                device_id=_flipped(x, y, c, k), device_id_type=MESH)

        pairs = [(i, k) for k in range(1, N_DEV) for i in range(self.n)]
        mine = [pltpu.make_async_copy(outgoing(i, me), dst_refs[i].at[me], local_sems.at[i]) for i in range(self.n)]
        sent = [remote(i, k, me) for i, k in pairs]
        arrivals = [remote(i, k, jnp.bitwise_xor(me, k)) for i, k in pairs] if with_arrivals else []
        return mine, sent, arrivals

    def start(self, src_refs, dst_refs, sems):
        mine, sent, _ = self._copies(src_refs, dst_refs, sems, False)
        for cp in mine + sent:
            cp.start()

    def finish(self, src_refs, dst_refs, sems):
        mine, sent, arrivals = self._copies(src_refs, dst_refs, sems, True)
        for cp in arrivals:
            cp.wait_recv()
        for cp in sent:
            cp.wait_send()
        for cp in mine:
            cp.wait()


_NO_RIDER = _Exchange([], True)


def _adamw(parts, w, m, v, *, name):
    layers, rows, cols = w.shape
    assert len(parts) == layers
    tr = rows
    while tr * cols > ADAM_TILE_ELEMS and tr % 32 == 0:
        tr //= 2
    row = pl.BlockSpec((1, tr, cols), lambda l, i: (l, i, 0))

    def body(*refs):
        p_refs = refs[:layers]
        w_ref, m_ref, v_ref, g_ref, d_ref, nm_ref, nv_ref = refs[layers:]
        layer = pl.program_id(0)
        g = None
        for k in range(N_DEV):
            part = p_refs[0][k]
            for l in range(1, layers):
                part = jnp.where(layer == l, p_refs[l][k], part)
            g = part.astype(F32) if g is None else g + part.astype(F32)
        m2 = ADAM_B1 * m_ref[0] + (1.0 - ADAM_B1) * g
        v2 = ADAM_B2 * v_ref[0] + (1.0 - ADAM_B2) * (g * g)
        m_hat = m2 / (1.0 - ADAM_B1 ** ADAM_STEP)
        v_hat = v2 / (1.0 - ADAM_B2 ** ADAM_STEP)
        g_ref[0] = g
        d_ref[0] = -ADAM_LR * (m_hat / (jnp.sqrt(v_hat) + ADAM_EPS) + ADAM_WD * w_ref[0])
        nm_ref[0] = m2
        nv_ref[0] = v2

    out = jax.ShapeDtypeStruct((layers, rows, cols), F32)
    return pl.pallas_call(
        body, name=name, grid=(layers, rows // tr),
        in_specs=[pl.BlockSpec((N_DEV, tr, cols), lambda l, i: (0, i, 0))] * layers + [row, row, row],
        out_specs=[row, row, row, row], out_shape=[out, out, out, out],
        compiler_params=_params(2),
    )(*parts, w, m, v)


MATMUL_WEIGHTS = ("w_in", "w_pool_out", "w_conv_out", "w_attn_out", "w_o")
SMALL = ("conv_w", "norm_pre", "pool_w", "pool_b", "pool_scale", "conv_b", "conv_ln_g", "conv_ln_b", "norm_post")
WEIGHT_ORDER = ("norm_pre", "w_in", "pool_w", "pool_b", "pool_scale", "w_pool_out", "conv_w", "conv_b",
                "conv_ln_g", "conv_ln_b", "w_conv_out", "w_attn_out", "w_o", "norm_post")


def _shard_axis(name):
    return -2 if name == "w_o" else -1


def _pack_rows(flat_parts, row_multiple):
    flat = jnp.concatenate(flat_parts, axis=-1)
    n = flat.shape[-1]
    chunk = row_multiple * LANES
    total = -(-n // chunk) * chunk
    pad = [(0, 0)] * (flat.ndim - 1) + [(0, total - n)]
    return jnp.pad(flat, pad).reshape(flat.shape[:-1] + (total // LANES, LANES))


def _unpack(buf, shapes):
    flat = buf.reshape(-1)
    out, at = {}, 0
    for name, shape in shapes:
        n = 1
        for s in shape:
            n *= s
        out[name] = flat[at:at + n].reshape(shape)
        at += n
    return out


def _to_dest_major(name, full):
    axis = full.ndim + _shard_axis(name)
    n = full.shape[axis] // N_DEV
    return jnp.stack([lax.slice_in_dim(full, d * n, (d + 1) * n, axis=axis) for d in range(N_DEV)])


def _from_source_major(name, gathered):
    return jnp.concatenate([gathered[d] for d in range(N_DEV)], axis=_shard_axis(name))


def kernel(x, norm_pre, w_in, pool_w, pool_b, pool_scale, w_pool_out, conv_w, conv_b, conv_ln_g, conv_ln_b, w_conv_out, w_attn_out, w_o, norm_post, loss_target, m_norm_pre, m_w_in, m_pool_w, m_pool_b, m_pool_scale, m_w_pool_out, m_conv_w, m_conv_b, m_conv_ln_g, m_conv_ln_b, m_w_conv_out, m_w_attn_out, m_w_o, m_norm_post, v_norm_pre, v_w_in, v_pool_w, v_pool_b, v_pool_scale, v_w_pool_out, v_conv_w, v_conv_b, v_conv_ln_g, v_conv_ln_b, v_w_conv_out, v_w_attn_out, v_w_o, v_norm_post):
    weights = dict(norm_pre=norm_pre, w_in=w_in, pool_w=pool_w, pool_b=pool_b, pool_scale=pool_scale,
                   w_pool_out=w_pool_out, conv_w=conv_w, conv_b=conv_b, conv_ln_g=conv_ln_g, conv_ln_b=conv_ln_b,
                   w_conv_out=w_conv_out, w_attn_out=w_attn_out, w_o=w_o, norm_post=norm_post)
    mom1 = dict(norm_pre=m_norm_pre, w_in=m_w_in, pool_w=m_pool_w, pool_b=m_pool_b, pool_scale=m_pool_scale,
                w_pool_out=m_w_pool_out, conv_w=m_conv_w, conv_b=m_conv_b, conv_ln_g=m_conv_ln_g, conv_ln_b=m_conv_ln_b,
                w_conv_out=m_w_conv_out, w_attn_out=m_w_attn_out, w_o=m_w_o, norm_post=m_norm_post)
    mom2 = dict(norm_pre=v_norm_pre, w_in=v_w_in, pool_w=v_pool_w, pool_b=v_pool_b, pool_scale=v_pool_scale,
                w_pool_out=v_w_pool_out, conv_w=v_conv_w, conv_b=v_conv_b, conv_ln_g=v_conv_ln_g, conv_ln_b=v_conv_ln_b,
                w_conv_out=v_w_conv_out, w_attn_out=v_w_attn_out, w_o=v_w_o, norm_post=v_norm_post)
    xs = x[0]
    target = loss_target[0]

    conv_rows = jnp.pad(conv_w, ((0, 0), (0, CONV_HALO - CONV_KERNEL), (0, 0)))
    shards = [[weights[n][l].astype(BF16) for n in MATMUL_WEIGHTS] for l in range(DEPTH)]
    gathered = _all_to_all([shards[0][0], conv_rows], name="gather_weights", same_block=True)
    full = [{"w_in": _from_source_major("w_in", gathered[0])}, None]
    conv_full = _from_source_major("conv_w", gathered[1])

    def in_sections(w):
        return dict(pg=w[:, 0:1024], c2gc=w[:, 1024:2560], q=w[:, 2560:3072], k=w[:, 3072:3584], v=w[:, 3584:4096],
                    gmga=jnp.concatenate([w[:, 4608:7680], w[:, 4096:4608]], axis=1))

    saved = []
    cur = xs
    for l in range(DEPTH):
        sec = in_sections(full[l]["w_in"])
        w_qkv = jnp.concatenate([sec["q"] * ATTN_SCALE, sec["k"], sec["v"]], axis=1)
        pw = pool_w[l].astype(BF16)
        pb, ps = pool_b[l].reshape(1, -1), pool_scale[l].reshape(1, -1)
        cb, lg, lb = conv_b[l].reshape(1, -1), conv_ln_g[l].reshape(1, -1), conv_ln_b[l].reshape(1, -1)
        h = _rms_fwd(cur, norm_pre[l].reshape(1, -1), name=f"rms_pre_fwd_{l}")
        pg = _matmul(h, sec["pg"], mode="nn", name=f"proj_pg_{l}")
        c2gc = _matmul(h, sec["c2gc"], mode="nn", name=f"proj_c2gc_{l}")
        qkv = _matmul(h, w_qkv, mode="nn", name=f"proj_qkv_{l}", out_dtype=BF16)
        if l == 0:
            gmga, *arrived = _matmul(h, sec["gmga"], mode="nn", name=f"proj_gmga_{l}", rider=_Exchange(shards[0][1:], True))
            full[0].update({n: _from_source_major(n, g) for n, g in zip(MATMUL_WEIGHTS[1:], arrived)})
        else:
            gmga = _matmul(h, sec["gmga"], mode="nn", name=f"proj_gmga_{l}")
        a_act = _pool_fwd(pg, pw, pb, ps, name=f"pool_fwd_{l}")
        b_act = _conv_fwd(c2gc, conv_full[l], cb, lg, lb, name=f"conv_fwd_{l}")
        rider = _Exchange(shards[l + 1], True) if l + 1 < DEPTH else None
        o, c_act, *arrived = _attn_fwd(qkv, gmga, name=f"attn_fwd_{l}", rider=rider)
        if rider is not None:
            full[l + 1] = {n: _from_source_major(n, g) for n, g in zip(MATMUL_WEIGHTS, arrived)}
        ya = _matmul(a_act, full[l]["w_pool_out"], mode="nn", name=f"out_pool_{l}")
        yb = _matmul(b_act, full[l]["w_conv_out"], mode="nn", name=f"out_conv_{l}")
        yc = _matmul(c_act, full[l]["w_attn_out"], mode="nn", name=f"out_attn_{l}")
        mix = _merge_fwd(gmga, ya, yb, yc, name=f"merge_fwd_{l}")
        out = _matmul(mix, full[l]["w_o"], mode="nn", name=f"out_proj_{l}")
        nxt = _rms_fwd(out, norm_post[l].reshape(1, -1), name=f"rms_post_fwd_{l}", resid=cur)
        saved.append(dict(x=cur, h=h, pg=pg, c2gc=c2gc, qkv=qkv, gmga=gmga, a=a_act, b=b_act, c=c_act, o=o,
                          ya=ya, yb=yb, yc=yc, mix=mix, out=out, pw=pw, pb=pb, ps=ps, cb=cb, lg=lg, lb=lb))
        cur = nxt

    loss_tile, dx = _loss_head(cur, target, name="loss_head")
    loss = lax.psum(loss_tile[0, 0], ("x", "y", "c"))

    grads = {n: [None] * DEPTH for n in WEIGHT_ORDER}
    parts = {n: [None] * DEPTH for n in MATMUL_WEIGHTS}
    for l in reversed(range(DEPTH)):
        s = saved[l]
        dout, grads["norm_post"][l] = _rms_bwd(s["out"], norm_post[l].reshape(1, -1), dx, name=f"rms_post_bwd_{l}")
        dmix = _matmul(dout, full[l]["w_o"], mode="nt", name=f"d_mix_{l}")
        grads["w_o"][l] = _matmul(s["mix"], dout, mode="tn", name=f"d_w_o_{l}", out_dtype=BF16)
        dya, dyb, dyc, dgm = _merge_bwd(dmix, s["gmga"], s["ya"], s["yb"], s["yc"], name=f"merge_bwd_{l}")
        da = _matmul(dya, full[l]["w_pool_out"], mode="nt", name=f"d_pool_act_{l}")
        grads["w_pool_out"][l] = _matmul(s["a"], dya, mode="tn", name=f"d_w_pool_out_{l}", out_dtype=BF16)
        db = _matmul(dyb, full[l]["w_conv_out"], mode="nt", name=f"d_conv_act_{l}")
        grads["w_conv_out"][l] = _matmul(s["b"], dyb, mode="tn", name=f"d_w_conv_out_{l}", out_dtype=BF16)
        dc = _matmul(dyc, full[l]["w_attn_out"], mode="nt", name=f"d_attn_act_{l}")
        grads["w_attn_out"][l] = _matmul(s["c"], dyc, mode="tn", name=f"d_w_attn_out_{l}", out_dtype=BF16)
        rider = _Exchange([_to_dest_major(n, grads[n][l]) for n in MATMUL_WEIGHTS[1:]], False)
        dq, dk, dv, dga, *arrived = _attn_bwd(s["qkv"], s["o"], s["gmga"], dc, name=f"attn_bwd_{l}", rider=rider)
        for n, p in zip(MATMUL_WEIGHTS[1:], arrived):
            parts[n][l] = p
        dc2gc, dcw, dcvec = _conv_bwd(s["c2gc"], db, conv_full[l], s["cb"], s["lg"], s["lb"], name=f"conv_bwd_{l}")
        dpg, dpw, dpvec = _pool_bwd(s["pg"], da, s["pw"], s["pb"], s["ps"], name=f"pool_bwd_{l}")
        grads["conv_w"][l] = dcw[:CONV_KERNEL]
        grads["conv_b"][l], grads["conv_ln_g"][l], grads["conv_ln_b"][l] = dcvec[0], dcvec[1], dcvec[2]
        grads["pool_w"][l] = dpw
        grads["pool_b"][l] = dpvec[0].reshape(4, POOL_GROUP_DIM)
        grads["pool_scale"][l] = dpvec[1]
        dproj = jnp.concatenate([dpg, dc2gc, dq, dk, dv, dga, dgm], axis=1)
        grads["w_in"][l] = _matmul(s["h"], dproj, mode="tn", name=f"d_w_in_{l}", out_dtype=BF16)
        rider = _Exchange([_to_dest_major("w_in", grads["w_in"][l])], False)
        dh, parts["w_in"][l] = _matmul(dproj, full[l]["w_in"], mode="nt", name=f"d_h_{l}", rider=rider)
        dx, dg_pre = _rms_bwd(s["x"], norm_pre[l].reshape(1, -1), dh, name=f"rms_pre_bwd_{l}", resid=dx)
        grads["norm_pre"][l] = dg_pre.reshape(-1)
        grads["norm_post"][l] = grads["norm_post"][l].reshape(-1)
    small_grads = {n: jnp.stack(grads[n]) for n in SMALL}
    replicated = jnp.concatenate([small_grads[n].reshape(-1) for n in SMALL[1:]])
    small = _pack_rows([_to_dest_major("conv_w", small_grads["conv_w"]).reshape(N_DEV, -1),
                        jnp.broadcast_to(replicated, (N_DEV, replicated.size))], 16).astype(BF16)
    (small_parts,) = _all_to_all([small], name="exchange_grads", same_block=False)

    outs = [dict(), dict(), dict(), dict()]
    for n in MATMUL_WEIGHTS:
        res = _adamw(parts[n], weights[n], mom1[n], mom2[n], name=f"adamw_{n}")
        for o, r in zip(outs, res):
            o[n] = r

    def packed(tree):
        return _pack_rows([tree[n].reshape(-1) for n in SMALL], 16)[None]

    shapes = [(n, weights[n].shape) for n in SMALL]
    res = _adamw([small_parts], packed(weights), packed(mom1), packed(mom2), name="adamw_small")
    for o, r in zip(outs, res):
        o.update(_unpack(r, shapes))
    return (loss, dx[None], *[o[n] for o in outs for n in WEIGHT_ORDER])
```

```python
import functools

import jax
import jax.numpy as jnp
from jax import lax
from jax.experimental import pallas as pl
from jax.experimental.pallas import tpu as pltpu

F32 = jnp.float32
BF16 = jnp.bfloat16

D_MODEL = 1024
DEPTH = 2
POOL_WINDOWS = (2, 4, 8, 16)
POOL_GROUP_DIM = 128
BRANCH_WIDTH = 512
CONV_KERNEL = 31
CONV_HALO = 32
POOL_HALO = 16
HEAD_DIM = 64
HEAD_PAIR = 128
N_HEAD_PAIRS = 4
ATTN_SCALE = 0.125
LOG_F32_ZERO = -104.0
RMS_EPS = 1e-6
LN_EPS = 1e-5
N_DEV = 8
LANES = 128

ADAM_LR = 0.001
ADAM_B1 = 0.9
ADAM_B2 = 0.999
ADAM_EPS = 1e-08
ADAM_WD = 0.01
ADAM_STEP = 10

ROW_TILE = 256
ATTN_BLOCK = 256
ATTN_BLOCKS_PER_STEP = 2
MM_TILE = 1024
MM_K_TILE = 1280
ADAM_TILE_ELEMS = 256 * 1024
VMEM_LIMIT = 48 * 1024 * 1024

MESH = pl.DeviceIdType.MESH


def _params(n_axes):
    return pltpu.CompilerParams(dimension_semantics=("arbitrary",) * n_axes, vmem_limit_bytes=VMEM_LIMIT)


def _tile(n, pref):
    if n <= pref:
        return n
    t = (pref // LANES) * LANES
    while n % t:
        t -= LANES
    return t


def _dot(a, b):
    return jnp.dot(a, b, preferred_element_type=F32)


def _dot_nt(a, b):
    return lax.dot_general(a, b, (((1,), (1,)), ((), ())), preferred_element_type=F32)


def _dot_tn(a, b):
    return lax.dot_general(a, b, (((0,), (0,)), ((), ())), preferred_element_type=F32)


def _sigmoid(x):
    return 1.0 / (1.0 + jnp.exp(-x))


def _silu_grad(x, s):
    return s * (1.0 + x * (1.0 - s))


def _matmul(a, b, *, mode, name, out_dtype=F32, rider=None):
    if mode == "nn":
        (m, k), n = a.shape, b.shape[1]
    elif mode == "nt":
        (m, k), n = a.shape, b.shape[0]
    else:
        (k, m), n = a.shape, b.shape[1]
    tm, tn, tk = _tile(m, MM_TILE), _tile(n, MM_TILE), _tile(k, MM_K_TILE)
    nk = k // tk
    grid = (m // tm, n // tn, nk)
    dot = {"nn": _dot, "nt": _dot_nt, "tn": _dot_tn}[mode]
    a_spec = pl.BlockSpec((tk, tm), lambda i, j, kk: (kk, i)) if mode == "tn" else pl.BlockSpec((tm, tk), lambda i, j, kk: (i, kk))
    b_spec = pl.BlockSpec((tn, tk), lambda i, j, kk: (j, kk)) if mode == "nt" else pl.BlockSpec((tk, tn), lambda i, j, kk: (kk, j))
    o_spec = pl.BlockSpec((tm, tn), lambda i, j, kk: (i, j))

    def body(*refs):
        (a_ref, b_ref), (o_ref,), scratch, finish_rider = _ride(rider, 2, 1, refs, grid)
        part = dot(a_ref[...], b_ref[...])
        if nk == 1:
            o_ref[...] = part.astype(out_dtype)
        else:
            scr = scratch[0]
            kk = pl.program_id(2)

            @pl.when(kk == 0)
            def _():
                scr[...] = part

            @pl.when(kk > 0)
            def _():
                scr[...] += part

            @pl.when(kk == nk - 1)
            def _():
                o_ref[...] = scr[...].astype(out_dtype)

        finish_rider()

    extra = rider or _NO_RIDER
    res = pl.pallas_call(
        body, name=name, grid=grid,
        in_specs=[a_spec, b_spec] + extra.specs, out_specs=[o_spec] + extra.specs,
        out_shape=[jax.ShapeDtypeStruct((m, n), out_dtype)] + extra.out_shape,
        scratch_shapes=([pltpu.VMEM((tm, tn), F32)] if nk > 1 else []) + extra.scratch_shapes,
        compiler_params=_params(3),
    )(a, b, *extra.srcs)
    return res if rider is not None else res[0]


def _rms_fwd(x, g, *, name, resid=None):
    t = x.shape[0]
    tm = _tile(t, ROW_TILE)
    row = pl.BlockSpec((tm, D_MODEL), lambda i: (i, 0))
    vec = pl.BlockSpec((1, D_MODEL), lambda i: (0, 0))
    has_resid = resid is not None

    def body(*refs):
        x_ref, g_ref = refs[0], refs[1]
        o_ref = refs[-1]
        xv = x_ref[...]
        y = xv * lax.rsqrt(jnp.mean(xv * xv, axis=-1, keepdims=True) + RMS_EPS) * g_ref[...]
        if has_resid:
            o_ref[...] = refs[2][...] + y
        else:
            o_ref[...] = y.astype(BF16)

    return pl.pallas_call(
        body, name=name, grid=(t // tm,),
        in_specs=[row, vec] + ([row] if has_resid else []), out_specs=row,
        out_shape=jax.ShapeDtypeStruct((t, D_MODEL), F32 if has_resid else BF16),
        compiler_params=_params(1),
    )(*((x, g) + ((resid,) if has_resid else ())))


def _rms_bwd(xin, g, dy, *, name, resid=None):
    t = xin.shape[0]
    tm = _tile(t, ROW_TILE)
    row = pl.BlockSpec((tm, D_MODEL), lambda i: (i, 0))
    vec = pl.BlockSpec((1, D_MODEL), lambda i: (0, 0))
    has_resid = resid is not None
    out_dtype = F32 if has_resid else BF16

    def body(*refs):
        x_ref, g_ref, dy_ref = refs[0], refs[1], refs[2]
        dx_ref, dg_ref = refs[-2], refs[-1]
        xv, dyv = x_ref[...], dy_ref[...]
        r = lax.rsqrt(jnp.mean(xv * xv, axis=-1, keepdims=True) + RMS_EPS)
        a = dyv * g_ref[...]
        dx = r * a - xv * (r * r * r) * jnp.mean(a * xv, axis=-1, keepdims=True)
        if has_resid:
            dx = dx + refs[3][...]
        dx_ref[...] = dx.astype(out_dtype)
        part = jnp.sum(dyv * xv * r, axis=0, keepdims=True)

        @pl.when(pl.program_id(0) == 0)
        def _():
            dg_ref[...] = part

        @pl.when(pl.program_id(0) > 0)
        def _():
            dg_ref[...] += part

    return pl.pallas_call(
        body, name=name, grid=(t // tm,),
        in_specs=[row, vec, row] + ([row] if has_resid else []), out_specs=[row, vec],
        out_shape=[jax.ShapeDtypeStruct((t, D_MODEL), out_dtype), jax.ShapeDtypeStruct((1, D_MODEL), F32)],
        compiler_params=_params(1),
    )(*((xin, g, dy) + ((resid,) if has_resid else ())))


def _loss_head(x, target, *, name):
    t = x.shape[0]
    tm = _tile(t, ROW_TILE)
    row = pl.BlockSpec((tm, D_MODEL), lambda i: (i, 0))
    acc = pl.BlockSpec((8, LANES), lambda i: (0, 0))

    def body(x_ref, t_ref, l_ref, dx_ref):
        diff = x_ref[...] - t_ref[...]
        dx_ref[...] = diff * (1.0 / D_MODEL)
        part = 0.5 * jnp.sum(jnp.mean(diff * diff, axis=-1, keepdims=True), axis=0, keepdims=True)

        @pl.when(pl.program_id(0) == 0)
        def _():
            l_ref[...] = jnp.zeros((8, LANES), F32) + part

        @pl.when(pl.program_id(0) > 0)
        def _():
            l_ref[...] += part

    return pl.pallas_call(
        body, name=name, grid=(t // tm,),
        in_specs=[row, row], out_specs=[acc, row],
        out_shape=[jax.ShapeDtypeStruct((8, LANES), F32), jax.ShapeDtypeStruct((t, D_MODEL), F32)],
        compiler_params=_params(1),
    )(x, target)


def _window_sum(ext, n_doublings, forward):
    rows = ext.shape[0]
    s, sh = ext, 1
    for _ in range(n_doublings):
        s = s + pltpu.roll(s, sh if forward else rows - sh, 0)
        sh *= 2
    return s


def _pool_fwd(pg, pool_w, pool_b, pool_scale, *, name):
    t = pg.shape[0]
    tm = _tile(t, ROW_TILE)

    def body(pg_ref, w_ref, b_ref, s_ref, o_ref, halo):
        i = pl.program_id(0)

        @pl.when(i == 0)
        def _():
            halo[...] = jnp.zeros_like(halo)

        p = pg_ref[:, :BRANCH_WIDTH]
        gate = pg_ref[:, BRANCH_WIDTH:]
        ext = jnp.concatenate([halo[...], p], axis=0)
        pos = i * tm + lax.broadcasted_iota(jnp.int32, (tm, 1), 0)
        outs = []
        for g, w in enumerate(POOL_WINDOWS):
            cols = slice(g * POOL_GROUP_DIM, (g + 1) * POOL_GROUP_DIM)
            cnt = jnp.minimum(pos + 1, w).astype(F32)
            d = _window_sum(ext[:, cols], g + 1, True)[POOL_HALO:] / cnt - p[:, cols]
            y = (_dot(d.astype(BF16), w_ref[g]) + b_ref[:, cols]) * s_ref[:, cols]
            gg = gate[:, cols]
            outs.append(y * (gg * _sigmoid(gg)))
        o_ref[...] = jnp.concatenate(outs, axis=1).astype(BF16)
        halo[...] = p[tm - POOL_HALO:, :]

    vec = pl.BlockSpec((1, BRANCH_WIDTH), lambda i: (0, 0))
    return pl.pallas_call(
        body, name=name, grid=(t // tm,),
        in_specs=[pl.BlockSpec((tm, 2 * BRANCH_WIDTH), lambda i: (i, 0)),
                  pl.BlockSpec((4, POOL_GROUP_DIM, POOL_GROUP_DIM), lambda i: (0, 0, 0)), vec, vec],
        out_specs=pl.BlockSpec((tm, BRANCH_WIDTH), lambda i: (i, 0)),
        out_shape=jax.ShapeDtypeStruct((t, BRANCH_WIDTH), BF16),
        scratch_shapes=[pltpu.VMEM((POOL_HALO, BRANCH_WIDTH), F32)],
        compiler_params=_params(1),
    )(pg, pool_w, pool_b, pool_scale)


def _pool_bwd(pg, d_out, pool_w, pool_b, pool_scale, *, name):
    t = pg.shape[0]
    tm = _tile(t, ROW_TILE)
    nt = t // tm
    halo_per_tile = tm // POOL_HALO

    def body(pg_ref, halo_ref, do_ref, w_ref, b_ref, s_ref, dpg_ref, dw_ref, dvec_ref, carry):
        i = pl.program_id(0)
        ri = nt - 1 - i

        @pl.when(i == 0)
        def _():
            carry[...] = jnp.zeros_like(carry)
            dw_ref[...] = jnp.zeros_like(dw_ref)
            dvec_ref[...] = jnp.zeros_like(dvec_ref)

        p = pg_ref[:, :BRANCH_WIDTH]
        gate = pg_ref[:, BRANCH_WIDTH:]
        hp = jnp.where(ri > 0, halo_ref[:, :BRANCH_WIDTH], 0.0)
        ext = jnp.concatenate([hp, p], axis=0)
        pos = ri * tm + lax.broadcasted_iota(jnp.int32, (tm, 1), 0)
        dps, dgs, dbs, dss = [], [], [], []
        for g, w in enumerate(POOL_WINDOWS):
            cols = slice(g * POOL_GROUP_DIM, (g + 1) * POOL_GROUP_DIM)
            cnt = jnp.minimum(pos + 1, w).astype(F32)
            d = (_window_sum(ext[:, cols], g + 1, True)[POOL_HALO:] / cnt - p[:, cols]).astype(BF16)
            y1 = _dot(d, w_ref[g]) + b_ref[:, cols]
            scale = s_ref[:, cols]
            y2 = y1 * scale
            gg = gate[:, cols]
            sg = _sigmoid(gg)
            do = do_ref[:, cols]
            dy2 = do * (gg * sg)
            dgs.append(do * y2 * _silu_grad(gg, sg))
            dss.append(jnp.sum(dy2 * y1, axis=0, keepdims=True))
            dy1 = dy2 * scale
            dbs.append(jnp.sum(dy1, axis=0, keepdims=True))
            dy1b = dy1.astype(BF16)
            dw_ref[g] += _dot_tn(d, dy1b)
            dd = _dot_nt(dy1b, w_ref[g])
            dpool = dd / cnt
            dext = jnp.concatenate([dpool, carry[:, cols]], axis=0)
            dps.append(_window_sum(dext, g + 1, False)[:tm] - dd)
            carry[:, cols] = dpool[:POOL_HALO]
        dpg_ref[...] = jnp.concatenate(dps + dgs, axis=1).astype(BF16)
        dvec_ref[0:1, :] += jnp.concatenate(dbs, axis=1)
        dvec_ref[1:2, :] += jnp.concatenate(dss, axis=1)

    vec = pl.BlockSpec((1, BRANCH_WIDTH), lambda i: (0, 0))
    wspec = pl.BlockSpec((4, POOL_GROUP_DIM, POOL_GROUP_DIM), lambda i: (0, 0, 0))
    return pl.pallas_call(
        body, name=name, grid=(nt,),
        in_specs=[pl.BlockSpec((tm, 2 * BRANCH_WIDTH), lambda i: (nt - 1 - i, 0)),
                  pl.BlockSpec((POOL_HALO, 2 * BRANCH_WIDTH), lambda i: (jnp.maximum((nt - 1 - i) * halo_per_tile - 1, 0), 0)),
                  pl.BlockSpec((tm, BRANCH_WIDTH), lambda i: (nt - 1 - i, 0)), wspec, vec, vec],
        out_specs=[pl.BlockSpec((tm, 2 * BRANCH_WIDTH), lambda i: (nt - 1 - i, 0)), wspec,
                   pl.BlockSpec((8, BRANCH_WIDTH), lambda i: (0, 0))],
        out_shape=[jax.ShapeDtypeStruct((t, 2 * BRANCH_WIDTH), BF16),
                   jax.ShapeDtypeStruct((4, POOL_GROUP_DIM, POOL_GROUP_DIM), F32),
                   jax.ShapeDtypeStruct((8, BRANCH_WIDTH), F32)],
        scratch_shapes=[pltpu.VMEM((POOL_HALO, BRANCH_WIDTH), F32)],
        compiler_params=_params(1),
    )(pg, pg, d_out, pool_w, pool_b, pool_scale)


CONV_TILE_ROWS = 64


def _for_conv_tiles(tm, fn):
    def step(it, carry):
        rows = pl.ds(pl.multiple_of(it * CONV_TILE_ROWS, CONV_TILE_ROWS), CONV_TILE_ROWS)
        for c in range(0, BRANCH_WIDTH, LANES):
            fn(rows, slice(c, c + LANES))
        return carry

    lax.fori_loop(0, tm // CONV_TILE_ROWS, step, 0)


def _sublane_shifts(shifted_ref, x, direction):
    rows = x.shape[0]
    shifted_ref[0] = x
    for b in range(1, 8):
        shifted_ref[b] = pltpu.roll(x, b if direction > 0 else rows - b, 0)


CONV_REACH = 8 * ((CONV_KERNEL - 1) // 8)


def _tap_tiles(shifted_ref, base, rows, cols, direction):
    for b in range(8):
        lo = pl.multiple_of(base + rows.start - (CONV_REACH if direction > 0 else 0), 8)
        window = shifted_ref[b, pl.ds(lo, CONV_TILE_ROWS + CONV_REACH), cols]
        for a in range((CONV_KERNEL - 1 - b) // 8 + 1):
            off = CONV_REACH - 8 * a if direction > 0 else 8 * a
            yield 8 * a + b, window[off:off + CONV_TILE_ROWS]


def _tap_sum(shifted_ref, w_ref, base, rows, cols, direction):
    acc = None
    for j, tile in _tap_tiles(shifted_ref, base, rows, cols, direction):
        k = CONV_KERNEL - 1 - j
        term = w_ref[k:k + 1, cols] * tile
        acc = term if acc is None else acc + term
    return acc


def _causal_conv(ext8_ref, w_ref, cv_ref, tm):
    def tile(rows, cols):
        cv_ref[rows, cols] = _tap_sum(ext8_ref, w_ref, CONV_HALO, rows, cols, 1)

    _for_conv_tiles(tm, tile)


def _conv_fwd(c2gc, conv_w, conv_b, ln_g, ln_b, *, name):
    t = c2gc.shape[0]
    tm = _tile(t, ROW_TILE)

    def body(c_ref, w_ref, cb_ref, g_ref, b_ref, o_ref, halo, ext8_ref, cv_ref):
        @pl.when(pl.program_id(0) == 0)
        def _():
            halo[...] = jnp.zeros_like(halo)

        u = c_ref[:, :BRANCH_WIDTH] * _sigmoid(c_ref[:, BRANCH_WIDTH:2 * BRANCH_WIDTH])
        gate = c_ref[:, 2 * BRANCH_WIDTH:]
        _sublane_shifts(ext8_ref, jnp.concatenate([halo[...], u], axis=0), 1)
        halo[...] = u[tm - CONV_HALO:, :]
        _causal_conv(ext8_ref, w_ref, cv_ref, tm)
        cv = cv_ref[...] + cb_ref[...]
        mu = jnp.mean(cv, axis=-1, keepdims=True)
        xc = cv - mu
        var = jnp.mean(xc * xc, axis=-1, keepdims=True)
        ln = xc * lax.rsqrt(var + LN_EPS) * g_ref[...] + b_ref[...]
        o_ref[...] = (ln * _sigmoid(ln) * (gate * _sigmoid(gate))).astype(BF16)

    vec = pl.BlockSpec((1, BRANCH_WIDTH), lambda i: (0, 0))
    return pl.pallas_call(
        body, name=name, grid=(t // tm,),
        in_specs=[pl.BlockSpec((tm, 3 * BRANCH_WIDTH), lambda i: (i, 0)),
                  pl.BlockSpec((CONV_HALO, BRANCH_WIDTH), lambda i: (0, 0)), vec, vec, vec],
        out_specs=pl.BlockSpec((tm, BRANCH_WIDTH), lambda i: (i, 0)),
        out_shape=jax.ShapeDtypeStruct((t, BRANCH_WIDTH), BF16),
        scratch_shapes=[pltpu.VMEM((CONV_HALO, BRANCH_WIDTH), F32), pltpu.VMEM((8, tm + CONV_HALO, BRANCH_WIDTH), F32),
                        pltpu.VMEM((tm, BRANCH_WIDTH), F32)],
        compiler_params=_params(1),
    )(c2gc, conv_w, conv_b, ln_g, ln_b)


def _conv_bwd(c2gc, d_out, conv_w, conv_b, ln_g, ln_b, *, name):
    t = c2gc.shape[0]
    tm = _tile(t, ROW_TILE)
    nt = t // tm
    halo_per_tile = tm // CONV_HALO

    def body(c_ref, halo_ref, do_ref, w_ref, cb_ref, g_ref, b_ref, dc_ref, dw_ref, dvec_ref,
             carry, ext8_ref, cv_ref, dext8_ref, du_ref, dw_acc):
        i = pl.program_id(0)
        ri = nt - 1 - i

        @pl.when(i == 0)
        def _():
            carry[...] = jnp.zeros_like(carry)
            dw_acc[...] = jnp.zeros_like(dw_acc)
            dvec_ref[...] = jnp.zeros_like(dvec_ref)

        a = c_ref[:, :BRANCH_WIDTH]
        sb = _sigmoid(c_ref[:, BRANCH_WIDTH:2 * BRANCH_WIDTH])
        gate = c_ref[:, 2 * BRANCH_WIDTH:]
        hu = halo_ref[:, :BRANCH_WIDTH] * _sigmoid(halo_ref[:, BRANCH_WIDTH:2 * BRANCH_WIDTH])
        _sublane_shifts(ext8_ref, jnp.concatenate([jnp.where(ri > 0, hu, 0.0), a * sb], axis=0), 1)
        _causal_conv(ext8_ref, w_ref, cv_ref, tm)
        cv = cv_ref[...] + cb_ref[...]
        mu = jnp.mean(cv, axis=-1, keepdims=True)
        xc = cv - mu
        rs = lax.rsqrt(jnp.mean(xc * xc, axis=-1, keepdims=True) + LN_EPS)
        n = xc * rs
        ln = n * g_ref[...] + b_ref[...]
        sl = _sigmoid(ln)
        sgate = _sigmoid(gate)
        do = do_ref[...]
        dgate = do * (ln * sl) * _silu_grad(gate, sgate)
        dln = do * (gate * sgate) * _silu_grad(ln, sl)
        dn = dln * g_ref[...]
        dcv = rs * (dn - jnp.mean(dn, axis=-1, keepdims=True) - n * jnp.mean(dn * n, axis=-1, keepdims=True))
        dvec_ref[0:1, :] += jnp.sum(dcv, axis=0, keepdims=True)
        dvec_ref[1:2, :] += jnp.sum(dln * n, axis=0, keepdims=True)
        dvec_ref[2:3, :] += jnp.sum(dln, axis=0, keepdims=True)
        _sublane_shifts(dext8_ref, jnp.concatenate([dcv, carry[...]], axis=0), -1)
        carry[...] = dcv[:CONV_HALO]
        def tile(rows, cols):
            du_ref[rows, cols] = _tap_sum(dext8_ref, w_ref, 0, rows, cols, -1)
            d_tile = dext8_ref[0, rows, cols]
            for j, u_tile in _tap_tiles(ext8_ref, CONV_HALO, rows, cols, 1):
                prod = d_tile * u_tile
                part = prod[0:8]
                for q in range(8, CONV_TILE_ROWS, 8):
                    part = part + prod[q:q + 8]
                dw_acc[CONV_KERNEL - 1 - j, :, cols] += part

        _for_conv_tiles(tm, tile)
        du = du_ref[...]
        dc_ref[...] = jnp.concatenate([du * sb, du * a * sb * (1.0 - sb), dgate], axis=1).astype(BF16)

        @pl.when(i == nt - 1)
        def _():
            dw_ref[...] = jnp.sum(dw_acc[...], axis=1)

    vec = pl.BlockSpec((1, BRANCH_WIDTH), lambda i: (0, 0))
    wspec = pl.BlockSpec((CONV_HALO, BRANCH_WIDTH), lambda i: (0, 0))
    return pl.pallas_call(
        body, name=name, grid=(nt,),
        in_specs=[pl.BlockSpec((tm, 3 * BRANCH_WIDTH), lambda i: (nt - 1 - i, 0)),
                  pl.BlockSpec((CONV_HALO, 3 * BRANCH_WIDTH), lambda i: (jnp.maximum((nt - 1 - i) * halo_per_tile - 1, 0), 0)),
                  pl.BlockSpec((tm, BRANCH_WIDTH), lambda i: (nt - 1 - i, 0)), wspec, vec, vec, vec],
        out_specs=[pl.BlockSpec((tm, 3 * BRANCH_WIDTH), lambda i: (nt - 1 - i, 0)), wspec,
                   pl.BlockSpec((8, BRANCH_WIDTH), lambda i: (0, 0))],
        out_shape=[jax.ShapeDtypeStruct((t, 3 * BRANCH_WIDTH), BF16),
                   jax.ShapeDtypeStruct((CONV_HALO, BRANCH_WIDTH), F32),
                   jax.ShapeDtypeStruct((8, BRANCH_WIDTH), F32)],
        scratch_shapes=[pltpu.VMEM((CONV_HALO, BRANCH_WIDTH), F32),
                        pltpu.VMEM((8, tm + CONV_HALO, BRANCH_WIDTH), F32), pltpu.VMEM((tm, BRANCH_WIDTH), F32),
                        pltpu.VMEM((8, tm + CONV_HALO, BRANCH_WIDTH), F32), pltpu.VMEM((tm, BRANCH_WIDTH), F32),
                        pltpu.VMEM((CONV_HALO, 8, BRANCH_WIDTH), F32)],
        compiler_params=_params(1),
    )(c2gc, c2gc, d_out, conv_w, conv_b, ln_g, ln_b)


def _merge_fwd(gmga, ya, yb, yc, *, name):
    t = ya.shape[0]
    tm = _tile(t, ROW_TILE)
    row = pl.BlockSpec((tm, D_MODEL), lambda i: (i, 0))

    def body(g0, g1, g2, a_ref, b_ref, c_ref, o_ref):
        m = _sigmoid(g0[...]) * a_ref[...] + _sigmoid(g1[...]) * b_ref[...] + _sigmoid(g2[...]) * c_ref[...]
        o_ref[...] = m.astype(BF16)

    gspecs = [pl.BlockSpec((tm, D_MODEL), functools.partial(lambda i, b: (i, b), b=b)) for b in range(3)]
    return pl.pallas_call(
        body, name=name, grid=(t // tm,),
        in_specs=gspecs + [row, row, row], out_specs=row,
        out_shape=jax.ShapeDtypeStruct((t, D_MODEL), BF16),
        compiler_params=_params(1),
    )(gmga, gmga, gmga, ya, yb, yc)


def _merge_bwd(dm, gmga, ya, yb, yc, *, name):
    t = ya.shape[0]
    tm = _tile(t, ROW_TILE)
    row = pl.BlockSpec((tm, D_MODEL), lambda i: (i, 0))
    wide = pl.BlockSpec((tm, 3 * D_MODEL), lambda i: (i, 0))

    def body(dm_ref, g0, g1, g2, a_ref, b_ref, c_ref, da_ref, db_ref, dc_ref, dg_ref):
        dmv = dm_ref[...]
        for k, (g_ref, y_ref, dy_ref) in enumerate(((g0, a_ref, da_ref), (g1, b_ref, db_ref), (g2, c_ref, dc_ref))):
            s = _sigmoid(g_ref[...])
            dy_ref[...] = (dmv * s).astype(BF16)
            dg_ref[:, k * D_MODEL:(k + 1) * D_MODEL] = (dmv * y_ref[...] * s * (1.0 - s)).astype(BF16)

    gspecs = [pl.BlockSpec((tm, D_MODEL), functools.partial(lambda i, b: (i, b), b=b)) for b in range(3)]
    return pl.pallas_call(
        body, name=name, grid=(t // tm,),
        in_specs=[row] + gspecs + [row, row, row], out_specs=[row, row, row, wide],
        out_shape=[jax.ShapeDtypeStruct((t, D_MODEL), BF16)] * 3 + [jax.ShapeDtypeStruct((t, 3 * D_MODEL), BF16)],
        compiler_params=_params(1),
    )(dm, gmga, gmga, gmga, ya, yb, yc)


GA_BLOCK = 3 * D_MODEL // HEAD_PAIR


def _split_heads(x, lane_is_first):
    zero = jnp.zeros_like(x)
    return jnp.concatenate([jnp.where(lane_is_first, x, zero), jnp.where(lane_is_first, zero, x)], axis=0)


def _side_by_side(x, rows):
    return jnp.concatenate([x[:rows], x[rows:]], axis=1)


def _split_bf16(x):
    hi = x.astype(BF16)
    return hi, (x - hi.astype(F32)).astype(BF16)


def _scores(qcat, kblk, mask):
    z = _dot_nt(qcat, kblk)
    e = jnp.exp(-jnp.abs(z))
    sp = jnp.maximum(z, 0.0) + jnp.log(1.0 + e)
    l1m = -sp
    if mask is not None:
        l1m = jnp.where(mask, l1m, 0.0)
    inv = 1.0 / (1.0 + e)
    pos = z >= 0.0
    return z - sp, l1m, jnp.where(pos, 1.0, e) * inv, jnp.where(pos, e, 1.0) * inv


def _attn_consts(blk):
    lane_is_first = lax.broadcasted_iota(jnp.int32, (1, HEAD_PAIR), 1) < HEAD_DIM
    r = lax.broadcasted_iota(jnp.int32, (blk, blk), 0)
    c = lax.broadcasted_iota(jnp.int32, (blk, blk), 1)
    after = (r > c).astype(BF16)
    from_here = (r >= c).astype(BF16)
    qrow = lax.broadcasted_iota(jnp.int32, (2 * blk, blk), 0)
    qrow = jnp.where(qrow >= blk, qrow - blk, qrow)
    causal = lax.broadcasted_iota(jnp.int32, (2 * blk, blk), 1) < qrow
    return lane_is_first, after, from_here, causal


def _while_mass_left(qi, carry, block):
    def alive(c):
        return jnp.max(c[0]) > LOG_F32_ZERO

    def cond(state):
        return jnp.logical_and(state[0] < qi, state[1])

    def step(state):
        new = block(qi - 1 - state[0], state[2])
        return state[0] + 1, alive(new), new

    return lax.while_loop(cond, step, (jnp.int32(0), alive(carry), carry))[2]


def _ride(rider, n_in, n_out, refs, grid):
    if rider is None:
        return refs[:n_in], refs[n_in:n_in + n_out], refs[n_in + n_out:], lambda: None
    n = rider.n
    ins, srcs = refs[:n_in], refs[n_in:n_in + n]
    outs, dsts = refs[n_in + n:n_in + n + n_out], refs[n_in + n + n_out:n_in + 2 * n + n_out]
    rest = refs[n_in + 2 * n + n_out:]
    scratch, sems = rest[:len(rest) - 3], rest[len(rest) - 3:]
    step, n_steps = 0, 1
    for axis, size in enumerate(grid):
        step, n_steps = step * size + pl.program_id(axis), n_steps * size

    @pl.when(step == 0)
    def _():
        rider.start(srcs, dsts, sems)

    def finish():
        @pl.when(step == n_steps - 1)
        def _():
            rider.finish(srcs, dsts, sems)

    return ins, outs, scratch, finish


def _attn_fwd(qkv, gmga, *, name, rider=None):
    t = qkv.shape[0]
    blk, step_rows = _tile(t, ATTN_BLOCK), _tile(t, ATTN_BLOCK * ATTN_BLOCKS_PER_STEP)
    nq = t // step_rows

    def body(*refs):
        (q_ref, k_ref, v_ref, ga_ref), (o_ref, cv_ref), _, finish_rider = _ride(rider, 4, 2, refs, (N_HEAD_PAIRS, nq))
        lane_is_first, after, _, causal = _attn_consts(blk)

        for sub in range(step_rows // blk):
            mine = slice(sub * blk, (sub + 1) * blk)
            qi = pl.program_id(1) * (step_rows // blk) + sub
            qcat = _split_heads(q_ref[mine, :], lane_is_first)

            def block(kb, carry, mask, qcat=qcat):
                run, acc = carry
                rows = pl.ds(pl.multiple_of(kb * blk, blk), blk)
                lb, l1m, _, _ = _scores(qcat, k_ref[rows, :], mask)
                hi, lo = _split_bf16(l1m)
                w = jnp.exp(lb + (_dot(hi, after) + _dot(lo, after) + run))
                if mask is not None:
                    w = jnp.where(mask, w, 0.0)
                vcat = _split_heads(v_ref[rows, :], lane_is_first)
                acc = acc + _dot(_side_by_side(w.astype(BF16), blk), vcat)
                return run + jnp.sum(l1m, axis=-1, keepdims=True), acc

            carry = block(qi, (jnp.zeros((2 * blk, 1), F32), jnp.zeros((blk, HEAD_PAIR), F32)), causal)
            _, o = _while_mass_left(qi, carry, lambda kb, c, block=block: block(kb, c, None))
            o_ref[mine, :] = o
            ga = ga_ref[mine, :]
            cv_ref[mine, :] = (o * (ga * _sigmoid(ga))).astype(BF16)
        finish_rider()

    qspec = pl.BlockSpec((step_rows, HEAD_PAIR), lambda p, i: (i, p))
    extra = rider or _NO_RIDER
    return pl.pallas_call(
        body, name=name, grid=(N_HEAD_PAIRS, nq),
        in_specs=[qspec,
                  pl.BlockSpec((t, HEAD_PAIR), lambda p, i: (0, N_HEAD_PAIRS + p)),
                  pl.BlockSpec((t, HEAD_PAIR), lambda p, i: (0, 2 * N_HEAD_PAIRS + p)),
                  pl.BlockSpec((step_rows, HEAD_PAIR), lambda p, i: (i, GA_BLOCK + p))] + extra.specs,
        out_specs=[qspec, qspec] + extra.specs,
        out_shape=[jax.ShapeDtypeStruct((t, BRANCH_WIDTH), F32), jax.ShapeDtypeStruct((t, BRANCH_WIDTH), BF16)] + extra.out_shape,
        scratch_shapes=extra.scratch_shapes,
        compiler_params=_params(2),
    )(qkv, qkv, qkv, gmga, *extra.srcs)


def _attn_bwd(qkv, o, gmga, dcv, *, name, rider=None):
    t = qkv.shape[0]
    blk, step_rows = _tile(t, ATTN_BLOCK), _tile(t, ATTN_BLOCK * ATTN_BLOCKS_PER_STEP)
    nq = t // step_rows

    def body(*refs):
        ins, outs, (dk_acc, dv_acc), finish_rider = _ride(rider, 6, 4, refs, (N_HEAD_PAIRS, nq))
        q_ref, k_ref, v_ref, o_ref, ga_ref, dcv_ref = ins
        dq_ref, dk_ref, dv_ref, dga_ref = outs
        lane_is_first, after, from_here, causal = _attn_consts(blk)

        @pl.when(pl.program_id(1) == 0)
        def _():
            dk_acc[...] = jnp.zeros_like(dk_acc)
            dv_acc[...] = jnp.zeros_like(dv_acc)

        for sub in range(step_rows // blk):
            mine = slice(sub * blk, (sub + 1) * blk)
            qi = pl.program_id(1) * (step_rows // blk) + sub
            ga, ov, dcvv = ga_ref[mine, :], o_ref[mine, :], dcv_ref[mine, :]
            sg = _sigmoid(ga)
            dob = (dcvv * (ga * sg)).astype(BF16)
            dga_ref[mine, :] = (dcvv * ov * _silu_grad(ga, sg)).astype(BF16)
            gt = dob.astype(F32) * ov
            g_total = jnp.concatenate(
                [jnp.sum(jnp.where(lane_is_first, gt, 0.0), axis=-1, keepdims=True),
                 jnp.sum(jnp.where(lane_is_first, 0.0, gt), axis=-1, keepdims=True)], axis=0)
            qcat = _split_heads(q_ref[mine, :], lane_is_first)
            docat = _split_heads(dob, lane_is_first)

            def block(kb, carry, mask, g_total=g_total, qcat=qcat, docat=docat):
                run, g_run, dq = carry
                rows = pl.ds(pl.multiple_of(kb * blk, blk), blk)
                kblk = k_ref[rows, :]
                lb, l1m, sig, one_m_sig = _scores(qcat, kblk, mask)
                hi, lo = _split_bf16(l1m)
                w = jnp.exp(lb + (_dot(hi, after) + _dot(lo, after) + run))
                if mask is not None:
                    w = jnp.where(mask, w, 0.0)
                wb = w.astype(BF16)
                g = _dot_nt(docat, v_ref[rows, :]) * wb.astype(F32)
                ghi, glo = _split_bf16(g)
                g_before = g_total - g_run - (_dot(ghi, from_here) + _dot(glo, from_here))
                dz = g * one_m_sig - g_before * sig
                if mask is not None:
                    dz = jnp.where(mask, dz, 0.0)
                dzb = dz.astype(BF16)
                dq = dq + _dot(_side_by_side(dzb, blk), _split_heads(kblk, lane_is_first))
                dk_acc[rows, :] += _dot_tn(dzb, qcat)
                dv_acc[rows, :] += _dot_tn(wb, docat)
                return (run + jnp.sum(l1m, axis=-1, keepdims=True), g_run + jnp.sum(g, axis=-1, keepdims=True), dq)

            zero = jnp.zeros((2 * blk, 1), F32)
            carry = block(qi, (zero, zero, jnp.zeros((blk, HEAD_PAIR), F32)), causal)
            _, _, dq = _while_mass_left(qi, carry, lambda kb, c, block=block: block(kb, c, None))
            dq_ref[mine, :] = (dq * ATTN_SCALE).astype(BF16)

        @pl.when(pl.program_id(1) == nq - 1)
        def _():
            dk_ref[...] = dk_acc[...].astype(BF16)
            dv_ref[...] = dv_acc[...].astype(BF16)

        finish_rider()

    qspec = pl.BlockSpec((step_rows, HEAD_PAIR), lambda p, i: (i, p))
    whole = pl.BlockSpec((t, HEAD_PAIR), lambda p, i: (0, p))
    out = jax.ShapeDtypeStruct((t, BRANCH_WIDTH), BF16)
    extra = rider or _NO_RIDER
    return pl.pallas_call(
        body, name=name, grid=(N_HEAD_PAIRS, nq),
        in_specs=[qspec,
                  pl.BlockSpec((t, HEAD_PAIR), lambda p, i: (0, N_HEAD_PAIRS + p)),
                  pl.BlockSpec((t, HEAD_PAIR), lambda p, i: (0, 2 * N_HEAD_PAIRS + p)),
                  qspec,
                  pl.BlockSpec((step_rows, HEAD_PAIR), lambda p, i: (i, GA_BLOCK + p)),
                  qspec] + extra.specs,
        out_specs=[qspec, whole, whole, qspec] + extra.specs,
        out_shape=[out, out, out, out] + extra.out_shape,
        scratch_shapes=[pltpu.VMEM((t, HEAD_PAIR), F32), pltpu.VMEM((t, HEAD_PAIR), F32)] + extra.scratch_shapes,
        compiler_params=_params(2),
    )(qkv, qkv, qkv, o, gmga, dcv, *extra.srcs)


def _mesh_position():
    x, y, c = lax.axis_index("x"), lax.axis_index("y"), lax.axis_index("c")
    return x, y, c, 4 * x + 2 * y + c


def _flipped(x, y, c, k):
    return (1 - x if k & 4 else x, 1 - y if k & 2 else y, 1 - c if k & 1 else c)


def _all_to_all(srcs, *, name, same_block):
    ex = _Exchange(srcs, same_block)

    def body(*refs):
        ex.start(refs[:ex.n], refs[ex.n:2 * ex.n], refs[2 * ex.n:])
        ex.finish(refs[:ex.n], refs[ex.n:2 * ex.n], refs[2 * ex.n:])

    return pl.pallas_call(
        body, name=name, in_specs=ex.specs, out_specs=ex.specs, out_shape=ex.out_shape,
        scratch_shapes=ex.scratch_shapes,
    )(*srcs)


def _gather_via_sibling(srcs, *, name):
    n = len(srcs)

    def body(*refs):
        src_refs, dst_refs = refs[:n], refs[n:2 * n]
        send_sems, recv_sems, local_sems = refs[2 * n:]
        x, y, c, me = _mesh_position()
        sibling = (x, y, 1 - c)
        chips = [(1 - x, y), (x, 1 - y), (1 - x, 1 - y)]

        def slot(px, py, pc):
            return 4 * px + 2 * py + pc

        def copy(i, k, block, to, from_src=False):
            return pltpu.make_async_remote_copy(
                src_ref=src_refs[i] if from_src else dst_refs[i].at[slot(*block)], dst_ref=dst_refs[i].at[slot(*block)],
                send_sem=send_sems.at[k, i], recv_sem=recv_sems.at[k, i], device_id=to, device_id_type=MESH)

        mine = [pltpu.make_async_copy(src_refs[i], dst_refs[i].at[me], local_sems.at[i]) for i in range(n)]
        first = [copy(i, 0, (x, y, c), sibling, True) for i in range(n)]
        first += [copy(i, 1 + j, (x, y, c), (*chip, c), True) for j, chip in enumerate(chips) for i in range(n)]
        for cp in mine + first:
            cp.start()
        passed = []
        for j, chip in enumerate(chips):
            for i in range(n):
                copy(i, 1 + j, (*chip, c), (x, y, c)).wait_recv()
                passed.append(copy(i, 4 + j, (*chip, c), sibling))
                passed[-1].start()
        for i in range(n):
            copy(i, 0, sibling, (x, y, c)).wait_recv()
            for j, chip in enumerate(chips):
                copy(i, 4 + j, (*chip, 1 - c), (x, y, c)).wait_recv()
        for cp in first + passed:
            cp.wait_send()
        for cp in mine:
            cp.wait()

    spec = [pl.BlockSpec(memory_space=pl.ANY)] * n
    return pl.pallas_call(
        body, name=name, in_specs=spec, out_specs=spec,
        out_shape=[jax.ShapeDtypeStruct((N_DEV,) + tuple(s.shape), s.dtype) for s in srcs],
        scratch_shapes=[pltpu.SemaphoreType.DMA((N_DEV - 1, n)), pltpu.SemaphoreType.DMA((N_DEV - 1, n)),
                        pltpu.SemaphoreType.DMA((n,))],
    )(*srcs)


class _Exchange:
    def __init__(self, srcs, same_block):
        self.srcs, self.same_block, self.n = list(srcs), same_block, len(srcs)
        self.specs = [pl.BlockSpec(memory_space=pl.ANY)] * self.n
        self.out_shape = [jax.ShapeDtypeStruct((N_DEV,) + tuple(s.shape if same_block else s.shape[1:]), s.dtype)
                          for s in self.srcs]
        self.scratch_shapes = [pltpu.SemaphoreType.DMA((N_DEV - 1, self.n)), pltpu.SemaphoreType.DMA((N_DEV - 1, self.n)),
                               pltpu.SemaphoreType.DMA((self.n,))] if self.n else []

    def _copies(self, src_refs, dst_refs, sems, with_arrivals):
        send_sems, recv_sems, local_sems = sems
        x, y, c, me = _mesh_position()

        def outgoing(i, j):
            return src_refs[i] if self.same_block else src_refs[i].at[j]

        def remote(i, k, slot):
            return pltpu.make_async_remote_copy(
                src_ref=outgoing(i, jnp.bitwise_xor(me, k)), dst_ref=dst_refs[i].at[slot],
                send_sem=send_sems.at[k - 1, i], recv_sem=recv_sems.at[k - 1, i],
                device_id=_flipped(x, y, c, k), device_id_type=MESH)

        pairs = [(i, k) for k in range(1, N_DEV) for i in range(self.n)]
        mine = [pltpu.make_async_copy(outgoing(i, me), dst_refs[i].at[me], local_sems.at[i]) for i in range(self.n)]
        sent = [remote(i, k, me) for i, k in pairs]
        arrivals = [remote(i, k, jnp.bitwise_xor(me, k)) for i, k in pairs] if with_arrivals else []
        return mine, sent, arrivals

    def start(self, src_refs, dst_refs, sems):
        mine, sent, _ = self._copies(src_refs, dst_refs, sems, False)
        for cp in mine + sent:
            cp.start()

    def finish(self, src_refs, dst_refs, sems):
        mine, sent, arrivals = self._copies(src_refs, dst_refs, sems, True)
        for cp in arrivals:
            cp.wait_recv()
        for cp in sent:
            cp.wait_send()
        for cp in mine:
            cp.wait()


_NO_RIDER = _Exchange([], True)


def _adamw(parts, w, m, v, *, name):
    layers, rows, cols = w.shape
    assert len(parts) == layers
    tr = rows
    while tr * cols > ADAM_TILE_ELEMS and tr % 32 == 0:
        tr //= 2
    row = pl.BlockSpec((1, tr, cols), lambda l, i: (l, i, 0))

    def body(*refs):
        p_refs = refs[:layers]
        w_ref, m_ref, v_ref, g_ref, d_ref, nm_ref, nv_ref = refs[layers:]
        layer = pl.program_id(0)
        g = None
        for k in range(N_DEV):
            part = p_refs[0][k]
            for l in range(1, layers):
                part = jnp.where(layer == l, p_refs[l][k], part)
            g = part.astype(F32) if g is None else g + part.astype(F32)
        m2 = ADAM_B1 * m_ref[0] + (1.0 - ADAM_B1) * g
        v2 = ADAM_B2 * v_ref[0] + (1.0 - ADAM_B2) * (g * g)
        m_hat = m2 / (1.0 - ADAM_B1 ** ADAM_STEP)
        v_hat = v2 / (1.0 - ADAM_B2 ** ADAM_STEP)
        g_ref[0] = g
        d_ref[0] = -ADAM_LR * (m_hat / (jnp.sqrt(v_hat) + ADAM_EPS) + ADAM_WD * w_ref[0])
        nm_ref[0] = m2
        nv_ref[0] = v2

    out = jax.ShapeDtypeStruct((layers, rows, cols), F32)
    return pl.pallas_call(
        body, name=name, grid=(layers, rows // tr),
        in_specs=[pl.BlockSpec((N_DEV, tr, cols), lambda l, i: (0, i, 0))] * layers + [row, row, row],
        out_specs=[row, row, row, row], out_shape=[out, out, out, out],
        compiler_params=_params(2),
    )(*parts, w, m, v)


MATMUL_WEIGHTS = ("w_in", "w_pool_out", "w_conv_out", "w_attn_out", "w_o")
SMALL = ("conv_w", "norm_pre", "pool_w", "pool_b", "pool_scale", "conv_b", "conv_ln_g", "conv_ln_b", "norm_post")
WEIGHT_ORDER = ("norm_pre", "w_in", "pool_w", "pool_b", "pool_scale", "w_pool_out", "conv_w", "conv_b",
                "conv_ln_g", "conv_ln_b", "w_conv_out", "w_attn_out", "w_o", "norm_post")


def _shard_axis(name):
    return -2 if name == "w_o" else -1


def _pack_rows(flat_parts, row_multiple):
    flat = jnp.concatenate(flat_parts, axis=-1)
    n = flat.shape[-1]
    chunk = row_multiple * LANES
    total = -(-n // chunk) * chunk
    pad = [(0, 0)] * (flat.ndim - 1) + [(0, total - n)]
    return jnp.pad(flat, pad).reshape(flat.shape[:-1] + (total // LANES, LANES))


def _unpack(buf, shapes):
    flat = buf.reshape(-1)
    out, at = {}, 0
    for name, shape in shapes:
        n = 1
        for s in shape:
            n *= s
        out[name] = flat[at:at + n].reshape(shape)
        at += n
    return out


def _to_dest_major(name, full):
    axis = full.ndim + _shard_axis(name)
    n = full.shape[axis] // N_DEV
    return jnp.stack([lax.slice_in_dim(full, d * n, (d + 1) * n, axis=axis) for d in range(N_DEV)])


def _from_source_major(name, gathered):
    return jnp.concatenate([gathered[d] for d in range(N_DEV)], axis=_shard_axis(name))


def kernel(x, norm_pre, w_in, pool_w, pool_b, pool_scale, w_pool_out, conv_w, conv_b, conv_ln_g, conv_ln_b, w_conv_out, w_attn_out, w_o, norm_post, loss_target, m_norm_pre, m_w_in, m_pool_w, m_pool_b, m_pool_scale, m_w_pool_out, m_conv_w, m_conv_b, m_conv_ln_g, m_conv_ln_b, m_w_conv_out, m_w_attn_out, m_w_o, m_norm_post, v_norm_pre, v_w_in, v_pool_w, v_pool_b, v_pool_scale, v_w_pool_out, v_conv_w, v_conv_b, v_conv_ln_g, v_conv_ln_b, v_w_conv_out, v_w_attn_out, v_w_o, v_norm_post):
    weights = dict(norm_pre=norm_pre, w_in=w_in, pool_w=pool_w, pool_b=pool_b, pool_scale=pool_scale,
                   w_pool_out=w_pool_out, conv_w=conv_w, conv_b=conv_b, conv_ln_g=conv_ln_g, conv_ln_b=conv_ln_b,
                   w_conv_out=w_conv_out, w_attn_out=w_attn_out, w_o=w_o, norm_post=norm_post)
    mom1 = dict(norm_pre=m_norm_pre, w_in=m_w_in, pool_w=m_pool_w, pool_b=m_pool_b, pool_scale=m_pool_scale,
                w_pool_out=m_w_pool_out, conv_w=m_conv_w, conv_b=m_conv_b, conv_ln_g=m_conv_ln_g, conv_ln_b=m_conv_ln_b,
                w_conv_out=m_w_conv_out, w_attn_out=m_w_attn_out, w_o=m_w_o, norm_post=m_norm_post)
    mom2 = dict(norm_pre=v_norm_pre, w_in=v_w_in, pool_w=v_pool_w, pool_b=v_pool_b, pool_scale=v_pool_scale,
                w_pool_out=v_w_pool_out, conv_w=v_conv_w, conv_b=v_conv_b, conv_ln_g=v_conv_ln_g, conv_ln_b=v_conv_ln_b,
                w_conv_out=v_w_conv_out, w_attn_out=v_w_attn_out, w_o=v_w_o, norm_post=v_norm_post)
    xs = x[0]
    target = loss_target[0]

    conv_rows = jnp.pad(conv_w, ((0, 0), (0, CONV_HALO - CONV_KERNEL), (0, 0)))
    shards = [[weights[n][l].astype(BF16) for n in MATMUL_WEIGHTS] for l in range(DEPTH)]
    gathered = _gather_via_sibling([shards[0][0], conv_rows], name="gather_weights")
    full = [{"w_in": _from_source_major("w_in", gathered[0])}, None]
    conv_full = _from_source_major("conv_w", gathered[1])

    def in_sections(w):
        return dict(pg=w[:, 0:1024], c2gc=w[:, 1024:2560], q=w[:, 2560:3072], k=w[:, 3072:3584], v=w[:, 3584:4096],
                    gmga=jnp.concatenate([w[:, 4608:7680], w[:, 4096:4608]], axis=1))

    saved = []
    cur = xs
    for l in range(DEPTH):
        sec = in_sections(full[l]["w_in"])
        w_qkv = jnp.concatenate([sec["q"] * ATTN_SCALE, sec["k"], sec["v"]], axis=1)
        pw = pool_w[l].astype(BF16)
        pb, ps = pool_b[l].reshape(1, -1), pool_scale[l].reshape(1, -1)
        cb, lg, lb = conv_b[l].reshape(1, -1), conv_ln_g[l].reshape(1, -1), conv_ln_b[l].reshape(1, -1)
        h = _rms_fwd(cur, norm_pre[l].reshape(1, -1), name=f"rms_pre_fwd_{l}")
        pg = _matmul(h, sec["pg"], mode="nn", name=f"proj_pg_{l}")
        c2gc = _matmul(h, sec["c2gc"], mode="nn", name=f"proj_c2gc_{l}")
        qkv = _matmul(h, w_qkv, mode="nn", name=f"proj_qkv_{l}", out_dtype=BF16)
        if l == 0:
            gmga, *arrived = _matmul(h, sec["gmga"], mode="nn", name=f"proj_gmga_{l}", rider=_Exchange(shards[0][1:], True))
            full[0].update({n: _from_source_major(n, g) for n, g in zip(MATMUL_WEIGHTS[1:], arrived)})
        else:
            gmga = _matmul(h, sec["gmga"], mode="nn", name=f"proj_gmga_{l}")
        a_act = _pool_fwd(pg, pw, pb, ps, name=f"pool_fwd_{l}")
        b_act = _conv_fwd(c2gc, conv_full[l], cb, lg, lb, name=f"conv_fwd_{l}")
        rider = _Exchange(shards[l + 1], True) if l + 1 < DEPTH else None
        o, c_act, *arrived = _attn_fwd(qkv, gmga, name=f"attn_fwd_{l}", rider=rider)
        if rider is not None:
            full[l + 1] = {n: _from_source_major(n, g) for n, g in zip(MATMUL_WEIGHTS, arrived)}
        ya = _matmul(a_act, full[l]["w_pool_out"], mode="nn", name=f"out_pool_{l}")
        yb = _matmul(b_act, full[l]["w_conv_out"], mode="nn", name=f"out_conv_{l}")
        yc = _matmul(c_act, full[l]["w_attn_out"], mode="nn", name=f"out_attn_{l}")
        mix = _merge_fwd(gmga, ya, yb, yc, name=f"merge_fwd_{l}")
        out = _matmul(mix, full[l]["w_o"], mode="nn", name=f"out_proj_{l}")
        nxt = _rms_fwd(out, norm_post[l].reshape(1, -1), name=f"rms_post_fwd_{l}", resid=cur)
        saved.append(dict(x=cur, h=h, pg=pg, c2gc=c2gc, qkv=qkv, gmga=gmga, a=a_act, b=b_act, c=c_act, o=o,
                          ya=ya, yb=yb, yc=yc, mix=mix, out=out, pw=pw, pb=pb, ps=ps, cb=cb, lg=lg, lb=lb))
        cur = nxt

    loss_tile, dx = _loss_head(cur, target, name="loss_head")
    loss = lax.psum(loss_tile[0, 0], ("x", "y", "c"))

    grads = {n: [None] * DEPTH for n in WEIGHT_ORDER}
    parts = {n: [None] * DEPTH for n in MATMUL_WEIGHTS}
    for l in reversed(range(DEPTH)):
        s = saved[l]
        dout, grads["norm_post"][l] = _rms_bwd(s["out"], norm_post[l].reshape(1, -1), dx, name=f"rms_post_bwd_{l}")
        dmix = _matmul(dout, full[l]["w_o"], mode="nt", name=f"d_mix_{l}")
        grads["w_o"][l] = _matmul(s["mix"], dout, mode="tn", name=f"d_w_o_{l}", out_dtype=BF16)
        dya, dyb, dyc, dgm = _merge_bwd(dmix, s["gmga"], s["ya"], s["yb"], s["yc"], name=f"merge_bwd_{l}")
        da = _matmul(dya, full[l]["w_pool_out"], mode="nt", name=f"d_pool_act_{l}")
        grads["w_pool_out"][l] = _matmul(s["a"], dya, mode="tn", name=f"d_w_pool_out_{l}", out_dtype=BF16)
        db = _matmul(dyb, full[l]["w_conv_out"], mode="nt", name=f"d_conv_act_{l}")
        grads["w_conv_out"][l] = _matmul(s["b"], dyb, mode="tn", name=f"d_w_conv_out_{l}", out_dtype=BF16)
        dc = _matmul(dyc, full[l]["w_attn_out"], mode="nt", name=f"d_attn_act_{l}")
        grads["w_attn_out"][l] = _matmul(s["c"], dyc, mode="tn", name=f"d_w_attn_out_{l}", out_dtype=BF16)
        rider = _Exchange([_to_dest_major(n, grads[n][l]) for n in MATMUL_WEIGHTS[1:]], False)
        dq, dk, dv, dga, *arrived = _attn_bwd(s["qkv"], s["o"], s["gmga"], dc, name=f"attn_bwd_{l}", rider=rider)
        for n, p in zip(MATMUL_WEIGHTS[1:], arrived):
            parts[n][l] = p
        dc2gc, dcw, dcvec = _conv_bwd(s["c2gc"], db, conv_full[l], s["cb"], s["lg"], s["lb"], name=f"conv_bwd_{l}")
        dpg, dpw, dpvec = _pool_bwd(s["pg"], da, s["pw"], s["pb"], s["ps"], name=f"pool_bwd_{l}")
        grads["conv_w"][l] = dcw[:CONV_KERNEL]
        grads["conv_b"][l], grads["conv_ln_g"][l], grads["conv_ln_b"][l] = dcvec[0], dcvec[1], dcvec[2]
        grads["pool_w"][l] = dpw
        grads["pool_b"][l] = dpvec[0].reshape(4, POOL_GROUP_DIM)
        grads["pool_scale"][l] = dpvec[1]
        dproj = jnp.concatenate([dpg, dc2gc, dq, dk, dv, dga, dgm], axis=1)
        grads["w_in"][l] = _matmul(s["h"], dproj, mode="tn", name=f"d_w_in_{l}", out_dtype=BF16)
        rider = _Exchange([_to_dest_major("w_in", grads["w_in"][l])], False)
        dh, parts["w_in"][l] = _matmul(dproj, full[l]["w_in"], mode="nt", name=f"d_h_{l}", rider=rider)
        dx, dg_pre = _rms_bwd(s["x"], norm_pre[l].reshape(1, -1), dh, name=f"rms_pre_bwd_{l}", resid=dx)
        grads["norm_pre"][l] = dg_pre.reshape(-1)
        grads["norm_post"][l] = grads["norm_post"][l].reshape(-1)
    small_grads = {n: jnp.stack(grads[n]) for n in SMALL}
    replicated = jnp.concatenate([small_grads[n].reshape(-1) for n in SMALL[1:]])
    small = _pack_rows([_to_dest_major("conv_w", small_grads["conv_w"]).reshape(N_DEV, -1),
                        jnp.broadcast_to(replicated, (N_DEV, replicated.size))], 16).astype(BF16)
    (small_parts,) = _all_to_all([small], name="exchange_grads", same_block=False)

    outs = [dict(), dict(), dict(), dict()]
    for n in MATMUL_WEIGHTS:
        res = _adamw(parts[n], weights[n], mom1[n], mom2[n], name=f"adamw_{n}")
        for o, r in zip(outs, res):
            o[n] = r

    def packed(tree):
        return _pack_rows([tree[n].reshape(-1) for n in SMALL], 16)[None]

    shapes = [(n, weights[n].shape) for n in SMALL]
    res = _adamw([small_parts], packed(weights), packed(mom1), packed(mom2), name="adamw_small")
    for o, r in zip(outs, res):
        o.update(_unpack(r, shapes))
    return (loss, dx[None], *[o[n] for o in outs for n in WEIGHT_ORDER])
```

```python
import functools

import jax
import jax.numpy as jnp
from jax import lax
from jax.experimental import pallas as pl
from jax.experimental.pallas import tpu as pltpu

F32 = jnp.float32
BF16 = jnp.bfloat16

D_MODEL = 1024
DEPTH = 2
POOL_WINDOWS = (2, 4, 8, 16)
POOL_GROUP_DIM = 128
BRANCH_WIDTH = 512
CONV_KERNEL = 31
CONV_HALO = 32
POOL_HALO = 16
HEAD_DIM = 64
HEAD_PAIR = 128
N_HEAD_PAIRS = 4
ATTN_SCALE = 0.125
LOG_F32_ZERO = -104.0
RMS_EPS = 1e-6
LN_EPS = 1e-5
N_DEV = 8
LANES = 128

ADAM_LR = 0.001
ADAM_B1 = 0.9
ADAM_B2 = 0.999
ADAM_EPS = 1e-08
ADAM_WD = 0.01
ADAM_STEP = 10

ROW_TILE = 256
NORM_ROW_TILE = 512
ATTN_BLOCK = 256
ATTN_BLOCKS_PER_STEP = 2
MM_TILE = 1024
MM_K_TILE = 1280
ADAM_TILE_ELEMS = 256 * 1024
VMEM_LIMIT = 48 * 1024 * 1024

MESH = pl.DeviceIdType.MESH


def _params(n_axes):
    return pltpu.CompilerParams(dimension_semantics=("arbitrary",) * n_axes, vmem_limit_bytes=VMEM_LIMIT)


def _tile(n, pref):
    if n <= pref:
        return n
    t = (pref // LANES) * LANES
    while n % t:
        t -= LANES
    return t


def _dot(a, b):
    return jnp.dot(a, b, preferred_element_type=F32)


def _dot_nt(a, b):
    return lax.dot_general(a, b, (((1,), (1,)), ((), ())), preferred_element_type=F32)


def _dot_tn(a, b):
    return lax.dot_general(a, b, (((0,), (0,)), ((), ())), preferred_element_type=F32)


def _sigmoid(x):
    return 1.0 / (1.0 + jnp.exp(-x))


def _silu_grad(x, s):
    return s * (1.0 + x * (1.0 - s))


def _matmul(a, b, *, mode, name, out_dtype=F32, rider=None):
    if mode == "nn":
        (m, k), n = a.shape, b.shape[1]
    elif mode == "nt":
        (m, k), n = a.shape, b.shape[0]
    else:
        (k, m), n = a.shape, b.shape[1]
    tm, tn, tk = _tile(m, MM_TILE), _tile(n, MM_TILE), _tile(k, MM_K_TILE)
    nk = k // tk
    grid = (m // tm, n // tn, nk)
    dot = {"nn": _dot, "nt": _dot_nt, "tn": _dot_tn}[mode]
    a_spec = pl.BlockSpec((tk, tm), lambda i, j, kk: (kk, i)) if mode == "tn" else pl.BlockSpec((tm, tk), lambda i, j, kk: (i, kk))
    b_spec = pl.BlockSpec((tn, tk), lambda i, j, kk: (j, kk)) if mode == "nt" else pl.BlockSpec((tk, tn), lambda i, j, kk: (kk, j))
    o_spec = pl.BlockSpec((tm, tn), lambda i, j, kk: (i, j))

    def body(*refs):
        (a_ref, b_ref), (o_ref,), scratch, finish_rider = _ride(rider, 2, 1, refs, grid)
        part = dot(a_ref[...], b_ref[...])
        if nk == 1:
            o_ref[...] = part.astype(out_dtype)
        else:
            scr = scratch[0]
            kk = pl.program_id(2)

            @pl.when(kk == 0)
            def _():
                scr[...] = part

            @pl.when(kk > 0)
            def _():
                scr[...] += part

            @pl.when(kk == nk - 1)
            def _():
                o_ref[...] = scr[...].astype(out_dtype)

        finish_rider()

    extra = rider or _NO_RIDER
    res = pl.pallas_call(
        body, name=name, grid=grid,
        in_specs=[a_spec, b_spec] + extra.specs, out_specs=[o_spec] + extra.specs,
        out_shape=[jax.ShapeDtypeStruct((m, n), out_dtype)] + extra.out_shape,
        scratch_shapes=([pltpu.VMEM((tm, tn), F32)] if nk > 1 else []) + extra.scratch_shapes,
        compiler_params=_params(3),
    )(a, b, *extra.srcs)
    return res if rider is not None else res[0]


def _rms_fwd(x, g, *, name, resid=None):
    t = x.shape[0]
    tm = _tile(t, NORM_ROW_TILE)
    row = pl.BlockSpec((tm, D_MODEL), lambda i: (i, 0))
    vec = pl.BlockSpec((1, D_MODEL), lambda i: (0, 0))
    has_resid = resid is not None

    def body(*refs):
        x_ref, g_ref = refs[0], refs[1]
        o_ref = refs[-1]
        xv = x_ref[...]
        y = xv * lax.rsqrt(jnp.mean(xv * xv, axis=-1, keepdims=True) + RMS_EPS) * g_ref[...]
        if has_resid:
            o_ref[...] = refs[2][...] + y
        else:
            o_ref[...] = y.astype(BF16)

    return pl.pallas_call(
        body, name=name, grid=(t // tm,),
        in_specs=[row, vec] + ([row] if has_resid else []), out_specs=row,
        out_shape=jax.ShapeDtypeStruct((t, D_MODEL), F32 if has_resid else BF16),
        compiler_params=_params(1),
    )(*((x, g) + ((resid,) if has_resid else ())))


def _rms_bwd(xin, g, dy, *, name, resid=None):
    t = xin.shape[0]
    tm = _tile(t, NORM_ROW_TILE)
    row = pl.BlockSpec((tm, D_MODEL), lambda i: (i, 0))
    vec = pl.BlockSpec((1, D_MODEL), lambda i: (0, 0))
    has_resid = resid is not None
    out_dtype = F32 if has_resid else BF16

    def body(*refs):
        x_ref, g_ref, dy_ref = refs[0], refs[1], refs[2]
        dx_ref, dg_ref = refs[-2], refs[-1]
        xv, dyv = x_ref[...], dy_ref[...]
        r = lax.rsqrt(jnp.mean(xv * xv, axis=-1, keepdims=True) + RMS_EPS)
        a = dyv * g_ref[...]
        dx = r * a - xv * (r * r * r) * jnp.mean(a * xv, axis=-1, keepdims=True)
        if has_resid:
            dx = dx + refs[3][...]
        dx_ref[...] = dx.astype(out_dtype)
        part = jnp.sum(dyv * xv * r, axis=0, keepdims=True)

        @pl.when(pl.program_id(0) == 0)
        def _():
            dg_ref[...] = part

        @pl.when(pl.program_id(0) > 0)
        def _():
            dg_ref[...] += part

    return pl.pallas_call(
        body, name=name, grid=(t // tm,),
        in_specs=[row, vec, row] + ([row] if has_resid else []), out_specs=[row, vec],
        out_shape=[jax.ShapeDtypeStruct((t, D_MODEL), out_dtype), jax.ShapeDtypeStruct((1, D_MODEL), F32)],
        compiler_params=_params(1),
    )(*((xin, g, dy) + ((resid,) if has_resid else ())))


def _loss_head(x, target, *, name):
    t = x.shape[0]
    tm = _tile(t, NORM_ROW_TILE)
    row = pl.BlockSpec((tm, D_MODEL), lambda i: (i, 0))
    acc = pl.BlockSpec((8, LANES), lambda i: (0, 0))

    def body(x_ref, t_ref, l_ref, dx_ref):
        diff = x_ref[...] - t_ref[...]
        dx_ref[...] = diff * (1.0 / D_MODEL)
        part = 0.5 * jnp.sum(jnp.mean(diff * diff, axis=-1, keepdims=True), axis=0, keepdims=True)

        @pl.when(pl.program_id(0) == 0)
        def _():
            l_ref[...] = jnp.zeros((8, LANES), F32) + part

        @pl.when(pl.program_id(0) > 0)
        def _():
            l_ref[...] += part

    return pl.pallas_call(
        body, name=name, grid=(t // tm,),
        in_specs=[row, row], out_specs=[acc, row],
        out_shape=[jax.ShapeDtypeStruct((8, LANES), F32), jax.ShapeDtypeStruct((t, D_MODEL), F32)],
        compiler_params=_params(1),
    )(x, target)


def _window_sum(ext, n_doublings, forward):
    rows = ext.shape[0]
    s, sh = ext, 1
    for _ in range(n_doublings):
        s = s + pltpu.roll(s, sh if forward else rows - sh, 0)
        sh *= 2
    return s


def _pool_fwd(pg, pool_w, pool_b, pool_scale, *, name):
    t = pg.shape[0]
    tm = _tile(t, ROW_TILE)

    def body(pg_ref, w_ref, b_ref, s_ref, o_ref, halo):
        i = pl.program_id(0)

        @pl.when(i == 0)
        def _():
            halo[...] = jnp.zeros_like(halo)

        p = pg_ref[:, :BRANCH_WIDTH]
        gate = pg_ref[:, BRANCH_WIDTH:]
        ext = jnp.concatenate([halo[...], p], axis=0)
        pos = i * tm + lax.broadcasted_iota(jnp.int32, (tm, 1), 0)
        outs = []
        for g, w in enumerate(POOL_WINDOWS):
            cols = slice(g * POOL_GROUP_DIM, (g + 1) * POOL_GROUP_DIM)
            cnt = jnp.minimum(pos + 1, w).astype(F32)
            d = _window_sum(ext[:, cols], g + 1, True)[POOL_HALO:] / cnt - p[:, cols]
            y = (_dot(d.astype(BF16), w_ref[g]) + b_ref[:, cols]) * s_ref[:, cols]
            gg = gate[:, cols]
            outs.append(y * (gg * _sigmoid(gg)))
        o_ref[...] = jnp.concatenate(outs, axis=1).astype(BF16)
        halo[...] = p[tm - POOL_HALO:, :]

    vec = pl.BlockSpec((1, BRANCH_WIDTH), lambda i: (0, 0))
    return pl.pallas_call(
        body, name=name, grid=(t // tm,),
        in_specs=[pl.BlockSpec((tm, 2 * BRANCH_WIDTH), lambda i: (i, 0)),
                  pl.BlockSpec((4, POOL_GROUP_DIM, POOL_GROUP_DIM), lambda i: (0, 0, 0)), vec, vec],
        out_specs=pl.BlockSpec((tm, BRANCH_WIDTH), lambda i: (i, 0)),
        out_shape=jax.ShapeDtypeStruct((t, BRANCH_WIDTH), BF16),
        scratch_shapes=[pltpu.VMEM((POOL_HALO, BRANCH_WIDTH), F32)],
        compiler_params=_params(1),
    )(pg, pool_w, pool_b, pool_scale)


def _pool_bwd(pg, d_out, pool_w, pool_b, pool_scale, *, name):
    t = pg.shape[0]
    tm = _tile(t, ROW_TILE)
    nt = t // tm
    halo_per_tile = tm // POOL_HALO

    def body(pg_ref, halo_ref, do_ref, w_ref, b_ref, s_ref, dpg_ref, dw_ref, dvec_ref, carry):
        i = pl.program_id(0)
        ri = nt - 1 - i

        @pl.when(i == 0)
        def _():
            carry[...] = jnp.zeros_like(carry)
            dw_ref[...] = jnp.zeros_like(dw_ref)
            dvec_ref[...] = jnp.zeros_like(dvec_ref)

        p = pg_ref[:, :BRANCH_WIDTH]
        gate = pg_ref[:, BRANCH_WIDTH:]
        hp = jnp.where(ri > 0, halo_ref[:, :BRANCH_WIDTH], 0.0)
        ext = jnp.concatenate([hp, p], axis=0)
        pos = ri * tm + lax.broadcasted_iota(jnp.int32, (tm, 1), 0)
        dps, dgs, dbs, dss = [], [], [], []
        for g, w in enumerate(POOL_WINDOWS):
            cols = slice(g * POOL_GROUP_DIM, (g + 1) * POOL_GROUP_DIM)
            cnt = jnp.minimum(pos + 1, w).astype(F32)
            d = (_window_sum(ext[:, cols], g + 1, True)[POOL_HALO:] / cnt - p[:, cols]).astype(BF16)
            y1 = _dot(d, w_ref[g]) + b_ref[:, cols]
            scale = s_ref[:, cols]
            y2 = y1 * scale
            gg = gate[:, cols]
            sg = _sigmoid(gg)
            do = do_ref[:, cols]
            dy2 = do * (gg * sg)
            dgs.append(do * y2 * _silu_grad(gg, sg))
            dss.append(jnp.sum(dy2 * y1, axis=0, keepdims=True))
            dy1 = dy2 * scale
            dbs.append(jnp.sum(dy1, axis=0, keepdims=True))
            dy1b = dy1.astype(BF16)
            dw_ref[g] += _dot_tn(d, dy1b)
            dd = _dot_nt(dy1b, w_ref[g])
            dpool = dd / cnt
            dext = jnp.concatenate([dpool, carry[:, cols]], axis=0)
            dps.append(_window_sum(dext, g + 1, False)[:tm] - dd)
            carry[:, cols] = dpool[:POOL_HALO]
        dpg_ref[...] = jnp.concatenate(dps + dgs, axis=1).astype(BF16)
        dvec_ref[0:1, :] += jnp.concatenate(dbs, axis=1)
        dvec_ref[1:2, :] += jnp.concatenate(dss, axis=1)

    vec = pl.BlockSpec((1, BRANCH_WIDTH), lambda i: (0, 0))
    wspec = pl.BlockSpec((4, POOL_GROUP_DIM, POOL_GROUP_DIM), lambda i: (0, 0, 0))
    return pl.pallas_call(
        body, name=name, grid=(nt,),
        in_specs=[pl.BlockSpec((tm, 2 * BRANCH_WIDTH), lambda i: (nt - 1 - i, 0)),
                  pl.BlockSpec((POOL_HALO, 2 * BRANCH_WIDTH), lambda i: (jnp.maximum((nt - 1 - i) * halo_per_tile - 1, 0), 0)),
                  pl.BlockSpec((tm, BRANCH_WIDTH), lambda i: (nt - 1 - i, 0)), wspec, vec, vec],
        out_specs=[pl.BlockSpec((tm, 2 * BRANCH_WIDTH), lambda i: (nt - 1 - i, 0)), wspec,
                   pl.BlockSpec((8, BRANCH_WIDTH), lambda i: (0, 0))],
        out_shape=[jax.ShapeDtypeStruct((t, 2 * BRANCH_WIDTH), BF16),
                   jax.ShapeDtypeStruct((4, POOL_GROUP_DIM, POOL_GROUP_DIM), F32),
                   jax.ShapeDtypeStruct((8, BRANCH_WIDTH), F32)],
        scratch_shapes=[pltpu.VMEM((POOL_HALO, BRANCH_WIDTH), F32)],
        compiler_params=_params(1),
    )(pg, pg, d_out, pool_w, pool_b, pool_scale)


CONV_TILE_ROWS = 64


def _for_conv_tiles(tm, fn):
    def step(it, carry):
        rows = pl.ds(pl.multiple_of(it * CONV_TILE_ROWS, CONV_TILE_ROWS), CONV_TILE_ROWS)
        for c in range(0, BRANCH_WIDTH, LANES):
            fn(rows, slice(c, c + LANES))
        return carry

    lax.fori_loop(0, tm // CONV_TILE_ROWS, step, 0)


def _sublane_shifts(shifted_ref, x, direction):
    rows = x.shape[0]
    shifted_ref[0] = x
    for b in range(1, 8):
        shifted_ref[b] = pltpu.roll(x, b if direction > 0 else rows - b, 0)


CONV_REACH = 8 * ((CONV_KERNEL - 1) // 8)


def _tap_tiles(shifted_ref, base, rows, cols, direction):
    for b in range(8):
        lo = pl.multiple_of(base + rows.start - (CONV_REACH if direction > 0 else 0), 8)
        window = shifted_ref[b, pl.ds(lo, CONV_TILE_ROWS + CONV_REACH), cols]
        for a in range((CONV_KERNEL - 1 - b) // 8 + 1):
            off = CONV_REACH - 8 * a if direction > 0 else 8 * a
            yield 8 * a + b, window[off:off + CONV_TILE_ROWS]


def _tap_sum(shifted_ref, w_ref, base, rows, cols, direction):
    acc = None
    for j, tile in _tap_tiles(shifted_ref, base, rows, cols, direction):
        k = CONV_KERNEL - 1 - j
        term = w_ref[k:k + 1, cols] * tile
        acc = term if acc is None else acc + term
    return acc


def _causal_conv(ext8_ref, w_ref, cv_ref, tm):
    def tile(rows, cols):
        cv_ref[rows, cols] = _tap_sum(ext8_ref, w_ref, CONV_HALO, rows, cols, 1)

    _for_conv_tiles(tm, tile)


def _conv_fwd(c2gc, conv_w, conv_b, ln_g, ln_b, *, name):
    t = c2gc.shape[0]
    tm = _tile(t, ROW_TILE)

    def body(c_ref, w_ref, cb_ref, g_ref, b_ref, o_ref, halo, ext8_ref, cv_ref):
        @pl.when(pl.program_id(0) == 0)
        def _():
            halo[...] = jnp.zeros_like(halo)

        u = c_ref[:, :BRANCH_WIDTH] * _sigmoid(c_ref[:, BRANCH_WIDTH:2 * BRANCH_WIDTH])
        gate = c_ref[:, 2 * BRANCH_WIDTH:]
        _sublane_shifts(ext8_ref, jnp.concatenate([halo[...], u], axis=0), 1)
        halo[...] = u[tm - CONV_HALO:, :]
        _causal_conv(ext8_ref, w_ref, cv_ref, tm)
        cv = cv_ref[...] + cb_ref[...]
        mu = jnp.mean(cv, axis=-1, keepdims=True)
        xc = cv - mu
        var = jnp.mean(xc * xc, axis=-1, keepdims=True)
        ln = xc * lax.rsqrt(var + LN_EPS) * g_ref[...] + b_ref[...]
        o_ref[...] = (ln * _sigmoid(ln) * (gate * _sigmoid(gate))).astype(BF16)

    vec = pl.BlockSpec((1, BRANCH_WIDTH), lambda i: (0, 0))
    return pl.pallas_call(
        body, name=name, grid=(t // tm,),
        in_specs=[pl.BlockSpec((tm, 3 * BRANCH_WIDTH), lambda i: (i, 0)),
                  pl.BlockSpec((CONV_HALO, BRANCH_WIDTH), lambda i: (0, 0)), vec, vec, vec],
        out_specs=pl.BlockSpec((tm, BRANCH_WIDTH), lambda i: (i, 0)),
        out_shape=jax.ShapeDtypeStruct((t, BRANCH_WIDTH), BF16),
        scratch_shapes=[pltpu.VMEM((CONV_HALO, BRANCH_WIDTH), F32), pltpu.VMEM((8, tm + CONV_HALO, BRANCH_WIDTH), F32),
                        pltpu.VMEM((tm, BRANCH_WIDTH), F32)],
        compiler_params=_params(1),
    )(c2gc, conv_w, conv_b, ln_g, ln_b)


def _conv_bwd(c2gc, d_out, conv_w, conv_b, ln_g, ln_b, *, name):
    t = c2gc.shape[0]
    tm = _tile(t, ROW_TILE)
    nt = t // tm
    halo_per_tile = tm // CONV_HALO

    def body(c_ref, halo_ref, do_ref, w_ref, cb_ref, g_ref, b_ref, dc_ref, dw_ref, dvec_ref,
             carry, ext8_ref, cv_ref, dext8_ref, du_ref, dw_acc):
        i = pl.program_id(0)
        ri = nt - 1 - i

        @pl.when(i == 0)
        def _():
            carry[...] = jnp.zeros_like(carry)
            dw_acc[...] = jnp.zeros_like(dw_acc)
            dvec_ref[...] = jnp.zeros_like(dvec_ref)

        a = c_ref[:, :BRANCH_WIDTH]
        sb = _sigmoid(c_ref[:, BRANCH_WIDTH:2 * BRANCH_WIDTH])
        gate = c_ref[:, 2 * BRANCH_WIDTH:]
        hu = halo_ref[:, :BRANCH_WIDTH] * _sigmoid(halo_ref[:, BRANCH_WIDTH:2 * BRANCH_WIDTH])
        _sublane_shifts(ext8_ref, jnp.concatenate([jnp.where(ri > 0, hu, 0.0), a * sb], axis=0), 1)
        _causal_conv(ext8_ref, w_ref, cv_ref, tm)
        cv = cv_ref[...] + cb_ref[...]
        mu = jnp.mean(cv, axis=-1, keepdims=True)
        xc = cv - mu
        rs = lax.rsqrt(jnp.mean(xc * xc, axis=-1, keepdims=True) + LN_EPS)
        n = xc * rs
        ln = n * g_ref[...] + b_ref[...]
        sl = _sigmoid(ln)
        sgate = _sigmoid(gate)
        do = do_ref[...]
        dgate = do * (ln * sl) * _silu_grad(gate, sgate)
        dln = do * (gate * sgate) * _silu_grad(ln, sl)
        dn = dln * g_ref[...]
        dcv = rs * (dn - jnp.mean(dn, axis=-1, keepdims=True) - n * jnp.mean(dn * n, axis=-1, keepdims=True))
        dvec_ref[0:1, :] += jnp.sum(dcv, axis=0, keepdims=True)
        dvec_ref[1:2, :] += jnp.sum(dln * n, axis=0, keepdims=True)
        dvec_ref[2:3, :] += jnp.sum(dln, axis=0, keepdims=True)
        _sublane_shifts(dext8_ref, jnp.concatenate([dcv, carry[...]], axis=0), -1)
        carry[...] = dcv[:CONV_HALO]
        def tile(rows, cols):
            du_ref[rows, cols] = _tap_sum(dext8_ref, w_ref, 0, rows, cols, -1)
            d_tile = dext8_ref[0, rows, cols]
            for j, u_tile in _tap_tiles(ext8_ref, CONV_HALO, rows, cols, 1):
                prod = d_tile * u_tile
                part = prod[0:8]
                for q in range(8, CONV_TILE_ROWS, 8):
                    part = part + prod[q:q + 8]
                dw_acc[CONV_KERNEL - 1 - j, :, cols] += part

        _for_conv_tiles(tm, tile)
        du = du_ref[...]
        dc_ref[...] = jnp.concatenate([du * sb, du * a * sb * (1.0 - sb), dgate], axis=1).astype(BF16)

        @pl.when(i == nt - 1)
        def _():
            dw_ref[...] = jnp.sum(dw_acc[...], axis=1)

    vec = pl.BlockSpec((1, BRANCH_WIDTH), lambda i: (0, 0))
    wspec = pl.BlockSpec((CONV_HALO, BRANCH_WIDTH), lambda i: (0, 0))
    return pl.pallas_call(
        body, name=name, grid=(nt,),
        in_specs=[pl.BlockSpec((tm, 3 * BRANCH_WIDTH), lambda i: (nt - 1 - i, 0)),
                  pl.BlockSpec((CONV_HALO, 3 * BRANCH_WIDTH), lambda i: (jnp.maximum((nt - 1 - i) * halo_per_tile - 1, 0), 0)),
                  pl.BlockSpec((tm, BRANCH_WIDTH), lambda i: (nt - 1 - i, 0)), wspec, vec, vec, vec],
        out_specs=[pl.BlockSpec((tm, 3 * BRANCH_WIDTH), lambda i: (nt - 1 - i, 0)), wspec,
                   pl.BlockSpec((8, BRANCH_WIDTH), lambda i: (0, 0))],
        out_shape=[jax.ShapeDtypeStruct((t, 3 * BRANCH_WIDTH), BF16),
                   jax.ShapeDtypeStruct((CONV_HALO, BRANCH_WIDTH), F32),
                   jax.ShapeDtypeStruct((8, BRANCH_WIDTH), F32)],
        scratch_shapes=[pltpu.VMEM((CONV_HALO, BRANCH_WIDTH), F32),
                        pltpu.VMEM((8, tm + CONV_HALO, BRANCH_WIDTH), F32), pltpu.VMEM((tm, BRANCH_WIDTH), F32),
                        pltpu.VMEM((8, tm + CONV_HALO, BRANCH_WIDTH), F32), pltpu.VMEM((tm, BRANCH_WIDTH), F32),
                        pltpu.VMEM((CONV_HALO, 8, BRANCH_WIDTH), F32)],
        compiler_params=_params(1),
    )(c2gc, c2gc, d_out, conv_w, conv_b, ln_g, ln_b)


def _merge_specs(tm):
    gates = [pl.BlockSpec((tm, D_MODEL), functools.partial(lambda i, b: (i, b), b=b)) for b in range(3)]
    acts = [pl.BlockSpec((tm, BRANCH_WIDTH), lambda i: (i, 0))] * 3
    weights = [pl.BlockSpec((BRANCH_WIDTH, D_MODEL), lambda i: (0, 0))] * 3
    return gates + acts + weights


def _merge_fwd(gmga, acts, weights, *, name):
    t = gmga.shape[0]
    tm = _tile(t, ROW_TILE)

    def body(g0, g1, g2, a0, a1, a2, w0, w1, w2, o_ref):
        m = None
        for g_ref, a_ref, w_ref in ((g0, a0, w0), (g1, a1, w1), (g2, a2, w2)):
            term = _sigmoid(g_ref[...]) * _dot(a_ref[...], w_ref[...])
            m = term if m is None else m + term
        o_ref[...] = m.astype(BF16)

    return pl.pallas_call(
        body, name=name, grid=(t // tm,),
        in_specs=_merge_specs(tm), out_specs=pl.BlockSpec((tm, D_MODEL), lambda i: (i, 0)),
        out_shape=jax.ShapeDtypeStruct((t, D_MODEL), BF16),
        compiler_params=_params(1),
    )(gmga, gmga, gmga, *acts, *weights)


def _merge_bwd(dm, gmga, acts, weights, *, name):
    t = gmga.shape[0]
    tm = _tile(t, ROW_TILE)
    row = pl.BlockSpec((tm, D_MODEL), lambda i: (i, 0))
    wide = pl.BlockSpec((tm, 3 * D_MODEL), lambda i: (i, 0))

    def body(dm_ref, g0, g1, g2, a0, a1, a2, w0, w1, w2, d0, d1, d2, dg_ref):
        dmv = dm_ref[...]
        for k, (g_ref, a_ref, w_ref, dy_ref) in enumerate(((g0, a0, w0, d0), (g1, a1, w1, d1), (g2, a2, w2, d2))):
            s = _sigmoid(g_ref[...])
            dy_ref[...] = (dmv * s).astype(BF16)
            y = _dot(a_ref[...], w_ref[...])
            dg_ref[:, k * D_MODEL:(k + 1) * D_MODEL] = (dmv * y * s * (1.0 - s)).astype(BF16)

    return pl.pallas_call(
        body, name=name, grid=(t // tm,),
        in_specs=[row] + _merge_specs(tm), out_specs=[row, row, row, wide],
        out_shape=[jax.ShapeDtypeStruct((t, D_MODEL), BF16)] * 3 + [jax.ShapeDtypeStruct((t, 3 * D_MODEL), BF16)],
        compiler_params=_params(1),
    )(dm, gmga, gmga, gmga, *acts, *weights)


GA_BLOCK = 3 * D_MODEL // HEAD_PAIR


def _split_heads(x, lane_is_first):
    zero = jnp.zeros_like(x)
    return jnp.concatenate([jnp.where(lane_is_first, x, zero), jnp.where(lane_is_first, zero, x)], axis=0)


def _side_by_side(x, rows):
    return jnp.concatenate([x[:rows], x[rows:]], axis=1)


def _split_bf16(x):
    hi = x.astype(BF16)
    return hi, (x - hi.astype(F32)).astype(BF16)


def _scores(qcat, kblk, mask):
    z = _dot_nt(qcat, kblk)
    e = jnp.exp(-jnp.abs(z))
    sp = jnp.maximum(z, 0.0) + jnp.log(1.0 + e)
    l1m = -sp
    if mask is not None:
        l1m = jnp.where(mask, l1m, 0.0)
    inv = 1.0 / (1.0 + e)
    pos = z >= 0.0
    return z - sp, l1m, jnp.where(pos, 1.0, e) * inv, jnp.where(pos, e, 1.0) * inv


def _attn_consts(blk):
    lane_is_first = lax.broadcasted_iota(jnp.int32, (1, HEAD_PAIR), 1) < HEAD_DIM
    r = lax.broadcasted_iota(jnp.int32, (blk, blk), 0)
    c = lax.broadcasted_iota(jnp.int32, (blk, blk), 1)
    after = (r > c).astype(BF16)
    from_here = (r >= c).astype(BF16)
    qrow = lax.broadcasted_iota(jnp.int32, (2 * blk, blk), 0)
    qrow = jnp.where(qrow >= blk, qrow - blk, qrow)
    causal = lax.broadcasted_iota(jnp.int32, (2 * blk, blk), 1) < qrow
    return lane_is_first, after, from_here, causal


def _while_mass_left(qi, carry, block):
    def alive(c):
        return jnp.max(c[0]) > LOG_F32_ZERO

    def cond(state):
        return jnp.logical_and(state[0] < qi, state[1])

    def step(state):
        new = block(qi - 1 - state[0], state[2])
        return state[0] + 1, alive(new), new

    return lax.while_loop(cond, step, (jnp.int32(0), alive(carry), carry))[2]


def _ride(rider, n_in, n_out, refs, grid):
    if rider is None:
        return refs[:n_in], refs[n_in:n_in + n_out], refs[n_in + n_out:], lambda: None
    n = rider.n
    ins, srcs = refs[:n_in], refs[n_in:n_in + n]
    outs, dsts = refs[n_in + n:n_in + n + n_out], refs[n_in + n + n_out:n_in + 2 * n + n_out]
    rest = refs[n_in + 2 * n + n_out:]
    scratch, sems = rest[:len(rest) - 3], rest[len(rest) - 3:]
    step, n_steps = 0, 1
    for axis, size in enumerate(grid):
        step, n_steps = step * size + pl.program_id(axis), n_steps * size

    @pl.when(step == 0)
    def _():
        rider.start(srcs, dsts, sems)

    def finish():
        @pl.when(step == n_steps - 1)
        def _():
            rider.finish(srcs, dsts, sems)

    return ins, outs, scratch, finish


def _attn_fwd(qkv, gmga, *, name, rider=None):
    t = qkv.shape[0]
    blk, step_rows = _tile(t, ATTN_BLOCK), _tile(t, ATTN_BLOCK * ATTN_BLOCKS_PER_STEP)
    nq = t // step_rows

    def body(*refs):
        (q_ref, k_ref, v_ref, ga_ref), (o_ref, cv_ref), _, finish_rider = _ride(rider, 4, 2, refs, (N_HEAD_PAIRS, nq))
        lane_is_first, after, _, causal = _attn_consts(blk)

        for sub in range(step_rows // blk):
            mine = slice(sub * blk, (sub + 1) * blk)
            qi = pl.program_id(1) * (step_rows // blk) + sub
            qcat = _split_heads(q_ref[mine, :], lane_is_first)

            def block(kb, carry, mask, qcat=qcat):
                run, acc = carry
                rows = pl.ds(pl.multiple_of(kb * blk, blk), blk)
                lb, l1m, _, _ = _scores(qcat, k_ref[rows, :], mask)
                hi, lo = _split_bf16(l1m)
                w = jnp.exp(lb + (_dot(hi, after) + _dot(lo, after) + run))
                if mask is not None:
                    w = jnp.where(mask, w, 0.0)
                vcat = _split_heads(v_ref[rows, :], lane_is_first)
                acc = acc + _dot(_side_by_side(w.astype(BF16), blk), vcat)
                return run + jnp.sum(l1m, axis=-1, keepdims=True), acc

            carry = block(qi, (jnp.zeros((2 * blk, 1), F32), jnp.zeros((blk, HEAD_PAIR), F32)), causal)
            _, o = _while_mass_left(qi, carry, lambda kb, c, block=block: block(kb, c, None))
            o_ref[mine, :] = o
            ga = ga_ref[mine, :]
            cv_ref[mine, :] = (o * (ga * _sigmoid(ga))).astype(BF16)
        finish_rider()

    qspec = pl.BlockSpec((step_rows, HEAD_PAIR), lambda p, i: (i, p))
    extra = rider or _NO_RIDER
    return pl.pallas_call(
        body, name=name, grid=(N_HEAD_PAIRS, nq),
        in_specs=[qspec,
                  pl.BlockSpec((t, HEAD_PAIR), lambda p, i: (0, N_HEAD_PAIRS + p)),
                  pl.BlockSpec((t, HEAD_PAIR), lambda p, i: (0, 2 * N_HEAD_PAIRS + p)),
                  pl.BlockSpec((step_rows, HEAD_PAIR), lambda p, i: (i, GA_BLOCK + p))] + extra.specs,
        out_specs=[qspec, qspec] + extra.specs,
        out_shape=[jax.ShapeDtypeStruct((t, BRANCH_WIDTH), F32), jax.ShapeDtypeStruct((t, BRANCH_WIDTH), BF16)] + extra.out_shape,
        scratch_shapes=extra.scratch_shapes,
        compiler_params=_params(2),
    )(qkv, qkv, qkv, gmga, *extra.srcs)


def _attn_bwd(qkv, o, gmga, dcv, *, name, rider=None):
    t = qkv.shape[0]
    blk, step_rows = _tile(t, ATTN_BLOCK), _tile(t, ATTN_BLOCK * ATTN_BLOCKS_PER_STEP)
    nq = t // step_rows

    def body(*refs):
        ins, outs, (dk_acc, dv_acc), finish_rider = _ride(rider, 6, 4, refs, (N_HEAD_PAIRS, nq))
        q_ref, k_ref, v_ref, o_ref, ga_ref, dcv_ref = ins
        dq_ref, dk_ref, dv_ref, dga_ref = outs
        lane_is_first, after, from_here, causal = _attn_consts(blk)

        @pl.when(pl.program_id(1) == 0)
        def _():
            dk_acc[...] = jnp.zeros_like(dk_acc)
            dv_acc[...] = jnp.zeros_like(dv_acc)

        for sub in range(step_rows // blk):
            mine = slice(sub * blk, (sub + 1) * blk)
            qi = pl.program_id(1) * (step_rows // blk) + sub
            ga, ov, dcvv = ga_ref[mine, :], o_ref[mine, :], dcv_ref[mine, :]
            sg = _sigmoid(ga)
            dob = (dcvv * (ga * sg)).astype(BF16)
            dga_ref[mine, :] = (dcvv * ov * _silu_grad(ga, sg)).astype(BF16)
            gt = dob.astype(F32) * ov
            g_total = jnp.concatenate(
                [jnp.sum(jnp.where(lane_is_first, gt, 0.0), axis=-1, keepdims=True),
                 jnp.sum(jnp.where(lane_is_first, 0.0, gt), axis=-1, keepdims=True)], axis=0)
            qcat = _split_heads(q_ref[mine, :], lane_is_first)
            docat = _split_heads(dob, lane_is_first)

            def block(kb, carry, mask, g_total=g_total, qcat=qcat, docat=docat):
                run, g_run, dq = carry
                rows = pl.ds(pl.multiple_of(kb * blk, blk), blk)
                kblk = k_ref[rows, :]
                lb, l1m, sig, one_m_sig = _scores(qcat, kblk, mask)
                hi, lo = _split_bf16(l1m)
                w = jnp.exp(lb + (_dot(hi, after) + _dot(lo, after) + run))
                if mask is not None:
                    w = jnp.where(mask, w, 0.0)
                wb = w.astype(BF16)
                g = _dot_nt(docat, v_ref[rows, :]) * wb.astype(F32)
                ghi, glo = _split_bf16(g)
                g_before = g_total - g_run - (_dot(ghi, from_here) + _dot(glo, from_here))
                dz = g * one_m_sig - g_before * sig
                if mask is not None:
                    dz = jnp.where(mask, dz, 0.0)
                dzb = dz.astype(BF16)
                dq = dq + _dot(_side_by_side(dzb, blk), _split_heads(kblk, lane_is_first))
                dk_acc[rows, :] += _dot_tn(dzb, qcat)
                dv_acc[rows, :] += _dot_tn(wb, docat)
                return (run + jnp.sum(l1m, axis=-1, keepdims=True), g_run + jnp.sum(g, axis=-1, keepdims=True), dq)

            zero = jnp.zeros((2 * blk, 1), F32)
            carry = block(qi, (zero, zero, jnp.zeros((blk, HEAD_PAIR), F32)), causal)
            _, _, dq = _while_mass_left(qi, carry, lambda kb, c, block=block: block(kb, c, None))
            dq_ref[mine, :] = (dq * ATTN_SCALE).astype(BF16)

        @pl.when(pl.program_id(1) == nq - 1)
        def _():
            dk_ref[...] = dk_acc[...].astype(BF16)
            dv_ref[...] = dv_acc[...].astype(BF16)

        finish_rider()

    qspec = pl.BlockSpec((step_rows, HEAD_PAIR), lambda p, i: (i, p))
    whole = pl.BlockSpec((t, HEAD_PAIR), lambda p, i: (0, p))
    out = jax.ShapeDtypeStruct((t, BRANCH_WIDTH), BF16)
    extra = rider or _NO_RIDER
    return pl.pallas_call(
        body, name=name, grid=(N_HEAD_PAIRS, nq),
        in_specs=[qspec,
                  pl.BlockSpec((t, HEAD_PAIR), lambda p, i: (0, N_HEAD_PAIRS + p)),
                  pl.BlockSpec((t, HEAD_PAIR), lambda p, i: (0, 2 * N_HEAD_PAIRS + p)),
                  qspec,
                  pl.BlockSpec((step_rows, HEAD_PAIR), lambda p, i: (i, GA_BLOCK + p)),
                  qspec] + extra.specs,
        out_specs=[qspec, whole, whole, qspec] + extra.specs,
        out_shape=[out, out, out, out] + extra.out_shape,
        scratch_shapes=[pltpu.VMEM((t, HEAD_PAIR), F32), pltpu.VMEM((t, HEAD_PAIR), F32)] + extra.scratch_shapes,
        compiler_params=_params(2),
    )(qkv, qkv, qkv, o, gmga, dcv, *extra.srcs)


def _mesh_position():
    x, y, c = lax.axis_index("x"), lax.axis_index("y"), lax.axis_index("c")
    return x, y, c, 4 * x + 2 * y + c


def _flipped(x, y, c, k):
    return (1 - x if k & 4 else x, 1 - y if k & 2 else y, 1 - c if k & 1 else c)


def _all_to_all(srcs, *, name, same_block):
    ex = _Exchange(srcs, same_block)

    def body(*refs):
        ex.start(refs[:ex.n], refs[ex.n:2 * ex.n], refs[2 * ex.n:])
        ex.finish(refs[:ex.n], refs[ex.n:2 * ex.n], refs[2 * ex.n:])

    return pl.pallas_call(
        body, name=name, in_specs=ex.specs, out_specs=ex.specs, out_shape=ex.out_shape,
        scratch_shapes=ex.scratch_shapes,
    )(*srcs)


def _gather_via_sibling(srcs, *, name):
    n = len(srcs)

    def body(*refs):
        src_refs, dst_refs = refs[:n], refs[n:2 * n]
        send_sems, recv_sems, local_sems = refs[2 * n:]
        x, y, c, me = _mesh_position()
        sibling = (x, y, 1 - c)
        chips = [(1 - x, y), (x, 1 - y), (1 - x, 1 - y)]

        def slot(px, py, pc):
            return 4 * px + 2 * py + pc

        def copy(i, k, block, to, from_src=False):
            return pltpu.make_async_remote_copy(
                src_ref=src_refs[i] if from_src else dst_refs[i].at[slot(*block)], dst_ref=dst_refs[i].at[slot(*block)],
                send_sem=send_sems.at[k, i], recv_sem=recv_sems.at[k, i], device_id=to, device_id_type=MESH)

        mine = [pltpu.make_async_copy(src_refs[i], dst_refs[i].at[me], local_sems.at[i]) for i in range(n)]
        first = [copy(i, 0, (x, y, c), sibling, True) for i in range(n)]
        first += [copy(i, 1 + j, (x, y, c), (*chip, c), True) for j, chip in enumerate(chips) for i in range(n)]
        for cp in mine + first:
            cp.start()
        passed = []
        for j, chip in enumerate(chips):
            for i in range(n):
                copy(i, 1 + j, (*chip, c), (x, y, c)).wait_recv()
                passed.append(copy(i, 4 + j, (*chip, c), sibling))
                passed[-1].start()
        for i in range(n):
            copy(i, 0, sibling, (x, y, c)).wait_recv()
            for j, chip in enumerate(chips):
                copy(i, 4 + j, (*chip, 1 - c), (x, y, c)).wait_recv()
        for cp in first + passed:
            cp.wait_send()
        for cp in mine:
            cp.wait()

    spec = [pl.BlockSpec(memory_space=pl.ANY)] * n
    return pl.pallas_call(
        body, name=name, in_specs=spec, out_specs=spec,
        out_shape=[jax.ShapeDtypeStruct((N_DEV,) + tuple(s.shape), s.dtype) for s in srcs],
        scratch_shapes=[pltpu.SemaphoreType.DMA((N_DEV - 1, n)), pltpu.SemaphoreType.DMA((N_DEV - 1, n)),
                        pltpu.SemaphoreType.DMA((n,))],
    )(*srcs)


class _Exchange:
    def __init__(self, srcs, same_block):
        self.srcs, self.same_block, self.n = list(srcs), same_block, len(srcs)
        self.specs = [pl.BlockSpec(memory_space=pl.ANY)] * self.n
        self.out_shape = [jax.ShapeDtypeStruct((N_DEV,) + tuple(s.shape if same_block else s.shape[1:]), s.dtype)
                          for s in self.srcs]
        self.scratch_shapes = [pltpu.SemaphoreType.DMA((N_DEV - 1, self.n)), pltpu.SemaphoreType.DMA((N_DEV - 1, self.n)),
                               pltpu.SemaphoreType.DMA((self.n,))] if self.n else []

    def _copies(self, src_refs, dst_refs, sems, with_arrivals):
        send_sems, recv_sems, local_sems = sems
        x, y, c, me = _mesh_position()

        def outgoing(i, j):
            return src_refs[i] if self.same_block else src_refs[i].at[j]

        def remote(i, k, slot):
            return pltpu.make_async_remote_copy(
                src_ref=outgoing(i, jnp.bitwise_xor(me, k)), dst_ref=dst_refs[i].at[slot],
                send_sem=send_sems.at[k - 1, i], recv_sem=recv_sems.at[k - 1, i],
                device_id=_flipped(x, y, c, k), device_id_type=MESH)

        pairs = [(i, k) for k in range(1, N_DEV) for i in range(self.n)]
        mine = [pltpu.make_async_copy(outgoing(i, me), dst_refs[i].at[me], local_sems.at[i]) for i in range(self.n)]
        sent = [remote(i, k, me) for i, k in pairs]
        arrivals = [remote(i, k, jnp.bitwise_xor(me, k)) for i, k in pairs] if with_arrivals else []
        return mine, sent, arrivals

    def start(self, src_refs, dst_refs, sems):
        mine, sent, _ = self._copies(src_refs, dst_refs, sems, False)
        for cp in mine + sent:
            cp.start()

    def finish(self, src_refs, dst_refs, sems):
        mine, sent, arrivals = self._copies(src_refs, dst_refs, sems, True)
        for cp in arrivals:
            cp.wait_recv()
        for cp in sent:
            cp.wait_send()
        for cp in mine:
            cp.wait()


_NO_RIDER = _Exchange([], True)


def _adamw(parts, w, m, v, *, name):
    layers, rows, cols = w.shape
    assert len(parts) == layers
    tr = rows
    while tr * cols > ADAM_TILE_ELEMS and tr % 32 == 0:
        tr //= 2
    row = pl.BlockSpec((1, tr, cols), lambda l, i: (l, i, 0))

    def body(*refs):
        p_refs = refs[:layers]
        w_ref, m_ref, v_ref, g_ref, d_ref, nm_ref, nv_ref = refs[layers:]
        layer = pl.program_id(0)
        g = None
        for k in range(N_DEV):
            part = p_refs[0][k]
            for l in range(1, layers):
                part = jnp.where(layer == l, p_refs[l][k], part)
            g = part.astype(F32) if g is None else g + part.astype(F32)
        m2 = ADAM_B1 * m_ref[0] + (1.0 - ADAM_B1) * g
        v2 = ADAM_B2 * v_ref[0] + (1.0 - ADAM_B2) * (g * g)
        m_hat = m2 / (1.0 - ADAM_B1 ** ADAM_STEP)
        v_hat = v2 / (1.0 - ADAM_B2 ** ADAM_STEP)
        g_ref[0] = g
        d_ref[0] = -ADAM_LR * (m_hat / (jnp.sqrt(v_hat) + ADAM_EPS) + ADAM_WD * w_ref[0])
        nm_ref[0] = m2
        nv_ref[0] = v2

    out = jax.ShapeDtypeStruct((layers, rows, cols), F32)
    return pl.pallas_call(
        body, name=name, grid=(layers, rows // tr),
        in_specs=[pl.BlockSpec((N_DEV, tr, cols), lambda l, i: (0, i, 0))] * layers + [row, row, row],
        out_specs=[row, row, row, row], out_shape=[out, out, out, out],
        compiler_params=_params(2),
    )(*parts, w, m, v)


MATMUL_WEIGHTS = ("w_in", "w_pool_out", "w_conv_out", "w_attn_out", "w_o")
SMALL = ("conv_w", "norm_pre", "pool_w", "pool_b", "pool_scale", "conv_b", "conv_ln_g", "conv_ln_b", "norm_post")
WEIGHT_ORDER = ("norm_pre", "w_in", "pool_w", "pool_b", "pool_scale", "w_pool_out", "conv_w", "conv_b",
                "conv_ln_g", "conv_ln_b", "w_conv_out", "w_attn_out", "w_o", "norm_post")


def _shard_axis(name):
    return -2 if name == "w_o" else -1


def _pack_rows(flat_parts, row_multiple):
    flat = jnp.concatenate(flat_parts, axis=-1)
    n = flat.shape[-1]
    chunk = row_multiple * LANES
    total = -(-n // chunk) * chunk
    pad = [(0, 0)] * (flat.ndim - 1) + [(0, total - n)]
    return jnp.pad(flat, pad).reshape(flat.shape[:-1] + (total // LANES, LANES))


def _unpack(buf, shapes):
    flat = buf.reshape(-1)
    out, at = {}, 0
    for name, shape in shapes:
        n = 1
        for s in shape:
            n *= s
        out[name] = flat[at:at + n].reshape(shape)
        at += n
    return out


def _to_dest_major(name, full):
    axis = full.ndim + _shard_axis(name)
    n = full.shape[axis] // N_DEV
    return jnp.stack([lax.slice_in_dim(full, d * n, (d + 1) * n, axis=axis) for d in range(N_DEV)])


def _from_source_major(name, gathered):
    return jnp.concatenate([gathered[d] for d in range(N_DEV)], axis=_shard_axis(name))


def kernel(x, norm_pre, w_in, pool_w, pool_b, pool_scale, w_pool_out, conv_w, conv_b, conv_ln_g, conv_ln_b, w_conv_out, w_attn_out, w_o, norm_post, loss_target, m_norm_pre, m_w_in, m_pool_w, m_pool_b, m_pool_scale, m_w_pool_out, m_conv_w, m_conv_b, m_conv_ln_g, m_conv_ln_b, m_w_conv_out, m_w_attn_out, m_w_o, m_norm_post, v_norm_pre, v_w_in, v_pool_w, v_pool_b, v_pool_scale, v_w_pool_out, v_conv_w, v_conv_b, v_conv_ln_g, v_conv_ln_b, v_w_conv_out, v_w_attn_out, v_w_o, v_norm_post):
    weights = dict(norm_pre=norm_pre, w_in=w_in, pool_w=pool_w, pool_b=pool_b, pool_scale=pool_scale,
                   w_pool_out=w_pool_out, conv_w=conv_w, conv_b=conv_b, conv_ln_g=conv_ln_g, conv_ln_b=conv_ln_b,
                   w_conv_out=w_conv_out, w_attn_out=w_attn_out, w_o=w_o, norm_post=norm_post)
    mom1 = dict(norm_pre=m_norm_pre, w_in=m_w_in, pool_w=m_pool_w, pool_b=m_pool_b, pool_scale=m_pool_scale,
                w_pool_out=m_w_pool_out, conv_w=m_conv_w, conv_b=m_conv_b, conv_ln_g=m_conv_ln_g, conv_ln_b=m_conv_ln_b,
                w_conv_out=m_w_conv_out, w_attn_out=m_w_attn_out, w_o=m_w_o, norm_post=m_norm_post)
    mom2 = dict(norm_pre=v_norm_pre, w_in=v_w_in, pool_w=v_pool_w, pool_b=v_pool_b, pool_scale=v_pool_scale,
                w_pool_out=v_w_pool_out, conv_w=v_conv_w, conv_b=v_conv_b, conv_ln_g=v_conv_ln_g, conv_ln_b=v_conv_ln_b,
                w_conv_out=v_w_conv_out, w_attn_out=v_w_attn_out, w_o=v_w_o, norm_post=v_norm_post)
    xs = x[0]
    target = loss_target[0]

    conv_rows = jnp.pad(conv_w, ((0, 0), (0, CONV_HALO - CONV_KERNEL), (0, 0)))
    shards = [[weights[n][l].astype(BF16) for n in MATMUL_WEIGHTS] for l in range(DEPTH)]
    gathered = _gather_via_sibling([shards[0][0], conv_rows], name="gather_weights")
    full = [{"w_in": _from_source_major("w_in", gathered[0])}, None]
    conv_full = _from_source_major("conv_w", gathered[1])

    def in_sections(w):
        return dict(pg=w[:, 0:1024], c2gc=w[:, 1024:2560], q=w[:, 2560:3072], k=w[:, 3072:3584], v=w[:, 3584:4096],
                    gmga=jnp.concatenate([w[:, 4608:7680], w[:, 4096:4608]], axis=1))

    saved = []
    cur = xs
    for l in range(DEPTH):
        sec = in_sections(full[l]["w_in"])
        w_qkv = jnp.concatenate([sec["q"] * ATTN_SCALE, sec["k"], sec["v"]], axis=1)
        pw = pool_w[l].astype(BF16)
        pb, ps = pool_b[l].reshape(1, -1), pool_scale[l].reshape(1, -1)
        cb, lg, lb = conv_b[l].reshape(1, -1), conv_ln_g[l].reshape(1, -1), conv_ln_b[l].reshape(1, -1)
        h = _rms_fwd(cur, norm_pre[l].reshape(1, -1), name=f"rms_pre_fwd_{l}")
        pg = _matmul(h, sec["pg"], mode="nn", name=f"proj_pg_{l}")
        c2gc = _matmul(h, sec["c2gc"], mode="nn", name=f"proj_c2gc_{l}")
        qkv = _matmul(h, w_qkv, mode="nn", name=f"proj_qkv_{l}", out_dtype=BF16)
        if l == 0:
            gmga, *arrived = _matmul(h, sec["gmga"], mode="nn", name=f"proj_gmga_{l}", rider=_Exchange(shards[0][1:], True))
            full[0].update({n: _from_source_major(n, g) for n, g in zip(MATMUL_WEIGHTS[1:], arrived)})
        else:
            gmga = _matmul(h, sec["gmga"], mode="nn", name=f"proj_gmga_{l}")
        a_act = _pool_fwd(pg, pw, pb, ps, name=f"pool_fwd_{l}")
        b_act = _conv_fwd(c2gc, conv_full[l], cb, lg, lb, name=f"conv_fwd_{l}")
        rider = _Exchange(shards[l + 1], True) if l + 1 < DEPTH else None
        o, c_act, *arrived = _attn_fwd(qkv, gmga, name=f"attn_fwd_{l}", rider=rider)
        if rider is not None:
            full[l + 1] = {n: _from_source_major(n, g) for n, g in zip(MATMUL_WEIGHTS, arrived)}
        out_weights = [full[l][n] for n in ("w_pool_out", "w_conv_out", "w_attn_out")]
        mix = _merge_fwd(gmga, [a_act, b_act, c_act], out_weights, name=f"merge_fwd_{l}")
        out = _matmul(mix, full[l]["w_o"], mode="nn", name=f"out_proj_{l}")
        nxt = _rms_fwd(out, norm_post[l].reshape(1, -1), name=f"rms_post_fwd_{l}", resid=cur)
        saved.append(dict(x=cur, h=h, pg=pg, c2gc=c2gc, qkv=qkv, gmga=gmga, a=a_act, b=b_act, c=c_act, o=o,
                          out_weights=out_weights, mix=mix, out=out, pw=pw, pb=pb, ps=ps, cb=cb, lg=lg, lb=lb))
        cur = nxt

    loss_tile, dx = _loss_head(cur, target, name="loss_head")
    loss = lax.psum(loss_tile[0, 0], ("x", "y", "c"))

    grads = {n: [None] * DEPTH for n in WEIGHT_ORDER}
    parts = {n: [None] * DEPTH for n in MATMUL_WEIGHTS}
    for l in reversed(range(DEPTH)):
        s = saved[l]
        dout, grads["norm_post"][l] = _rms_bwd(s["out"], norm_post[l].reshape(1, -1), dx, name=f"rms_post_bwd_{l}")
        dmix = _matmul(dout, full[l]["w_o"], mode="nt", name=f"d_mix_{l}")
        grads["w_o"][l] = _matmul(s["mix"], dout, mode="tn", name=f"d_w_o_{l}", out_dtype=BF16)
        dya, dyb, dyc, dgm = _merge_bwd(dmix, s["gmga"], [s["a"], s["b"], s["c"]], s["out_weights"], name=f"merge_bwd_{l}")
        da = _matmul(dya, full[l]["w_pool_out"], mode="nt", name=f"d_pool_act_{l}")
        grads["w_pool_out"][l] = _matmul(s["a"], dya, mode="tn", name=f"d_w_pool_out_{l}", out_dtype=BF16)
        db = _matmul(dyb, full[l]["w_conv_out"], mode="nt", name=f"d_conv_act_{l}")
        grads["w_conv_out"][l] = _matmul(s["b"], dyb, mode="tn", name=f"d_w_conv_out_{l}", out_dtype=BF16)
        dc = _matmul(dyc, full[l]["w_attn_out"], mode="nt", name=f"d_attn_act_{l}")
        grads["w_attn_out"][l] = _matmul(s["c"], dyc, mode="tn", name=f"d_w_attn_out_{l}", out_dtype=BF16)
        rider = _Exchange([_to_dest_major(n, grads[n][l]) for n in MATMUL_WEIGHTS[1:]], False)
        dq, dk, dv, dga, *arrived = _attn_bwd(s["qkv"], s["o"], s["gmga"], dc, name=f"attn_bwd_{l}", rider=rider)
        for n, p in zip(MATMUL_WEIGHTS[1:], arrived):
            parts[n][l] = p
        dc2gc, dcw, dcvec = _conv_bwd(s["c2gc"], db, conv_full[l], s["cb"], s["lg"], s["lb"], name=f"conv_bwd_{l}")
        dpg, dpw, dpvec = _pool_bwd(s["pg"], da, s["pw"], s["pb"], s["ps"], name=f"pool_bwd_{l}")
        grads["conv_w"][l] = dcw[:CONV_KERNEL]
        grads["conv_b"][l], grads["conv_ln_g"][l], grads["conv_ln_b"][l] = dcvec[0], dcvec[1], dcvec[2]
        grads["pool_w"][l] = dpw
        grads["pool_b"][l] = dpvec[0].reshape(4, POOL_GROUP_DIM)
        grads["pool_scale"][l] = dpvec[1]
        dproj = jnp.concatenate([dpg, dc2gc, dq, dk, dv, dga, dgm], axis=1)
        grads["w_in"][l] = _matmul(s["h"], dproj, mode="tn", name=f"d_w_in_{l}", out_dtype=BF16)
        rider = _Exchange([_to_dest_major("w_in", grads["w_in"][l])], False)
        dh, parts["w_in"][l] = _matmul(dproj, full[l]["w_in"], mode="nt", name=f"d_h_{l}", rider=rider)
        dx, dg_pre = _rms_bwd(s["x"], norm_pre[l].reshape(1, -1), dh, name=f"rms_pre_bwd_{l}", resid=dx)
        grads["norm_pre"][l] = dg_pre.reshape(-1)
        grads["norm_post"][l] = grads["norm_post"][l].reshape(-1)
    small_grads = {n: jnp.stack(grads[n]) for n in SMALL}
    replicated = jnp.concatenate([small_grads[n].reshape(-1) for n in SMALL[1:]])
    small = _pack_rows([_to_dest_major("conv_w", small_grads["conv_w"]).reshape(N_DEV, -1),
                        jnp.broadcast_to(replicated, (N_DEV, replicated.size))], 16).astype(BF16)
    (small_parts,) = _all_to_all([small], name="exchange_grads", same_block=False)

    outs = [dict(), dict(), dict(), dict()]
    for n in MATMUL_WEIGHTS:
        res = _adamw(parts[n], weights[n], mom1[n], mom2[n], name=f"adamw_{n}")
        for o, r in zip(outs, res):
            o[n] = r

    def packed(tree):
        return _pack_rows([tree[n].reshape(-1) for n in SMALL], 16)[None]

    shapes = [(n, weights[n].shape) for n in SMALL]
    res = _adamw([small_parts], packed(weights), packed(mom1), packed(mom2), name="adamw_small")
    for o, r in zip(outs, res):
        o.update(_unpack(r, shapes))
    return (loss, dx[None], *[o[n] for o in outs for n in WEIGHT_ORDER])
```

```python
import functools

import jax
import jax.numpy as jnp
from jax import lax
from jax.experimental import pallas as pl
from jax.experimental.pallas import tpu as pltpu

F32 = jnp.float32
BF16 = jnp.bfloat16

D_MODEL = 1024
DEPTH = 2
POOL_WINDOWS = (2, 4, 8, 16)
POOL_GROUP_DIM = 128
BRANCH_WIDTH = 512
CONV_KERNEL = 31
CONV_HALO = 32
POOL_HALO = 16
HEAD_DIM = 64
HEAD_PAIR = 128
N_HEAD_PAIRS = 4
ATTN_SCALE = 0.125
LOG_F32_ZERO = -104.0
RMS_EPS = 1e-6
LN_EPS = 1e-5
N_DEV = 8
LANES = 128

ADAM_LR = 0.001
ADAM_B1 = 0.9
ADAM_B2 = 0.999
ADAM_EPS = 1e-08
ADAM_WD = 0.01
ADAM_STEP = 10

ROW_TILE = 256
NORM_ROW_TILE = 512
ATTN_BLOCK = 256
ATTN_BLOCKS_PER_STEP = 2
MM_TILE = 1024
MM_K_TILE = 1280
ADAM_TILE_ELEMS = 256 * 1024
VMEM_LIMIT = 48 * 1024 * 1024

MESH = pl.DeviceIdType.MESH


def _params(n_axes):
    return pltpu.CompilerParams(dimension_semantics=("arbitrary",) * n_axes, vmem_limit_bytes=VMEM_LIMIT)


def _tile(n, pref):
    if n <= pref:
        return n
    t = (pref // LANES) * LANES
    while n % t:
        t -= LANES
    return t


def _dot(a, b):
    return jnp.dot(a, b, preferred_element_type=F32)


def _dot_nt(a, b):
    return lax.dot_general(a, b, (((1,), (1,)), ((), ())), preferred_element_type=F32)


def _dot_tn(a, b):
    return lax.dot_general(a, b, (((0,), (0,)), ((), ())), preferred_element_type=F32)


def _sigmoid(x):
    return 1.0 / (1.0 + jnp.exp(-x))


def _silu_grad(x, s):
    return s * (1.0 + x * (1.0 - s))


def _matmul(a, b, *, mode, name, out_dtype=F32, rider=None):
    if mode == "nn":
        (m, k), n = a.shape, b.shape[1]
    elif mode == "nt":
        (m, k), n = a.shape, b.shape[0]
    else:
        (k, m), n = a.shape, b.shape[1]
    tm, tn, tk = _tile(m, MM_TILE), _tile(n, MM_TILE), _tile(k, MM_K_TILE)
    nk = k // tk
    grid = (m // tm, n // tn, nk)
    dot = {"nn": _dot, "nt": _dot_nt, "tn": _dot_tn}[mode]
    a_spec = pl.BlockSpec((tk, tm), lambda i, j, kk: (kk, i)) if mode == "tn" else pl.BlockSpec((tm, tk), lambda i, j, kk: (i, kk))
    b_spec = pl.BlockSpec((tn, tk), lambda i, j, kk: (j, kk)) if mode == "nt" else pl.BlockSpec((tk, tn), lambda i, j, kk: (kk, j))
    o_spec = pl.BlockSpec((tm, tn), lambda i, j, kk: (i, j))

    def body(*refs):
        (a_ref, b_ref), (o_ref,), scratch, finish_rider = _ride(rider, 2, 1, refs, grid)
        part = dot(a_ref[...], b_ref[...])
        if nk == 1:
            o_ref[...] = part.astype(out_dtype)
        else:
            scr = scratch[0]
            kk = pl.program_id(2)

            @pl.when(kk == 0)
            def _():
                scr[...] = part

            @pl.when(kk > 0)
            def _():
                scr[...] += part

            @pl.when(kk == nk - 1)
            def _():
                o_ref[...] = scr[...].astype(out_dtype)

        finish_rider()

    extra = rider or _NO_RIDER
    res = pl.pallas_call(
        body, name=name, grid=grid,
        in_specs=[a_spec, b_spec] + extra.specs, out_specs=[o_spec] + extra.specs,
        out_shape=[jax.ShapeDtypeStruct((m, n), out_dtype)] + extra.out_shape,
        scratch_shapes=([pltpu.VMEM((tm, tn), F32)] if nk > 1 else []) + extra.scratch_shapes,
        compiler_params=_params(3),
    )(a, b, *extra.srcs)
    return res if rider is not None else res[0]


def _rms_fwd(x, g, *, name, resid=None):
    t = x.shape[0]
    tm = _tile(t, NORM_ROW_TILE)
    row = pl.BlockSpec((tm, D_MODEL), lambda i: (i, 0))
    vec = pl.BlockSpec((1, D_MODEL), lambda i: (0, 0))
    has_resid = resid is not None

    def body(*refs):
        x_ref, g_ref = refs[0], refs[1]
        o_ref = refs[-1]
        xv = x_ref[...]
        y = xv * lax.rsqrt(jnp.mean(xv * xv, axis=-1, keepdims=True) + RMS_EPS) * g_ref[...]
        if has_resid:
            o_ref[...] = refs[2][...] + y
        else:
            o_ref[...] = y.astype(BF16)

    return pl.pallas_call(
        body, name=name, grid=(t // tm,),
        in_specs=[row, vec] + ([row] if has_resid else []), out_specs=row,
        out_shape=jax.ShapeDtypeStruct((t, D_MODEL), F32 if has_resid else BF16),
        compiler_params=_params(1),
    )(*((x, g) + ((resid,) if has_resid else ())))


def _rms_bwd(xin, g, dy, *, name, resid=None):
    t = xin.shape[0]
    tm = _tile(t, NORM_ROW_TILE)
    row = pl.BlockSpec((tm, D_MODEL), lambda i: (i, 0))
    vec = pl.BlockSpec((1, D_MODEL), lambda i: (0, 0))
    has_resid = resid is not None
    out_dtype = F32 if has_resid else BF16

    def body(*refs):
        x_ref, g_ref, dy_ref = refs[0], refs[1], refs[2]
        dx_ref, dg_ref = refs[-2], refs[-1]
        xv, dyv = x_ref[...], dy_ref[...]
        r = lax.rsqrt(jnp.mean(xv * xv, axis=-1, keepdims=True) + RMS_EPS)
        a = dyv * g_ref[...]
        dx = r * a - xv * (r * r * r) * jnp.mean(a * xv, axis=-1, keepdims=True)
        if has_resid:
            dx = dx + refs[3][...]
        dx_ref[...] = dx.astype(out_dtype)
        part = jnp.sum(dyv * xv * r, axis=0, keepdims=True)

        @pl.when(pl.program_id(0) == 0)
        def _():
            dg_ref[...] = part

        @pl.when(pl.program_id(0) > 0)
        def _():
            dg_ref[...] += part

    return pl.pallas_call(
        body, name=name, grid=(t // tm,),
        in_specs=[row, vec, row] + ([row] if has_resid else []), out_specs=[row, vec],
        out_shape=[jax.ShapeDtypeStruct((t, D_MODEL), out_dtype), jax.ShapeDtypeStruct((1, D_MODEL), F32)],
        compiler_params=_params(1),
    )(*((xin, g, dy) + ((resid,) if has_resid else ())))


def _loss_head(x, target, *, name):
    t = x.shape[0]
    tm = _tile(t, NORM_ROW_TILE)
    row = pl.BlockSpec((tm, D_MODEL), lambda i: (i, 0))
    acc = pl.BlockSpec((8, LANES), lambda i: (0, 0))

    def body(x_ref, t_ref, l_ref, dx_ref):
        diff = x_ref[...] - t_ref[...]
        dx_ref[...] = diff * (1.0 / D_MODEL)
        part = 0.5 * jnp.sum(jnp.mean(diff * diff, axis=-1, keepdims=True), axis=0, keepdims=True)

        @pl.when(pl.program_id(0) == 0)
        def _():
            l_ref[...] = jnp.zeros((8, LANES), F32) + part

        @pl.when(pl.program_id(0) > 0)
        def _():
            l_ref[...] += part

    return pl.pallas_call(
        body, name=name, grid=(t // tm,),
        in_specs=[row, row], out_specs=[acc, row],
        out_shape=[jax.ShapeDtypeStruct((8, LANES), F32), jax.ShapeDtypeStruct((t, D_MODEL), F32)],
        compiler_params=_params(1),
    )(x, target)


def _window_sum(ext, n_doublings, forward):
    rows = ext.shape[0]
    s, sh = ext, 1
    for _ in range(n_doublings):
        s = s + pltpu.roll(s, sh if forward else rows - sh, 0)
        sh *= 2
    return s


def _pool_fwd(pg, pool_w, pool_b, pool_scale, *, name):
    t = pg.shape[0]
    tm = _tile(t, ROW_TILE)

    def body(pg_ref, w_ref, b_ref, s_ref, o_ref, halo):
        i = pl.program_id(0)

        @pl.when(i == 0)
        def _():
            halo[...] = jnp.zeros_like(halo)

        p = pg_ref[:, :BRANCH_WIDTH]
        gate = pg_ref[:, BRANCH_WIDTH:]
        ext = jnp.concatenate([halo[...], p], axis=0)
        pos = i * tm + lax.broadcasted_iota(jnp.int32, (tm, 1), 0)
        outs = []
        for g, w in enumerate(POOL_WINDOWS):
            cols = slice(g * POOL_GROUP_DIM, (g + 1) * POOL_GROUP_DIM)
            cnt = jnp.minimum(pos + 1, w).astype(F32)
            d = _window_sum(ext[:, cols], g + 1, True)[POOL_HALO:] / cnt - p[:, cols]
            y = (_dot(d.astype(BF16), w_ref[g]) + b_ref[:, cols]) * s_ref[:, cols]
            gg = gate[:, cols]
            outs.append(y * (gg * _sigmoid(gg)))
        o_ref[...] = jnp.concatenate(outs, axis=1).astype(BF16)
        halo[...] = p[tm - POOL_HALO:, :]

    vec = pl.BlockSpec((1, BRANCH_WIDTH), lambda i: (0, 0))
    return pl.pallas_call(
        body, name=name, grid=(t // tm,),
        in_specs=[pl.BlockSpec((tm, 2 * BRANCH_WIDTH), lambda i: (i, 0)),
                  pl.BlockSpec((4, POOL_GROUP_DIM, POOL_GROUP_DIM), lambda i: (0, 0, 0)), vec, vec],
        out_specs=pl.BlockSpec((tm, BRANCH_WIDTH), lambda i: (i, 0)),
        out_shape=jax.ShapeDtypeStruct((t, BRANCH_WIDTH), BF16),
        scratch_shapes=[pltpu.VMEM((POOL_HALO, BRANCH_WIDTH), F32)],
        compiler_params=_params(1),
    )(pg, pool_w, pool_b, pool_scale)


def _pool_bwd(pg, d_out, pool_w, pool_b, pool_scale, *, name):
    t = pg.shape[0]
    tm = _tile(t, ROW_TILE)
    nt = t // tm
    halo_per_tile = tm // POOL_HALO

    def body(pg_ref, halo_ref, do_ref, w_ref, b_ref, s_ref, dpg_ref, dw_ref, dvec_ref, carry):
        i = pl.program_id(0)
        ri = nt - 1 - i

        @pl.when(i == 0)
        def _():
            carry[...] = jnp.zeros_like(carry)
            dw_ref[...] = jnp.zeros_like(dw_ref)
            dvec_ref[...] = jnp.zeros_like(dvec_ref)

        p = pg_ref[:, :BRANCH_WIDTH]
        gate = pg_ref[:, BRANCH_WIDTH:]
        hp = jnp.where(ri > 0, halo_ref[:, :BRANCH_WIDTH], 0.0)
        ext = jnp.concatenate([hp, p], axis=0)
        pos = ri * tm + lax.broadcasted_iota(jnp.int32, (tm, 1), 0)
        dps, dgs, dbs, dss = [], [], [], []
        for g, w in enumerate(POOL_WINDOWS):
            cols = slice(g * POOL_GROUP_DIM, (g + 1) * POOL_GROUP_DIM)
            cnt = jnp.minimum(pos + 1, w).astype(F32)
            d = (_window_sum(ext[:, cols], g + 1, True)[POOL_HALO:] / cnt - p[:, cols]).astype(BF16)
            y1 = _dot(d, w_ref[g]) + b_ref[:, cols]
            scale = s_ref[:, cols]
            y2 = y1 * scale
            gg = gate[:, cols]
            sg = _sigmoid(gg)
            do = do_ref[:, cols]
            dy2 = do * (gg * sg)
            dgs.append(do * y2 * _silu_grad(gg, sg))
            dss.append(jnp.sum(dy2 * y1, axis=0, keepdims=True))
            dy1 = dy2 * scale
            dbs.append(jnp.sum(dy1, axis=0, keepdims=True))
            dy1b = dy1.astype(BF16)
            dw_ref[g] += _dot_tn(d, dy1b)
            dd = _dot_nt(dy1b, w_ref[g])
            dpool = dd / cnt
            dext = jnp.concatenate([dpool, carry[:, cols]], axis=0)
            dps.append(_window_sum(dext, g + 1, False)[:tm] - dd)
            carry[:, cols] = dpool[:POOL_HALO]
        dpg_ref[...] = jnp.concatenate(dps + dgs, axis=1).astype(BF16)
        dvec_ref[0:1, :] += jnp.concatenate(dbs, axis=1)
        dvec_ref[1:2, :] += jnp.concatenate(dss, axis=1)

    vec = pl.BlockSpec((1, BRANCH_WIDTH), lambda i: (0, 0))
    wspec = pl.BlockSpec((4, POOL_GROUP_DIM, POOL_GROUP_DIM), lambda i: (0, 0, 0))
    return pl.pallas_call(
        body, name=name, grid=(nt,),
        in_specs=[pl.BlockSpec((tm, 2 * BRANCH_WIDTH), lambda i: (nt - 1 - i, 0)),
                  pl.BlockSpec((POOL_HALO, 2 * BRANCH_WIDTH), lambda i: (jnp.maximum((nt - 1 - i) * halo_per_tile - 1, 0), 0)),
                  pl.BlockSpec((tm, BRANCH_WIDTH), lambda i: (nt - 1 - i, 0)), wspec, vec, vec],
        out_specs=[pl.BlockSpec((tm, 2 * BRANCH_WIDTH), lambda i: (nt - 1 - i, 0)), wspec,
                   pl.BlockSpec((8, BRANCH_WIDTH), lambda i: (0, 0))],
        out_shape=[jax.ShapeDtypeStruct((t, 2 * BRANCH_WIDTH), BF16),
                   jax.ShapeDtypeStruct((4, POOL_GROUP_DIM, POOL_GROUP_DIM), F32),
                   jax.ShapeDtypeStruct((8, BRANCH_WIDTH), F32)],
        scratch_shapes=[pltpu.VMEM((POOL_HALO, BRANCH_WIDTH), F32)],
        compiler_params=_params(1),
    )(pg, pg, d_out, pool_w, pool_b, pool_scale)


CONV_TILE_ROWS = 64


def _for_conv_tiles(tm, fn):
    def step(it, carry):
        rows = pl.ds(pl.multiple_of(it * CONV_TILE_ROWS, CONV_TILE_ROWS), CONV_TILE_ROWS)
        for c in range(0, BRANCH_WIDTH, LANES):
            fn(rows, slice(c, c + LANES))
        return carry

    lax.fori_loop(0, tm // CONV_TILE_ROWS, step, 0)


def _sublane_shifts(shifted_ref, x, direction):
    rows = x.shape[0]
    shifted_ref[0] = x
    for b in range(1, 8):
        shifted_ref[b] = pltpu.roll(x, b if direction > 0 else rows - b, 0)


CONV_REACH = 8 * ((CONV_KERNEL - 1) // 8)


def _tap_tiles(shifted_ref, base, rows, cols, direction):
    for b in range(8):
        lo = pl.multiple_of(base + rows.start - (CONV_REACH if direction > 0 else 0), 8)
        window = shifted_ref[b, pl.ds(lo, CONV_TILE_ROWS + CONV_REACH), cols]
        for a in range((CONV_KERNEL - 1 - b) // 8 + 1):
            off = CONV_REACH - 8 * a if direction > 0 else 8 * a
            yield 8 * a + b, window[off:off + CONV_TILE_ROWS]


def _tap_sum(shifted_ref, w_ref, base, rows, cols, direction):
    acc = None
    for j, tile in _tap_tiles(shifted_ref, base, rows, cols, direction):
        k = CONV_KERNEL - 1 - j
        term = w_ref[k:k + 1, cols] * tile
        acc = term if acc is None else acc + term
    return acc


def _causal_conv(ext8_ref, w_ref, cv_ref, tm):
    def tile(rows, cols):
        cv_ref[rows, cols] = _tap_sum(ext8_ref, w_ref, CONV_HALO, rows, cols, 1)

    _for_conv_tiles(tm, tile)


def _conv_fwd(c2gc, conv_w, conv_b, ln_g, ln_b, *, name):
    t = c2gc.shape[0]
    tm = _tile(t, ROW_TILE)

    def body(c_ref, w_ref, cb_ref, g_ref, b_ref, o_ref, halo, ext8_ref, cv_ref):
        @pl.when(pl.program_id(0) == 0)
        def _():
            halo[...] = jnp.zeros_like(halo)

        u = c_ref[:, :BRANCH_WIDTH] * _sigmoid(c_ref[:, BRANCH_WIDTH:2 * BRANCH_WIDTH])
        gate = c_ref[:, 2 * BRANCH_WIDTH:]
        _sublane_shifts(ext8_ref, jnp.concatenate([halo[...], u], axis=0), 1)
        halo[...] = u[tm - CONV_HALO:, :]
        _causal_conv(ext8_ref, w_ref, cv_ref, tm)
        cv = cv_ref[...] + cb_ref[...]
        mu = jnp.mean(cv, axis=-1, keepdims=True)
        xc = cv - mu
        var = jnp.mean(xc * xc, axis=-1, keepdims=True)
        ln = xc * lax.rsqrt(var + LN_EPS) * g_ref[...] + b_ref[...]
        o_ref[...] = (ln * _sigmoid(ln) * (gate * _sigmoid(gate))).astype(BF16)

    vec = pl.BlockSpec((1, BRANCH_WIDTH), lambda i: (0, 0))
    return pl.pallas_call(
        body, name=name, grid=(t // tm,),
        in_specs=[pl.BlockSpec((tm, 3 * BRANCH_WIDTH), lambda i: (i, 0)),
                  pl.BlockSpec((CONV_HALO, BRANCH_WIDTH), lambda i: (0, 0)), vec, vec, vec],
        out_specs=pl.BlockSpec((tm, BRANCH_WIDTH), lambda i: (i, 0)),
        out_shape=jax.ShapeDtypeStruct((t, BRANCH_WIDTH), BF16),
        scratch_shapes=[pltpu.VMEM((CONV_HALO, BRANCH_WIDTH), F32), pltpu.VMEM((8, tm + CONV_HALO, BRANCH_WIDTH), F32),
                        pltpu.VMEM((tm, BRANCH_WIDTH), F32)],
        compiler_params=_params(1),
    )(c2gc, conv_w, conv_b, ln_g, ln_b)


def _conv_bwd(c2gc, d_out, conv_w, conv_b, ln_g, ln_b, *, name):
    t = c2gc.shape[0]
    tm = _tile(t, ROW_TILE)
    nt = t // tm
    halo_per_tile = tm // CONV_HALO

    def body(c_ref, halo_ref, do_ref, w_ref, cb_ref, g_ref, b_ref, dc_ref, dw_ref, dvec_ref,
             carry, ext8_ref, cv_ref, dext8_ref, du_ref, dw_acc):
        i = pl.program_id(0)
        ri = nt - 1 - i

        @pl.when(i == 0)
        def _():
            carry[...] = jnp.zeros_like(carry)
            dw_acc[...] = jnp.zeros_like(dw_acc)
            dvec_ref[...] = jnp.zeros_like(dvec_ref)

        a = c_ref[:, :BRANCH_WIDTH]
        sb = _sigmoid(c_ref[:, BRANCH_WIDTH:2 * BRANCH_WIDTH])
        gate = c_ref[:, 2 * BRANCH_WIDTH:]
        hu = halo_ref[:, :BRANCH_WIDTH] * _sigmoid(halo_ref[:, BRANCH_WIDTH:2 * BRANCH_WIDTH])
        _sublane_shifts(ext8_ref, jnp.concatenate([jnp.where(ri > 0, hu, 0.0), a * sb], axis=0), 1)
        _causal_conv(ext8_ref, w_ref, cv_ref, tm)
        cv = cv_ref[...] + cb_ref[...]
        mu = jnp.mean(cv, axis=-1, keepdims=True)
        xc = cv - mu
        rs = lax.rsqrt(jnp.mean(xc * xc, axis=-1, keepdims=True) + LN_EPS)
        n = xc * rs
        ln = n * g_ref[...] + b_ref[...]
        sl = _sigmoid(ln)
        sgate = _sigmoid(gate)
        do = do_ref[...]
        dgate = do * (ln * sl) * _silu_grad(gate, sgate)
        dln = do * (gate * sgate) * _silu_grad(ln, sl)
        dn = dln * g_ref[...]
        dcv = rs * (dn - jnp.mean(dn, axis=-1, keepdims=True) - n * jnp.mean(dn * n, axis=-1, keepdims=True))
        dvec_ref[0:1, :] += jnp.sum(dcv, axis=0, keepdims=True)
        dvec_ref[1:2, :] += jnp.sum(dln * n, axis=0, keepdims=True)
        dvec_ref[2:3, :] += jnp.sum(dln, axis=0, keepdims=True)
        _sublane_shifts(dext8_ref, jnp.concatenate([dcv, carry[...]], axis=0), -1)
        carry[...] = dcv[:CONV_HALO]
        def tile(rows, cols):
            du_ref[rows, cols] = _tap_sum(dext8_ref, w_ref, 0, rows, cols, -1)
            d_tile = dext8_ref[0, rows, cols]
            for j, u_tile in _tap_tiles(ext8_ref, CONV_HALO, rows, cols, 1):
                prod = d_tile * u_tile
                part = prod[0:8]
                for q in range(8, CONV_TILE_ROWS, 8):
                    part = part + prod[q:q + 8]
                dw_acc[CONV_KERNEL - 1 - j, :, cols] += part

        _for_conv_tiles(tm, tile)
        du = du_ref[...]
        dc_ref[...] = jnp.concatenate([du * sb, du * a * sb * (1.0 - sb), dgate], axis=1).astype(BF16)

        @pl.when(i == nt - 1)
        def _():
            dw_ref[...] = jnp.sum(dw_acc[...], axis=1)

    vec = pl.BlockSpec((1, BRANCH_WIDTH), lambda i: (0, 0))
    wspec = pl.BlockSpec((CONV_HALO, BRANCH_WIDTH), lambda i: (0, 0))
    return pl.pallas_call(
        body, name=name, grid=(nt,),
        in_specs=[pl.BlockSpec((tm, 3 * BRANCH_WIDTH), lambda i: (nt - 1 - i, 0)),
                  pl.BlockSpec((CONV_HALO, 3 * BRANCH_WIDTH), lambda i: (jnp.maximum((nt - 1 - i) * halo_per_tile - 1, 0), 0)),
                  pl.BlockSpec((tm, BRANCH_WIDTH), lambda i: (nt - 1 - i, 0)), wspec, vec, vec, vec],
        out_specs=[pl.BlockSpec((tm, 3 * BRANCH_WIDTH), lambda i: (nt - 1 - i, 0)), wspec,
                   pl.BlockSpec((8, BRANCH_WIDTH), lambda i: (0, 0))],
        out_shape=[jax.ShapeDtypeStruct((t, 3 * BRANCH_WIDTH), BF16),
                   jax.ShapeDtypeStruct((CONV_HALO, BRANCH_WIDTH), F32),
                   jax.ShapeDtypeStruct((8, BRANCH_WIDTH), F32)],
        scratch_shapes=[pltpu.VMEM((CONV_HALO, BRANCH_WIDTH), F32),
                        pltpu.VMEM((8, tm + CONV_HALO, BRANCH_WIDTH), F32), pltpu.VMEM((tm, BRANCH_WIDTH), F32),
                        pltpu.VMEM((8, tm + CONV_HALO, BRANCH_WIDTH), F32), pltpu.VMEM((tm, BRANCH_WIDTH), F32),
                        pltpu.VMEM((CONV_HALO, 8, BRANCH_WIDTH), F32)],
        compiler_params=_params(1),
    )(c2gc, c2gc, d_out, conv_w, conv_b, ln_g, ln_b)


def _merge_specs(tm):
    gates = [pl.BlockSpec((tm, D_MODEL), functools.partial(lambda i, b: (i, b), b=b)) for b in range(3)]
    acts = [pl.BlockSpec((tm, BRANCH_WIDTH), lambda i: (i, 0))] * 3
    weights = [pl.BlockSpec((BRANCH_WIDTH, D_MODEL), lambda i: (0, 0))] * 3
    return gates + acts + weights


def _merge_fwd(gmga, acts, weights, *, name):
    t = gmga.shape[0]
    tm = _tile(t, ROW_TILE)

    def body(g0, g1, g2, a0, a1, a2, w0, w1, w2, o_ref):
        m = None
        for g_ref, a_ref, w_ref in ((g0, a0, w0), (g1, a1, w1), (g2, a2, w2)):
            term = _sigmoid(g_ref[...]) * _dot(a_ref[...], w_ref[...])
            m = term if m is None else m + term
        o_ref[...] = m.astype(BF16)

    return pl.pallas_call(
        body, name=name, grid=(t // tm,),
        in_specs=_merge_specs(tm), out_specs=pl.BlockSpec((tm, D_MODEL), lambda i: (i, 0)),
        out_shape=jax.ShapeDtypeStruct((t, D_MODEL), BF16),
        compiler_params=_params(1),
    )(gmga, gmga, gmga, *acts, *weights)


def _merge_bwd(dout, w_o, gmga, acts, weights, *, name):
    t = gmga.shape[0]
    tm = _tile(t, ROW_TILE)
    row = pl.BlockSpec((tm, D_MODEL), lambda i: (i, 0))
    wide = pl.BlockSpec((tm, 3 * D_MODEL), lambda i: (i, 0))
    act = pl.BlockSpec((tm, BRANCH_WIDTH), lambda i: (i, 0))

    def body(dout_ref, wo_ref, g0, g1, g2, a0, a1, a2, w0, w1, w2, d0, d1, d2, dg_ref, da0, da1, da2):
        dmv = _dot_nt(dout_ref[...], wo_ref[...])
        for k, (g_ref, a_ref, w_ref, dy_ref, da_ref) in enumerate(
                ((g0, a0, w0, d0, da0), (g1, a1, w1, d1, da1), (g2, a2, w2, d2, da2))):
            s = _sigmoid(g_ref[...])
            dy = (dmv * s).astype(BF16)
            dy_ref[...] = dy
            y = _dot(a_ref[...], w_ref[...])
            dg_ref[:, k * D_MODEL:(k + 1) * D_MODEL] = (dmv * y * s * (1.0 - s)).astype(BF16)
            da_ref[...] = _dot_nt(dy, w_ref[...])

    return pl.pallas_call(
        body, name=name, grid=(t // tm,),
        in_specs=[row, pl.BlockSpec((D_MODEL, D_MODEL), lambda i: (0, 0))] + _merge_specs(tm),
        out_specs=[row, row, row, wide, act, act, act],
        out_shape=[jax.ShapeDtypeStruct((t, D_MODEL), BF16)] * 3 + [jax.ShapeDtypeStruct((t, 3 * D_MODEL), BF16)]
        + [jax.ShapeDtypeStruct((t, BRANCH_WIDTH), F32)] * 3,
        compiler_params=_params(1),
    )(dout, w_o, gmga, gmga, gmga, *acts, *weights)


GA_BLOCK = 3 * D_MODEL // HEAD_PAIR


def _split_heads(x, lane_is_first):
    zero = jnp.zeros_like(x)
    return jnp.concatenate([jnp.where(lane_is_first, x, zero), jnp.where(lane_is_first, zero, x)], axis=0)


def _side_by_side(x, rows):
    return jnp.concatenate([x[:rows], x[rows:]], axis=1)


def _split_bf16(x):
    hi = x.astype(BF16)
    return hi, (x - hi.astype(F32)).astype(BF16)


def _scores(qcat, kblk, mask):
    z = _dot_nt(qcat, kblk)
    e = jnp.exp(-jnp.abs(z))
    sp = jnp.maximum(z, 0.0) + jnp.log(1.0 + e)
    l1m = -sp
    if mask is not None:
        l1m = jnp.where(mask, l1m, 0.0)
    inv = 1.0 / (1.0 + e)
    pos = z >= 0.0
    return z - sp, l1m, jnp.where(pos, 1.0, e) * inv, jnp.where(pos, e, 1.0) * inv


def _attn_consts(blk):
    lane_is_first = lax.broadcasted_iota(jnp.int32, (1, HEAD_PAIR), 1) < HEAD_DIM
    r = lax.broadcasted_iota(jnp.int32, (blk, blk), 0)
    c = lax.broadcasted_iota(jnp.int32, (blk, blk), 1)
    after = (r > c).astype(BF16)
    from_here = (r >= c).astype(BF16)
    qrow = lax.broadcasted_iota(jnp.int32, (2 * blk, blk), 0)
    qrow = jnp.where(qrow >= blk, qrow - blk, qrow)
    causal = lax.broadcasted_iota(jnp.int32, (2 * blk, blk), 1) < qrow
    return lane_is_first, after, from_here, causal


def _while_mass_left(qi, carry, block):
    def alive(c):
        return jnp.max(c[0]) > LOG_F32_ZERO

    def cond(state):
        return jnp.logical_and(state[0] < qi, state[1])

    def step(state):
        new = block(qi - 1 - state[0], state[2])
        return state[0] + 1, alive(new), new

    return lax.while_loop(cond, step, (jnp.int32(0), alive(carry), carry))[2]


def _ride(rider, n_in, n_out, refs, grid):
    if rider is None:
        return refs[:n_in], refs[n_in:n_in + n_out], refs[n_in + n_out:], lambda: None
    n = rider.n
    ins, srcs = refs[:n_in], refs[n_in:n_in + n]
    outs, dsts = refs[n_in + n:n_in + n + n_out], refs[n_in + n + n_out:n_in + 2 * n + n_out]
    rest = refs[n_in + 2 * n + n_out:]
    scratch, sems = rest[:len(rest) - 3], rest[len(rest) - 3:]
    step, n_steps = 0, 1
    for axis, size in enumerate(grid):
        step, n_steps = step * size + pl.program_id(axis), n_steps * size

    @pl.when(step == 0)
    def _():
        rider.start(srcs, dsts, sems)

    def finish():
        @pl.when(step == n_steps - 1)
        def _():
            rider.finish(srcs, dsts, sems)

    return ins, outs, scratch, finish


def _attn_fwd(qkv, gmga, *, name, rider=None):
    t = qkv.shape[0]
    blk, step_rows = _tile(t, ATTN_BLOCK), _tile(t, ATTN_BLOCK * ATTN_BLOCKS_PER_STEP)
    nq = t // step_rows

    def body(*refs):
        (q_ref, k_ref, v_ref, ga_ref), (o_ref, cv_ref), _, finish_rider = _ride(rider, 4, 2, refs, (N_HEAD_PAIRS, nq))
        lane_is_first, after, _, causal = _attn_consts(blk)

        for sub in range(step_rows // blk):
            mine = slice(sub * blk, (sub + 1) * blk)
            qi = pl.program_id(1) * (step_rows // blk) + sub
            qcat = _split_heads(q_ref[mine, :], lane_is_first)

            def block(kb, carry, mask, qcat=qcat):
                run, acc = carry
                rows = pl.ds(pl.multiple_of(kb * blk, blk), blk)
                lb, l1m, _, _ = _scores(qcat, k_ref[rows, :], mask)
                hi, lo = _split_bf16(l1m)
                w = jnp.exp(lb + (_dot(hi, after) + _dot(lo, after) + run))
                if mask is not None:
                    w = jnp.where(mask, w, 0.0)
                vcat = _split_heads(v_ref[rows, :], lane_is_first)
                acc = acc + _dot(_side_by_side(w.astype(BF16), blk), vcat)
                return run + jnp.sum(l1m, axis=-1, keepdims=True), acc

            carry = block(qi, (jnp.zeros((2 * blk, 1), F32), jnp.zeros((blk, HEAD_PAIR), F32)), causal)
            _, o = _while_mass_left(qi, carry, lambda kb, c, block=block: block(kb, c, None))
            o_ref[mine, :] = o
            ga = ga_ref[mine, :]
            cv_ref[mine, :] = (o * (ga * _sigmoid(ga))).astype(BF16)
        finish_rider()

    qspec = pl.BlockSpec((step_rows, HEAD_PAIR), lambda p, i: (i, p))
    extra = rider or _NO_RIDER
    return pl.pallas_call(
        body, name=name, grid=(N_HEAD_PAIRS, nq),
        in_specs=[qspec,
                  pl.BlockSpec((t, HEAD_PAIR), lambda p, i: (0, N_HEAD_PAIRS + p)),
                  pl.BlockSpec((t, HEAD_PAIR), lambda p, i: (0, 2 * N_HEAD_PAIRS + p)),
                  pl.BlockSpec((step_rows, HEAD_PAIR), lambda p, i: (i, GA_BLOCK + p))] + extra.specs,
        out_specs=[qspec, qspec] + extra.specs,
        out_shape=[jax.ShapeDtypeStruct((t, BRANCH_WIDTH), F32), jax.ShapeDtypeStruct((t, BRANCH_WIDTH), BF16)] + extra.out_shape,
        scratch_shapes=extra.scratch_shapes,
        compiler_params=_params(2),
    )(qkv, qkv, qkv, gmga, *extra.srcs)


def _attn_bwd(qkv, o, gmga, dcv, *, name, rider=None):
    t = qkv.shape[0]
    blk, step_rows = _tile(t, ATTN_BLOCK), _tile(t, ATTN_BLOCK * ATTN_BLOCKS_PER_STEP)
    nq = t // step_rows

    def body(*refs):
        ins, outs, (dk_acc, dv_acc), finish_rider = _ride(rider, 6, 4, refs, (N_HEAD_PAIRS, nq))
        q_ref, k_ref, v_ref, o_ref, ga_ref, dcv_ref = ins
        dq_ref, dk_ref, dv_ref, dga_ref = outs
        lane_is_first, after, from_here, causal = _attn_consts(blk)

        @pl.when(pl.program_id(1) == 0)
        def _():
            dk_acc[...] = jnp.zeros_like(dk_acc)
            dv_acc[...] = jnp.zeros_like(dv_acc)

        for sub in range(step_rows // blk):
            mine = slice(sub * blk, (sub + 1) * blk)
            qi = pl.program_id(1) * (step_rows // blk) + sub
            ga, ov, dcvv = ga_ref[mine, :], o_ref[mine, :], dcv_ref[mine, :]
            sg = _sigmoid(ga)
            dob = (dcvv * (ga * sg)).astype(BF16)
            dga_ref[mine, :] = (dcvv * ov * _silu_grad(ga, sg)).astype(BF16)
            gt = dob.astype(F32) * ov
            g_total = jnp.concatenate(
                [jnp.sum(jnp.where(lane_is_first, gt, 0.0), axis=-1, keepdims=True),
                 jnp.sum(jnp.where(lane_is_first, 0.0, gt), axis=-1, keepdims=True)], axis=0)
            qcat = _split_heads(q_ref[mine, :], lane_is_first)
            docat = _split_heads(dob, lane_is_first)

            def block(kb, carry, mask, g_total=g_total, qcat=qcat, docat=docat):
                run, g_run, dq = carry
                rows = pl.ds(pl.multiple_of(kb * blk, blk), blk)
                kblk = k_ref[rows, :]
                lb, l1m, sig, one_m_sig = _scores(qcat, kblk, mask)
                hi, lo = _split_bf16(l1m)
                w = jnp.exp(lb + (_dot(hi, after) + _dot(lo, after) + run))
                if mask is not None:
                    w = jnp.where(mask, w, 0.0)
                wb = w.astype(BF16)
                g = _dot_nt(docat, v_ref[rows, :]) * wb.astype(F32)
                ghi, glo = _split_bf16(g)
                g_before = g_total - g_run - (_dot(ghi, from_here) + _dot(glo, from_here))
                dz = g * one_m_sig - g_before * sig
                if mask is not None:
                    dz = jnp.where(mask, dz, 0.0)
                dzb = dz.astype(BF16)
                dq = dq + _dot(_side_by_side(dzb, blk), _split_heads(kblk, lane_is_first))
                dk_acc[rows, :] += _dot_tn(dzb, qcat)
                dv_acc[rows, :] += _dot_tn(wb, docat)
                return (run + jnp.sum(l1m, axis=-1, keepdims=True), g_run + jnp.sum(g, axis=-1, keepdims=True), dq)

            zero = jnp.zeros((2 * blk, 1), F32)
            carry = block(qi, (zero, zero, jnp.zeros((blk, HEAD_PAIR), F32)), causal)
            _, _, dq = _while_mass_left(qi, carry, lambda kb, c, block=block: block(kb, c, None))
            dq_ref[mine, :] = (dq * ATTN_SCALE).astype(BF16)

        @pl.when(pl.program_id(1) == nq - 1)
        def _():
            dk_ref[...] = dk_acc[...].astype(BF16)
            dv_ref[...] = dv_acc[...].astype(BF16)

        finish_rider()

    qspec = pl.BlockSpec((step_rows, HEAD_PAIR), lambda p, i: (i, p))
    whole = pl.BlockSpec((t, HEAD_PAIR), lambda p, i: (0, p))
    out = jax.ShapeDtypeStruct((t, BRANCH_WIDTH), BF16)
    extra = rider or _NO_RIDER
    return pl.pallas_call(
        body, name=name, grid=(N_HEAD_PAIRS, nq),
        in_specs=[qspec,
                  pl.BlockSpec((t, HEAD_PAIR), lambda p, i: (0, N_HEAD_PAIRS + p)),
                  pl.BlockSpec((t, HEAD_PAIR), lambda p, i: (0, 2 * N_HEAD_PAIRS + p)),
                  qspec,
                  pl.BlockSpec((step_rows, HEAD_PAIR), lambda p, i: (i, GA_BLOCK + p)),
                  qspec] + extra.specs,
        out_specs=[qspec, whole, whole, qspec] + extra.specs,
        out_shape=[out, out, out, out] + extra.out_shape,
        scratch_shapes=[pltpu.VMEM((t, HEAD_PAIR), F32), pltpu.VMEM((t, HEAD_PAIR), F32)] + extra.scratch_shapes,
        compiler_params=_params(2),
    )(qkv, qkv, qkv, o, gmga, dcv, *extra.srcs)


def _mesh_position():
    x, y, c = lax.axis_index("x"), lax.axis_index("y"), lax.axis_index("c")
    return x, y, c, 4 * x + 2 * y + c


def _flipped(x, y, c, k):
    return (1 - x if k & 4 else x, 1 - y if k & 2 else y, 1 - c if k & 1 else c)


def _all_to_all(srcs, *, name, same_block):
    ex = _Exchange(srcs, same_block)

    def body(*refs):
        ex.start(refs[:ex.n], refs[ex.n:2 * ex.n], refs[2 * ex.n:])
        ex.finish(refs[:ex.n], refs[ex.n:2 * ex.n], refs[2 * ex.n:])

    return pl.pallas_call(
        body, name=name, in_specs=ex.specs, out_specs=ex.specs, out_shape=ex.out_shape,
        scratch_shapes=ex.scratch_shapes,
    )(*srcs)


def _gather_via_sibling(srcs, *, name):
    n = len(srcs)

    def body(*refs):
        src_refs, dst_refs = refs[:n], refs[n:2 * n]
        send_sems, recv_sems, local_sems = refs[2 * n:]
        x, y, c, me = _mesh_position()
        sibling = (x, y, 1 - c)
        chips = [(1 - x, y), (x, 1 - y), (1 - x, 1 - y)]

        def slot(px, py, pc):
            return 4 * px + 2 * py + pc

        def copy(i, k, block, to, from_src=False):
            return pltpu.make_async_remote_copy(
                src_ref=src_refs[i] if from_src else dst_refs[i].at[slot(*block)], dst_ref=dst_refs[i].at[slot(*block)],
                send_sem=send_sems.at[k, i], recv_sem=recv_sems.at[k, i], device_id=to, device_id_type=MESH)

        mine = [pltpu.make_async_copy(src_refs[i], dst_refs[i].at[me], local_sems.at[i]) for i in range(n)]
        first = [copy(i, 0, (x, y, c), sibling, True) for i in range(n)]
        first += [copy(i, 1 + j, (x, y, c), (*chip, c), True) for j, chip in enumerate(chips) for i in range(n)]
        for cp in mine + first:
            cp.start()
        passed = []
        for j, chip in enumerate(chips):
            for i in range(n):
                copy(i, 1 + j, (*chip, c), (x, y, c)).wait_recv()
                passed.append(copy(i, 4 + j, (*chip, c), sibling))
                passed[-1].start()
        for i in range(n):
            copy(i, 0, sibling, (x, y, c)).wait_recv()
            for j, chip in enumerate(chips):
                copy(i, 4 + j, (*chip, 1 - c), (x, y, c)).wait_recv()
        for cp in first + passed:
            cp.wait_send()
        for cp in mine:
            cp.wait()

    spec = [pl.BlockSpec(memory_space=pl.ANY)] * n
    return pl.pallas_call(
        body, name=name, in_specs=spec, out_specs=spec,
        out_shape=[jax.ShapeDtypeStruct((N_DEV,) + tuple(s.shape), s.dtype) for s in srcs],
        scratch_shapes=[pltpu.SemaphoreType.DMA((N_DEV - 1, n)), pltpu.SemaphoreType.DMA((N_DEV - 1, n)),
                        pltpu.SemaphoreType.DMA((n,))],
    )(*srcs)


class _Exchange:
    def __init__(self, srcs, same_block):
        self.srcs, self.same_block, self.n = list(srcs), same_block, len(srcs)
        self.specs = [pl.BlockSpec(memory_space=pl.ANY)] * self.n
        self.out_shape = [jax.ShapeDtypeStruct((N_DEV,) + tuple(s.shape if same_block else s.shape[1:]), s.dtype)
                          for s in self.srcs]
        self.scratch_shapes = [pltpu.SemaphoreType.DMA((N_DEV - 1, self.n)), pltpu.SemaphoreType.DMA((N_DEV - 1, self.n)),
                               pltpu.SemaphoreType.DMA((self.n,))] if self.n else []

    def _copies(self, src_refs, dst_refs, sems, with_arrivals):
        send_sems, recv_sems, local_sems = sems
        x, y, c, me = _mesh_position()

        def outgoing(i, j):
            return src_refs[i] if self.same_block else src_refs[i].at[j]

        def remote(i, k, slot):
            return pltpu.make_async_remote_copy(
                src_ref=outgoing(i, jnp.bitwise_xor(me, k)), dst_ref=dst_refs[i].at[slot],
                send_sem=send_sems.at[k - 1, i], recv_sem=recv_sems.at[k - 1, i],
                device_id=_flipped(x, y, c, k), device_id_type=MESH)

        pairs = [(i, k) for k in range(1, N_DEV) for i in range(self.n)]
        mine = [pltpu.make_async_copy(outgoing(i, me), dst_refs[i].at[me], local_sems.at[i]) for i in range(self.n)]
        sent = [remote(i, k, me) for i, k in pairs]
        arrivals = [remote(i, k, jnp.bitwise_xor(me, k)) for i, k in pairs] if with_arrivals else []
        return mine, sent, arrivals

    def start(self, src_refs, dst_refs, sems):
        mine, sent, _ = self._copies(src_refs, dst_refs, sems, False)
        for cp in mine + sent:
            cp.start()

    def finish(self, src_refs, dst_refs, sems):
        mine, sent, arrivals = self._copies(src_refs, dst_refs, sems, True)
        for cp in arrivals:
            cp.wait_recv()
        for cp in sent:
            cp.wait_send()
        for cp in mine:
            cp.wait()


_NO_RIDER = _Exchange([], True)


def _adamw(parts, w, m, v, *, name):
    layers, rows, cols = w.shape
    assert len(parts) == layers
    tr = rows
    while tr * cols > ADAM_TILE_ELEMS and tr % 32 == 0:
        tr //= 2
    row = pl.BlockSpec((1, tr, cols), lambda l, i: (l, i, 0))

    def body(*refs):
        p_refs = refs[:layers]
        w_ref, m_ref, v_ref, g_ref, d_ref, nm_ref, nv_ref = refs[layers:]
        layer = pl.program_id(0)
        g = None
        for k in range(N_DEV):
            part = p_refs[0][k]
            for l in range(1, layers):
                part = jnp.where(layer == l, p_refs[l][k], part)
            g = part.astype(F32) if g is None else g + part.astype(F32)
        m2 = ADAM_B1 * m_ref[0] + (1.0 - ADAM_B1) * g
        v2 = ADAM_B2 * v_ref[0] + (1.0 - ADAM_B2) * (g * g)
        m_hat = m2 / (1.0 - ADAM_B1 ** ADAM_STEP)
        v_hat = v2 / (1.0 - ADAM_B2 ** ADAM_STEP)
        g_ref[0] = g
        d_ref[0] = -ADAM_LR * (m_hat / (jnp.sqrt(v_hat) + ADAM_EPS) + ADAM_WD * w_ref[0])
        nm_ref[0] = m2
        nv_ref[0] = v2

    out = jax.ShapeDtypeStruct((layers, rows, cols), F32)
    return pl.pallas_call(
        body, name=name, grid=(layers, rows // tr),
        in_specs=[pl.BlockSpec((N_DEV, tr, cols), lambda l, i: (0, i, 0))] * layers + [row, row, row],
        out_specs=[row, row, row, row], out_shape=[out, out, out, out],
        compiler_params=_params(2),
    )(*parts, w, m, v)


MATMUL_WEIGHTS = ("w_in", "w_pool_out", "w_conv_out", "w_attn_out", "w_o")
SMALL = ("conv_w", "norm_pre", "pool_w", "pool_b", "pool_scale", "conv_b", "conv_ln_g", "conv_ln_b", "norm_post")
WEIGHT_ORDER = ("norm_pre", "w_in", "pool_w", "pool_b", "pool_scale", "w_pool_out", "conv_w", "conv_b",
                "conv_ln_g", "conv_ln_b", "w_conv_out", "w_attn_out", "w_o", "norm_post")


def _shard_axis(name):
    return -2 if name == "w_o" else -1


def _pack_rows(flat_parts, row_multiple):
    flat = jnp.concatenate(flat_parts, axis=-1)
    n = flat.shape[-1]
    chunk = row_multiple * LANES
    total = -(-n // chunk) * chunk
    pad = [(0, 0)] * (flat.ndim - 1) + [(0, total - n)]
    return jnp.pad(flat, pad).reshape(flat.shape[:-1] + (total // LANES, LANES))


def _unpack(buf, shapes):
    flat = buf.reshape(-1)
    out, at = {}, 0
    for name, shape in shapes:
        n = 1
        for s in shape:
            n *= s
        out[name] = flat[at:at + n].reshape(shape)
        at += n
    return out


def _to_dest_major(name, full):
    axis = full.ndim + _shard_axis(name)
    n = full.shape[axis] // N_DEV
    return jnp.stack([lax.slice_in_dim(full, d * n, (d + 1) * n, axis=axis) for d in range(N_DEV)])


def _from_source_major(name, gathered):
    return jnp.concatenate([gathered[d] for d in range(N_DEV)], axis=_shard_axis(name))


def kernel(x, norm_pre, w_in, pool_w, pool_b, pool_scale, w_pool_out, conv_w, conv_b, conv_ln_g, conv_ln_b, w_conv_out, w_attn_out, w_o, norm_post, loss_target, m_norm_pre, m_w_in, m_pool_w, m_pool_b, m_pool_scale, m_w_pool_out, m_conv_w, m_conv_b, m_conv_ln_g, m_conv_ln_b, m_w_conv_out, m_w_attn_out, m_w_o, m_norm_post, v_norm_pre, v_w_in, v_pool_w, v_pool_b, v_pool_scale, v_w_pool_out, v_conv_w, v_conv_b, v_conv_ln_g, v_conv_ln_b, v_w_conv_out, v_w_attn_out, v_w_o, v_norm_post):
    weights = dict(norm_pre=norm_pre, w_in=w_in, pool_w=pool_w, pool_b=pool_b, pool_scale=pool_scale,
                   w_pool_out=w_pool_out, conv_w=conv_w, conv_b=conv_b, conv_ln_g=conv_ln_g, conv_ln_b=conv_ln_b,
                   w_conv_out=w_conv_out, w_attn_out=w_attn_out, w_o=w_o, norm_post=norm_post)
    mom1 = dict(norm_pre=m_norm_pre, w_in=m_w_in, pool_w=m_pool_w, pool_b=m_pool_b, pool_scale=m_pool_scale,
                w_pool_out=m_w_pool_out, conv_w=m_conv_w, conv_b=m_conv_b, conv_ln_g=m_conv_ln_g, conv_ln_b=m_conv_ln_b,
                w_conv_out=m_w_conv_out, w_attn_out=m_w_attn_out, w_o=m_w_o, norm_post=m_norm_post)
    mom2 = dict(norm_pre=v_norm_pre, w_in=v_w_in, pool_w=v_pool_w, pool_b=v_pool_b, pool_scale=v_pool_scale,
                w_pool_out=v_w_pool_out, conv_w=v_conv_w, conv_b=v_conv_b, conv_ln_g=v_conv_ln_g, conv_ln_b=v_conv_ln_b,
                w_conv_out=v_w_conv_out, w_attn_out=v_w_attn_out, w_o=v_w_o, norm_post=v_norm_post)
    xs = x[0]
    target = loss_target[0]

    conv_rows = jnp.pad(conv_w, ((0, 0), (0, CONV_HALO - CONV_KERNEL), (0, 0)))
    shards = [[weights[n][l].astype(BF16) for n in MATMUL_WEIGHTS] for l in range(DEPTH)]
    gathered = _gather_via_sibling([shards[0][0], conv_rows], name="gather_weights")
    full = [{"w_in": _from_source_major("w_in", gathered[0])}, None]
    conv_full = _from_source_major("conv_w", gathered[1])

    def in_sections(w):
        return dict(pg=w[:, 0:1024], c2gc=w[:, 1024:2560], q=w[:, 2560:3072], k=w[:, 3072:3584], v=w[:, 3584:4096],
                    gmga=jnp.concatenate([w[:, 4608:7680], w[:, 4096:4608]], axis=1))

    saved = []
    cur = xs
    for l in range(DEPTH):
        sec = in_sections(full[l]["w_in"])
        w_qkv = jnp.concatenate([sec["q"] * ATTN_SCALE, sec["k"], sec["v"]], axis=1)
        pw = pool_w[l].astype(BF16)
        pb, ps = pool_b[l].reshape(1, -1), pool_scale[l].reshape(1, -1)
        cb, lg, lb = conv_b[l].reshape(1, -1), conv_ln_g[l].reshape(1, -1), conv_ln_b[l].reshape(1, -1)
        h = _rms_fwd(cur, norm_pre[l].reshape(1, -1), name=f"rms_pre_fwd_{l}")
        pg = _matmul(h, sec["pg"], mode="nn", name=f"proj_pg_{l}")
        c2gc = _matmul(h, sec["c2gc"], mode="nn", name=f"proj_c2gc_{l}")
        qkv = _matmul(h, w_qkv, mode="nn", name=f"proj_qkv_{l}", out_dtype=BF16)
        if l == 0:
            gmga, *arrived = _matmul(h, sec["gmga"], mode="nn", name=f"proj_gmga_{l}", rider=_Exchange(shards[0][1:], True))
            full[0].update({n: _from_source_major(n, g) for n, g in zip(MATMUL_WEIGHTS[1:], arrived)})
        else:
            gmga = _matmul(h, sec["gmga"], mode="nn", name=f"proj_gmga_{l}")
        a_act = _pool_fwd(pg, pw, pb, ps, name=f"pool_fwd_{l}")
        b_act = _conv_fwd(c2gc, conv_full[l], cb, lg, lb, name=f"conv_fwd_{l}")
        rider = _Exchange(shards[l + 1], True) if l + 1 < DEPTH else None
        o, c_act, *arrived = _attn_fwd(qkv, gmga, name=f"attn_fwd_{l}", rider=rider)
        if rider is not None:
            full[l + 1] = {n: _from_source_major(n, g) for n, g in zip(MATMUL_WEIGHTS, arrived)}
        out_weights = [full[l][n] for n in ("w_pool_out", "w_conv_out", "w_attn_out")]
        mix = _merge_fwd(gmga, [a_act, b_act, c_act], out_weights, name=f"merge_fwd_{l}")
        out = _matmul(mix, full[l]["w_o"], mode="nn", name=f"out_proj_{l}")
        nxt = _rms_fwd(out, norm_post[l].reshape(1, -1), name=f"rms_post_fwd_{l}", resid=cur)
        saved.append(dict(x=cur, h=h, pg=pg, c2gc=c2gc, qkv=qkv, gmga=gmga, a=a_act, b=b_act, c=c_act, o=o,
                          out_weights=out_weights, mix=mix, out=out, pw=pw, pb=pb, ps=ps, cb=cb, lg=lg, lb=lb))
        cur = nxt

    loss_tile, dx = _loss_head(cur, target, name="loss_head")
    loss = lax.psum(loss_tile[0, 0], ("x", "y", "c"))

    grads = {n: [None] * DEPTH for n in WEIGHT_ORDER}
    parts = {n: [None] * DEPTH for n in MATMUL_WEIGHTS}
    for l in reversed(range(DEPTH)):
        s = saved[l]
        dout, grads["norm_post"][l] = _rms_bwd(s["out"], norm_post[l].reshape(1, -1), dx, name=f"rms_post_bwd_{l}")
        grads["w_o"][l] = _matmul(s["mix"], dout, mode="tn", name=f"d_w_o_{l}", out_dtype=BF16)
        dya, dyb, dyc, dgm, da, db, dc = _merge_bwd(dout, full[l]["w_o"], s["gmga"], [s["a"], s["b"], s["c"]],
                                                    s["out_weights"], name=f"merge_bwd_{l}")
        grads["w_pool_out"][l] = _matmul(s["a"], dya, mode="tn", name=f"d_w_pool_out_{l}", out_dtype=BF16)
        grads["w_conv_out"][l] = _matmul(s["b"], dyb, mode="tn", name=f"d_w_conv_out_{l}", out_dtype=BF16)
        grads["w_attn_out"][l] = _matmul(s["c"], dyc, mode="tn", name=f"d_w_attn_out_{l}", out_dtype=BF16)
        rider = _Exchange([_to_dest_major(n, grads[n][l]) for n in MATMUL_WEIGHTS[1:]], False)
        dq, dk, dv, dga, *arrived = _attn_bwd(s["qkv"], s["o"], s["gmga"], dc, name=f"attn_bwd_{l}", rider=rider)
        for n, p in zip(MATMUL_WEIGHTS[1:], arrived):
            parts[n][l] = p
        dc2gc, dcw, dcvec = _conv_bwd(s["c2gc"], db, conv_full[l], s["cb"], s["lg"], s["lb"], name=f"conv_bwd_{l}")
        dpg, dpw, dpvec = _pool_bwd(s["pg"], da, s["pw"], s["pb"], s["ps"], name=f"pool_bwd_{l}")
        grads["conv_w"][l] = dcw[:CONV_KERNEL]
        grads["conv_b"][l], grads["conv_ln_g"][l], grads["conv_ln_b"][l] = dcvec[0], dcvec[1], dcvec[2]
        grads["pool_w"][l] = dpw
        grads["pool_b"][l] = dpvec[0].reshape(4, POOL_GROUP_DIM)
        grads["pool_scale"][l] = dpvec[1]
        dproj = jnp.concatenate([dpg, dc2gc, dq, dk, dv, dga, dgm], axis=1)
        grads["w_in"][l] = _matmul(s["h"], dproj, mode="tn", name=f"d_w_in_{l}", out_dtype=BF16)
        rider = _Exchange([_to_dest_major("w_in", grads["w_in"][l])], False)
        dh, parts["w_in"][l] = _matmul(dproj, full[l]["w_in"], mode="nt", name=f"d_h_{l}", rider=rider)
        dx, dg_pre = _rms_bwd(s["x"], norm_pre[l].reshape(1, -1), dh, name=f"rms_pre_bwd_{l}", resid=dx)
        grads["norm_pre"][l] = dg_pre.reshape(-1)
        grads["norm_post"][l] = grads["norm_post"][l].reshape(-1)
    small_grads = {n: jnp.stack(grads[n]) for n in SMALL}
    replicated = jnp.concatenate([small_grads[n].reshape(-1) for n in SMALL[1:]])
    small = _pack_rows([_to_dest_major("conv_w", small_grads["conv_w"]).reshape(N_DEV, -1),
                        jnp.broadcast_to(replicated, (N_DEV, replicated.size))], 16).astype(BF16)
    (small_parts,) = _all_to_all([small], name="exchange_grads", same_block=False)

    outs = [dict(), dict(), dict(), dict()]
    for n in MATMUL_WEIGHTS:
        res = _adamw(parts[n], weights[n], mom1[n], mom2[n], name=f"adamw_{n}")
        for o, r in zip(outs, res):
            o[n] = r

    def packed(tree):
        return _pack_rows([tree[n].reshape(-1) for n in SMALL], 16)[None]

    shapes = [(n, weights[n].shape) for n in SMALL]
    res = _adamw([small_parts], packed(weights), packed(mom1), packed(mom2), name="adamw_small")
    for o, r in zip(outs, res):
        o.update(_unpack(r, shapes))
    return (loss, dx[None], *[o[n] for o in outs for n in WEIGHT_ORDER])
```

```python
import functools

import jax
import jax.numpy as jnp
from jax import lax
from jax.experimental import pallas as pl
from jax.experimental.pallas import tpu as pltpu

F32 = jnp.float32
BF16 = jnp.bfloat16

D_MODEL = 1024
DEPTH = 2
POOL_WINDOWS = (2, 4, 8, 16)
POOL_GROUP_DIM = 128
BRANCH_WIDTH = 512
CONV_KERNEL = 31
CONV_HALO = 32
POOL_HALO = 16
HEAD_DIM = 64
HEAD_PAIR = 128
N_HEAD_PAIRS = 4
ATTN_SCALE = 0.125
LOG_F32_ZERO = -104.0
RMS_EPS = 1e-6
LN_EPS = 1e-5
N_DEV = 8
LANES = 128

ADAM_LR = 0.001
ADAM_B1 = 0.9
ADAM_B2 = 0.999
ADAM_EPS = 1e-08
ADAM_WD = 0.01
ADAM_STEP = 10

ROW_TILE = 256
NORM_ROW_TILE = 512
ATTN_BLOCK = 256
ATTN_BLOCKS_PER_STEP = 2
MM_TILE = 1024
MM_K_TILE = 1280
ADAM_TILE_ELEMS = 256 * 1024
VMEM_LIMIT = 48 * 1024 * 1024

MESH = pl.DeviceIdType.MESH


def _params(n_axes):
    return pltpu.CompilerParams(dimension_semantics=("arbitrary",) * n_axes, vmem_limit_bytes=VMEM_LIMIT)


def _tile(n, pref):
    if n <= pref:
        return n
    t = (pref // LANES) * LANES
    while n % t:
        t -= LANES
    return t


def _dot(a, b):
    return jnp.dot(a, b, preferred_element_type=F32)


def _dot_nt(a, b):
    return lax.dot_general(a, b, (((1,), (1,)), ((), ())), preferred_element_type=F32)


def _dot_tn(a, b):
    return lax.dot_general(a, b, (((0,), (0,)), ((), ())), preferred_element_type=F32)


def _sigmoid(x):
    return 1.0 / (1.0 + jnp.exp(-x))


def _silu_grad(x, s):
    return s * (1.0 + x * (1.0 - s))


def _matmul(a, b, *, mode, name, out_dtype=F32, rider=None):
    if mode == "nn":
        (m, k), n = a.shape, b.shape[1]
    elif mode == "nt":
        (m, k), n = a.shape, b.shape[0]
    else:
        (k, m), n = a.shape, b.shape[1]
    tm, tn, tk = _tile(m, MM_TILE), _tile(n, MM_TILE), _tile(k, MM_K_TILE)
    nk = k // tk
    grid = (m // tm, n // tn, nk)
    dot = {"nn": _dot, "nt": _dot_nt, "tn": _dot_tn}[mode]
    a_spec = pl.BlockSpec((tk, tm), lambda i, j, kk: (kk, i)) if mode == "tn" else pl.BlockSpec((tm, tk), lambda i, j, kk: (i, kk))
    b_spec = pl.BlockSpec((tn, tk), lambda i, j, kk: (j, kk)) if mode == "nt" else pl.BlockSpec((tk, tn), lambda i, j, kk: (kk, j))
    o_spec = pl.BlockSpec((tm, tn), lambda i, j, kk: (i, j))

    def body(*refs):
        (a_ref, b_ref), (o_ref,), scratch, finish_rider = _ride(rider, 2, 1, refs, grid)
        part = dot(a_ref[...], b_ref[...])
        if nk == 1:
            o_ref[...] = part.astype(out_dtype)
        else:
            scr = scratch[0]
            kk = pl.program_id(2)

            @pl.when(kk == 0)
            def _():
                scr[...] = part

            @pl.when(kk > 0)
            def _():
                scr[...] += part

            @pl.when(kk == nk - 1)
            def _():
                o_ref[...] = scr[...].astype(out_dtype)

        finish_rider()

    extra = rider or _NO_RIDER
    res = pl.pallas_call(
        body, name=name, grid=grid,
        in_specs=[a_spec, b_spec] + extra.specs, out_specs=[o_spec] + extra.specs,
        out_shape=[jax.ShapeDtypeStruct((m, n), out_dtype)] + extra.out_shape,
        scratch_shapes=([pltpu.VMEM((tm, tn), F32)] if nk > 1 else []) + extra.scratch_shapes,
        compiler_params=_params(3),
    )(a, b, *extra.srcs)
    return res if rider is not None else res[0]


def _rms_fwd(x, g, *, name):
    t = x.shape[0]
    tm = _tile(t, NORM_ROW_TILE)
    row = pl.BlockSpec((tm, D_MODEL), lambda i: (i, 0))
    vec = pl.BlockSpec((1, D_MODEL), lambda i: (0, 0))

    def body(x_ref, g_ref, o_ref):
        xv = x_ref[...]
        y = xv * lax.rsqrt(jnp.mean(xv * xv, axis=-1, keepdims=True) + RMS_EPS) * g_ref[...]
        o_ref[...] = y.astype(BF16)

    return pl.pallas_call(
        body, name=name, grid=(t // tm,),
        in_specs=[row, vec], out_specs=row,
        out_shape=jax.ShapeDtypeStruct((t, D_MODEL), BF16),
        compiler_params=_params(1),
    )(x, g)


def _rms_bwd(xin, g, dy, *, name, resid=None):
    t = xin.shape[0]
    tm = _tile(t, NORM_ROW_TILE)
    row = pl.BlockSpec((tm, D_MODEL), lambda i: (i, 0))
    vec = pl.BlockSpec((1, D_MODEL), lambda i: (0, 0))
    has_resid = resid is not None
    out_dtype = F32 if has_resid else BF16

    def body(*refs):
        x_ref, g_ref, dy_ref = refs[0], refs[1], refs[2]
        dx_ref, dg_ref = refs[-2], refs[-1]
        xv, dyv = x_ref[...], dy_ref[...]
        r = lax.rsqrt(jnp.mean(xv * xv, axis=-1, keepdims=True) + RMS_EPS)
        a = dyv * g_ref[...]
        dx = r * a - xv * (r * r * r) * jnp.mean(a * xv, axis=-1, keepdims=True)
        if has_resid:
            dx = dx + refs[3][...]
        dx_ref[...] = dx.astype(out_dtype)
        part = jnp.sum(dyv * xv * r, axis=0, keepdims=True)

        @pl.when(pl.program_id(0) == 0)
        def _():
            dg_ref[...] = part

        @pl.when(pl.program_id(0) > 0)
        def _():
            dg_ref[...] += part

    return pl.pallas_call(
        body, name=name, grid=(t // tm,),
        in_specs=[row, vec, row] + ([row] if has_resid else []), out_specs=[row, vec],
        out_shape=[jax.ShapeDtypeStruct((t, D_MODEL), out_dtype), jax.ShapeDtypeStruct((1, D_MODEL), F32)],
        compiler_params=_params(1),
    )(*((xin, g, dy) + ((resid,) if has_resid else ())))


def _loss_head(x, target, *, name):
    t = x.shape[0]
    tm = _tile(t, NORM_ROW_TILE)
    row = pl.BlockSpec((tm, D_MODEL), lambda i: (i, 0))
    acc = pl.BlockSpec((8, LANES), lambda i: (0, 0))

    def body(x_ref, t_ref, l_ref, dx_ref):
        diff = x_ref[...] - t_ref[...]
        dx_ref[...] = diff * (1.0 / D_MODEL)
        part = 0.5 * jnp.sum(jnp.mean(diff * diff, axis=-1, keepdims=True), axis=0, keepdims=True)

        @pl.when(pl.program_id(0) == 0)
        def _():
            l_ref[...] = jnp.zeros((8, LANES), F32) + part

        @pl.when(pl.program_id(0) > 0)
        def _():
            l_ref[...] += part

    return pl.pallas_call(
        body, name=name, grid=(t // tm,),
        in_specs=[row, row], out_specs=[acc, row],
        out_shape=[jax.ShapeDtypeStruct((8, LANES), F32), jax.ShapeDtypeStruct((t, D_MODEL), F32)],
        compiler_params=_params(1),
    )(x, target)


def _window_sum(ext, n_doublings, forward):
    rows = ext.shape[0]
    s, sh = ext, 1
    for _ in range(n_doublings):
        s = s + pltpu.roll(s, sh if forward else rows - sh, 0)
        sh *= 2
    return s


def _pool_fwd(pg, pool_w, pool_b, pool_scale, *, name):
    t = pg.shape[0]
    tm = _tile(t, ROW_TILE)

    def body(pg_ref, w_ref, b_ref, s_ref, o_ref, halo):
        i = pl.program_id(0)

        @pl.when(i == 0)
        def _():
            halo[...] = jnp.zeros_like(halo)

        p = pg_ref[:, :BRANCH_WIDTH]
        gate = pg_ref[:, BRANCH_WIDTH:]
        ext = jnp.concatenate([halo[...], p], axis=0)
        pos = i * tm + lax.broadcasted_iota(jnp.int32, (tm, 1), 0)
        outs = []
        for g, w in enumerate(POOL_WINDOWS):
            cols = slice(g * POOL_GROUP_DIM, (g + 1) * POOL_GROUP_DIM)
            cnt = jnp.minimum(pos + 1, w).astype(F32)
            d = _window_sum(ext[:, cols], g + 1, True)[POOL_HALO:] / cnt - p[:, cols]
            y = (_dot(d.astype(BF16), w_ref[g]) + b_ref[:, cols]) * s_ref[:, cols]
            gg = gate[:, cols]
            outs.append(y * (gg * _sigmoid(gg)))
        o_ref[...] = jnp.concatenate(outs, axis=1).astype(BF16)
        halo[...] = p[tm - POOL_HALO:, :]

    vec = pl.BlockSpec((1, BRANCH_WIDTH), lambda i: (0, 0))
    return pl.pallas_call(
        body, name=name, grid=(t // tm,),
        in_specs=[pl.BlockSpec((tm, 2 * BRANCH_WIDTH), lambda i: (i, 0)),
                  pl.BlockSpec((4, POOL_GROUP_DIM, POOL_GROUP_DIM), lambda i: (0, 0, 0)), vec, vec],
        out_specs=pl.BlockSpec((tm, BRANCH_WIDTH), lambda i: (i, 0)),
        out_shape=jax.ShapeDtypeStruct((t, BRANCH_WIDTH), BF16),
        scratch_shapes=[pltpu.VMEM((POOL_HALO, BRANCH_WIDTH), F32)],
        compiler_params=_params(1),
    )(pg, pool_w, pool_b, pool_scale)


def _pool_bwd(pg, d_out, pool_w, pool_b, pool_scale, *, name):
    t = pg.shape[0]
    tm = _tile(t, ROW_TILE)
    nt = t // tm
    halo_per_tile = tm // POOL_HALO

    def body(pg_ref, halo_ref, do_ref, w_ref, b_ref, s_ref, dpg_ref, dw_ref, dvec_ref, carry):
        i = pl.program_id(0)
        ri = nt - 1 - i

        @pl.when(i == 0)
        def _():
            carry[...] = jnp.zeros_like(carry)
            dw_ref[...] = jnp.zeros_like(dw_ref)
            dvec_ref[...] = jnp.zeros_like(dvec_ref)

        p = pg_ref[:, :BRANCH_WIDTH]
        gate = pg_ref[:, BRANCH_WIDTH:]
        hp = jnp.where(ri > 0, halo_ref[:, :BRANCH_WIDTH], 0.0)
        ext = jnp.concatenate([hp, p], axis=0)
        pos = ri * tm + lax.broadcasted_iota(jnp.int32, (tm, 1), 0)
        dps, dgs, dbs, dss = [], [], [], []
        for g, w in enumerate(POOL_WINDOWS):
            cols = slice(g * POOL_GROUP_DIM, (g + 1) * POOL_GROUP_DIM)
            cnt = jnp.minimum(pos + 1, w).astype(F32)
            d = (_window_sum(ext[:, cols], g + 1, True)[POOL_HALO:] / cnt - p[:, cols]).astype(BF16)
            y1 = _dot(d, w_ref[g]) + b_ref[:, cols]
            scale = s_ref[:, cols]
            y2 = y1 * scale
            gg = gate[:, cols]
            sg = _sigmoid(gg)
            do = do_ref[:, cols]
            dy2 = do * (gg * sg)
            dgs.append(do * y2 * _silu_grad(gg, sg))
            dss.append(jnp.sum(dy2 * y1, axis=0, keepdims=True))
            dy1 = dy2 * scale
            dbs.append(jnp.sum(dy1, axis=0, keepdims=True))
            dy1b = dy1.astype(BF16)
            dw_ref[g] += _dot_tn(d, dy1b)
            dd = _dot_nt(dy1b, w_ref[g])
            dpool = dd / cnt
            dext = jnp.concatenate([dpool, carry[:, cols]], axis=0)
            dps.append(_window_sum(dext, g + 1, False)[:tm] - dd)
            carry[:, cols] = dpool[:POOL_HALO]
        dpg_ref[...] = jnp.concatenate(dps + dgs, axis=1).astype(BF16)
        dvec_ref[0:1, :] += jnp.concatenate(dbs, axis=1)
        dvec_ref[1:2, :] += jnp.concatenate(dss, axis=1)

    vec = pl.BlockSpec((1, BRANCH_WIDTH), lambda i: (0, 0))
    wspec = pl.BlockSpec((4, POOL_GROUP_DIM, POOL_GROUP_DIM), lambda i: (0, 0, 0))
    return pl.pallas_call(
        body, name=name, grid=(nt,),
        in_specs=[pl.BlockSpec((tm, 2 * BRANCH_WIDTH), lambda i: (nt - 1 - i, 0)),
                  pl.BlockSpec((POOL_HALO, 2 * BRANCH_WIDTH), lambda i: (jnp.maximum((nt - 1 - i) * halo_per_tile - 1, 0), 0)),
                  pl.BlockSpec((tm, BRANCH_WIDTH), lambda i: (nt - 1 - i, 0)), wspec, vec, vec],
        out_specs=[pl.BlockSpec((tm, 2 * BRANCH_WIDTH), lambda i: (nt - 1 - i, 0)), wspec,
                   pl.BlockSpec((8, BRANCH_WIDTH), lambda i: (0, 0))],
        out_shape=[jax.ShapeDtypeStruct((t, 2 * BRANCH_WIDTH), BF16),
                   jax.ShapeDtypeStruct((4, POOL_GROUP_DIM, POOL_GROUP_DIM), F32),
                   jax.ShapeDtypeStruct((8, BRANCH_WIDTH), F32)],
        scratch_shapes=[pltpu.VMEM((POOL_HALO, BRANCH_WIDTH), F32)],
        compiler_params=_params(1),
    )(pg, pg, d_out, pool_w, pool_b, pool_scale)


CONV_TILE_ROWS = 64


def _for_conv_tiles(tm, fn):
    def step(it, carry):
        rows = pl.ds(pl.multiple_of(it * CONV_TILE_ROWS, CONV_TILE_ROWS), CONV_TILE_ROWS)
        for c in range(0, BRANCH_WIDTH, LANES):
            fn(rows, slice(c, c + LANES))
        return carry

    lax.fori_loop(0, tm // CONV_TILE_ROWS, step, 0)


def _sublane_shifts(shifted_ref, x, direction):
    rows = x.shape[0]
    shifted_ref[0] = x
    for b in range(1, 8):
        shifted_ref[b] = pltpu.roll(x, b if direction > 0 else rows - b, 0)


CONV_REACH = 8 * ((CONV_KERNEL - 1) // 8)


def _tap_tiles(shifted_ref, base, rows, cols, direction):
    for b in range(8):
        lo = pl.multiple_of(base + rows.start - (CONV_REACH if direction > 0 else 0), 8)
        window = shifted_ref[b, pl.ds(lo, CONV_TILE_ROWS + CONV_REACH), cols]
        for a in range((CONV_KERNEL - 1 - b) // 8 + 1):
            off = CONV_REACH - 8 * a if direction > 0 else 8 * a
            yield 8 * a + b, window[off:off + CONV_TILE_ROWS]


def _tap_sum(shifted_ref, w_ref, base, rows, cols, direction):
    acc = None
    for j, tile in _tap_tiles(shifted_ref, base, rows, cols, direction):
        k = CONV_KERNEL - 1 - j
        term = w_ref[k:k + 1, cols] * tile
        acc = term if acc is None else acc + term
    return acc


def _causal_conv(ext8_ref, w_ref, cv_ref, tm):
    def tile(rows, cols):
        cv_ref[rows, cols] = _tap_sum(ext8_ref, w_ref, CONV_HALO, rows, cols, 1)

    _for_conv_tiles(tm, tile)


def _conv_fwd(c2gc, conv_w, conv_b, ln_g, ln_b, *, name):
    t = c2gc.shape[0]
    tm = _tile(t, ROW_TILE)

    def body(c_ref, w_ref, cb_ref, g_ref, b_ref, o_ref, halo, ext8_ref, cv_ref):
        @pl.when(pl.program_id(0) == 0)
        def _():
            halo[...] = jnp.zeros_like(halo)

        u = c_ref[:, :BRANCH_WIDTH] * _sigmoid(c_ref[:, BRANCH_WIDTH:2 * BRANCH_WIDTH])
        gate = c_ref[:, 2 * BRANCH_WIDTH:]
        _sublane_shifts(ext8_ref, jnp.concatenate([halo[...], u], axis=0), 1)
        halo[...] = u[tm - CONV_HALO:, :]
        _causal_conv(ext8_ref, w_ref, cv_ref, tm)
        cv = cv_ref[...] + cb_ref[...]
        mu = jnp.mean(cv, axis=-1, keepdims=True)
        xc = cv - mu
        var = jnp.mean(xc * xc, axis=-1, keepdims=True)
        ln = xc * lax.rsqrt(var + LN_EPS) * g_ref[...] + b_ref[...]
        o_ref[...] = (ln * _sigmoid(ln) * (gate * _sigmoid(gate))).astype(BF16)

    vec = pl.BlockSpec((1, BRANCH_WIDTH), lambda i: (0, 0))
    return pl.pallas_call(
        body, name=name, grid=(t // tm,),
        in_specs=[pl.BlockSpec((tm, 3 * BRANCH_WIDTH), lambda i: (i, 0)),
                  pl.BlockSpec((CONV_HALO, BRANCH_WIDTH), lambda i: (0, 0)), vec, vec, vec],
        out_specs=pl.BlockSpec((tm, BRANCH_WIDTH), lambda i: (i, 0)),
        out_shape=jax.ShapeDtypeStruct((t, BRANCH_WIDTH), BF16),
        scratch_shapes=[pltpu.VMEM((CONV_HALO, BRANCH_WIDTH), F32), pltpu.VMEM((8, tm + CONV_HALO, BRANCH_WIDTH), F32),
                        pltpu.VMEM((tm, BRANCH_WIDTH), F32)],
        compiler_params=_params(1),
    )(c2gc, conv_w, conv_b, ln_g, ln_b)


def _conv_bwd(c2gc, d_out, conv_w, conv_b, ln_g, ln_b, *, name):
    t = c2gc.shape[0]
    tm = _tile(t, ROW_TILE)
    nt = t // tm
    halo_per_tile = tm // CONV_HALO

    def body(c_ref, halo_ref, do_ref, w_ref, cb_ref, g_ref, b_ref, dc_ref, dw_ref, dvec_ref,
             carry, ext8_ref, cv_ref, dext8_ref, du_ref, dw_acc):
        i = pl.program_id(0)
        ri = nt - 1 - i

        @pl.when(i == 0)
        def _():
            carry[...] = jnp.zeros_like(carry)
            dw_acc[...] = jnp.zeros_like(dw_acc)
            dvec_ref[...] = jnp.zeros_like(dvec_ref)

        a = c_ref[:, :BRANCH_WIDTH]
        sb = _sigmoid(c_ref[:, BRANCH_WIDTH:2 * BRANCH_WIDTH])
        gate = c_ref[:, 2 * BRANCH_WIDTH:]
        hu = halo_ref[:, :BRANCH_WIDTH] * _sigmoid(halo_ref[:, BRANCH_WIDTH:2 * BRANCH_WIDTH])
        _sublane_shifts(ext8_ref, jnp.concatenate([jnp.where(ri > 0, hu, 0.0), a * sb], axis=0), 1)
        _causal_conv(ext8_ref, w_ref, cv_ref, tm)
        cv = cv_ref[...] + cb_ref[...]
        mu = jnp.mean(cv, axis=-1, keepdims=True)
        xc = cv - mu
        rs = lax.rsqrt(jnp.mean(xc * xc, axis=-1, keepdims=True) + LN_EPS)
        n = xc * rs
        ln = n * g_ref[...] + b_ref[...]
        sl = _sigmoid(ln)
        sgate = _sigmoid(gate)
        do = do_ref[...]
        dgate = do * (ln * sl) * _silu_grad(gate, sgate)
        dln = do * (gate * sgate) * _silu_grad(ln, sl)
        dn = dln * g_ref[...]
        dcv = rs * (dn - jnp.mean(dn, axis=-1, keepdims=True) - n * jnp.mean(dn * n, axis=-1, keepdims=True))
        dvec_ref[0:1, :] += jnp.sum(dcv, axis=0, keepdims=True)
        dvec_ref[1:2, :] += jnp.sum(dln * n, axis=0, keepdims=True)
        dvec_ref[2:3, :] += jnp.sum(dln, axis=0, keepdims=True)
        _sublane_shifts(dext8_ref, jnp.concatenate([dcv, carry[...]], axis=0), -1)
        carry[...] = dcv[:CONV_HALO]
        def tile(rows, cols):
            du_ref[rows, cols] = _tap_sum(dext8_ref, w_ref, 0, rows, cols, -1)
            d_tile = dext8_ref[0, rows, cols]
            for j, u_tile in _tap_tiles(ext8_ref, CONV_HALO, rows, cols, 1):
                prod = d_tile * u_tile
                part = prod[0:8]
                for q in range(8, CONV_TILE_ROWS, 8):
                    part = part + prod[q:q + 8]
                dw_acc[CONV_KERNEL - 1 - j, :, cols] += part

        _for_conv_tiles(tm, tile)
        du = du_ref[...]
        dc_ref[...] = jnp.concatenate([du * sb, du * a * sb * (1.0 - sb), dgate], axis=1).astype(BF16)

        @pl.when(i == nt - 1)
        def _():
            dw_ref[...] = jnp.sum(dw_acc[...], axis=1)

    vec = pl.BlockSpec((1, BRANCH_WIDTH), lambda i: (0, 0))
    wspec = pl.BlockSpec((CONV_HALO, BRANCH_WIDTH), lambda i: (0, 0))
    return pl.pallas_call(
        body, name=name, grid=(nt,),
        in_specs=[pl.BlockSpec((tm, 3 * BRANCH_WIDTH), lambda i: (nt - 1 - i, 0)),
                  pl.BlockSpec((CONV_HALO, 3 * BRANCH_WIDTH), lambda i: (jnp.maximum((nt - 1 - i) * halo_per_tile - 1, 0), 0)),
                  pl.BlockSpec((tm, BRANCH_WIDTH), lambda i: (nt - 1 - i, 0)), wspec, vec, vec, vec],
        out_specs=[pl.BlockSpec((tm, 3 * BRANCH_WIDTH), lambda i: (nt - 1 - i, 0)), wspec,
                   pl.BlockSpec((8, BRANCH_WIDTH), lambda i: (0, 0))],
        out_shape=[jax.ShapeDtypeStruct((t, 3 * BRANCH_WIDTH), BF16),
                   jax.ShapeDtypeStruct((CONV_HALO, BRANCH_WIDTH), F32),
                   jax.ShapeDtypeStruct((8, BRANCH_WIDTH), F32)],
        scratch_shapes=[pltpu.VMEM((CONV_HALO, BRANCH_WIDTH), F32),
                        pltpu.VMEM((8, tm + CONV_HALO, BRANCH_WIDTH), F32), pltpu.VMEM((tm, BRANCH_WIDTH), F32),
                        pltpu.VMEM((8, tm + CONV_HALO, BRANCH_WIDTH), F32), pltpu.VMEM((tm, BRANCH_WIDTH), F32),
                        pltpu.VMEM((CONV_HALO, 8, BRANCH_WIDTH), F32)],
        compiler_params=_params(1),
    )(c2gc, c2gc, d_out, conv_w, conv_b, ln_g, ln_b)


def _merge_specs(tm):
    gates = [pl.BlockSpec((tm, D_MODEL), functools.partial(lambda i, b: (i, b), b=b)) for b in range(3)]
    acts = [pl.BlockSpec((tm, BRANCH_WIDTH), lambda i: (i, 0))] * 3
    weights = [pl.BlockSpec((BRANCH_WIDTH, D_MODEL), lambda i: (0, 0))] * 3
    return gates + acts + weights


def _merge_fwd(gmga, acts, weights, w_o, x, g_post, *, name):
    t = gmga.shape[0]
    tm = _tile(t, ROW_TILE)
    row = pl.BlockSpec((tm, D_MODEL), lambda i: (i, 0))

    def body(g0, g1, g2, a0, a1, a2, w0, w1, w2, wo_ref, x_ref, gp_ref, mix_ref, out_ref, nxt_ref):
        m = None
        for g_ref, a_ref, w_ref in ((g0, a0, w0), (g1, a1, w1), (g2, a2, w2)):
            term = _sigmoid(g_ref[...]) * _dot(a_ref[...], w_ref[...])
            m = term if m is None else m + term
        mix = m.astype(BF16)
        mix_ref[...] = mix
        out = _dot(mix, wo_ref[...])
        out_ref[...] = out
        y = out * lax.rsqrt(jnp.mean(out * out, axis=-1, keepdims=True) + RMS_EPS) * gp_ref[...]
        nxt_ref[...] = x_ref[...] + y

    return pl.pallas_call(
        body, name=name, grid=(t // tm,),
        in_specs=_merge_specs(tm) + [pl.BlockSpec((D_MODEL, D_MODEL), lambda i: (0, 0)), row,
                                     pl.BlockSpec((1, D_MODEL), lambda i: (0, 0))],
        out_specs=[row, row, row],
        out_shape=[jax.ShapeDtypeStruct((t, D_MODEL), BF16), jax.ShapeDtypeStruct((t, D_MODEL), F32),
                   jax.ShapeDtypeStruct((t, D_MODEL), F32)],
        compiler_params=_params(1),
    )(gmga, gmga, gmga, *acts, *weights, w_o, x, g_post)


def _merge_bwd(dout, w_o, gmga, acts, weights, *, name):
    t = gmga.shape[0]
    tm = _tile(t, ROW_TILE)
    row = pl.BlockSpec((tm, D_MODEL), lambda i: (i, 0))
    wide = pl.BlockSpec((tm, 3 * D_MODEL), lambda i: (i, 0))
    act = pl.BlockSpec((tm, BRANCH_WIDTH), lambda i: (i, 0))

    def body(dout_ref, wo_ref, g0, g1, g2, a0, a1, a2, w0, w1, w2, d0, d1, d2, dg_ref, da0, da1, da2):
        dmv = _dot_nt(dout_ref[...], wo_ref[...])
        for k, (g_ref, a_ref, w_ref, dy_ref, da_ref) in enumerate(
                ((g0, a0, w0, d0, da0), (g1, a1, w1, d1, da1), (g2, a2, w2, d2, da2))):
            s = _sigmoid(g_ref[...])
            dy = (dmv * s).astype(BF16)
            dy_ref[...] = dy
            y = _dot(a_ref[...], w_ref[...])
            dg_ref[:, k * D_MODEL:(k + 1) * D_MODEL] = (dmv * y * s * (1.0 - s)).astype(BF16)
            da_ref[...] = _dot_nt(dy, w_ref[...])

    return pl.pallas_call(
        body, name=name, grid=(t // tm,),
        in_specs=[row, pl.BlockSpec((D_MODEL, D_MODEL), lambda i: (0, 0))] + _merge_specs(tm),
        out_specs=[row, row, row, wide, act, act, act],
        out_shape=[jax.ShapeDtypeStruct((t, D_MODEL), BF16)] * 3 + [jax.ShapeDtypeStruct((t, 3 * D_MODEL), BF16)]
        + [jax.ShapeDtypeStruct((t, BRANCH_WIDTH), F32)] * 3,
        compiler_params=_params(1),
    )(dout, w_o, gmga, gmga, gmga, *acts, *weights)


GA_BLOCK = 3 * D_MODEL // HEAD_PAIR


def _split_heads(x, lane_is_first):
    zero = jnp.zeros_like(x)
    return jnp.concatenate([jnp.where(lane_is_first, x, zero), jnp.where(lane_is_first, zero, x)], axis=0)


def _side_by_side(x, rows):
    return jnp.concatenate([x[:rows], x[rows:]], axis=1)


def _split_bf16(x):
    hi = x.astype(BF16)
    return hi, (x - hi.astype(F32)).astype(BF16)


def _scores(qcat, kblk, mask):
    z = _dot_nt(qcat, kblk)
    e = jnp.exp(-jnp.abs(z))
    sp = jnp.maximum(z, 0.0) + jnp.log(1.0 + e)
    l1m = -sp
    if mask is not None:
        l1m = jnp.where(mask, l1m, 0.0)
    inv = 1.0 / (1.0 + e)
    pos = z >= 0.0
    return z - sp, l1m, jnp.where(pos, 1.0, e) * inv, jnp.where(pos, e, 1.0) * inv


def _attn_consts(blk):
    lane_is_first = lax.broadcasted_iota(jnp.int32, (1, HEAD_PAIR), 1) < HEAD_DIM
    r = lax.broadcasted_iota(jnp.int32, (blk, blk), 0)
    c = lax.broadcasted_iota(jnp.int32, (blk, blk), 1)
    after = (r > c).astype(BF16)
    from_here = (r >= c).astype(BF16)
    qrow = lax.broadcasted_iota(jnp.int32, (2 * blk, blk), 0)
    qrow = jnp.where(qrow >= blk, qrow - blk, qrow)
    causal = lax.broadcasted_iota(jnp.int32, (2 * blk, blk), 1) < qrow
    return lane_is_first, after, from_here, causal


def _while_mass_left(qi, carry, block):
    def alive(c):
        return jnp.max(c[0]) > LOG_F32_ZERO

    def cond(state):
        return jnp.logical_and(state[0] < qi, state[1])

    def step(state):
        new = block(qi - 1 - state[0], state[2])
        return state[0] + 1, alive(new), new

    return lax.while_loop(cond, step, (jnp.int32(0), alive(carry), carry))[2]


def _ride(rider, n_in, n_out, refs, grid):
    if rider is None:
        return refs[:n_in], refs[n_in:n_in + n_out], refs[n_in + n_out:], lambda: None
    n = rider.n
    ins, srcs = refs[:n_in], refs[n_in:n_in + n]
    outs, dsts = refs[n_in + n:n_in + n + n_out], refs[n_in + n + n_out:n_in + 2 * n + n_out]
    rest = refs[n_in + 2 * n + n_out:]
    scratch, sems = rest[:len(rest) - 3], rest[len(rest) - 3:]
    step, n_steps = 0, 1
    for axis, size in enumerate(grid):
        step, n_steps = step * size + pl.program_id(axis), n_steps * size

    @pl.when(step == 0)
    def _():
        rider.start(srcs, dsts, sems)

    def finish():
        @pl.when(step == n_steps - 1)
        def _():
            rider.finish(srcs, dsts, sems)

    return ins, outs, scratch, finish


def _attn_fwd(qkv, gmga, *, name, rider=None):
    t = qkv.shape[0]
    blk, step_rows = _tile(t, ATTN_BLOCK), _tile(t, ATTN_BLOCK * ATTN_BLOCKS_PER_STEP)
    nq = t // step_rows

    def body(*refs):
        (q_ref, k_ref, v_ref, ga_ref), (o_ref, cv_ref), _, finish_rider = _ride(rider, 4, 2, refs, (N_HEAD_PAIRS, nq))
        lane_is_first, after, _, causal = _attn_consts(blk)

        for sub in range(step_rows // blk):
            mine = slice(sub * blk, (sub + 1) * blk)
            qi = pl.program_id(1) * (step_rows // blk) + sub
            qcat = _split_heads(q_ref[mine, :], lane_is_first)

            def block(kb, carry, mask, qcat=qcat):
                run, acc = carry
                rows = pl.ds(pl.multiple_of(kb * blk, blk), blk)
                lb, l1m, _, _ = _scores(qcat, k_ref[rows, :], mask)
                hi, lo = _split_bf16(l1m)
                w = jnp.exp(lb + (_dot(hi, after) + _dot(lo, after) + run))
                if mask is not None:
                    w = jnp.where(mask, w, 0.0)
                vcat = _split_heads(v_ref[rows, :], lane_is_first)
                acc = acc + _dot(_side_by_side(w.astype(BF16), blk), vcat)
                return run + jnp.sum(l1m, axis=-1, keepdims=True), acc

            carry = block(qi, (jnp.zeros((2 * blk, 1), F32), jnp.zeros((blk, HEAD_PAIR), F32)), causal)
            _, o = _while_mass_left(qi, carry, lambda kb, c, block=block: block(kb, c, None))
            o_ref[mine, :] = o
            ga = ga_ref[mine, :]
            cv_ref[mine, :] = (o * (ga * _sigmoid(ga))).astype(BF16)
        finish_rider()

    qspec = pl.BlockSpec((step_rows, HEAD_PAIR), lambda p, i: (i, p))
    extra = rider or _NO_RIDER
    return pl.pallas_call(
        body, name=name, grid=(N_HEAD_PAIRS, nq),
        in_specs=[qspec,
                  pl.BlockSpec((t, HEAD_PAIR), lambda p, i: (0, N_HEAD_PAIRS + p)),
                  pl.BlockSpec((t, HEAD_PAIR), lambda p, i: (0, 2 * N_HEAD_PAIRS + p)),
                  pl.BlockSpec((step_rows, HEAD_PAIR), lambda p, i: (i, GA_BLOCK + p))] + extra.specs,
        out_specs=[qspec, qspec] + extra.specs,
        out_shape=[jax.ShapeDtypeStruct((t, BRANCH_WIDTH), F32), jax.ShapeDtypeStruct((t, BRANCH_WIDTH), BF16)] + extra.out_shape,
        scratch_shapes=extra.scratch_shapes,
        compiler_params=_params(2),
    )(qkv, qkv, qkv, gmga, *extra.srcs)


def _attn_bwd(qkv, o, gmga, dcv, *, name, rider=None):
    t = qkv.shape[0]
    blk, step_rows = _tile(t, ATTN_BLOCK), _tile(t, ATTN_BLOCK * ATTN_BLOCKS_PER_STEP)
    nq = t // step_rows

    def body(*refs):
        ins, outs, (dk_acc, dv_acc), finish_rider = _ride(rider, 6, 4, refs, (N_HEAD_PAIRS, nq))
        q_ref, k_ref, v_ref, o_ref, ga_ref, dcv_ref = ins
        dq_ref, dk_ref, dv_ref, dga_ref = outs
        lane_is_first, after, from_here, causal = _attn_consts(blk)

        @pl.when(pl.program_id(1) == 0)
        def _():
            dk_acc[...] = jnp.zeros_like(dk_acc)
            dv_acc[...] = jnp.zeros_like(dv_acc)

        for sub in range(step_rows // blk):
            mine = slice(sub * blk, (sub + 1) * blk)
            qi = pl.program_id(1) * (step_rows // blk) + sub
            ga, ov, dcvv = ga_ref[mine, :], o_ref[mine, :], dcv_ref[mine, :]
            sg = _sigmoid(ga)
            dob = (dcvv * (ga * sg)).astype(BF16)
            dga_ref[mine, :] = (dcvv * ov * _silu_grad(ga, sg)).astype(BF16)
            gt = dob.astype(F32) * ov
            g_total = jnp.concatenate(
                [jnp.sum(jnp.where(lane_is_first, gt, 0.0), axis=-1, keepdims=True),
                 jnp.sum(jnp.where(lane_is_first, 0.0, gt), axis=-1, keepdims=True)], axis=0)
            qcat = _split_heads(q_ref[mine, :], lane_is_first)
            docat = _split_heads(dob, lane_is_first)

            def block(kb, carry, mask, g_total=g_total, qcat=qcat, docat=docat):
                run, g_run, dq = carry
                rows = pl.ds(pl.multiple_of(kb * blk, blk), blk)
                kblk = k_ref[rows, :]
                lb, l1m, sig, one_m_sig = _scores(qcat, kblk, mask)
                hi, lo = _split_bf16(l1m)
                w = jnp.exp(lb + (_dot(hi, after) + _dot(lo, after) + run))
                if mask is not None:
                    w = jnp.where(mask, w, 0.0)
                wb = w.astype(BF16)
                g = _dot_nt(docat, v_ref[rows, :]) * wb.astype(F32)
                ghi, glo = _split_bf16(g)
                g_before = g_total - g_run - (_dot(ghi, from_here) + _dot(glo, from_here))
                dz = g * one_m_sig - g_before * sig
                if mask is not None:
                    dz = jnp.where(mask, dz, 0.0)
                dzb = dz.astype(BF16)
                dq = dq + _dot(_side_by_side(dzb, blk), _split_heads(kblk, lane_is_first))
                dk_acc[rows, :] += _dot_tn(dzb, qcat)
                dv_acc[rows, :] += _dot_tn(wb, docat)
                return (run + jnp.sum(l1m, axis=-1, keepdims=True), g_run + jnp.sum(g, axis=-1, keepdims=True), dq)

            zero = jnp.zeros((2 * blk, 1), F32)
            carry = block(qi, (zero, zero, jnp.zeros((blk, HEAD_PAIR), F32)), causal)
            _, _, dq = _while_mass_left(qi, carry, lambda kb, c, block=block: block(kb, c, None))
            dq_ref[mine, :] = (dq * ATTN_SCALE).astype(BF16)

        @pl.when(pl.program_id(1) == nq - 1)
        def _():
            dk_ref[...] = dk_acc[...].astype(BF16)
            dv_ref[...] = dv_acc[...].astype(BF16)

        finish_rider()

    qspec = pl.BlockSpec((step_rows, HEAD_PAIR), lambda p, i: (i, p))
    whole = pl.BlockSpec((t, HEAD_PAIR), lambda p, i: (0, p))
    out = jax.ShapeDtypeStruct((t, BRANCH_WIDTH), BF16)
    extra = rider or _NO_RIDER
    return pl.pallas_call(
        body, name=name, grid=(N_HEAD_PAIRS, nq),
        in_specs=[qspec,
                  pl.BlockSpec((t, HEAD_PAIR), lambda p, i: (0, N_HEAD_PAIRS + p)),
                  pl.BlockSpec((t, HEAD_PAIR), lambda p, i: (0, 2 * N_HEAD_PAIRS + p)),
                  qspec,
                  pl.BlockSpec((step_rows, HEAD_PAIR), lambda p, i: (i, GA_BLOCK + p)),
                  qspec] + extra.specs,
        out_specs=[qspec, whole, whole, qspec] + extra.specs,
        out_shape=[out, out, out, out] + extra.out_shape,
        scratch_shapes=[pltpu.VMEM((t, HEAD_PAIR), F32), pltpu.VMEM((t, HEAD_PAIR), F32)] + extra.scratch_shapes,
        compiler_params=_params(2),
    )(qkv, qkv, qkv, o, gmga, dcv, *extra.srcs)


def _mesh_position():
    x, y, c = lax.axis_index("x"), lax.axis_index("y"), lax.axis_index("c")
    return x, y, c, 4 * x + 2 * y + c


def _flipped(x, y, c, k):
    return (1 - x if k & 4 else x, 1 - y if k & 2 else y, 1 - c if k & 1 else c)


def _all_to_all(srcs, *, name, same_block):
    ex = _Exchange(srcs, same_block)

    def body(*refs):
        ex.start(refs[:ex.n], refs[ex.n:2 * ex.n], refs[2 * ex.n:])
        ex.finish(refs[:ex.n], refs[ex.n:2 * ex.n], refs[2 * ex.n:])

    return pl.pallas_call(
        body, name=name, in_specs=ex.specs, out_specs=ex.specs, out_shape=ex.out_shape,
        scratch_shapes=ex.scratch_shapes,
    )(*srcs)


def _gather_via_sibling(srcs, *, name):
    n = len(srcs)

    def body(*refs):
        src_refs, dst_refs = refs[:n], refs[n:2 * n]
        send_sems, recv_sems, local_sems = refs[2 * n:]
        x, y, c, me = _mesh_position()
        sibling = (x, y, 1 - c)
        chips = [(1 - x, y), (x, 1 - y), (1 - x, 1 - y)]

        def slot(px, py, pc):
            return 4 * px + 2 * py + pc

        def copy(i, k, block, to, from_src=False):
            return pltpu.make_async_remote_copy(
                src_ref=src_refs[i] if from_src else dst_refs[i].at[slot(*block)], dst_ref=dst_refs[i].at[slot(*block)],
                send_sem=send_sems.at[k, i], recv_sem=recv_sems.at[k, i], device_id=to, device_id_type=MESH)

        mine = [pltpu.make_async_copy(src_refs[i], dst_refs[i].at[me], local_sems.at[i]) for i in range(n)]
        first = [copy(i, 0, (x, y, c), sibling, True) for i in range(n)]
        first += [copy(i, 1 + j, (x, y, c), (*chip, c), True) for j, chip in enumerate(chips) for i in range(n)]
        for cp in mine + first:
            cp.start()
        passed = []
        for j, chip in enumerate(chips):
            for i in range(n):
                copy(i, 1 + j, (*chip, c), (x, y, c)).wait_recv()
                passed.append(copy(i, 4 + j, (*chip, c), sibling))
                passed[-1].start()
        for i in range(n):
            copy(i, 0, sibling, (x, y, c)).wait_recv()
            for j, chip in enumerate(chips):
                copy(i, 4 + j, (*chip, 1 - c), (x, y, c)).wait_recv()
        for cp in first + passed:
            cp.wait_send()
        for cp in mine:
            cp.wait()

    spec = [pl.BlockSpec(memory_space=pl.ANY)] * n
    return pl.pallas_call(
        body, name=name, in_specs=spec, out_specs=spec,
        out_shape=[jax.ShapeDtypeStruct((N_DEV,) + tuple(s.shape), s.dtype) for s in srcs],
        scratch_shapes=[pltpu.SemaphoreType.DMA((N_DEV - 1, n)), pltpu.SemaphoreType.DMA((N_DEV - 1, n)),
                        pltpu.SemaphoreType.DMA((n,))],
    )(*srcs)


class _Exchange:
    def __init__(self, srcs, same_block):
        self.srcs, self.same_block, self.n = list(srcs), same_block, len(srcs)
        self.specs = [pl.BlockSpec(memory_space=pl.ANY)] * self.n
        self.out_shape = [jax.ShapeDtypeStruct((N_DEV,) + tuple(s.shape if same_block else s.shape[1:]), s.dtype)
                          for s in self.srcs]
        self.scratch_shapes = [pltpu.SemaphoreType.DMA((N_DEV - 1, self.n)), pltpu.SemaphoreType.DMA((N_DEV - 1, self.n)),
                               pltpu.SemaphoreType.DMA((self.n,))] if self.n else []

    def _copies(self, src_refs, dst_refs, sems, with_arrivals):
        send_sems, recv_sems, local_sems = sems
        x, y, c, me = _mesh_position()

        def outgoing(i, j):
            return src_refs[i] if self.same_block else src_refs[i].at[j]

        def remote(i, k, slot):
            return pltpu.make_async_remote_copy(
                src_ref=outgoing(i, jnp.bitwise_xor(me, k)), dst_ref=dst_refs[i].at[slot],
                send_sem=send_sems.at[k - 1, i], recv_sem=recv_sems.at[k - 1, i],
                device_id=_flipped(x, y, c, k), device_id_type=MESH)

        pairs = [(i, k) for k in range(1, N_DEV) for i in range(self.n)]
        mine = [pltpu.make_async_copy(outgoing(i, me), dst_refs[i].at[me], local_sems.at[i]) for i in range(self.n)]
        sent = [remote(i, k, me) for i, k in pairs]
        arrivals = [remote(i, k, jnp.bitwise_xor(me, k)) for i, k in pairs] if with_arrivals else []
        return mine, sent, arrivals

    def start(self, src_refs, dst_refs, sems):
        mine, sent, _ = self._copies(src_refs, dst_refs, sems, False)
        for cp in mine + sent:
            cp.start()

    def finish(self, src_refs, dst_refs, sems):
        mine, sent, arrivals = self._copies(src_refs, dst_refs, sems, True)
        for cp in arrivals:
            cp.wait_recv()
        for cp in sent:
            cp.wait_send()
        for cp in mine:
            cp.wait()


_NO_RIDER = _Exchange([], True)


def _adamw(parts, w, m, v, *, name):
    layers, rows, cols = w.shape
    assert len(parts) == layers
    tr = rows
    while tr * cols > ADAM_TILE_ELEMS and tr % 32 == 0:
        tr //= 2
    row = pl.BlockSpec((1, tr, cols), lambda l, i: (l, i, 0))

    def body(*refs):
        p_refs = refs[:layers]
        w_ref, m_ref, v_ref, g_ref, d_ref, nm_ref, nv_ref = refs[layers:]
        layer = pl.program_id(0)
        g = None
        for k in range(N_DEV):
            part = p_refs[0][k]
            for l in range(1, layers):
                part = jnp.where(layer == l, p_refs[l][k], part)
            g = part.astype(F32) if g is None else g + part.astype(F32)
        m2 = ADAM_B1 * m_ref[0] + (1.0 - ADAM_B1) * g
        v2 = ADAM_B2 * v_ref[0] + (1.0 - ADAM_B2) * (g * g)
        m_hat = m2 / (1.0 - ADAM_B1 ** ADAM_STEP)
        v_hat = v2 / (1.0 - ADAM_B2 ** ADAM_STEP)
        g_ref[0] = g
        d_ref[0] = -ADAM_LR * (m_hat / (jnp.sqrt(v_hat) + ADAM_EPS) + ADAM_WD * w_ref[0])
        nm_ref[0] = m2
        nv_ref[0] = v2

    out = jax.ShapeDtypeStruct((layers, rows, cols), F32)
    return pl.pallas_call(
        body, name=name, grid=(layers, rows // tr),
        in_specs=[pl.BlockSpec((N_DEV, tr, cols), lambda l, i: (0, i, 0))] * layers + [row, row, row],
        out_specs=[row, row, row, row], out_shape=[out, out, out, out],
        compiler_params=_params(2),
    )(*parts, w, m, v)


MATMUL_WEIGHTS = ("w_in", "w_pool_out", "w_conv_out", "w_attn_out", "w_o")
SMALL = ("conv_w", "norm_pre", "pool_w", "pool_b", "pool_scale", "conv_b", "conv_ln_g", "conv_ln_b", "norm_post")
WEIGHT_ORDER = ("norm_pre", "w_in", "pool_w", "pool_b", "pool_scale", "w_pool_out", "conv_w", "conv_b",
                "conv_ln_g", "conv_ln_b", "w_conv_out", "w_attn_out", "w_o", "norm_post")


def _shard_axis(name):
    return -2 if name == "w_o" else -1


def _pack_rows(flat_parts, row_multiple):
    flat = jnp.concatenate(flat_parts, axis=-1)
    n = flat.shape[-1]
    chunk = row_multiple * LANES
    total = -(-n // chunk) * chunk
    pad = [(0, 0)] * (flat.ndim - 1) + [(0, total - n)]
    return jnp.pad(flat, pad).reshape(flat.shape[:-1] + (total // LANES, LANES))


def _unpack(buf, shapes):
    flat = buf.reshape(-1)
    out, at = {}, 0
    for name, shape in shapes:
        n = 1
        for s in shape:
            n *= s
        out[name] = flat[at:at + n].reshape(shape)
        at += n
    return out


def _to_dest_major(name, full):
    axis = full.ndim + _shard_axis(name)
    n = full.shape[axis] // N_DEV
    return jnp.stack([lax.slice_in_dim(full, d * n, (d + 1) * n, axis=axis) for d in range(N_DEV)])


def _from_source_major(name, gathered):
    return jnp.concatenate([gathered[d] for d in range(N_DEV)], axis=_shard_axis(name))


def kernel(x, norm_pre, w_in, pool_w, pool_b, pool_scale, w_pool_out, conv_w, conv_b, conv_ln_g, conv_ln_b, w_conv_out, w_attn_out, w_o, norm_post, loss_target, m_norm_pre, m_w_in, m_pool_w, m_pool_b, m_pool_scale, m_w_pool_out, m_conv_w, m_conv_b, m_conv_ln_g, m_conv_ln_b, m_w_conv_out, m_w_attn_out, m_w_o, m_norm_post, v_norm_pre, v_w_in, v_pool_w, v_pool_b, v_pool_scale, v_w_pool_out, v_conv_w, v_conv_b, v_conv_ln_g, v_conv_ln_b, v_w_conv_out, v_w_attn_out, v_w_o, v_norm_post):
    weights = dict(norm_pre=norm_pre, w_in=w_in, pool_w=pool_w, pool_b=pool_b, pool_scale=pool_scale,
                   w_pool_out=w_pool_out, conv_w=conv_w, conv_b=conv_b, conv_ln_g=conv_ln_g, conv_ln_b=conv_ln_b,
                   w_conv_out=w_conv_out, w_attn_out=w_attn_out, w_o=w_o, norm_post=norm_post)
    mom1 = dict(norm_pre=m_norm_pre, w_in=m_w_in, pool_w=m_pool_w, pool_b=m_pool_b, pool_scale=m_pool_scale,
                w_pool_out=m_w_pool_out, conv_w=m_conv_w, conv_b=m_conv_b, conv_ln_g=m_conv_ln_g, conv_ln_b=m_conv_ln_b,
                w_conv_out=m_w_conv_out, w_attn_out=m_w_attn_out, w_o=m_w_o, norm_post=m_norm_post)
    mom2 = dict(norm_pre=v_norm_pre, w_in=v_w_in, pool_w=v_pool_w, pool_b=v_pool_b, pool_scale=v_pool_scale,
                w_pool_out=v_w_pool_out, conv_w=v_conv_w, conv_b=v_conv_b, conv_ln_g=v_conv_ln_g, conv_ln_b=v_conv_ln_b,
                w_conv_out=v_w_conv_out, w_attn_out=v_w_attn_out, w_o=v_w_o, norm_post=v_norm_post)
    xs = x[0]
    target = loss_target[0]

    conv_rows = jnp.pad(conv_w, ((0, 0), (0, CONV_HALO - CONV_KERNEL), (0, 0)))
    shards = [[weights[n][l].astype(BF16) for n in MATMUL_WEIGHTS] for l in range(DEPTH)]
    gathered = _gather_via_sibling([shards[0][0], conv_rows], name="gather_weights")
    full = [{"w_in": _from_source_major("w_in", gathered[0])}, None]
    conv_full = _from_source_major("conv_w", gathered[1])

    def in_sections(w):
        return dict(pg=w[:, 0:1024], c2gc=w[:, 1024:2560], q=w[:, 2560:3072], k=w[:, 3072:3584], v=w[:, 3584:4096],
                    gmga=jnp.concatenate([w[:, 4608:7680], w[:, 4096:4608]], axis=1))

    saved = []
    cur = xs
    for l in range(DEPTH):
        sec = in_sections(full[l]["w_in"])
        w_qkv = jnp.concatenate([sec["q"] * ATTN_SCALE, sec["k"], sec["v"]], axis=1)
        pw = pool_w[l].astype(BF16)
        pb, ps = pool_b[l].reshape(1, -1), pool_scale[l].reshape(1, -1)
        cb, lg, lb = conv_b[l].reshape(1, -1), conv_ln_g[l].reshape(1, -1), conv_ln_b[l].reshape(1, -1)
        h = _rms_fwd(cur, norm_pre[l].reshape(1, -1), name=f"rms_pre_fwd_{l}")
        pg = _matmul(h, sec["pg"], mode="nn", name=f"proj_pg_{l}")
        c2gc = _matmul(h, sec["c2gc"], mode="nn", name=f"proj_c2gc_{l}")
        qkv = _matmul(h, w_qkv, mode="nn", name=f"proj_qkv_{l}", out_dtype=BF16)
        if l == 0:
            gmga, *arrived = _matmul(h, sec["gmga"], mode="nn", name=f"proj_gmga_{l}", rider=_Exchange(shards[0][1:], True))
            full[0].update({n: _from_source_major(n, g) for n, g in zip(MATMUL_WEIGHTS[1:], arrived)})
        else:
            gmga = _matmul(h, sec["gmga"], mode="nn", name=f"proj_gmga_{l}")
        a_act = _pool_fwd(pg, pw, pb, ps, name=f"pool_fwd_{l}")
        b_act = _conv_fwd(c2gc, conv_full[l], cb, lg, lb, name=f"conv_fwd_{l}")
        rider = _Exchange(shards[l + 1], True) if l + 1 < DEPTH else None
        o, c_act, *arrived = _attn_fwd(qkv, gmga, name=f"attn_fwd_{l}", rider=rider)
        if rider is not None:
            full[l + 1] = {n: _from_source_major(n, g) for n, g in zip(MATMUL_WEIGHTS, arrived)}
        out_weights = [full[l][n] for n in ("w_pool_out", "w_conv_out", "w_attn_out")]
        mix, out, nxt = _merge_fwd(gmga, [a_act, b_act, c_act], out_weights, full[l]["w_o"], cur,
                                   norm_post[l].reshape(1, -1), name=f"merge_fwd_{l}")
        saved.append(dict(x=cur, h=h, pg=pg, c2gc=c2gc, qkv=qkv, gmga=gmga, a=a_act, b=b_act, c=c_act, o=o,
                          out_weights=out_weights, mix=mix, out=out, pw=pw, pb=pb, ps=ps, cb=cb, lg=lg, lb=lb))
        cur = nxt

    loss_tile, dx = _loss_head(cur, target, name="loss_head")
    loss = lax.psum(loss_tile[0, 0], ("x", "y", "c"))

    grads = {n: [None] * DEPTH for n in WEIGHT_ORDER}
    parts = {n: [None] * DEPTH for n in MATMUL_WEIGHTS}
    for l in reversed(range(DEPTH)):
        s = saved[l]
        dout, grads["norm_post"][l] = _rms_bwd(s["out"], norm_post[l].reshape(1, -1), dx, name=f"rms_post_bwd_{l}")
        grads["w_o"][l] = _matmul(s["mix"], dout, mode="tn", name=f"d_w_o_{l}", out_dtype=BF16)
        dya, dyb, dyc, dgm, da, db, dc = _merge_bwd(dout, full[l]["w_o"], s["gmga"], [s["a"], s["b"], s["c"]],
                                                    s["out_weights"], name=f"merge_bwd_{l}")
        grads["w_pool_out"][l] = _matmul(s["a"], dya, mode="tn", name=f"d_w_pool_out_{l}", out_dtype=BF16)
        grads["w_conv_out"][l] = _matmul(s["b"], dyb, mode="tn", name=f"d_w_conv_out_{l}", out_dtype=BF16)
        grads["w_attn_out"][l] = _matmul(s["c"], dyc, mode="tn", name=f"d_w_attn_out_{l}", out_dtype=BF16)
        rider = _Exchange([_to_dest_major(n, grads[n][l]) for n in MATMUL_WEIGHTS[1:]], False)
        dq, dk, dv, dga, *arrived = _attn_bwd(s["qkv"], s["o"], s["gmga"], dc, name=f"attn_bwd_{l}", rider=rider)
        for n, p in zip(MATMUL_WEIGHTS[1:], arrived):
            parts[n][l] = p
        dc2gc, dcw, dcvec = _conv_bwd(s["c2gc"], db, conv_full[l], s["cb"], s["lg"], s["lb"], name=f"conv_bwd_{l}")
        dpg, dpw, dpvec = _pool_bwd(s["pg"], da, s["pw"], s["pb"], s["ps"], name=f"pool_bwd_{l}")
        grads["conv_w"][l] = dcw[:CONV_KERNEL]
        grads["conv_b"][l], grads["conv_ln_g"][l], grads["conv_ln_b"][l] = dcvec[0], dcvec[1], dcvec[2]
        grads["pool_w"][l] = dpw
        grads["pool_b"][l] = dpvec[0].reshape(4, POOL_GROUP_DIM)
        grads["pool_scale"][l] = dpvec[1]
        dproj = jnp.concatenate([dpg, dc2gc, dq, dk, dv, dga, dgm], axis=1)
        grads["w_in"][l] = _matmul(s["h"], dproj, mode="tn", name=f"d_w_in_{l}", out_dtype=BF16)
        rider = _Exchange([_to_dest_major("w_in", grads["w_in"][l])], False)
        dh, parts["w_in"][l] = _matmul(dproj, full[l]["w_in"], mode="nt", name=f"d_h_{l}", rider=rider)
        dx, dg_pre = _rms_bwd(s["x"], norm_pre[l].reshape(1, -1), dh, name=f"rms_pre_bwd_{l}", resid=dx)
        grads["norm_pre"][l] = dg_pre.reshape(-1)
        grads["norm_post"][l] = grads["norm_post"][l].reshape(-1)
    small_grads = {n: jnp.stack(grads[n]) for n in SMALL}
    replicated = jnp.concatenate([small_grads[n].reshape(-1) for n in SMALL[1:]])
    small = _pack_rows([_to_dest_major("conv_w", small_grads["conv_w"]).reshape(N_DEV, -1),
                        jnp.broadcast_to(replicated, (N_DEV, replicated.size))], 16).astype(BF16)
    (small_parts,) = _all_to_all([small], name="exchange_grads", same_block=False)

    outs = [dict(), dict(), dict(), dict()]
    for n in MATMUL_WEIGHTS:
        res = _adamw(parts[n], weights[n], mom1[n], mom2[n], name=f"adamw_{n}")
        for o, r in zip(outs, res):
            o[n] = r

    def packed(tree):
        return _pack_rows([tree[n].reshape(-1) for n in SMALL], 16)[None]

    shapes = [(n, weights[n].shape) for n in SMALL]
    res = _adamw([small_parts], packed(weights), packed(mom1), packed(mom2), name="adamw_small")
    for o, r in zip(outs, res):
        o.update(_unpack(r, shapes))
    return (loss, dx[None], *[o[n] for o in outs for n in WEIGHT_ORDER])
```

```python
import functools

import jax
import jax.numpy as jnp
from jax import lax
from jax.experimental import pallas as pl
from jax.experimental.pallas import tpu as pltpu

F32 = jnp.float32
BF16 = jnp.bfloat16

D_MODEL = 1024
DEPTH = 2
POOL_WINDOWS = (2, 4, 8, 16)
POOL_GROUP_DIM = 128
BRANCH_WIDTH = 512
CONV_KERNEL = 31
CONV_HALO = 32
POOL_HALO = 16
HEAD_DIM = 64
HEAD_PAIR = 128
N_HEAD_PAIRS = 4
ATTN_SCALE = 0.125
LOG_F32_ZERO = -104.0
RMS_EPS = 1e-6
LN_EPS = 1e-5
N_DEV = 8
LANES = 128

ADAM_LR = 0.001
ADAM_B1 = 0.9
ADAM_B2 = 0.999
ADAM_EPS = 1e-08
ADAM_WD = 0.01
ADAM_STEP = 10

ROW_TILE = 256
NORM_ROW_TILE = 512
ATTN_BLOCK = 256
ATTN_BLOCKS_PER_STEP = 2
MM_TILE = 1024
MM_K_TILE = 1280
ADAM_TILE_ELEMS = 256 * 1024
VMEM_LIMIT = 48 * 1024 * 1024

MESH = pl.DeviceIdType.MESH


def _params(n_axes):
    return pltpu.CompilerParams(dimension_semantics=("arbitrary",) * n_axes, vmem_limit_bytes=VMEM_LIMIT)


def _tile(n, pref):
    if n <= pref:
        return n
    t = (pref // LANES) * LANES
    while n % t:
        t -= LANES
    return t


def _dot(a, b):
    return jnp.dot(a, b, preferred_element_type=F32)


def _dot_nt(a, b):
    return lax.dot_general(a, b, (((1,), (1,)), ((), ())), preferred_element_type=F32)


def _dot_tn(a, b):
    return lax.dot_general(a, b, (((0,), (0,)), ((), ())), preferred_element_type=F32)


def _sigmoid(x):
    return 1.0 / (1.0 + jnp.exp(-x))


def _silu_grad(x, s):
    return s * (1.0 + x * (1.0 - s))


def _matmul(a, b, *, mode, name, out_dtype=F32, rider=None):
    if mode == "nn":
        (m, k), n = a.shape, b.shape[1]
    elif mode == "nt":
        (m, k), n = a.shape, b.shape[0]
    else:
        (k, m), n = a.shape, b.shape[1]
    tm, tn, tk = _tile(m, MM_TILE), _tile(n, MM_TILE), _tile(k, MM_K_TILE)
    nk = k // tk
    grid = (m // tm, n // tn, nk)
    dot = {"nn": _dot, "nt": _dot_nt, "tn": _dot_tn}[mode]
    a_spec = pl.BlockSpec((tk, tm), lambda i, j, kk: (kk, i)) if mode == "tn" else pl.BlockSpec((tm, tk), lambda i, j, kk: (i, kk))
    b_spec = pl.BlockSpec((tn, tk), lambda i, j, kk: (j, kk)) if mode == "nt" else pl.BlockSpec((tk, tn), lambda i, j, kk: (kk, j))
    o_spec = pl.BlockSpec((tm, tn), lambda i, j, kk: (i, j))

    def body(*refs):
        (a_ref, b_ref), (o_ref,), scratch, finish_rider = _ride(rider, 2, 1, refs, grid)
        part = dot(a_ref[...], b_ref[...])
        if nk == 1:
            o_ref[...] = part.astype(out_dtype)
        else:
            scr = scratch[0]
            kk = pl.program_id(2)

            @pl.when(kk == 0)
            def _():
                scr[...] = part

            @pl.when(kk > 0)
            def _():
                scr[...] += part

            @pl.when(kk == nk - 1)
            def _():
                o_ref[...] = scr[...].astype(out_dtype)

        finish_rider()

    extra = rider or _NO_RIDER
    res = pl.pallas_call(
        body, name=name, grid=grid,
        in_specs=[a_spec, b_spec] + extra.specs, out_specs=[o_spec] + extra.specs,
        out_shape=[jax.ShapeDtypeStruct((m, n), out_dtype)] + extra.out_shape,
        scratch_shapes=([pltpu.VMEM((tm, tn), F32)] if nk > 1 else []) + extra.scratch_shapes,
        compiler_params=_params(3),
    )(a, b, *extra.srcs)
    return res if rider is not None else res[0]


def _rms_fwd(x, g, *, name):
    t = x.shape[0]
    tm = _tile(t, NORM_ROW_TILE)
    row = pl.BlockSpec((tm, D_MODEL), lambda i: (i, 0))
    vec = pl.BlockSpec((1, D_MODEL), lambda i: (0, 0))

    def body(x_ref, g_ref, o_ref):
        xv = x_ref[...]
        y = xv * lax.rsqrt(jnp.mean(xv * xv, axis=-1, keepdims=True) + RMS_EPS) * g_ref[...]
        o_ref[...] = y.astype(BF16)

    return pl.pallas_call(
        body, name=name, grid=(t // tm,),
        in_specs=[row, vec], out_specs=row,
        out_shape=jax.ShapeDtypeStruct((t, D_MODEL), BF16),
        compiler_params=_params(1),
    )(x, g)


def _rms_bwd(xin, g, dy, *, name, resid=None):
    t = xin.shape[0]
    tm = _tile(t, NORM_ROW_TILE)
    row = pl.BlockSpec((tm, D_MODEL), lambda i: (i, 0))
    vec = pl.BlockSpec((1, D_MODEL), lambda i: (0, 0))
    has_resid = resid is not None
    out_dtype = F32 if has_resid else BF16

    def body(*refs):
        x_ref, g_ref, dy_ref = refs[0], refs[1], refs[2]
        dx_ref, dg_ref = refs[-2], refs[-1]
        xv, dyv = x_ref[...], dy_ref[...]
        r = lax.rsqrt(jnp.mean(xv * xv, axis=-1, keepdims=True) + RMS_EPS)
        a = dyv * g_ref[...]
        dx = r * a - xv * (r * r * r) * jnp.mean(a * xv, axis=-1, keepdims=True)
        if has_resid:
            dx = dx + refs[3][...]
        dx_ref[...] = dx.astype(out_dtype)
        part = jnp.sum(dyv * xv * r, axis=0, keepdims=True)

        @pl.when(pl.program_id(0) == 0)
        def _():
            dg_ref[...] = part

        @pl.when(pl.program_id(0) > 0)
        def _():
            dg_ref[...] += part

    return pl.pallas_call(
        body, name=name, grid=(t // tm,),
        in_specs=[row, vec, row] + ([row] if has_resid else []), out_specs=[row, vec],
        out_shape=[jax.ShapeDtypeStruct((t, D_MODEL), out_dtype), jax.ShapeDtypeStruct((1, D_MODEL), F32)],
        compiler_params=_params(1),
    )(*((xin, g, dy) + ((resid,) if has_resid else ())))


def _loss_head(x, target, *, name):
    t = x.shape[0]
    tm = _tile(t, NORM_ROW_TILE)
    row = pl.BlockSpec((tm, D_MODEL), lambda i: (i, 0))
    acc = pl.BlockSpec((8, LANES), lambda i: (0, 0))

    def body(x_ref, t_ref, l_ref, dx_ref):
        diff = x_ref[...] - t_ref[...]
        dx_ref[...] = diff * (1.0 / D_MODEL)
        part = 0.5 * jnp.sum(jnp.mean(diff * diff, axis=-1, keepdims=True), axis=0, keepdims=True)

        @pl.when(pl.program_id(0) == 0)
        def _():
            l_ref[...] = jnp.zeros((8, LANES), F32) + part

        @pl.when(pl.program_id(0) > 0)
        def _():
            l_ref[...] += part

    return pl.pallas_call(
        body, name=name, grid=(t // tm,),
        in_specs=[row, row], out_specs=[acc, row],
        out_shape=[jax.ShapeDtypeStruct((8, LANES), F32), jax.ShapeDtypeStruct((t, D_MODEL), F32)],
        compiler_params=_params(1),
    )(x, target)


def _window_sum(ext, n_doublings, forward):
    rows = ext.shape[0]
    s, sh = ext, 1
    for _ in range(n_doublings):
        s = s + pltpu.roll(s, sh if forward else rows - sh, 0)
        sh *= 2
    return s


def _pool_fwd(pg, pool_w, pool_b, pool_scale, *, name):
    t = pg.shape[0]
    tm = _tile(t, ROW_TILE)

    def body(pg_ref, w_ref, b_ref, s_ref, o_ref, halo):
        i = pl.program_id(0)

        @pl.when(i == 0)
        def _():
            halo[...] = jnp.zeros_like(halo)

        p = pg_ref[:, :BRANCH_WIDTH]
        gate = pg_ref[:, BRANCH_WIDTH:]
        ext = jnp.concatenate([halo[...], p], axis=0)
        pos = i * tm + lax.broadcasted_iota(jnp.int32, (tm, 1), 0)
        outs = []
        for g, w in enumerate(POOL_WINDOWS):
            cols = slice(g * POOL_GROUP_DIM, (g + 1) * POOL_GROUP_DIM)
            cnt = jnp.minimum(pos + 1, w).astype(F32)
            d = _window_sum(ext[:, cols], g + 1, True)[POOL_HALO:] / cnt - p[:, cols]
            y = (_dot(d.astype(BF16), w_ref[g]) + b_ref[:, cols]) * s_ref[:, cols]
            gg = gate[:, cols]
            outs.append(y * (gg * _sigmoid(gg)))
        o_ref[...] = jnp.concatenate(outs, axis=1).astype(BF16)
        halo[...] = p[tm - POOL_HALO:, :]

    vec = pl.BlockSpec((1, BRANCH_WIDTH), lambda i: (0, 0))
    return pl.pallas_call(
        body, name=name, grid=(t // tm,),
        in_specs=[pl.BlockSpec((tm, 2 * BRANCH_WIDTH), lambda i: (i, 0)),
                  pl.BlockSpec((4, POOL_GROUP_DIM, POOL_GROUP_DIM), lambda i: (0, 0, 0)), vec, vec],
        out_specs=pl.BlockSpec((tm, BRANCH_WIDTH), lambda i: (i, 0)),
        out_shape=jax.ShapeDtypeStruct((t, BRANCH_WIDTH), BF16),
        scratch_shapes=[pltpu.VMEM((POOL_HALO, BRANCH_WIDTH), F32)],
        compiler_params=_params(1),
    )(pg, pool_w, pool_b, pool_scale)


def _pool_bwd(pg, d_out, pool_w, pool_b, pool_scale, *, name):
    t = pg.shape[0]
    tm = _tile(t, ROW_TILE)
    nt = t // tm
    halo_per_tile = tm // POOL_HALO

    def body(pg_ref, halo_ref, do_ref, w_ref, b_ref, s_ref, dpg_ref, dw_ref, dvec_ref, carry):
        i = pl.program_id(0)
        ri = nt - 1 - i

        @pl.when(i == 0)
        def _():
            carry[...] = jnp.zeros_like(carry)
            dw_ref[...] = jnp.zeros_like(dw_ref)
            dvec_ref[...] = jnp.zeros_like(dvec_ref)

        p = pg_ref[:, :BRANCH_WIDTH]
        gate = pg_ref[:, BRANCH_WIDTH:]
        hp = jnp.where(ri > 0, halo_ref[:, :BRANCH_WIDTH], 0.0)
        ext = jnp.concatenate([hp, p], axis=0)
        pos = ri * tm + lax.broadcasted_iota(jnp.int32, (tm, 1), 0)
        dps, dgs, dbs, dss = [], [], [], []
        for g, w in enumerate(POOL_WINDOWS):
            cols = slice(g * POOL_GROUP_DIM, (g + 1) * POOL_GROUP_DIM)
            cnt = jnp.minimum(pos + 1, w).astype(F32)
            d = (_window_sum(ext[:, cols], g + 1, True)[POOL_HALO:] / cnt - p[:, cols]).astype(BF16)
            y1 = _dot(d, w_ref[g]) + b_ref[:, cols]
            scale = s_ref[:, cols]
            y2 = y1 * scale
            gg = gate[:, cols]
            sg = _sigmoid(gg)
            do = do_ref[:, cols]
            dy2 = do * (gg * sg)
            dgs.append(do * y2 * _silu_grad(gg, sg))
            dss.append(jnp.sum(dy2 * y1, axis=0, keepdims=True))
            dy1 = dy2 * scale
            dbs.append(jnp.sum(dy1, axis=0, keepdims=True))
            dy1b = dy1.astype(BF16)
            dw_ref[g] += _dot_tn(d, dy1b)
            dd = _dot_nt(dy1b, w_ref[g])
            dpool = dd / cnt
            dext = jnp.concatenate([dpool, carry[:, cols]], axis=0)
            dps.append(_window_sum(dext, g + 1, False)[:tm] - dd)
            carry[:, cols] = dpool[:POOL_HALO]
        dpg_ref[...] = jnp.concatenate(dps + dgs, axis=1).astype(BF16)
        dvec_ref[0:1, :] += jnp.concatenate(dbs, axis=1)
        dvec_ref[1:2, :] += jnp.concatenate(dss, axis=1)

    vec = pl.BlockSpec((1, BRANCH_WIDTH), lambda i: (0, 0))
    wspec = pl.BlockSpec((4, POOL_GROUP_DIM, POOL_GROUP_DIM), lambda i: (0, 0, 0))
    return pl.pallas_call(
        body, name=name, grid=(nt,),
        in_specs=[pl.BlockSpec((tm, 2 * BRANCH_WIDTH), lambda i: (nt - 1 - i, 0)),
                  pl.BlockSpec((POOL_HALO, 2 * BRANCH_WIDTH), lambda i: (jnp.maximum((nt - 1 - i) * halo_per_tile - 1, 0), 0)),
                  pl.BlockSpec((tm, BRANCH_WIDTH), lambda i: (nt - 1 - i, 0)), wspec, vec, vec],
        out_specs=[pl.BlockSpec((tm, 2 * BRANCH_WIDTH), lambda i: (nt - 1 - i, 0)), wspec,
                   pl.BlockSpec((8, BRANCH_WIDTH), lambda i: (0, 0))],
        out_shape=[jax.ShapeDtypeStruct((t, 2 * BRANCH_WIDTH), BF16),
                   jax.ShapeDtypeStruct((4, POOL_GROUP_DIM, POOL_GROUP_DIM), F32),
                   jax.ShapeDtypeStruct((8, BRANCH_WIDTH), F32)],
        scratch_shapes=[pltpu.VMEM((POOL_HALO, BRANCH_WIDTH), F32)],
        compiler_params=_params(1),
    )(pg, pg, d_out, pool_w, pool_b, pool_scale)


CONV_TILE_ROWS = 64


def _for_conv_tiles(tm, fn):
    def step(it, carry):
        rows = pl.ds(pl.multiple_of(it * CONV_TILE_ROWS, CONV_TILE_ROWS), CONV_TILE_ROWS)
        for c in range(0, BRANCH_WIDTH, LANES):
            fn(rows, slice(c, c + LANES))
        return carry

    lax.fori_loop(0, tm // CONV_TILE_ROWS, step, 0)


def _sublane_shifts(shifted_ref, x, direction):
    rows = x.shape[0]
    shifted_ref[0] = x
    for b in range(1, 8):
        shifted_ref[b] = pltpu.roll(x, b if direction > 0 else rows - b, 0)


CONV_REACH = 8 * ((CONV_KERNEL - 1) // 8)


def _tap_tiles(shifted_ref, base, rows, cols, direction):
    for b in range(8):
        lo = pl.multiple_of(base + rows.start - (CONV_REACH if direction > 0 else 0), 8)
        window = shifted_ref[b, pl.ds(lo, CONV_TILE_ROWS + CONV_REACH), cols]
        for a in range((CONV_KERNEL - 1 - b) // 8 + 1):
            off = CONV_REACH - 8 * a if direction > 0 else 8 * a
            yield 8 * a + b, window[off:off + CONV_TILE_ROWS]


def _tap_sum(shifted_ref, w_ref, base, rows, cols, direction):
    acc = None
    for j, tile in _tap_tiles(shifted_ref, base, rows, cols, direction):
        k = CONV_KERNEL - 1 - j
        term = w_ref[k:k + 1, cols] * tile
        acc = term if acc is None else acc + term
    return acc


def _causal_conv(ext8_ref, w_ref, cv_ref, tm):
    def tile(rows, cols):
        cv_ref[rows, cols] = _tap_sum(ext8_ref, w_ref, CONV_HALO, rows, cols, 1)

    _for_conv_tiles(tm, tile)


def _conv_fwd(c2gc, conv_w, conv_b, ln_g, ln_b, *, name):
    t = c2gc.shape[0]
    tm = _tile(t, ROW_TILE)

    def body(c_ref, w_ref, cb_ref, g_ref, b_ref, o_ref, halo, ext8_ref, cv_ref):
        @pl.when(pl.program_id(0) == 0)
        def _():
            halo[...] = jnp.zeros_like(halo)

        u = c_ref[:, :BRANCH_WIDTH] * _sigmoid(c_ref[:, BRANCH_WIDTH:2 * BRANCH_WIDTH])
        gate = c_ref[:, 2 * BRANCH_WIDTH:]
        _sublane_shifts(ext8_ref, jnp.concatenate([halo[...], u], axis=0), 1)
        halo[...] = u[tm - CONV_HALO:, :]
        _causal_conv(ext8_ref, w_ref, cv_ref, tm)
        cv = cv_ref[...] + cb_ref[...]
        mu = jnp.mean(cv, axis=-1, keepdims=True)
        xc = cv - mu
        var = jnp.mean(xc * xc, axis=-1, keepdims=True)
        ln = xc * lax.rsqrt(var + LN_EPS) * g_ref[...] + b_ref[...]
        o_ref[...] = (ln * _sigmoid(ln) * (gate * _sigmoid(gate))).astype(BF16)

    vec = pl.BlockSpec((1, BRANCH_WIDTH), lambda i: (0, 0))
    return pl.pallas_call(
        body, name=name, grid=(t // tm,),
        in_specs=[pl.BlockSpec((tm, 3 * BRANCH_WIDTH), lambda i: (i, 0)),
                  pl.BlockSpec((CONV_HALO, BRANCH_WIDTH), lambda i: (0, 0)), vec, vec, vec],
        out_specs=pl.BlockSpec((tm, BRANCH_WIDTH), lambda i: (i, 0)),
        out_shape=jax.ShapeDtypeStruct((t, BRANCH_WIDTH), BF16),
        scratch_shapes=[pltpu.VMEM((CONV_HALO, BRANCH_WIDTH), F32), pltpu.VMEM((8, tm + CONV_HALO, BRANCH_WIDTH), F32),
                        pltpu.VMEM((tm, BRANCH_WIDTH), F32)],
        compiler_params=_params(1),
    )(c2gc, conv_w, conv_b, ln_g, ln_b)


def _conv_bwd(c2gc, d_out, conv_w, conv_b, ln_g, ln_b, *, name):
    t = c2gc.shape[0]
    tm = _tile(t, ROW_TILE)
    nt = t // tm
    halo_per_tile = tm // CONV_HALO

    def body(c_ref, halo_ref, do_ref, w_ref, cb_ref, g_ref, b_ref, dc_ref, dw_ref, dvec_ref,
             carry, ext8_ref, cv_ref, dext8_ref, du_ref, dw_acc):
        i = pl.program_id(0)
        ri = nt - 1 - i

        @pl.when(i == 0)
        def _():
            carry[...] = jnp.zeros_like(carry)
            dw_acc[...] = jnp.zeros_like(dw_acc)
            dvec_ref[...] = jnp.zeros_like(dvec_ref)

        a = c_ref[:, :BRANCH_WIDTH]
        sb = _sigmoid(c_ref[:, BRANCH_WIDTH:2 * BRANCH_WIDTH])
        gate = c_ref[:, 2 * BRANCH_WIDTH:]
        hu = halo_ref[:, :BRANCH_WIDTH] * _sigmoid(halo_ref[:, BRANCH_WIDTH:2 * BRANCH_WIDTH])
        _sublane_shifts(ext8_ref, jnp.concatenate([jnp.where(ri > 0, hu, 0.0), a * sb], axis=0), 1)
        _causal_conv(ext8_ref, w_ref, cv_ref, tm)
        cv = cv_ref[...] + cb_ref[...]
        mu = jnp.mean(cv, axis=-1, keepdims=True)
        xc = cv - mu
        rs = lax.rsqrt(jnp.mean(xc * xc, axis=-1, keepdims=True) + LN_EPS)
        n = xc * rs
        ln = n * g_ref[...] + b_ref[...]
        sl = _sigmoid(ln)
        sgate = _sigmoid(gate)
        do = do_ref[...]
        dgate = do * (ln * sl) * _silu_grad(gate, sgate)
        dln = do * (gate * sgate) * _silu_grad(ln, sl)
        dn = dln * g_ref[...]
        dcv = rs * (dn - jnp.mean(dn, axis=-1, keepdims=True) - n * jnp.mean(dn * n, axis=-1, keepdims=True))
        dvec_ref[0:1, :] += jnp.sum(dcv, axis=0, keepdims=True)
        dvec_ref[1:2, :] += jnp.sum(dln * n, axis=0, keepdims=True)
        dvec_ref[2:3, :] += jnp.sum(dln, axis=0, keepdims=True)
        _sublane_shifts(dext8_ref, jnp.concatenate([dcv, carry[...]], axis=0), -1)
        carry[...] = dcv[:CONV_HALO]
        def tile(rows, cols):
            du_ref[rows, cols] = _tap_sum(dext8_ref, w_ref, 0, rows, cols, -1)
            d_tile = dext8_ref[0, rows, cols]
            for j, u_tile in _tap_tiles(ext8_ref, CONV_HALO, rows, cols, 1):
                prod = d_tile * u_tile
                part = prod[0:8]
                for q in range(8, CONV_TILE_ROWS, 8):
                    part = part + prod[q:q + 8]
                dw_acc[CONV_KERNEL - 1 - j, :, cols] += part

        _for_conv_tiles(tm, tile)
        du = du_ref[...]
        dc_ref[...] = jnp.concatenate([du * sb, du * a * sb * (1.0 - sb), dgate], axis=1).astype(BF16)

        @pl.when(i == nt - 1)
        def _():
            dw_ref[...] = jnp.sum(dw_acc[...], axis=1)

    vec = pl.BlockSpec((1, BRANCH_WIDTH), lambda i: (0, 0))
    wspec = pl.BlockSpec((CONV_HALO, BRANCH_WIDTH), lambda i: (0, 0))
    return pl.pallas_call(
        body, name=name, grid=(nt,),
        in_specs=[pl.BlockSpec((tm, 3 * BRANCH_WIDTH), lambda i: (nt - 1 - i, 0)),
                  pl.BlockSpec((CONV_HALO, 3 * BRANCH_WIDTH), lambda i: (jnp.maximum((nt - 1 - i) * halo_per_tile - 1, 0), 0)),
                  pl.BlockSpec((tm, BRANCH_WIDTH), lambda i: (nt - 1 - i, 0)), wspec, vec, vec, vec],
        out_specs=[pl.BlockSpec((tm, 3 * BRANCH_WIDTH), lambda i: (nt - 1 - i, 0)), wspec,
                   pl.BlockSpec((8, BRANCH_WIDTH), lambda i: (0, 0))],
        out_shape=[jax.ShapeDtypeStruct((t, 3 * BRANCH_WIDTH), BF16),
                   jax.ShapeDtypeStruct((CONV_HALO, BRANCH_WIDTH), F32),
                   jax.ShapeDtypeStruct((8, BRANCH_WIDTH), F32)],
        scratch_shapes=[pltpu.VMEM((CONV_HALO, BRANCH_WIDTH), F32),
                        pltpu.VMEM((8, tm + CONV_HALO, BRANCH_WIDTH), F32), pltpu.VMEM((tm, BRANCH_WIDTH), F32),
                        pltpu.VMEM((8, tm + CONV_HALO, BRANCH_WIDTH), F32), pltpu.VMEM((tm, BRANCH_WIDTH), F32),
                        pltpu.VMEM((CONV_HALO, 8, BRANCH_WIDTH), F32)],
        compiler_params=_params(1),
    )(c2gc, c2gc, d_out, conv_w, conv_b, ln_g, ln_b)


def _merge_specs(tm):
    gates = [pl.BlockSpec((tm, D_MODEL), functools.partial(lambda i, b: (i, b), b=b)) for b in range(3)]
    acts = [pl.BlockSpec((tm, BRANCH_WIDTH), lambda i: (i, 0))] * 3
    weights = [pl.BlockSpec((BRANCH_WIDTH, D_MODEL), lambda i: (0, 0))] * 3
    return gates + acts + weights


def _merge_fwd(gmga, acts, weights, w_o, x, g_post, g_next, *, name):
    t = gmga.shape[0]
    tm = _tile(t, ROW_TILE)
    row = pl.BlockSpec((tm, D_MODEL), lambda i: (i, 0))
    vec = pl.BlockSpec((1, D_MODEL), lambda i: (0, 0))

    has_next = g_next is not None

    def body(g0, g1, g2, a0, a1, a2, w0, w1, w2, wo_ref, x_ref, gp_ref, *rest):
        gn_ref = rest[0] if has_next else None
        mix_ref, out_ref, nxt_ref = rest[-4:-1] if has_next else rest[-3:]
        m = None
        for g_ref, a_ref, w_ref in ((g0, a0, w0), (g1, a1, w1), (g2, a2, w2)):
            term = _sigmoid(g_ref[...]) * _dot(a_ref[...], w_ref[...])
            m = term if m is None else m + term
        mix = m.astype(BF16)
        mix_ref[...] = mix
        out = _dot(mix, wo_ref[...])
        out_ref[...] = out
        y = out * lax.rsqrt(jnp.mean(out * out, axis=-1, keepdims=True) + RMS_EPS) * gp_ref[...]
        nxt = x_ref[...] + y
        nxt_ref[...] = nxt
        if has_next:
            h = nxt * lax.rsqrt(jnp.mean(nxt * nxt, axis=-1, keepdims=True) + RMS_EPS) * gn_ref[...]
            rest[-1][...] = h.astype(BF16)

    extra_in = [g_next] if has_next else []
    return pl.pallas_call(
        body, name=name, grid=(t // tm,),
        in_specs=_merge_specs(tm) + [pl.BlockSpec((D_MODEL, D_MODEL), lambda i: (0, 0)), row, vec] + [vec] * has_next,
        out_specs=[row, row, row] + [row] * has_next,
        out_shape=[jax.ShapeDtypeStruct((t, D_MODEL), BF16), jax.ShapeDtypeStruct((t, D_MODEL), F32),
                   jax.ShapeDtypeStruct((t, D_MODEL), F32)] + [jax.ShapeDtypeStruct((t, D_MODEL), BF16)] * has_next,
        compiler_params=_params(1),
    )(gmga, gmga, gmga, *acts, *weights, w_o, x, g_post, *extra_in)


def _merge_bwd(dout, w_o, gmga, acts, weights, *, name):
    t = gmga.shape[0]
    tm = _tile(t, ROW_TILE)
    row = pl.BlockSpec((tm, D_MODEL), lambda i: (i, 0))
    wide = pl.BlockSpec((tm, 3 * D_MODEL), lambda i: (i, 0))
    act = pl.BlockSpec((tm, BRANCH_WIDTH), lambda i: (i, 0))

    def body(dout_ref, wo_ref, g0, g1, g2, a0, a1, a2, w0, w1, w2, d0, d1, d2, dg_ref, da0, da1, da2):
        dmv = _dot_nt(dout_ref[...], wo_ref[...])
        for k, (g_ref, a_ref, w_ref, dy_ref, da_ref) in enumerate(
                ((g0, a0, w0, d0, da0), (g1, a1, w1, d1, da1), (g2, a2, w2, d2, da2))):
            s = _sigmoid(g_ref[...])
            dy = (dmv * s).astype(BF16)
            dy_ref[...] = dy
            y = _dot(a_ref[...], w_ref[...])
            dg_ref[:, k * D_MODEL:(k + 1) * D_MODEL] = (dmv * y * s * (1.0 - s)).astype(BF16)
            da_ref[...] = _dot_nt(dy, w_ref[...])

    return pl.pallas_call(
        body, name=name, grid=(t // tm,),
        in_specs=[row, pl.BlockSpec((D_MODEL, D_MODEL), lambda i: (0, 0))] + _merge_specs(tm),
        out_specs=[row, row, row, wide, act, act, act],
        out_shape=[jax.ShapeDtypeStruct((t, D_MODEL), BF16)] * 3 + [jax.ShapeDtypeStruct((t, 3 * D_MODEL), BF16)]
        + [jax.ShapeDtypeStruct((t, BRANCH_WIDTH), F32)] * 3,
        compiler_params=_params(1),
    )(dout, w_o, gmga, gmga, gmga, *acts, *weights)


GA_BLOCK = 3 * D_MODEL // HEAD_PAIR


def _split_heads(x, lane_is_first):
    zero = jnp.zeros_like(x)
    return jnp.concatenate([jnp.where(lane_is_first, x, zero), jnp.where(lane_is_first, zero, x)], axis=0)


def _side_by_side(x, rows):
    return jnp.concatenate([x[:rows], x[rows:]], axis=1)


def _split_bf16(x):
    hi = x.astype(BF16)
    return hi, (x - hi.astype(F32)).astype(BF16)


def _scores(qcat, kblk, mask):
    z = _dot_nt(qcat, kblk)
    e = jnp.exp(-jnp.abs(z))
    sp = jnp.maximum(z, 0.0) + jnp.log(1.0 + e)
    l1m = -sp
    if mask is not None:
        l1m = jnp.where(mask, l1m, 0.0)
    inv = 1.0 / (1.0 + e)
    pos = z >= 0.0
    return z - sp, l1m, jnp.where(pos, 1.0, e) * inv, jnp.where(pos, e, 1.0) * inv


def _attn_consts(blk):
    lane_is_first = lax.broadcasted_iota(jnp.int32, (1, HEAD_PAIR), 1) < HEAD_DIM
    r = lax.broadcasted_iota(jnp.int32, (blk, blk), 0)
    c = lax.broadcasted_iota(jnp.int32, (blk, blk), 1)
    after = (r > c).astype(BF16)
    from_here = (r >= c).astype(BF16)
    qrow = lax.broadcasted_iota(jnp.int32, (2 * blk, blk), 0)
    qrow = jnp.where(qrow >= blk, qrow - blk, qrow)
    causal = lax.broadcasted_iota(jnp.int32, (2 * blk, blk), 1) < qrow
    return lane_is_first, after, from_here, causal


def _while_mass_left(qi, carry, block):
    def alive(c):
        return jnp.max(c[0]) > LOG_F32_ZERO

    def cond(state):
        return jnp.logical_and(state[0] < qi, state[1])

    def step(state):
        new = block(qi - 1 - state[0], state[2])
        return state[0] + 1, alive(new), new

    return lax.while_loop(cond, step, (jnp.int32(0), alive(carry), carry))[2]


def _ride(rider, n_in, n_out, refs, grid):
    if rider is None:
        return refs[:n_in], refs[n_in:n_in + n_out], refs[n_in + n_out:], lambda: None
    n = rider.n
    ins, srcs = refs[:n_in], refs[n_in:n_in + n]
    outs, dsts = refs[n_in + n:n_in + n + n_out], refs[n_in + n + n_out:n_in + 2 * n + n_out]
    rest = refs[n_in + 2 * n + n_out:]
    scratch, sems = rest[:len(rest) - 3], rest[len(rest) - 3:]
    step, n_steps = 0, 1
    for axis, size in enumerate(grid):
        step, n_steps = step * size + pl.program_id(axis), n_steps * size

    @pl.when(step == 0)
    def _():
        rider.start(srcs, dsts, sems)

    def finish():
        @pl.when(step == n_steps - 1)
        def _():
            rider.finish(srcs, dsts, sems)

    return ins, outs, scratch, finish


def _attn_fwd(qkv, gmga, *, name, rider=None):
    t = qkv.shape[0]
    blk, step_rows = _tile(t, ATTN_BLOCK), _tile(t, ATTN_BLOCK * ATTN_BLOCKS_PER_STEP)
    nq = t // step_rows

    def body(*refs):
        (q_ref, k_ref, v_ref, ga_ref), (o_ref, cv_ref), _, finish_rider = _ride(rider, 4, 2, refs, (N_HEAD_PAIRS, nq))
        lane_is_first, after, _, causal = _attn_consts(blk)

        for sub in range(step_rows // blk):
            mine = slice(sub * blk, (sub + 1) * blk)
            qi = pl.program_id(1) * (step_rows // blk) + sub
            qcat = _split_heads(q_ref[mine, :], lane_is_first)

            def block(kb, carry, mask, qcat=qcat):
                run, acc = carry
                rows = pl.ds(pl.multiple_of(kb * blk, blk), blk)
                lb, l1m, _, _ = _scores(qcat, k_ref[rows, :], mask)
                hi, lo = _split_bf16(l1m)
                w = jnp.exp(lb + (_dot(hi, after) + _dot(lo, after) + run))
                if mask is not None:
                    w = jnp.where(mask, w, 0.0)
                vcat = _split_heads(v_ref[rows, :], lane_is_first)
                acc = acc + _dot(_side_by_side(w.astype(BF16), blk), vcat)
                return run + jnp.sum(l1m, axis=-1, keepdims=True), acc

            carry = block(qi, (jnp.zeros((2 * blk, 1), F32), jnp.zeros((blk, HEAD_PAIR), F32)), causal)
            _, o = _while_mass_left(qi, carry, lambda kb, c, block=block: block(kb, c, None))
            o_ref[mine, :] = o
            ga = ga_ref[mine, :]
            cv_ref[mine, :] = (o * (ga * _sigmoid(ga))).astype(BF16)
        finish_rider()

    qspec = pl.BlockSpec((step_rows, HEAD_PAIR), lambda p, i: (i, p))
    extra = rider or _NO_RIDER
    return pl.pallas_call(
        body, name=name, grid=(N_HEAD_PAIRS, nq),
        in_specs=[qspec,
                  pl.BlockSpec((t, HEAD_PAIR), lambda p, i: (0, N_HEAD_PAIRS + p)),
                  pl.BlockSpec((t, HEAD_PAIR), lambda p, i: (0, 2 * N_HEAD_PAIRS + p)),
                  pl.BlockSpec((step_rows, HEAD_PAIR), lambda p, i: (i, GA_BLOCK + p))] + extra.specs,
        out_specs=[qspec, qspec] + extra.specs,
        out_shape=[jax.ShapeDtypeStruct((t, BRANCH_WIDTH), F32), jax.ShapeDtypeStruct((t, BRANCH_WIDTH), BF16)] + extra.out_shape,
        scratch_shapes=extra.scratch_shapes,
        compiler_params=_params(2),
    )(qkv, qkv, qkv, gmga, *extra.srcs)


def _attn_bwd(qkv, o, gmga, dcv, *, name, rider=None):
    t = qkv.shape[0]
    blk, step_rows = _tile(t, ATTN_BLOCK), _tile(t, ATTN_BLOCK * ATTN_BLOCKS_PER_STEP)
    nq = t // step_rows

    def body(*refs):
        ins, outs, (dk_acc, dv_acc), finish_rider = _ride(rider, 6, 4, refs, (N_HEAD_PAIRS, nq))
        q_ref, k_ref, v_ref, o_ref, ga_ref, dcv_ref = ins
        dq_ref, dk_ref, dv_ref, dga_ref = outs
        lane_is_first, after, from_here, causal = _attn_consts(blk)

        @pl.when(pl.program_id(1) == 0)
        def _():
            dk_acc[...] = jnp.zeros_like(dk_acc)
            dv_acc[...] = jnp.zeros_like(dv_acc)

        for sub in range(step_rows // blk):
            mine = slice(sub * blk, (sub + 1) * blk)
            qi = pl.program_id(1) * (step_rows // blk) + sub
            ga, ov, dcvv = ga_ref[mine, :], o_ref[mine, :], dcv_ref[mine, :]
            sg = _sigmoid(ga)
            dob = (dcvv * (ga * sg)).astype(BF16)
            dga_ref[mine, :] = (dcvv * ov * _silu_grad(ga, sg)).astype(BF16)
            gt = dob.astype(F32) * ov
            g_total = jnp.concatenate(
                [jnp.sum(jnp.where(lane_is_first, gt, 0.0), axis=-1, keepdims=True),
                 jnp.sum(jnp.where(lane_is_first, 0.0, gt), axis=-1, keepdims=True)], axis=0)
            qcat = _split_heads(q_ref[mine, :], lane_is_first)
            docat = _split_heads(dob, lane_is_first)

            def block(kb, carry, mask, g_total=g_total, qcat=qcat, docat=docat):
                run, g_run, dq = carry
                rows = pl.ds(pl.multiple_of(kb * blk, blk), blk)
                kblk = k_ref[rows, :]
                lb, l1m, sig, one_m_sig = _scores(qcat, kblk, mask)
                hi, lo = _split_bf16(l1m)
                w = jnp.exp(lb + (_dot(hi, after) + _dot(lo, after) + run))
                if mask is not None:
                    w = jnp.where(mask, w, 0.0)
                wb = w.astype(BF16)
                g = _dot_nt(docat, v_ref[rows, :]) * wb.astype(F32)
                ghi, glo = _split_bf16(g)
                g_before = g_total - g_run - (_dot(ghi, from_here) + _dot(glo, from_here))
                dz = g * one_m_sig - g_before * sig
                if mask is not None:
                    dz = jnp.where(mask, dz, 0.0)
                dzb = dz.astype(BF16)
                dq = dq + _dot(_side_by_side(dzb, blk), _split_heads(kblk, lane_is_first))
                dk_acc[rows, :] += _dot_tn(dzb, qcat)
                dv_acc[rows, :] += _dot_tn(wb, docat)
                return (run + jnp.sum(l1m, axis=-1, keepdims=True), g_run + jnp.sum(g, axis=-1, keepdims=True), dq)

            zero = jnp.zeros((2 * blk, 1), F32)
            carry = block(qi, (zero, zero, jnp.zeros((blk, HEAD_PAIR), F32)), causal)
            _, _, dq = _while_mass_left(qi, carry, lambda kb, c, block=block: block(kb, c, None))
            dq_ref[mine, :] = (dq * ATTN_SCALE).astype(BF16)

        @pl.when(pl.program_id(1) == nq - 1)
        def _():
            dk_ref[...] = dk_acc[...].astype(BF16)
            dv_ref[...] = dv_acc[...].astype(BF16)

        finish_rider()

    qspec = pl.BlockSpec((step_rows, HEAD_PAIR), lambda p, i: (i, p))
    whole = pl.BlockSpec((t, HEAD_PAIR), lambda p, i: (0, p))
    out = jax.ShapeDtypeStruct((t, BRANCH_WIDTH), BF16)
    extra = rider or _NO_RIDER
    return pl.pallas_call(
        body, name=name, grid=(N_HEAD_PAIRS, nq),
        in_specs=[qspec,
                  pl.BlockSpec((t, HEAD_PAIR), lambda p, i: (0, N_HEAD_PAIRS + p)),
                  pl.BlockSpec((t, HEAD_PAIR), lambda p, i: (0, 2 * N_HEAD_PAIRS + p)),
                  qspec,
                  pl.BlockSpec((step_rows, HEAD_PAIR), lambda p, i: (i, GA_BLOCK + p)),
                  qspec] + extra.specs,
        out_specs=[qspec, whole, whole, qspec] + extra.specs,
        out_shape=[out, out, out, out] + extra.out_shape,
        scratch_shapes=[pltpu.VMEM((t, HEAD_PAIR), F32), pltpu.VMEM((t, HEAD_PAIR), F32)] + extra.scratch_shapes,
        compiler_params=_params(2),
    )(qkv, qkv, qkv, o, gmga, dcv, *extra.srcs)


def _mesh_position():
    x, y, c = lax.axis_index("x"), lax.axis_index("y"), lax.axis_index("c")
    return x, y, c, 4 * x + 2 * y + c


def _flipped(x, y, c, k):
    return (1 - x if k & 4 else x, 1 - y if k & 2 else y, 1 - c if k & 1 else c)


def _all_to_all(srcs, *, name, same_block):
    ex = _Exchange(srcs, same_block)

    def body(*refs):
        ex.start(refs[:ex.n], refs[ex.n:2 * ex.n], refs[2 * ex.n:])
        ex.finish(refs[:ex.n], refs[ex.n:2 * ex.n], refs[2 * ex.n:])

    return pl.pallas_call(
        body, name=name, in_specs=ex.specs, out_specs=ex.specs, out_shape=ex.out_shape,
        scratch_shapes=ex.scratch_shapes,
    )(*srcs)


def _gather_via_sibling(srcs, *, name):
    n = len(srcs)

    def body(*refs):
        src_refs, dst_refs = refs[:n], refs[n:2 * n]
        send_sems, recv_sems, local_sems = refs[2 * n:]
        x, y, c, me = _mesh_position()
        sibling = (x, y, 1 - c)
        chips = [(1 - x, y), (x, 1 - y), (1 - x, 1 - y)]

        def slot(px, py, pc):
            return 4 * px + 2 * py + pc

        def copy(i, k, block, to, from_src=False):
            return pltpu.make_async_remote_copy(
                src_ref=src_refs[i] if from_src else dst_refs[i].at[slot(*block)], dst_ref=dst_refs[i].at[slot(*block)],
                send_sem=send_sems.at[k, i], recv_sem=recv_sems.at[k, i], device_id=to, device_id_type=MESH)

        mine = [pltpu.make_async_copy(src_refs[i], dst_refs[i].at[me], local_sems.at[i]) for i in range(n)]
        first = [copy(i, 0, (x, y, c), sibling, True) for i in range(n)]
        first += [copy(i, 1 + j, (x, y, c), (*chip, c), True) for j, chip in enumerate(chips) for i in range(n)]
        for cp in mine + first:
            cp.start()
        passed = []
        for j, chip in enumerate(chips):
            for i in range(n):
                copy(i, 1 + j, (*chip, c), (x, y, c)).wait_recv()
                passed.append(copy(i, 4 + j, (*chip, c), sibling))
                passed[-1].start()
        for i in range(n):
            copy(i, 0, sibling, (x, y, c)).wait_recv()
            for j, chip in enumerate(chips):
                copy(i, 4 + j, (*chip, 1 - c), (x, y, c)).wait_recv()
        for cp in first + passed:
            cp.wait_send()
        for cp in mine:
            cp.wait()

    spec = [pl.BlockSpec(memory_space=pl.ANY)] * n
    return pl.pallas_call(
        body, name=name, in_specs=spec, out_specs=spec,
        out_shape=[jax.ShapeDtypeStruct((N_DEV,) + tuple(s.shape), s.dtype) for s in srcs],
        scratch_shapes=[pltpu.SemaphoreType.DMA((N_DEV - 1, n)), pltpu.SemaphoreType.DMA((N_DEV - 1, n)),
                        pltpu.SemaphoreType.DMA((n,))],
    )(*srcs)


class _Exchange:
    def __init__(self, srcs, same_block):
        self.srcs, self.same_block, self.n = list(srcs), same_block, len(srcs)
        self.specs = [pl.BlockSpec(memory_space=pl.ANY)] * self.n
        self.out_shape = [jax.ShapeDtypeStruct((N_DEV,) + tuple(s.shape if same_block else s.shape[1:]), s.dtype)
                          for s in self.srcs]
        self.scratch_shapes = [pltpu.SemaphoreType.DMA((N_DEV - 1, self.n)), pltpu.SemaphoreType.DMA((N_DEV - 1, self.n)),
                               pltpu.SemaphoreType.DMA((self.n,))] if self.n else []

    def _copies(self, src_refs, dst_refs, sems, with_arrivals):
        send_sems, recv_sems, local_sems = sems
        x, y, c, me = _mesh_position()

        def outgoing(i, j):
            return src_refs[i] if self.same_block else src_refs[i].at[j]

        def remote(i, k, slot):
            return pltpu.make_async_remote_copy(
                src_ref=outgoing(i, jnp.bitwise_xor(me, k)), dst_ref=dst_refs[i].at[slot],
                send_sem=send_sems.at[k - 1, i], recv_sem=recv_sems.at[k - 1, i],
                device_id=_flipped(x, y, c, k), device_id_type=MESH)

        pairs = [(i, k) for k in range(1, N_DEV) for i in range(self.n)]
        mine = [pltpu.make_async_copy(outgoing(i, me), dst_refs[i].at[me], local_sems.at[i]) for i in range(self.n)]
        sent = [remote(i, k, me) for i, k in pairs]
        arrivals = [remote(i, k, jnp.bitwise_xor(me, k)) for i, k in pairs] if with_arrivals else []
        return mine, sent, arrivals

    def start(self, src_refs, dst_refs, sems):
        mine, sent, _ = self._copies(src_refs, dst_refs, sems, False)
        for cp in mine + sent:
            cp.start()

    def finish(self, src_refs, dst_refs, sems):
        mine, sent, arrivals = self._copies(src_refs, dst_refs, sems, True)
        for cp in arrivals:
            cp.wait_recv()
        for cp in sent:
            cp.wait_send()
        for cp in mine:
            cp.wait()


_NO_RIDER = _Exchange([], True)


def _adamw(parts, w, m, v, *, name):
    layers, rows, cols = w.shape
    assert len(parts) == layers
    tr = rows
    while tr * cols > ADAM_TILE_ELEMS and tr % 32 == 0:
        tr //= 2
    row = pl.BlockSpec((1, tr, cols), lambda l, i: (l, i, 0))

    def body(*refs):
        p_refs = refs[:layers]
        w_ref, m_ref, v_ref, g_ref, d_ref, nm_ref, nv_ref = refs[layers:]
        layer = pl.program_id(0)
        g = None
        for k in range(N_DEV):
            part = p_refs[0][k]
            for l in range(1, layers):
                part = jnp.where(layer == l, p_refs[l][k], part)
            g = part.astype(F32) if g is None else g + part.astype(F32)
        m2 = ADAM_B1 * m_ref[0] + (1.0 - ADAM_B1) * g
        v2 = ADAM_B2 * v_ref[0] + (1.0 - ADAM_B2) * (g * g)
        m_hat = m2 / (1.0 - ADAM_B1 ** ADAM_STEP)
        v_hat = v2 / (1.0 - ADAM_B2 ** ADAM_STEP)
        g_ref[0] = g
        d_ref[0] = -ADAM_LR * (m_hat / (jnp.sqrt(v_hat) + ADAM_EPS) + ADAM_WD * w_ref[0])
        nm_ref[0] = m2
        nv_ref[0] = v2

    out = jax.ShapeDtypeStruct((layers, rows, cols), F32)
    return pl.pallas_call(
        body, name=name, grid=(layers, rows // tr),
        in_specs=[pl.BlockSpec((N_DEV, tr, cols), lambda l, i: (0, i, 0))] * layers + [row, row, row],
        out_specs=[row, row, row, row], out_shape=[out, out, out, out],
        compiler_params=_params(2),
    )(*parts, w, m, v)


MATMUL_WEIGHTS = ("w_in", "w_pool_out", "w_conv_out", "w_attn_out", "w_o")
SMALL = ("conv_w", "norm_pre", "pool_w", "pool_b", "pool_scale", "conv_b", "conv_ln_g", "conv_ln_b", "norm_post")
WEIGHT_ORDER = ("norm_pre", "w_in", "pool_w", "pool_b", "pool_scale", "w_pool_out", "conv_w", "conv_b",
                "conv_ln_g", "conv_ln_b", "w_conv_out", "w_attn_out", "w_o", "norm_post")


def _shard_axis(name):
    return -2 if name == "w_o" else -1


def _pack_rows(flat_parts, row_multiple):
    flat = jnp.concatenate(flat_parts, axis=-1)
    n = flat.shape[-1]
    chunk = row_multiple * LANES
    total = -(-n // chunk) * chunk
    pad = [(0, 0)] * (flat.ndim - 1) + [(0, total - n)]
    return jnp.pad(flat, pad).reshape(flat.shape[:-1] + (total // LANES, LANES))


def _unpack(buf, shapes):
    flat = buf.reshape(-1)
    out, at = {}, 0
    for name, shape in shapes:
        n = 1
        for s in shape:
            n *= s
        out[name] = flat[at:at + n].reshape(shape)
        at += n
    return out


def _to_dest_major(name, full):
    axis = full.ndim + _shard_axis(name)
    n = full.shape[axis] // N_DEV
    return jnp.stack([lax.slice_in_dim(full, d * n, (d + 1) * n, axis=axis) for d in range(N_DEV)])


def _from_source_major(name, gathered):
    return jnp.concatenate([gathered[d] for d in range(N_DEV)], axis=_shard_axis(name))


def kernel(x, norm_pre, w_in, pool_w, pool_b, pool_scale, w_pool_out, conv_w, conv_b, conv_ln_g, conv_ln_b, w_conv_out, w_attn_out, w_o, norm_post, loss_target, m_norm_pre, m_w_in, m_pool_w, m_pool_b, m_pool_scale, m_w_pool_out, m_conv_w, m_conv_b, m_conv_ln_g, m_conv_ln_b, m_w_conv_out, m_w_attn_out, m_w_o, m_norm_post, v_norm_pre, v_w_in, v_pool_w, v_pool_b, v_pool_scale, v_w_pool_out, v_conv_w, v_conv_b, v_conv_ln_g, v_conv_ln_b, v_w_conv_out, v_w_attn_out, v_w_o, v_norm_post):
    weights = dict(norm_pre=norm_pre, w_in=w_in, pool_w=pool_w, pool_b=pool_b, pool_scale=pool_scale,
                   w_pool_out=w_pool_out, conv_w=conv_w, conv_b=conv_b, conv_ln_g=conv_ln_g, conv_ln_b=conv_ln_b,
                   w_conv_out=w_conv_out, w_attn_out=w_attn_out, w_o=w_o, norm_post=norm_post)
    mom1 = dict(norm_pre=m_norm_pre, w_in=m_w_in, pool_w=m_pool_w, pool_b=m_pool_b, pool_scale=m_pool_scale,
                w_pool_out=m_w_pool_out, conv_w=m_conv_w, conv_b=m_conv_b, conv_ln_g=m_conv_ln_g, conv_ln_b=m_conv_ln_b,
                w_conv_out=m_w_conv_out, w_attn_out=m_w_attn_out, w_o=m_w_o, norm_post=m_norm_post)
    mom2 = dict(norm_pre=v_norm_pre, w_in=v_w_in, pool_w=v_pool_w, pool_b=v_pool_b, pool_scale=v_pool_scale,
                w_pool_out=v_w_pool_out, conv_w=v_conv_w, conv_b=v_conv_b, conv_ln_g=v_conv_ln_g, conv_ln_b=v_conv_ln_b,
                w_conv_out=v_w_conv_out, w_attn_out=v_w_attn_out, w_o=v_w_o, norm_post=v_norm_post)
    xs = x[0]
    target = loss_target[0]

    conv_rows = jnp.pad(conv_w, ((0, 0), (0, CONV_HALO - CONV_KERNEL), (0, 0)))
    shards = [[weights[n][l].astype(BF16) for n in MATMUL_WEIGHTS] for l in range(DEPTH)]
    gathered = _gather_via_sibling([shards[0][0], conv_rows], name="gather_weights")
    full = [{"w_in": _from_source_major("w_in", gathered[0])}, None]
    conv_full = _from_source_major("conv_w", gathered[1])

    def in_sections(w):
        return dict(pg=w[:, 0:1024], c2gc=w[:, 1024:2560], q=w[:, 2560:3072], k=w[:, 3072:3584], v=w[:, 3584:4096],
                    gmga=jnp.concatenate([w[:, 4608:7680], w[:, 4096:4608]], axis=1))

    saved = []
    cur = xs
    for l in range(DEPTH):
        sec = in_sections(full[l]["w_in"])
        w_qkv = jnp.concatenate([sec["q"] * ATTN_SCALE, sec["k"], sec["v"]], axis=1)
        pw = pool_w[l].astype(BF16)
        pb, ps = pool_b[l].reshape(1, -1), pool_scale[l].reshape(1, -1)
        cb, lg, lb = conv_b[l].reshape(1, -1), conv_ln_g[l].reshape(1, -1), conv_ln_b[l].reshape(1, -1)
        h = _rms_fwd(cur, norm_pre[l].reshape(1, -1), name=f"rms_pre_fwd_{l}") if l == 0 else h_next
        pg = _matmul(h, sec["pg"], mode="nn", name=f"proj_pg_{l}")
        c2gc = _matmul(h, sec["c2gc"], mode="nn", name=f"proj_c2gc_{l}")
        qkv = _matmul(h, w_qkv, mode="nn", name=f"proj_qkv_{l}", out_dtype=BF16)
        if l == 0:
            gmga, *arrived = _matmul(h, sec["gmga"], mode="nn", name=f"proj_gmga_{l}", rider=_Exchange(shards[0][1:], True))
            full[0].update({n: _from_source_major(n, g) for n, g in zip(MATMUL_WEIGHTS[1:], arrived)})
        else:
            gmga = _matmul(h, sec["gmga"], mode="nn", name=f"proj_gmga_{l}")
        a_act = _pool_fwd(pg, pw, pb, ps, name=f"pool_fwd_{l}")
        b_act = _conv_fwd(c2gc, conv_full[l], cb, lg, lb, name=f"conv_fwd_{l}")
        rider = _Exchange(shards[l + 1], True) if l + 1 < DEPTH else None
        o, c_act, *arrived = _attn_fwd(qkv, gmga, name=f"attn_fwd_{l}", rider=rider)
        if rider is not None:
            full[l + 1] = {n: _from_source_major(n, g) for n, g in zip(MATMUL_WEIGHTS, arrived)}
        out_weights = [full[l][n] for n in ("w_pool_out", "w_conv_out", "w_attn_out")]
        g_next = norm_pre[l + 1].reshape(1, -1) if l + 1 < DEPTH else None
        mix, out, nxt, *h_next = _merge_fwd(gmga, [a_act, b_act, c_act], out_weights, full[l]["w_o"], cur,
                                            norm_post[l].reshape(1, -1), g_next, name=f"merge_fwd_{l}")
        h_next = h_next[0] if h_next else None
        saved.append(dict(x=cur, h=h, pg=pg, c2gc=c2gc, qkv=qkv, gmga=gmga, a=a_act, b=b_act, c=c_act, o=o,
                          out_weights=out_weights, mix=mix, out=out, pw=pw, pb=pb, ps=ps, cb=cb, lg=lg, lb=lb))
        cur = nxt

    loss_tile, dx = _loss_head(cur, target, name="loss_head")
    loss = lax.psum(loss_tile[0, 0], ("x", "y", "c"))

    grads = {n: [None] * DEPTH for n in WEIGHT_ORDER}
    parts = {n: [None] * DEPTH for n in MATMUL_WEIGHTS}
    for l in reversed(range(DEPTH)):
        s = saved[l]
        dout, grads["norm_post"][l] = _rms_bwd(s["out"], norm_post[l].reshape(1, -1), dx, name=f"rms_post_bwd_{l}")
        grads["w_o"][l] = _matmul(s["mix"], dout, mode="tn", name=f"d_w_o_{l}", out_dtype=BF16)
        dya, dyb, dyc, dgm, da, db, dc = _merge_bwd(dout, full[l]["w_o"], s["gmga"], [s["a"], s["b"], s["c"]],
                                                    s["out_weights"], name=f"merge_bwd_{l}")
        grads["w_pool_out"][l] = _matmul(s["a"], dya, mode="tn", name=f"d_w_pool_out_{l}", out_dtype=BF16)
        grads["w_conv_out"][l] = _matmul(s["b"], dyb, mode="tn", name=f"d_w_conv_out_{l}", out_dtype=BF16)
        grads["w_attn_out"][l] = _matmul(s["c"], dyc, mode="tn", name=f"d_w_attn_out_{l}", out_dtype=BF16)
        rider = _Exchange([_to_dest_major(n, grads[n][l]) for n in MATMUL_WEIGHTS[1:]], False)
        dq, dk, dv, dga, *arrived = _attn_bwd(s["qkv"], s["o"], s["gmga"], dc, name=f"attn_bwd_{l}", rider=rider)
        for n, p in zip(MATMUL_WEIGHTS[1:], arrived):
            parts[n][l] = p
        dc2gc, dcw, dcvec = _conv_bwd(s["c2gc"], db, conv_full[l], s["cb"], s["lg"], s["lb"], name=f"conv_bwd_{l}")
        dpg, dpw, dpvec = _pool_bwd(s["pg"], da, s["pw"], s["pb"], s["ps"], name=f"pool_bwd_{l}")
        grads["conv_w"][l] = dcw[:CONV_KERNEL]
        grads["conv_b"][l], grads["conv_ln_g"][l], grads["conv_ln_b"][l] = dcvec[0], dcvec[1], dcvec[2]
        grads["pool_w"][l] = dpw
        grads["pool_b"][l] = dpvec[0].reshape(4, POOL_GROUP_DIM)
        grads["pool_scale"][l] = dpvec[1]
        dproj = jnp.concatenate([dpg, dc2gc, dq, dk, dv, dga, dgm], axis=1)
        grads["w_in"][l] = _matmul(s["h"], dproj, mode="tn", name=f"d_w_in_{l}", out_dtype=BF16)
        rider = _Exchange([_to_dest_major("w_in", grads["w_in"][l])], False)
        dh, parts["w_in"][l] = _matmul(dproj, full[l]["w_in"], mode="nt", name=f"d_h_{l}", rider=rider)
        dx, dg_pre = _rms_bwd(s["x"], norm_pre[l].reshape(1, -1), dh, name=f"rms_pre_bwd_{l}", resid=dx)
        grads["norm_pre"][l] = dg_pre.reshape(-1)
        grads["norm_post"][l] = grads["norm_post"][l].reshape(-1)
    small_grads = {n: jnp.stack(grads[n]) for n in SMALL}
    replicated = jnp.concatenate([small_grads[n].reshape(-1) for n in SMALL[1:]])
    small = _pack_rows([_to_dest_major("conv_w", small_grads["conv_w"]).reshape(N_DEV, -1),
                        jnp.broadcast_to(replicated, (N_DEV, replicated.size))], 16).astype(BF16)
    (small_parts,) = _all_to_all([small], name="exchange_grads", same_block=False)

    outs = [dict(), dict(), dict(), dict()]
    for n in MATMUL_WEIGHTS:
        res = _adamw(parts[n], weights[n], mom1[n], mom2[n], name=f"adamw_{n}")
        for o, r in zip(outs, res):
            o[n] = r

    def packed(tree):
        return _pack_rows([tree[n].reshape(-1) for n in SMALL], 16)[None]

    shapes = [(n, weights[n].shape) for n in SMALL]
    res = _adamw([small_parts], packed(weights), packed(mom1), packed(mom2), name="adamw_small")
    for o, r in zip(outs, res):
        o.update(_unpack(r, shapes))
    return (loss, dx[None], *[o[n] for o in outs for n in WEIGHT_ORDER])
```
